```python
import jax
import jax.numpy as jnp
from jax import lax
import numpy as np

D_MODEL = 1024
BATCH = 4
SEQ = 4096
DEPTH = 4

HEAD_DIM = 64
N_HEADS = D_MODEL // HEAD_DIM
D_MIX = N_HEADS * HEAD_DIM
GLA_HEADS = N_HEADS // 4
NSA_HEADS = N_HEADS // 2
NSA_KV_HEADS = NSA_HEADS // 4
RET_HEADS = N_HEADS - GLA_HEADS - NSA_HEADS
GLA_W = GLA_HEADS * HEAD_DIM
NSA_W = NSA_HEADS * HEAD_DIM
NSA_KV_W = NSA_KV_HEADS * HEAD_DIM
RET_W = RET_HEADS * HEAD_DIM
GLA_LOWRANK = 16
GLA_TAU = 16.0
CHUNK = 64
CMP_LEN = 32
CMP_STRIDE = 16
CMP_HIDDEN = 128
SLC_LEN = 64
SLC_TOPK = 16
WINDOW = 512
Q_BLOCK = 128
ROPE_THETA = 10000.0
D_FF = 2816
FFN_HALF = 0.5
NORM_EPS = 1e-6
NEG_INF = -1e30
FORCED_SCORE = 1e4
N_ADA = 9

IN_SIZES = (GLA_W, GLA_W, GLA_W, GLA_W, GLA_LOWRANK,
            NSA_W, NSA_KV_W, NSA_KV_W, NSA_KV_W, NSA_KV_W, NSA_KV_W, NSA_KV_W, 3 * NSA_HEADS,
            RET_W, RET_W, RET_W, RET_W)
IN_W = sum(IN_SIZES)
IN_SPLITS = tuple(int(s) for s in np.cumsum(IN_SIZES)[:-1])

kernel_name = 'hybrid_gla_nsa_retnet_macaron_adaln'


def rms_norm(x, g):
    xf = x.astype(jnp.float32)
    y = xf * lax.rsqrt(jnp.mean(xf * xf, axis=-1, keepdims=True) + NORM_EPS) * g
    return y.astype(x.dtype)


def group_norm(x, g):
    xf = x.astype(jnp.float32)
    xc = xf - jnp.mean(xf, axis=-1, keepdims=True)
    return xc * lax.rsqrt(jnp.mean(xc * xc, axis=-1, keepdims=True) + NORM_EPS) * g


def rotary(x, pos):
    half = x.shape[-1] // 2
    inv_freq = ROPE_THETA ** (-jnp.arange(half, dtype=jnp.float32) / half)
    ang = pos.astype(jnp.float32)[:, None] * inv_freq[None, :]
    cos, sin = jnp.cos(ang), jnp.sin(ang)
    xf = x.astype(jnp.float32)
    x1, x2 = xf[..., :half], xf[..., half:]
    return jnp.concatenate([x1 * cos - x2 * sin, x1 * sin + x2 * cos], axis=-1)


def modulate(h, shift, scale):
    return h * (1.0 + scale[:, None, :]) + shift[:, None, :]


def swiglu(h, w_in, w_out):
    gate, up = jnp.split(h @ w_in, 2, axis=-1)
    return (jax.nn.silu(gate) * up) @ w_out


def gla_chunked(q, k, v, log_a):
    B, H, T, dk = q.shape
    dv = v.shape[-1]
    n = T // CHUNK
    f32 = jnp.float32
    q = (q.astype(f32) * dk ** -0.5).reshape(B, H, n, CHUNK, dk)
    k = k.astype(f32).reshape(B, H, n, CHUNK, dk)
    v = v.astype(f32).reshape(B, H, n, CHUNK, dv)
    b = jnp.cumsum(log_a.astype(f32).reshape(B, H, n, CHUNK, dk), axis=3)
    b_last = b[:, :, :, -1:, :]
    q_dec = q * jnp.exp(b)
    causal = jnp.tril(jnp.ones((CHUNK, CHUNK), dtype=bool))
    attn = jnp.einsum('bhnid,bhnjd->bhnij', q_dec, k * jnp.exp(-b))
    attn = jnp.where(causal, attn, 0.0)
    o_intra = jnp.einsum('bhnij,bhnjv->bhniv', attn, v)
    kv = jnp.einsum('bhnjd,bhnjv->bhndv', k * jnp.exp(b_last - b), v)
    chunk_decay = jnp.exp(b_last[:, :, :, 0, :])

    def step(state, inp):
        kv_n, dec_n = inp
        return state * dec_n[..., None] + kv_n, state

    _, s_prev = lax.scan(step, jnp.zeros((B, H, dk, dv), f32),
                         (jnp.moveaxis(kv, 2, 0), jnp.moveaxis(chunk_decay, 2, 0)))
    o_inter = jnp.einsum('bhnid,bhndv->bhniv', q_dec, jnp.moveaxis(s_prev, 0, 2))
    return (o_intra + o_inter).reshape(B, H, T, dv)


def retention_chunked(q, k, v, log_gamma):
    B, H, T, dk = q.shape
    dv = v.shape[-1]
    n = T // CHUNK
    f32 = jnp.float32
    q = q.astype(f32).reshape(B, H, n, CHUNK, dk)
    k = (k.astype(f32) * dk ** -0.5).reshape(B, H, n, CHUNK, dk)
    v = v.astype(f32).reshape(B, H, n, CHUNK, dv)
    idx = jnp.arange(CHUNK, dtype=f32)
    rel = idx[:, None] - idx[None, :]
    decay = jnp.where(rel >= 0, jnp.exp(log_gamma[:, None, None] * rel), 0.0)
    attn = jnp.einsum('bhnid,bhnjd->bhnij', q, k) * decay[None, :, None]
    o_intra = jnp.einsum('bhnij,bhnjv->bhniv', attn, v)
    q_dec = q * jnp.exp(log_gamma[:, None] * (idx + 1.0))[None, :, None, :, None]
    k_dec = k * jnp.exp(log_gamma[:, None] * (CHUNK - 1.0 - idx))[None, :, None, :, None]
    kv = jnp.einsum('bhnjd,bhnjv->bhndv', k_dec, v)
    chunk_decay = jnp.exp(log_gamma * CHUNK)[None, :, None, None]

    def step(state, kv_n):
        return state * chunk_decay + kv_n, state

    _, s_prev = lax.scan(step, jnp.zeros((B, H, dk, dv), f32), jnp.moveaxis(kv, 2, 0))
    o_inter = jnp.einsum('bhnid,bhndv->bhniv', q_dec, jnp.moveaxis(s_prev, 0, 2))
    return (o_intra + o_inter).reshape(B, H, T, dv)


def compress_blocks(blocks, pe, w1, w2):
    h = (blocks + pe).reshape(blocks.shape[:-2] + (CMP_LEN * blocks.shape[-1],))
    return jax.nn.silu(h @ w1) @ w2


def nsa_attention(q, k_cmp, v_cmp, k_slc, v_slc, k_win, v_win, gate_logits,
                  pe_k, pe_v, w1_k, w2_k, w1_v, w2_v, pos):
    f32 = jnp.float32
    B, H, T, dh = q.shape
    G = k_cmp.shape[1]
    R = H // G
    scale = dh ** -0.5
    t = jnp.arange(T)
    q_raw = q.astype(f32).reshape(B, G, R, T, dh) * scale
    q_rot = rotary(q, pos).reshape(B, G, R, T, dh) * scale
    k_cmp, v_cmp, v_slc, v_win = (a.astype(f32) for a in (k_cmp, v_cmp, v_slc, v_win))
    k_slc_r = rotary(k_slc, pos)
    k_win_r = rotary(k_win, pos)

    n_cmp = (T - CMP_LEN) // CMP_STRIDE + 1
    c_start = np.arange(n_cmp) * CMP_STRIDE
    c_idx = c_start[:, None] + np.arange(CMP_LEN)[None, :]
    k_c = compress_blocks(k_cmp[:, :, c_idx], pe_k, w1_k, w2_k)
    v_c = compress_blocks(v_cmp[:, :, c_idx], pe_v, w1_v, w2_v)
    s_c = jnp.einsum('bgrtd,bgnd->bgrtn', q_raw, k_c)
    valid_c = (c_start + CMP_LEN - 1)[None, :] <= t[:, None]
    p_c = jax.nn.softmax(jnp.where(valid_c, s_c, NEG_INF), axis=-1) * valid_c
    o_cmp = jnp.einsum('bgrtn,bgnd->bgrtd', p_c, v_c)

    n_slc = T // SLC_LEN
    s_start = np.arange(n_slc) * SLC_LEN
    overlap = ((c_start[:, None] < s_start[None, :] + SLC_LEN)
               & (c_start[:, None] + CMP_LEN > s_start[None, :])).astype(np.float32)
    imp = jnp.einsum('bgrtn,ns->bgts', p_c, jnp.asarray(overlap))
    cur = t // SLC_LEN
    blk = jnp.arange(n_slc)
    forced = (blk[None, :] == 0) | (blk[None, :] == cur[:, None]) | (blk[None, :] == cur[:, None] - 1)
    causal_s = blk[None, :] <= cur[:, None]
    imp = jnp.where(forced, FORCED_SCORE, jnp.where(causal_s, imp, -1.0))
    n_sel = min(SLC_TOPK, n_slc)
    _, sel = lax.top_k(imp, n_sel)

    k_sb = k_slc_r.reshape(B, G, n_slc, SLC_LEN, dh)
    v_sb = v_slc.reshape(B, G, n_slc, SLC_LEN, dh)
    nqb = T // Q_BLOCK
    b_ix = jnp.arange(B)[:, None, None, None]
    g_ix = jnp.arange(G)[None, :, None, None]

    def select_block(args):
        qb, selb, tb = args
        kg = k_sb[b_ix, g_ix, selb]
        vg = v_sb[b_ix, g_ix, selb]
        s = jnp.einsum('bgrqd,bgqkld->bgrqkl', qb, kg)
        kpos = selb[..., None] * SLC_LEN + jnp.arange(SLC_LEN)
        ok = (kpos <= tb[:, None, None])[:, :, None]
        s = jnp.where(ok, s, NEG_INF).reshape(B, G, R, Q_BLOCK, n_sel * SLC_LEN)
        p = jax.nn.softmax(s, axis=-1).reshape(B, G, R, Q_BLOCK, n_sel, SLC_LEN)
        return jnp.einsum('bgrqkl,bgqkld->bgrqd', p, vg)

    q_blocks = jnp.moveaxis(q_rot.reshape(B, G, R, nqb, Q_BLOCK, dh), 3, 0)
    sel_blocks = jnp.moveaxis(sel.reshape(B, G, nqb, Q_BLOCK, n_sel), 2, 0)
    o_slc = lax.map(select_block, (q_blocks, sel_blocks, t.reshape(nqb, Q_BLOCK)))
    o_slc = jnp.moveaxis(o_slc, 0, 3).reshape(B, G, R, T, dh)

    n_band = WINDOW // Q_BLOCK + 1

    def band(a):
        ap = jnp.pad(a, ((0, 0), (0, 0), (WINDOW, 0), (0, 0))).reshape(B, G, nqb + n_band - 1, Q_BLOCK, dh)
        return jnp.concatenate([ap[:, :, i:i + nqb] for i in range(n_band)], axis=3)

    k_w, v_w = band(k_win_r), band(v_win)
    s_w = jnp.einsum('bgrnid,bgnjd->bgrnij', q_rot.reshape(B, G, R, nqb, Q_BLOCK, dh), k_w)
    qpos = t.reshape(nqb, Q_BLOCK)[:, :, None]
    kpos = ((jnp.arange(nqb)[:, None] - (n_band - 1)) * Q_BLOCK + jnp.arange(n_band * Q_BLOCK)[None, :])[:, None, :]
    ok_w = (kpos <= qpos) & (kpos > qpos - WINDOW) & (kpos >= 0)
    p_w = jax.nn.softmax(jnp.where(ok_w, s_w, NEG_INF), axis=-1)
    o_win = jnp.einsum('bgrnij,bgnjd->bgrnid', p_w, v_w).reshape(B, G, R, T, dh)

    gate = jax.nn.sigmoid(gate_logits.astype(f32)).reshape(B, T, H, 3)
    gate = gate.transpose(0, 2, 1, 3).reshape(B, G, R, T, 3)
    o = gate[..., 0:1] * o_cmp + gate[..., 1:2] * o_slc + gate[..., 2:3] * o_win
    return o.reshape(B, H, T, dh)


def token_mixer(h, w_in, gla_a2, gla_a_bias, gla_norm_g, nsa_pe_k, nsa_pe_v, nsa_w1_k, nsa_w2_k,
                nsa_w1_v, nsa_w2_v, nsa_gate_bias, ret_norm_g, w_out):
    B, T, _ = h.shape
    pos = jnp.arange(T)

    def heads(a, n):
        return a.reshape(B, T, n, HEAD_DIM).transpose(0, 2, 1, 3)

    def merge(a):
        return a.transpose(0, 2, 1, 3).reshape(B, T, -1).astype(h.dtype)

    proj = h @ w_in
    (g_q, g_k, g_v, g_g, g_lr,
     n_q, n_kc, n_vc, n_ks, n_vs, n_kw, n_vw, n_gate,
     r_q, r_k, r_v, r_g) = jnp.split(proj, IN_SPLITS, axis=-1)

    log_a = jax.nn.log_sigmoid((g_lr @ gla_a2 + gla_a_bias).astype(jnp.float32)) / GLA_TAU
    o_gla = gla_chunked(heads(g_q, GLA_HEADS), heads(g_k, GLA_HEADS), heads(g_v, GLA_HEADS),
                        heads(log_a, GLA_HEADS))
    o_gla = merge(rms_norm(o_gla, gla_norm_g)) * jax.nn.silu(g_g)

    o_nsa = nsa_attention(heads(n_q, NSA_HEADS), heads(n_kc, NSA_KV_HEADS), heads(n_vc, NSA_KV_HEADS),
                          heads(n_ks, NSA_KV_HEADS), heads(n_vs, NSA_KV_HEADS),
                          heads(n_kw, NSA_KV_HEADS), heads(n_vw, NSA_KV_HEADS),
                          n_gate + nsa_gate_bias, nsa_pe_k, nsa_pe_v, nsa_w1_k, nsa_w2_k,
                          nsa_w1_v, nsa_w2_v, pos)
    o_nsa = merge(o_nsa)

    log_gamma = jnp.log1p(-jnp.exp2(-5.0 - jnp.arange(RET_HEADS, dtype=jnp.float32)))
    o_ret = retention_chunked(rotary(heads(r_q, RET_HEADS), pos), rotary(heads(r_k, RET_HEADS), pos),
                              heads(r_v, RET_HEADS), log_gamma)
    o_ret = merge(group_norm(o_ret, ret_norm_g)) * jax.nn.silu(r_g)

    return jnp.concatenate([o_gla, o_nsa, o_ret], axis=-1) @ w_out


def setup_inputs(seed: int = 0) -> dict:
    key = jax.random.key(seed)
    ks = jax.random.split(key, 23)
    f32 = jnp.float32

    def nrm(k, shape, scale):
        return jax.random.normal(k, shape, f32) * scale

    L, D = DEPTH, D_MODEL
    cmp_in = CMP_LEN * HEAD_DIM
    return {
        'x': nrm(ks[0], (BATCH, SEQ, D), 1.0),
        'c': nrm(ks[1], (BATCH, D), 1.0),
        'w_ada': nrm(ks[2], (L, D, N_ADA * D), 0.5 * D ** -0.5),
        'b_ada': nrm(ks[3], (L, N_ADA * D), 0.02),
        'norm_g': 1.0 + nrm(ks[4], (L, 3, D), 0.05),
        'ffn1_in': nrm(ks[5], (L, D, 2 * D_FF), D ** -0.5),
        'ffn1_out': nrm(ks[6], (L, D_FF, D), D_FF ** -0.5),
        'w_in': nrm(ks[7], (L, D, IN_W), D ** -0.5),
        'gla_a2': nrm(ks[8], (L, GLA_LOWRANK, GLA_W), GLA_LOWRANK ** -0.5),
        'gla_a_bias': nrm(ks[9], (L, GLA_W), 0.1),
        'gla_norm_g': 1.0 + nrm(ks[10], (L, HEAD_DIM), 0.05),
        'nsa_pe_k': nrm(ks[11], (L, CMP_LEN, HEAD_DIM), 0.1),
        'nsa_pe_v': nrm(ks[12], (L, CMP_LEN, HEAD_DIM), 0.1),
        'nsa_w1_k': nrm(ks[13], (L, cmp_in, CMP_HIDDEN), cmp_in ** -0.5),
        'nsa_w2_k': nrm(ks[14], (L, CMP_HIDDEN, HEAD_DIM), CMP_HIDDEN ** -0.5),
        'nsa_w1_v': nrm(ks[15], (L, cmp_in, CMP_HIDDEN), cmp_in ** -0.5),
        'nsa_w2_v': nrm(ks[16], (L, CMP_HIDDEN, HEAD_DIM), CMP_HIDDEN ** -0.5),
        'nsa_gate_bias': nrm(ks[17], (L, 3 * NSA_HEADS), 0.1),
        'ret_norm_g': 1.0 + nrm(ks[18], (L, HEAD_DIM), 0.05),
        'w_out': nrm(ks[19], (L, D_MIX, D), D_MIX ** -0.5),
        'ffn2_in': nrm(ks[20], (L, D, 2 * D_FF), D ** -0.5),
        'ffn2_out': nrm(ks[21], (L, D_FF, D), D_FF ** -0.5),
        'final_norm_g': 1.0 + nrm(ks[22], (D,), 0.05),
    }


def reference(x, c, w_ada, b_ada, norm_g, ffn1_in, ffn1_out, w_in, gla_a2, gla_a_bias, gla_norm_g,
              nsa_pe_k, nsa_pe_v, nsa_w1_k, nsa_w2_k, nsa_w1_v, nsa_w2_v, nsa_gate_bias, ret_norm_g,
              w_out, ffn2_in, ffn2_out, final_norm_g):
    c_act = jax.nn.silu(c)
    for l in range(DEPTH):
        mod = c_act @ w_ada[l] + b_ada[l]
        sh1, sc1, gt1, shm, scm, gtm, sh2, sc2, gt2 = jnp.split(mod, N_ADA, axis=-1)
        h = modulate(rms_norm(x, norm_g[l, 0]), sh1, sc1)
        x = x + FFN_HALF * gt1[:, None, :] * swiglu(h, ffn1_in[l], ffn1_out[l])
        h = modulate(rms_norm(x, norm_g[l, 1]), shm, scm)
        x = x + gtm[:, None, :] * token_mixer(h, w_in[l], gla_a2[l], gla_a_bias[l], gla_norm_g[l],
                                              nsa_pe_k[l], nsa_pe_v[l], nsa_w1_k[l], nsa_w2_k[l],
                                              nsa_w1_v[l], nsa_w2_v[l], nsa_gate_bias[l],
                                              ret_norm_g[l], w_out[l])
        h = modulate(rms_norm(x, norm_g[l, 2]), sh2, sc2)
        x = x + FFN_HALF * gt2[:, None, :] * swiglu(h, ffn2_in[l], ffn2_out[l])
    return rms_norm(x, final_norm_g)
```

```python
import functools
import math

import numpy as np
import jax
import jax.numpy as jnp
from jax import lax
from jax.experimental import pallas as pl
from jax.experimental.pallas import tpu as pltpu

F32 = jnp.float32
BF16 = jnp.bfloat16

HEAD_DIM = 64
LANES = 128
GLA_HEADS = 4
NSA_HEADS = 8
NSA_KV_HEADS = 2
NSA_REP = NSA_HEADS // NSA_KV_HEADS
RET_HEADS = 4
GLA_LOWRANK = 16
GLA_TAU = 16.0
CHUNK = 64
CMP_LEN = 32
CMP_STRIDE = 16
CMP_HIDDEN = 128
SLC_LEN = 64
SLC_TOPK = 16
WINDOW = 512
Q_BLOCK = 128
ROPE_THETA = 10000.0
FFN_HALF = 0.5
NORM_EPS = 1e-6
NEG_INF = -1e30
FORCED_SCORE = 1e4
N_ADA = 9
QK_SCALE = HEAD_DIM ** -0.5

GLA_W = GLA_HEADS * HEAD_DIM
NSA_W = NSA_HEADS * HEAD_DIM
NSA_KV_W = NSA_KV_HEADS * HEAD_DIM
RET_W = RET_HEADS * HEAD_DIM
IN_SIZES = (GLA_W, GLA_W, GLA_W, GLA_W, GLA_LOWRANK,
            NSA_W, NSA_KV_W, NSA_KV_W, NSA_KV_W, NSA_KV_W, NSA_KV_W, NSA_KV_W, 3 * NSA_HEADS,
            RET_W, RET_W, RET_W, RET_W)
IN_SPLITS = tuple(int(s) for s in np.cumsum(IN_SIZES)[:-1])

SEG_GLA = 0
SEG_LR = SEG_GLA + 4 * GLA_W
SEG_NQ = SEG_LR + LANES
SEG_KV = SEG_NQ + NSA_HEADS * LANES
SEG_GATE = SEG_KV + 6 * NSA_KV_W
SEG_RET = SEG_GATE + NSA_KV_HEADS * LANES
PROJ_W = SEG_RET + 4 * RET_W

VMEM_LIMIT_BYTES = 56 * 1024 * 1024

NT_DIMS = (((1,), (1,)), ((), ()))


def _dot(a, b):
    return jnp.dot(a, b, preferred_element_type=F32)


def _dot_nt(a, b):
    return lax.dot_general(a, b, NT_DIMS, preferred_element_type=F32)


def _split3(x):
    hi = x.astype(BF16)
    r1 = x - hi.astype(F32)
    mid = r1.astype(BF16)
    lo = (r1 - mid.astype(F32)).astype(BF16)
    return hi, mid, lo


def _dot3_exact_rhs(x, w_bf16):
    hi, mid, lo = _split3(x)
    return _dot(hi, w_bf16) + _dot(mid, w_bf16) + _dot(lo, w_bf16)


def _dot3_exact_lhs(w_bf16, x):
    hi, mid, lo = _split3(x)
    return _dot(w_bf16, hi) + _dot(w_bf16, mid) + _dot(w_bf16, lo)


def _sigmoid(x):
    return 1.0 / (1.0 + jnp.exp(-x))


def _silu(x):
    return x * _sigmoid(x)


def _iota(shape, dim):
    return lax.broadcasted_iota(jnp.int32, shape, dim)


def _params(n_grid):
    return pltpu.CompilerParams(dimension_semantics=("arbitrary",) * n_grid,
                                vmem_limit_bytes=VMEM_LIMIT_BYTES)


def _norm_mod(x, g, sh, sc):
    ms = jnp.mean(x * x, axis=-1, keepdims=True)
    return x * lax.rsqrt(ms + NORM_EPS) * (g * (1.0 + sc)) + sh


def _ada_kernel(c_ref, w_ref, b_ref, o_ref):
    c = c_ref[...]
    o_ref[...] = _dot(_silu(c).astype(BF16), w_ref[...].astype(BF16)) + b_ref[...]


def _ada_call(c_pad, w_ada, b_ada):
    n_layers, d, n_out = w_ada.shape
    tn = 2304 if n_out % 2304 == 0 else n_out
    return pl.pallas_call(
        _ada_kernel,
        grid=(n_layers, n_out // tn),
        in_specs=[pl.BlockSpec((8, d), lambda l, j: (0, 0)),
                  pl.BlockSpec((None, d, tn), lambda l, j: (l, 0, j)),
                  pl.BlockSpec((None, 1, tn), lambda l, j: (l, 0, j))],
        out_specs=pl.BlockSpec((None, 8, tn), lambda l, j: (l, 0, j)),
        out_shape=jax.ShapeDtypeStruct((n_layers, 8, n_out), F32),
        compiler_params=_params(2),
        name="ada_mod",
    )(c_pad, w_ada, b_ada.reshape(n_layers, 1, n_out))


def _ffn_kernel(x_ref, g_ref, sh_ref, sc_ref, gt_ref, win_ref, wout_ref, *rest, d_ff, chunks, final):
    if final:
        fg_ref, o_ref, act_ref = rest
    else:
        o_ref, act_ref = rest
    x = x_ref[...]
    h = _norm_mod(x, g_ref[...], sh_ref[...], sc_ref[...]).astype(BF16)
    for (c0, cw) in chunks:
        gate = _dot(h, win_ref[:, c0:c0 + cw])
        up = _dot(h, win_ref[:, d_ff + c0:d_ff + c0 + cw])
        act_ref[:, c0:c0 + cw] = (_silu(gate) * up).astype(BF16)
    y = _dot(act_ref[...], wout_ref[...])
    xn = x + (FFN_HALF * gt_ref[...]) * y
    if final:
        ms = jnp.mean(xn * xn, axis=-1, keepdims=True)
        xn = xn * lax.rsqrt(ms + NORM_EPS) * fg_ref[...]
    o_ref[...] = xn


def _ffn_call(x, ng, mod, w_in, w_out, *, layer, sub, seq, final_g=None, tm=512):
    m, d = x.shape
    d_ff = w_out.shape[1]
    tiles_per_seq = seq // tm
    chunks, c0 = [], 0
    while c0 < d_ff:
        cw = min(512, d_ff - c0)
        chunks.append((c0, cw))
        c0 += cw

    def mod_spec(k):
        return pl.BlockSpec((None, 1, d), lambda i: ((layer * 8 + i // tiles_per_seq) * N_ADA + k, 0, 0))

    in_specs = [pl.BlockSpec((tm, d), lambda i: (i, 0)),
                pl.BlockSpec((None, 1, d), lambda i: (layer * 3 + sub, 0, 0)),
                mod_spec(3 * sub), mod_spec(3 * sub + 1), mod_spec(3 * sub + 2),
                pl.BlockSpec((None, d, 2 * d_ff), lambda i: (layer, 0, 0), pipeline_mode=pl.Buffered(1)),
                pl.BlockSpec((None, d_ff, d), lambda i: (layer, 0, 0), pipeline_mode=pl.Buffered(1))]
    args = [x, ng, mod, mod, mod, w_in, w_out]
    if final_g is not None:
        in_specs.append(pl.BlockSpec((1, d), lambda i: (0, 0)))
        args.append(final_g)
    return pl.pallas_call(
        functools.partial(_ffn_kernel, d_ff=d_ff, chunks=tuple(chunks), final=final_g is not None),
        grid=(m // tm,),
        in_specs=in_specs,
        out_specs=pl.BlockSpec((tm, d), lambda i: (i, 0)),
        out_shape=jax.ShapeDtypeStruct((m, d), F32),
        scratch_shapes=[pltpu.VMEM((tm, d_ff), BF16)],
        compiler_params=_params(1),
        name=f"ffn_l{layer}_s{sub}",
    )(*args)


def _rotate(x, cosf, sinf):
    rows, width = x.shape
    first = (_iota((rows, LANES), 1) & (HEAD_DIM - 1)) < HEAD_DIM // 2
    outs = []
    for j in range(width // LANES):
        xs = x[:, j * LANES:(j + 1) * LANES]
        swapped = jnp.where(first, pltpu.roll(xs, LANES - HEAD_DIM // 2, 1), pltpu.roll(xs, HEAD_DIM // 2, 1))
        outs.append(xs * cosf + swapped * sinf)
    return outs[0] if len(outs) == 1 else jnp.concatenate(outs, axis=1)


def _proj_kernel(x_ref, g_ref, sh_ref, sc_ref, w_ref, gb_ref, cos_ref, sin_ref,
                 gla_ref, lr_ref, qraw_ref, qrot_ref, kcvc_ref, ks_ref, vs_ref, kw_ref, vw_ref,
                 gate_ref, ret_ref):
    h = _norm_mod(x_ref[...], g_ref[...], sh_ref[...], sc_ref[...]).astype(BF16)
    cosf = cos_ref[...]
    sinf = sin_ref[...]

    def seg(c0, width):
        return _dot(h, w_ref[:, c0:c0 + width])

    gla_ref[...] = seg(SEG_GLA, 4 * GLA_W)
    lr_ref[...] = seg(SEG_LR, LANES)
    nq = seg(SEG_NQ, NSA_HEADS * LANES)
    qraw_ref[...] = (nq * QK_SCALE).astype(BF16)
    qrot_ref[...] = (_rotate(nq, cosf, sinf) * QK_SCALE).astype(BF16)
    kv = seg(SEG_KV, 6 * NSA_KV_W)
    kcvc_ref[...] = kv[:, 0:2 * NSA_KV_W]
    ks_ref[...] = _rotate(kv[:, 2 * NSA_KV_W:3 * NSA_KV_W], cosf, sinf).astype(BF16)
    vs_ref[...] = kv[:, 3 * NSA_KV_W:4 * NSA_KV_W].astype(BF16)
    kw_ref[...] = _rotate(kv[:, 4 * NSA_KV_W:5 * NSA_KV_W], cosf, sinf).astype(BF16)
    vw_ref[...] = kv[:, 5 * NSA_KV_W:6 * NSA_KV_W].astype(BF16)
    gate_ref[...] = _sigmoid(seg(SEG_GATE, NSA_KV_HEADS * LANES) + gb_ref[...])
    ret_ref[:, 0:RET_W] = _rotate(seg(SEG_RET, RET_W), cosf, sinf)
    ret_ref[:, RET_W:2 * RET_W] = _rotate(seg(SEG_RET + RET_W, RET_W), cosf, sinf) * QK_SCALE
    ret_ref[:, 2 * RET_W:4 * RET_W] = seg(SEG_RET + 2 * RET_W, 2 * RET_W)


def _proj_call(x, ng, mod, w_proj, gate_bias, cosf, sinf, *, layer, seq, tm=512):
    m, d = x.shape
    tiles_per_seq = seq // tm

    def mod_spec(k):
        return pl.BlockSpec((None, 1, d), lambda i: ((layer * 8 + i // tiles_per_seq) * N_ADA + k, 0, 0))

    def row_spec(width):
        return pl.BlockSpec((tm, width), lambda i: (i, 0))

    out_widths = [(4 * GLA_W, F32), (LANES, F32), (NSA_HEADS * LANES, BF16), (NSA_HEADS * LANES, BF16),
                  (2 * NSA_KV_W, F32), (NSA_KV_W, BF16), (NSA_KV_W, BF16), (NSA_KV_W, BF16), (NSA_KV_W, BF16),
                  (NSA_KV_HEADS * LANES, F32), (4 * RET_W, F32)]
    return pl.pallas_call(
        _proj_kernel,
        grid=(m // tm,),
        in_specs=[row_spec(d),
                  pl.BlockSpec((None, 1, d), lambda i: (layer * 3 + 1, 0, 0)),
                  mod_spec(3), mod_spec(4),
                  pl.BlockSpec((None, d, PROJ_W), lambda i: (layer, 0, 0), pipeline_mode=pl.Buffered(1)),
                  pl.BlockSpec((None, 1, NSA_KV_HEADS * LANES), lambda i: (layer, 0, 0)),
                  pl.BlockSpec((tm, LANES), lambda i: (i % tiles_per_seq, 0)),
                  pl.BlockSpec((tm, LANES), lambda i: (i % tiles_per_seq, 0))],
        out_specs=[row_spec(w) for w, _ in out_widths],
        out_shape=[jax.ShapeDtypeStruct((m, w), dt) for w, dt in out_widths],
        compiler_params=_params(1),
        name=f"proj_l{layer}",
    )(x, ng, mod, mod, w_proj, gate_bias, cosf, sinf)


def _cmp_kernel(x_ref, pet_ref, peb_ref, wt_ref, wb_ref, w2_ref, kc_ref, vc_ref):
    x = x_ref[...]
    n_rows = x.shape[0]
    a = _dot((x + pet_ref[...]).astype(BF16), wt_ref[...])
    b = _dot((x + peb_ref[...]).astype(BF16), wb_ref[...])
    hid = a + pltpu.roll(b, n_rows - 1, 0)
    out = _dot(_silu(hid).astype(BF16), w2_ref[...])
    out = jnp.where(_iota(out.shape, 0) < n_rows - 1, out, 0.0)
    kc_ref[...] = out[:, 0:NSA_KV_W].astype(BF16)
    vc_ref[...] = out[:, NSA_KV_W:2 * NSA_KV_W].astype(BF16)


def _cmp_call(kcvc, pe_top, pe_bot, w_top, w_bot, w2, *, layer, batch, seq):
    rows = seq // CMP_STRIDE
    width = CMP_STRIDE * 2 * NSA_KV_W
    x = kcvc.reshape(batch, rows, width)
    hid_w = 4 * CMP_HIDDEN

    def wspec(shape):
        return pl.BlockSpec((None,) + shape, lambda b: (layer, 0, 0))

    return pl.pallas_call(
        _cmp_kernel,
        grid=(batch,),
        in_specs=[pl.BlockSpec((None, rows, width), lambda b: (b, 0, 0)),
                  wspec((1, width)), wspec((1, width)),
                  wspec((width, hid_w)), wspec((width, hid_w)), wspec((hid_w, 2 * NSA_KV_W))],
        out_specs=[pl.BlockSpec((None, rows, NSA_KV_W), lambda b: (b, 0, 0)),
                   pl.BlockSpec((None, rows, NSA_KV_W), lambda b: (b, 0, 0))],
        out_shape=[jax.ShapeDtypeStruct((batch, rows, NSA_KV_W), BF16),
                   jax.ShapeDtypeStruct((batch, rows, NSA_KV_W), BF16)],
        compiler_params=_params(1),
        name=f"nsa_compress_l{layer}",
    )(x, pe_top, pe_bot, w_top, w_bot, w2)


def _softmax_rows(s, ok):
    sm = jnp.where(ok, s, NEG_INF)
    mx = jnp.max(sm, axis=1, keepdims=True)
    e = jnp.where(ok, jnp.exp(sm - mx), 0.0)
    den = jnp.sum(e, axis=1, keepdims=True)
    return e * (1.0 / jnp.maximum(den, 1e-30))


def _nsa_kernel(qraw_ref, qrot_ref, kc_ref, vc_ref, ks_ref, vs_ref, kw_ref, vw_ref, gate_ref,
                o_ref, imp_ref, *, seq, tq, tk):
    g = pl.program_id(1)
    t0 = pl.program_id(2) * tq
    rows = NSA_REP * tq
    n_cp = seq // CMP_STRIDE
    n_cmp = (seq - CMP_LEN) // CMP_STRIDE + 1
    n_slc = seq // SLC_LEN
    n_sel = min(SLC_TOPK, n_slc)
    slc_shift = int(math.log2(SLC_LEN))

    def stack_heads(ref):
        x = ref[...]
        return jnp.concatenate([x[:, r * LANES:(r + 1) * LANES] for r in range(NSA_REP)], axis=0)

    q_raw = stack_heads(qraw_ref)
    q_rot = stack_heads(qrot_ref)

    s_c = _dot_nt(q_raw, kc_ref[...])
    tt = t0 + (_iota((rows, n_cp), 0) & (tq - 1))
    nn = _iota((rows, n_cp), 1)
    valid_c = (nn * CMP_STRIDE + (CMP_LEN - 1) <= tt) & (nn < n_cmp)
    p_c = _softmax_rows(s_c, valid_c)
    o_cmp = _dot(p_c.astype(BF16), vc_ref[...])

    p_sum = p_c[0:tq]
    for r in range(1, NSA_REP):
        p_sum = p_sum + p_c[r * tq:(r + 1) * tq]
    sb = _iota((LANES, n_cp), 0)
    cb = _iota((LANES, n_cp), 1)
    overlap_t = ((cb * CMP_STRIDE < sb * SLC_LEN + SLC_LEN) & (cb * CMP_STRIDE + CMP_LEN > sb * SLC_LEN)
                 & (sb < n_slc) & (cb < n_cmp))
    overlap_t = jnp.where(overlap_t, 1.0, 0.0).astype(BF16)
    ph, pm, plo = _split3(p_sum)
    imp_t = _dot_nt(overlap_t, ph) + _dot_nt(overlap_t, pm) + _dot_nt(overlap_t, plo)
    blk = _iota((LANES, tq), 0)
    cur = (t0 + _iota((LANES, tq), 1)) >> slc_shift
    forced = (blk == 0) | (blk == cur) | (blk == cur - 1)
    imp_t = jnp.where(forced, FORCED_SCORE, jnp.where(blk <= cur, imp_t, -1.0))
    imp_ref[...] = imp_t

    n_grp = n_slc // 8
    vals = [imp_ref[8 * j:8 * (j + 1), :] for j in range(n_grp)]
    ranks = [jnp.zeros((8, tq), F32) for _ in range(n_grp)]
    sub = _iota((8, tq), 0)
    for i in range(n_slc):
        row = jnp.broadcast_to(imp_ref[i:i + 1, :], (8, tq))
        for j in range(n_grp):
            if 8 * j > i:
                beats = jnp.where(row >= vals[j], 1.0, 0.0)
            elif 8 * j + 7 <= i:
                beats = jnp.where(row > vals[j], 1.0, 0.0)
            else:
                beats = jnp.where(sub + 8 * j > i, jnp.where(row >= vals[j], 1.0, 0.0),
                                  jnp.where(row > vals[j], 1.0, 0.0))
            ranks[j] = ranks[j] + beats
    sel_t = jnp.concatenate([jnp.where(rk < n_sel, 1.0, 0.0) for rk in ranks]
                            + [jnp.zeros((LANES - n_slc, tq), F32)], axis=0)
    sel = jnp.transpose(sel_t).astype(BF16)

    def slc_step(kt, carry):
        m_i, l_i, acc = carry
        k0 = pl.multiple_of(kt * tk, tk)
        k = ks_ref[pl.ds(k0, tk), :]
        v = vs_ref[pl.ds(k0, tk), :]
        s = _dot_nt(q_rot, k)
        expand_t = jnp.where(((k0 + _iota((tk, LANES), 0)) >> slc_shift) == _iota((tk, LANES), 1), 1.0, 0.0)
        sel_keys = _dot_nt(sel, expand_t.astype(BF16))
        ok = (sel_keys > 0.5) & (k0 + _iota((tq, tk), 1) <= t0 + _iota((tq, tk), 0))
        bias = jnp.where(ok, 0.0, NEG_INF)
        s = s + jnp.concatenate([bias] * NSA_REP, axis=0)
        m_new = jnp.maximum(m_i, jnp.max(s, axis=1, keepdims=True))
        alpha = jnp.exp(m_i - m_new)
        p = jnp.exp(s - m_new)
        l_new = alpha * l_i + jnp.sum(p, axis=1, keepdims=True)
        acc = alpha * acc + _dot(p.astype(BF16), v)
        return m_new, l_new, acc

    n_kt = (t0 + tq + tk - 1) // tk
    m0 = jnp.full((rows, 1), NEG_INF, F32)
    l0 = jnp.zeros((rows, 1), F32)
    a0 = jnp.zeros((rows, LANES), F32)
    _, l_s, acc_s = lax.fori_loop(0, n_kt, slc_step, (m0, l0, a0))
    o_slc = acc_s * (1.0 / l_s)

    span = WINDOW + tq
    w0 = pl.multiple_of(jnp.maximum(t0 - WINDOW, 0), tq)
    s_w = _dot_nt(q_rot, kw_ref[pl.ds(w0, span), :])
    kpos = w0 + _iota((rows, span), 1)
    tpos = t0 + (_iota((rows, span), 0) & (tq - 1))
    ok_w = (kpos <= tpos) & (kpos > tpos - WINDOW)
    p_w = _softmax_rows(s_w, ok_w)
    o_win = _dot(p_w.astype(BF16), vw_ref[pl.ds(w0, span), :])

    gate = gate_ref[...]

    def gate_col(j):
        return jnp.concatenate([jnp.broadcast_to(gate[:, 3 * r + j:3 * r + j + 1], (tq, LANES))
                                for r in range(NSA_REP)], axis=0)

    o = gate_col(0) * o_cmp + gate_col(1) * o_slc + gate_col(2) * o_win
    low = _iota((tq, LANES), 1) < HEAD_DIM
    is_g0 = g == 0
    for pair in range(NSA_REP // 2):
        a = o[(2 * pair) * tq:(2 * pair + 1) * tq]
        b = o[(2 * pair + 1) * tq:(2 * pair + 2) * tq]
        a_sw = pltpu.roll(a, HEAD_DIM, 1)
        b_sw = pltpu.roll(b, HEAD_DIM, 1)
        lo_src = jnp.where(is_g0, a, a_sw)
        hi_src = jnp.where(is_g0, b_sw, b)
        o_ref[:, pair * LANES:(pair + 1) * LANES] = jnp.where(low, lo_src, hi_src).astype(BF16)


def _nsa_call(qraw, qrot, kc, vc, ks, vs, kw, vw, gate, *, layer, batch, seq, tq=Q_BLOCK, tk=256):
    nq = seq // tq
    n_cp = seq // CMP_STRIDE
    grp_w = NSA_REP * LANES

    def q_spec():
        return pl.BlockSpec((tq, grp_w), lambda b, g, i: (b * nq + i, g))

    def seq_spec(rows):
        return pl.BlockSpec((None, rows, LANES), lambda b, g, i: (b, 0, 0))

    seq3 = lambda a: a.reshape(batch, seq, LANES)
    return pl.pallas_call(
        functools.partial(_nsa_kernel, seq=seq, tq=tq, tk=tk),
        grid=(batch, NSA_KV_HEADS, nq),
        in_specs=[q_spec(), q_spec(), seq_spec(n_cp), seq_spec(n_cp),
                  seq_spec(seq), seq_spec(seq), seq_spec(seq), seq_spec(seq),
                  pl.BlockSpec((tq, LANES), lambda b, g, i: (b * nq + i, g))],
        out_specs=pl.BlockSpec((tq, NSA_REP * HEAD_DIM), lambda b, g, i: (b * nq + i, g)),
        out_shape=jax.ShapeDtypeStruct((batch * seq, NSA_W), BF16),
        scratch_shapes=[pltpu.VMEM((LANES, tq), F32)],
        compiler_params=_params(3),
        name=f"nsa_attn_l{layer}",
    )(qraw, qrot, kc, vc, seq3(ks), seq3(vs), seq3(kw), seq3(vw), gate)


LIN_BLOCK = 2 * CHUNK


def _lin_block_consts():
    r = _iota((LIN_BLOCK, LANES), 0)
    c = _iota((LIN_BLOCK, LANES), 1)
    same_chunk = (r >= CHUNK) == (c >= CHUNK)
    causal = same_chunk & (r >= c)
    head_diag = same_chunk
    first_rows = r < CHUNK
    low_lanes = c < HEAD_DIM
    return causal, head_diag, first_rows, low_lanes


def _head_stats(x, head_diag_mean):
    hi = x.astype(BF16)
    lo = (x - hi.astype(F32)).astype(BF16)
    return _dot(hi, head_diag_mean) + _dot(lo, head_diag_mean)


def _lin_core(q_dec, k_inv, k_dec, v, q_int, intra_scale, dec_a, dec_b, st_prev, consts):
    causal, head_diag, first_rows, low_lanes = consts
    zero = jnp.zeros_like(q_dec)
    q2 = jnp.concatenate([jnp.where(low_lanes, q_dec, zero), jnp.where(low_lanes, zero, q_dec)], axis=0)
    a2 = _dot_nt(q2.astype(BF16), k_inv.astype(BF16))
    mask2 = jnp.concatenate([causal, causal], axis=0)
    a2 = jnp.where(mask2, a2, 0.0)
    if intra_scale is not None:
        a2 = a2 * intra_scale
    o2 = _dot(a2.astype(BF16), v.astype(BF16))
    o_intra = jnp.where(low_lanes, o2[0:LIN_BLOCK], o2[LIN_BLOCK:])

    v_t = jnp.transpose(v).astype(BF16)
    kd_a = jnp.where(first_rows, k_dec, 0.0).astype(BF16)
    kd_b = jnp.where(first_rows, 0.0, k_dec).astype(BF16)
    st_a = st_prev * dec_a + jnp.where(head_diag, _dot(v_t, kd_a), 0.0)
    st_b = st_a * dec_b + jnp.where(head_diag, _dot(v_t, kd_b), 0.0)
    qi = q_int.astype(BF16)
    o_inter = jnp.where(first_rows, _dot_nt(qi, st_prev.astype(BF16)), _dot_nt(qi, st_a.astype(BF16)))
    return o_intra + o_inter, st_b


def _gla_kernel(x_ref, lr_ref, a2_ref, ab_ref, ng_ref, o_ref, st_ref, *, n_blocks):
    @pl.when(pl.program_id(1) == 0)
    def _():
        st_ref[...] = jnp.zeros_like(st_ref)

    consts = _lin_block_consts()
    causal, head_diag, first_rows, low_lanes = consts
    tri = jnp.where(causal, 1.0, 0.0).astype(BF16)
    mean_op = jnp.where(head_diag, 1.0 / HEAD_DIM, 0.0).astype(BF16)
    a2h = a2_ref[...].astype(BF16)
    a2l = (a2_ref[...] - a2h.astype(F32)).astype(BF16)

    def block(i, carry):
        r0 = pl.multiple_of(i * LIN_BLOCK, LIN_BLOCK)
        lr = lr_ref[pl.ds(r0, LIN_BLOCK), :]
        lh = lr.astype(BF16)
        ll = (lr - lh.astype(F32)).astype(BF16)
        z = _dot(lh, a2h) + _dot(lh, a2l) + _dot(ll, a2h) + ab_ref[...]
        log_a = (jnp.minimum(z, 0.0) - jnp.log(1.0 + jnp.exp(-jnp.abs(z)))) * (1.0 / GLA_TAU)
        cum = _dot3_exact_lhs(tri, log_a)
        for s in range(GLA_W // LANES):
            cols = slice(s * LANES, (s + 1) * LANES)

            def ld(k):
                return x_ref[pl.ds(r0, LIN_BLOCK), k * GLA_W + s * LANES:k * GLA_W + (s + 1) * LANES]

            q, k, v, gg = ld(0), ld(1), ld(2), ld(3)
            b = cum[:, cols]
            bl_a = b[CHUNK - 1:CHUNK, :]
            bl_b = b[LIN_BLOCK - 1:LIN_BLOCK, :]
            bl = jnp.where(first_rows, bl_a, bl_b)
            q_dec = q * QK_SCALE * jnp.exp(b)
            o, st_new = _lin_core(q_dec, k * jnp.exp(-b), k * jnp.exp(bl - b), v, q_dec, None,
                                  jnp.exp(bl_a), jnp.exp(bl_b), st_ref[s], consts)
            st_ref[s] = st_new
            ms = _head_stats(o * o, mean_op)
            y = o * lax.rsqrt(ms + NORM_EPS) * ng_ref[:, cols] * _silu(gg)
            o_ref[pl.ds(r0, LIN_BLOCK), cols] = y.astype(BF16)
        return carry

    lax.fori_loop(0, n_blocks, block, 0)


def _ret_kernel(x_ref, dmat_ref, qf_ref, kf_ref, cd_ref, ng_ref, o_ref, st_ref, *, n_blocks):
    @pl.when(pl.program_id(1) == 0)
    def _():
        st_ref[...] = jnp.zeros_like(st_ref)

    consts = _lin_block_consts()
    causal, head_diag, first_rows, low_lanes = consts
    mean_op = jnp.where(head_diag, 1.0 / HEAD_DIM, 0.0).astype(BF16)

    def block(i, carry):
        r0 = pl.multiple_of(i * LIN_BLOCK, LIN_BLOCK)
        for s in range(RET_W // LANES):
            cols = slice(s * LANES, (s + 1) * LANES)

            def ld(k):
                return x_ref[pl.ds(r0, LIN_BLOCK), k * RET_W + s * LANES:k * RET_W + (s + 1) * LANES]

            q, k, v, gg = ld(0), ld(1), ld(2), ld(3)
            cd = cd_ref[:, cols]
            o, st_new = _lin_core(q, k, k * kf_ref[:, cols], v, q * qf_ref[:, cols], dmat_ref[s],
                                  cd, cd, st_ref[s], consts)
            st_ref[s] = st_new
            mu = _head_stats(o, mean_op)
            oc = o - mu
            var = _head_stats(oc * oc, mean_op)
            y = oc * lax.rsqrt(var + NORM_EPS) * ng_ref[:, cols] * _silu(gg)
            o_ref[pl.ds(r0, LIN_BLOCK), cols] = y.astype(BF16)
        return carry

    lax.fori_loop(0, n_blocks, block, 0)


def _gla_call(gla, lr, a2_pad, a_bias, norm_g, *, layer, batch, seq, tb=512):
    nt = seq // tb
    n_slab = GLA_W // LANES

    def lspec(shape):
        return pl.BlockSpec((None,) + shape, lambda b, t: (layer, 0, 0))

    return pl.pallas_call(
        functools.partial(_gla_kernel, n_blocks=tb // LIN_BLOCK),
        grid=(batch, nt),
        in_specs=[pl.BlockSpec((tb, 4 * GLA_W), lambda b, t: (b * nt + t, 0)),
                  pl.BlockSpec((tb, LANES), lambda b, t: (b * nt + t, 0)),
                  lspec((LANES, GLA_W)), lspec((1, GLA_W)), lspec((1, GLA_W))],
        out_specs=pl.BlockSpec((tb, GLA_W), lambda b, t: (b * nt + t, 0)),
        out_shape=jax.ShapeDtypeStruct((batch * seq, GLA_W), BF16),
        scratch_shapes=[pltpu.VMEM((n_slab, LANES, LANES), F32)],
        compiler_params=_params(2),
        name=f"gla_l{layer}",
    )(gla, lr, a2_pad, a_bias, norm_g)


def _ret_call(ret, dmat, qf, kf, cd, norm_g, *, layer, batch, seq, tb=512):
    nt = seq // tb
    n_slab = RET_W // LANES
    full = lambda a: pl.BlockSpec(a.shape, lambda b, t: (0,) * a.ndim)
    return pl.pallas_call(
        functools.partial(_ret_kernel, n_blocks=tb // LIN_BLOCK),
        grid=(batch, nt),
        in_specs=[pl.BlockSpec((tb, 4 * RET_W), lambda b, t: (b * nt + t, 0)),
                  full(dmat), full(qf), full(kf), full(cd),
                  pl.BlockSpec((None, 1, RET_W), lambda b, t: (layer, 0, 0))],
        out_specs=pl.BlockSpec((tb, RET_W), lambda b, t: (b * nt + t, 0)),
        out_shape=jax.ShapeDtypeStruct((batch * seq, RET_W), BF16),
        scratch_shapes=[pltpu.VMEM((n_slab, LANES, LANES), F32)],
        compiler_params=_params(2),
        name=f"ret_l{layer}",
    )(ret, dmat, qf, kf, cd, norm_g)


def _out_kernel(x_ref, og_ref, on_ref, or_ref, gt_ref, w_ref, o_ref):
    y = (_dot(og_ref[...], w_ref[0:GLA_W, :])
         + _dot(on_ref[...], w_ref[GLA_W:GLA_W + NSA_W, :])
         + _dot(or_ref[...], w_ref[GLA_W + NSA_W:, :]))
    o_ref[...] = x_ref[...] + gt_ref[...] * y


def _out_call(x, o_gla, o_nsa, o_ret, mod, w_out, *, layer, seq, tm=1024):
    m, d = x.shape
    tiles_per_seq = seq // tm
    row = lambda w: pl.BlockSpec((tm, w), lambda i: (i, 0))
    return pl.pallas_call(
        _out_kernel,
        grid=(m // tm,),
        in_specs=[row(d), row(GLA_W), row(NSA_W), row(RET_W),
                  pl.BlockSpec((None, 1, d), lambda i: ((layer * 8 + i // tiles_per_seq) * N_ADA + 5, 0, 0)),
                  pl.BlockSpec((None, GLA_W + NSA_W + RET_W, d), lambda i: (layer, 0, 0))],
        out_specs=row(d),
        out_shape=jax.ShapeDtypeStruct((m, d), F32),
        compiler_params=_params(1),
        name=f"out_proj_l{layer}",
    )(x, o_gla, o_nsa, o_ret, mod, w_out)


def _layout_proj_weights(w_in, nsa_gate_bias):
    n_layers, d, _ = w_in.shape
    (g_q, g_k, g_v, g_g, g_lr, n_q, n_kc, n_vc, n_ks, n_vs, n_kw, n_vw, n_gate,
     r_q, r_k, r_v, r_g) = jnp.split(w_in, IN_SPLITS, axis=-1)
    zeros = lambda w: jnp.zeros((n_layers, d, w), w_in.dtype)
    nq = n_q.reshape(n_layers, d, NSA_HEADS, HEAD_DIM)
    in_g0 = (jnp.arange(NSA_HEADS) < NSA_REP)[None, None, :, None]
    nq_pad = jnp.concatenate([jnp.where(in_g0, nq, 0.0), jnp.where(in_g0, 0.0, nq)], axis=-1)
    nq_pad = nq_pad.reshape(n_layers, d, NSA_HEADS * LANES)
    per_grp = 3 * NSA_REP
    gate_cols = []
    bias_cols = []
    for g in range(NSA_KV_HEADS):
        gate_cols += [n_gate[..., g * per_grp:(g + 1) * per_grp], zeros(LANES - per_grp)]
        bias_cols += [nsa_gate_bias[:, g * per_grp:(g + 1) * per_grp],
                      jnp.zeros((n_layers, LANES - per_grp), nsa_gate_bias.dtype)]
    w = jnp.concatenate([g_q, g_k, g_v, g_g, g_lr, zeros(LANES - GLA_LOWRANK), nq_pad,
                         n_kc, n_vc, n_ks, n_vs, n_kw, n_vw] + gate_cols + [r_q, r_k, r_v, r_g], axis=-1)
    assert w.shape[-1] == PROJ_W
    bias = jnp.concatenate(bias_cols, axis=-1).reshape(n_layers, 1, NSA_KV_HEADS * LANES)
    return w.astype(BF16), bias


def _layout_cmp_weights(pe_k, pe_v, w1_k, w2_k, w1_v, w2_v):
    n_layers = pe_k.shape[0]
    eye = jnp.eye(2, dtype=F32)
    w1 = jnp.stack([w1_k, w1_v], axis=1).reshape(n_layers, 2, CMP_LEN, HEAD_DIM, CMP_HIDDEN)
    w1_full = jnp.einsum('zkldn,kq,gh->zlkgdqhn', w1, eye, eye)
    w1_full = w1_full.reshape(n_layers, CMP_LEN * 2 * NSA_KV_W, 4 * CMP_HIDDEN).astype(BF16)
    half = CMP_STRIDE * 2 * NSA_KV_W
    w2 = jnp.stack([w2_k, w2_v], axis=1)
    w2_full = jnp.einsum('zknd,kq,gh->zqhnkgd', w2, eye, eye)
    w2_full = w2_full.reshape(n_layers, 4 * CMP_HIDDEN, 2 * NSA_KV_W).astype(BF16)
    pe = jnp.stack([pe_k, pe_v], axis=1)
    pe = jnp.broadcast_to(pe.transpose(0, 2, 1, 3)[:, :, :, None, :],
                          (n_layers, CMP_LEN, 2, NSA_KV_HEADS, HEAD_DIM))
    pe = pe.reshape(n_layers, 1, CMP_LEN * 2 * NSA_KV_W)
    return pe[:, :, :half], pe[:, :, half:], w1_full[:, :half], w1_full[:, half:], w2_full


def _rotary_tables(seq):
    half = HEAD_DIM // 2
    inv_freq = ROPE_THETA ** (-jnp.arange(half, dtype=F32) / half)
    ang = jnp.arange(seq).astype(F32)[:, None] * inv_freq[None, :]
    cos, sin = jnp.cos(ang), jnp.sin(ang)
    reps = LANES // HEAD_DIM
    return jnp.tile(jnp.concatenate([cos, cos], axis=-1), (1, reps)), \
        jnp.tile(jnp.concatenate([-sin, sin], axis=-1), (1, reps))


def _retention_tables():
    log_gamma = jnp.log1p(-jnp.exp2(-5.0 - jnp.arange(RET_HEADS, dtype=F32)))
    lg_lane = jnp.repeat(log_gamma, HEAD_DIM)[None, :]
    pos = (jnp.arange(LIN_BLOCK) % CHUNK).astype(F32)[:, None]
    qf = jnp.exp(lg_lane * (pos + 1.0))
    kf = jnp.exp(lg_lane * (CHUNK - 1.0 - pos))
    cd = jnp.exp(lg_lane * CHUNK)
    r = jnp.arange(LIN_BLOCK)
    rel = (r[:, None] - r[None, :]).astype(F32)
    ok = ((r[:, None] // CHUNK) == (r[None, :] // CHUNK)) & (rel >= 0)
    dm = jnp.where(ok[None], jnp.exp(log_gamma[:, None, None] * rel[None]), 0.0)
    dmat = dm.reshape(RET_W // LANES, 2 * LIN_BLOCK, LIN_BLOCK)
    return dmat, qf, kf, cd


def kernel(x, c, w_ada, b_ada, norm_g, ffn1_in, ffn1_out, w_in, gla_a2, gla_a_bias, gla_norm_g,
           nsa_pe_k, nsa_pe_v, nsa_w1_k, nsa_w2_k, nsa_w1_v, nsa_w2_v, nsa_gate_bias, ret_norm_g,
           w_out, ffn2_in, ffn2_out, final_norm_g):
    batch, seq, d = x.shape
    n_layers = w_ada.shape[0]
    assert batch <= 8 and seq % 512 == 0 and seq >= WINDOW + Q_BLOCK

    c_pad = jnp.zeros((8, d), F32).at[:batch].set(c)
    mod = _ada_call(c_pad, w_ada, b_ada).reshape(n_layers * 8 * N_ADA, 1, d)
    ng = norm_g.reshape(n_layers * 3, 1, d)

    w_proj, gate_bias = _layout_proj_weights(w_in, nsa_gate_bias)
    pe_top, pe_bot, w1_top, w1_bot, w2_cmp = _layout_cmp_weights(nsa_pe_k, nsa_pe_v, nsa_w1_k, nsa_w2_k,
                                                                  nsa_w1_v, nsa_w2_v)
    cosf, sinf = _rotary_tables(seq)
    dmat, qf, kf, cd = _retention_tables()
    a2_pad = jnp.zeros((n_layers, LANES, GLA_W), F32).at[:, :GLA_LOWRANK].set(gla_a2)
    a_bias = gla_a_bias.reshape(n_layers, 1, GLA_W)
    gla_g = jnp.tile(gla_norm_g, (1, GLA_HEADS)).reshape(n_layers, 1, GLA_W)
    ret_g = jnp.tile(ret_norm_g, (1, RET_HEADS)).reshape(n_layers, 1, RET_W)
    f1_in, f1_out = ffn1_in.astype(BF16), ffn1_out.astype(BF16)
    f2_in, f2_out = ffn2_in.astype(BF16), ffn2_out.astype(BF16)
    w_o = w_out.astype(BF16)

    xs = x.reshape(batch * seq, d)
    for l in range(n_layers):
        xs = _ffn_call(xs, ng, mod, f1_in, f1_out, layer=l, sub=0, seq=seq)
        (gla, lr, qraw, qrot, kcvc, ks, vs, kw, vw, gate, ret) = _proj_call(
            xs, ng, mod, w_proj, gate_bias, cosf, sinf, layer=l, seq=seq)
        kc, vc = _cmp_call(kcvc, pe_top, pe_bot, w1_top, w1_bot, w2_cmp, layer=l, batch=batch, seq=seq)
        o_gla = _gla_call(gla, lr, a2_pad, a_bias, gla_g, layer=l, batch=batch, seq=seq)
        o_ret = _ret_call(ret, dmat, qf, kf, cd, ret_g, layer=l, batch=batch, seq=seq)
        o_nsa = _nsa_call(qraw, qrot, kc, vc, ks, vs, kw, vw, gate, layer=l, batch=batch, seq=seq)
        xs = _out_call(xs, o_gla, o_nsa, o_ret, mod, w_o, layer=l, seq=seq)
        final_g = final_norm_g.reshape(1, d) if l == n_layers - 1 else None
        xs = _ffn_call(xs, ng, mod, f2_in, f2_out, layer=l, sub=2, seq=seq, final_g=final_g)
    return xs.reshape(batch, seq, d)
```

```python
import functools
import math

import numpy as np
import jax
import jax.numpy as jnp
from jax import lax
from jax.experimental import pallas as pl
from jax.experimental.pallas import tpu as pltpu

F32 = jnp.float32
BF16 = jnp.bfloat16

HEAD_DIM = 64
LANES = 128
GLA_HEADS = 4
NSA_HEADS = 8
NSA_KV_HEADS = 2
NSA_REP = NSA_HEADS // NSA_KV_HEADS
RET_HEADS = 4
GLA_LOWRANK = 16
GLA_TAU = 16.0
CHUNK = 64
CMP_LEN = 32
CMP_STRIDE = 16
CMP_HIDDEN = 128
SLC_LEN = 64
SLC_TOPK = 16
WINDOW = 512
Q_BLOCK = 128
ROPE_THETA = 10000.0
FFN_HALF = 0.5
NORM_EPS = 1e-6
NEG_INF = -1e30
FORCED_SCORE = 1e4
N_ADA = 9
QK_SCALE = HEAD_DIM ** -0.5
SLC_SHIFT = int(math.log2(SLC_LEN))

GLA_W = GLA_HEADS * HEAD_DIM
NSA_W = NSA_HEADS * HEAD_DIM
NSA_KV_W = NSA_KV_HEADS * HEAD_DIM
RET_W = RET_HEADS * HEAD_DIM
IN_SIZES = (GLA_W, GLA_W, GLA_W, GLA_W, GLA_LOWRANK,
            NSA_W, NSA_KV_W, NSA_KV_W, NSA_KV_W, NSA_KV_W, NSA_KV_W, NSA_KV_W, 3 * NSA_HEADS,
            RET_W, RET_W, RET_W, RET_W)
IN_SPLITS = tuple(int(s) for s in np.cumsum(IN_SIZES)[:-1])

SEG_GLA = 0
SEG_LR = SEG_GLA + 4 * GLA_W
SEG_NQ = SEG_LR + LANES
SEG_KCVC = SEG_NQ + NSA_HEADS * LANES
SEG_KS = SEG_KCVC + 2 * NSA_KV_W
SEG_KW = SEG_KS + NSA_KV_HEADS * LANES
SEG_VS = SEG_KW + NSA_KV_HEADS * LANES
SEG_VW = SEG_VS + NSA_KV_W
SEG_GATE = SEG_VW + NSA_KV_W
SEG_RET = SEG_GATE + NSA_KV_HEADS * LANES
PROJ_W = SEG_RET + 4 * RET_W

VMEM_LIMIT_BYTES = 56 * 1024 * 1024

NT_DIMS = (((1,), (1,)), ((), ()))


def _dot(a, b):
    return jnp.dot(a, b, preferred_element_type=F32)


def _dot_nt(a, b):
    return lax.dot_general(a, b, NT_DIMS, preferred_element_type=F32)


def _split3(x):
    hi = x.astype(BF16)
    r1 = x - hi.astype(F32)
    mid = r1.astype(BF16)
    lo = (r1 - mid.astype(F32)).astype(BF16)
    return hi, mid, lo


def _dot3_exact_rhs(x, w_bf16):
    hi, mid, lo = _split3(x)
    return _dot(hi, w_bf16) + _dot(mid, w_bf16) + _dot(lo, w_bf16)


def _dot3_exact_lhs(w_bf16, x):
    hi, mid, lo = _split3(x)
    return _dot(w_bf16, hi) + _dot(w_bf16, mid) + _dot(w_bf16, lo)


def _sigmoid(x):
    return 1.0 / (1.0 + jnp.exp(-x))


def _silu(x):
    return x * _sigmoid(x)


def _iota(shape, dim):
    return lax.broadcasted_iota(jnp.int32, shape, dim)


def _params(n_grid):
    return pltpu.CompilerParams(dimension_semantics=("arbitrary",) * n_grid,
                                vmem_limit_bytes=VMEM_LIMIT_BYTES)


def _norm_mod(x, g, sh, sc):
    ms = jnp.mean(x * x, axis=-1, keepdims=True)
    return x * lax.rsqrt(ms + NORM_EPS) * (g * (1.0 + sc)) + sh


def _ada_kernel(c_ref, w_ref, b_ref, o_ref):
    c = c_ref[...]
    o_ref[...] = _dot(_silu(c).astype(BF16), w_ref[...].astype(BF16)) + b_ref[...]


def _ada_call(c_pad, w_ada, b_ada):
    n_layers, d, n_out = w_ada.shape
    tn = 2304 if n_out % 2304 == 0 else n_out
    return pl.pallas_call(
        _ada_kernel,
        grid=(n_layers, n_out // tn),
        in_specs=[pl.BlockSpec((8, d), lambda l, j: (0, 0)),
                  pl.BlockSpec((None, d, tn), lambda l, j: (l, 0, j)),
                  pl.BlockSpec((None, 1, tn), lambda l, j: (l, 0, j))],
        out_specs=pl.BlockSpec((None, 8, tn), lambda l, j: (l, 0, j)),
        out_shape=jax.ShapeDtypeStruct((n_layers, 8, n_out), F32),
        compiler_params=_params(2),
        name="ada_mod",
    )(c_pad, w_ada, b_ada.reshape(n_layers, 1, n_out))


def _ffn_kernel(x_ref, g_ref, sh_ref, sc_ref, gt_ref, win_ref, wout_ref, *rest, d_ff, chunks, final):
    if final:
        fg_ref, o_ref, act_ref = rest
    else:
        o_ref, act_ref = rest
    x = x_ref[...]
    h = _norm_mod(x, g_ref[...], sh_ref[...], sc_ref[...]).astype(BF16)
    for (c0, cw) in chunks:
        gate = _dot(h, win_ref[:, c0:c0 + cw])
        up = _dot(h, win_ref[:, d_ff + c0:d_ff + c0 + cw])
        act_ref[:, c0:c0 + cw] = (_silu(gate) * up).astype(BF16)
    y = _dot(act_ref[...], wout_ref[...])
    xn = x + (FFN_HALF * gt_ref[...]) * y
    if final:
        ms = jnp.mean(xn * xn, axis=-1, keepdims=True)
        xn = xn * lax.rsqrt(ms + NORM_EPS) * fg_ref[...]
    o_ref[...] = xn


def _ffn_call(x, ng, mod, w_in, w_out, *, layer, sub, seq, final_g=None, tm=512):
    m, d = x.shape
    d_ff = w_out.shape[1]
    tiles_per_seq = seq // tm
    chunks, c0 = [], 0
    while c0 < d_ff:
        cw = min(512, d_ff - c0)
        chunks.append((c0, cw))
        c0 += cw

    def mod_spec(k):
        return pl.BlockSpec((None, 1, d), lambda i: ((layer * 8 + i // tiles_per_seq) * N_ADA + k, 0, 0))

    in_specs = [pl.BlockSpec((tm, d), lambda i: (i, 0)),
                pl.BlockSpec((None, 1, d), lambda i: (layer * 3 + sub, 0, 0)),
                mod_spec(3 * sub), mod_spec(3 * sub + 1), mod_spec(3 * sub + 2),
                pl.BlockSpec((None, d, 2 * d_ff), lambda i: (layer, 0, 0), pipeline_mode=pl.Buffered(1)),
                pl.BlockSpec((None, d_ff, d), lambda i: (layer, 0, 0), pipeline_mode=pl.Buffered(1))]
    args = [x, ng, mod, mod, mod, w_in, w_out]
    if final_g is not None:
        in_specs.append(pl.BlockSpec((1, d), lambda i: (0, 0)))
        args.append(final_g)
    return pl.pallas_call(
        functools.partial(_ffn_kernel, d_ff=d_ff, chunks=tuple(chunks), final=final_g is not None),
        grid=(m // tm,),
        in_specs=in_specs,
        out_specs=pl.BlockSpec((tm, d), lambda i: (i, 0)),
        out_shape=jax.ShapeDtypeStruct((m, d), F32),
        scratch_shapes=[pltpu.VMEM((tm, d_ff), BF16)],
        compiler_params=_params(1),
        name=f"ffn_l{layer}_s{sub}",
    )(*args)


def _rotate(x, cosf, sinf):
    rows, width = x.shape
    first = (_iota((rows, LANES), 1) & (HEAD_DIM - 1)) < HEAD_DIM // 2
    outs = []
    for j in range(width // LANES):
        xs = x[:, j * LANES:(j + 1) * LANES]
        swapped = jnp.where(first, pltpu.roll(xs, LANES - HEAD_DIM // 2, 1), pltpu.roll(xs, HEAD_DIM // 2, 1))
        outs.append(xs * cosf + swapped * sinf)
    return outs[0] if len(outs) == 1 else jnp.concatenate(outs, axis=1)


def _transposed_pieces(x):
    return [jnp.transpose(x[c * LANES:(c + 1) * LANES]) for c in range(x.shape[0] // LANES)]


def _proj_kernel(x_ref, g_ref, sh_ref, sc_ref, w_ref, gb_ref, cos_ref, sin_ref,
                 gla_ref, lr_ref, qraw_ref, qrot_ref, kcvc_ref, ks_ref, vst_ref, kw_ref, vwt_ref,
                 gate_ref, ret_ref, *, tiles_per_seq):
    h = _norm_mod(x_ref[...], g_ref[...], sh_ref[...], sc_ref[...]).astype(BF16)
    cosf = cos_ref[...]
    sinf = sin_ref[...]
    tm = h.shape[0]

    def seg(c0, width):
        return _dot(h, w_ref[:, c0:c0 + width])

    gla_ref[...] = seg(SEG_GLA, 4 * GLA_W)
    lr_ref[...] = seg(SEG_LR, LANES)
    nq = seg(SEG_NQ, NSA_HEADS * LANES)
    qraw_ref[...] = (nq * QK_SCALE).astype(BF16)
    qrot_ref[...] = (_rotate(nq, cosf, sinf) * QK_SCALE).astype(BF16)
    kcvc_ref[...] = seg(SEG_KCVC, 2 * NSA_KV_W)
    grp_w = NSA_KV_HEADS * LANES
    pos = (pl.program_id(0) % tiles_per_seq) * tm + _iota((tm, grp_w), 0)
    spare = (_iota((tm, grp_w), 1) & (LANES - 1)) - HEAD_DIM
    blk_onehot = jnp.where(spare == (pos >> SLC_SHIFT), 1.0, 0.0)
    ks_ref[...] = (_rotate(seg(SEG_KS, grp_w), cosf, sinf) + blk_onehot).astype(BF16)
    kw_ref[...] = _rotate(seg(SEG_KW, grp_w), cosf, sinf).astype(BF16)
    for c, piece in enumerate(_transposed_pieces(seg(SEG_VS, NSA_KV_W))):
        vst_ref[c] = piece.astype(BF16)
    for c, piece in enumerate(_transposed_pieces(seg(SEG_VW, NSA_KV_W))):
        vwt_ref[c] = piece.astype(BF16)
    gate_ref[...] = _sigmoid(seg(SEG_GATE, NSA_KV_HEADS * LANES) + gb_ref[...])
    ret_ref[:, 0:RET_W] = _rotate(seg(SEG_RET, RET_W), cosf, sinf)
    ret_ref[:, RET_W:2 * RET_W] = _rotate(seg(SEG_RET + RET_W, RET_W), cosf, sinf) * QK_SCALE
    ret_ref[:, 2 * RET_W:4 * RET_W] = seg(SEG_RET + 2 * RET_W, 2 * RET_W)


def _proj_call(x, ng, mod, w_proj, gate_bias, cosf, sinf, *, layer, seq, tm=512):
    m, d = x.shape
    tiles_per_seq = seq // tm

    def mod_spec(k):
        return pl.BlockSpec((None, 1, d), lambda i: ((layer * 8 + i // tiles_per_seq) * N_ADA + k, 0, 0))

    def row_spec(width):
        return pl.BlockSpec((tm, width), lambda i: (i, 0))

    batch = m // seq
    grp_w = NSA_KV_HEADS * LANES

    def vt_spec():
        return pl.BlockSpec((None, tm // LANES, NSA_KV_W, LANES),
                            lambda i: (i // tiles_per_seq, i % tiles_per_seq, 0, 0))

    vt_shape = jax.ShapeDtypeStruct((batch, seq // LANES, NSA_KV_W, LANES), BF16)
    out_widths = [(4 * GLA_W, F32), (LANES, F32), (NSA_HEADS * LANES, BF16), (NSA_HEADS * LANES, BF16),
                  (2 * NSA_KV_W, F32), (grp_w, BF16), None, (grp_w, BF16), None,
                  (NSA_KV_HEADS * LANES, F32), (4 * RET_W, F32)]
    return pl.pallas_call(
        functools.partial(_proj_kernel, tiles_per_seq=tiles_per_seq),
        grid=(m // tm,),
        in_specs=[row_spec(d),
                  pl.BlockSpec((None, 1, d), lambda i: (layer * 3 + 1, 0, 0)),
                  mod_spec(3), mod_spec(4),
                  pl.BlockSpec((None, d, PROJ_W), lambda i: (layer, 0, 0), pipeline_mode=pl.Buffered(1)),
                  pl.BlockSpec((None, 1, NSA_KV_HEADS * LANES), lambda i: (layer, 0, 0)),
                  pl.BlockSpec((tm, LANES), lambda i: (i % tiles_per_seq, 0)),
                  pl.BlockSpec((tm, LANES), lambda i: (i % tiles_per_seq, 0))],
        out_specs=[vt_spec() if ow is None else row_spec(ow[0]) for ow in out_widths],
        out_shape=[vt_shape if ow is None else jax.ShapeDtypeStruct((m, ow[0]), ow[1]) for ow in out_widths],
        compiler_params=_params(1),
        name=f"proj_l{layer}",
    )(x, ng, mod, mod, w_proj, gate_bias, cosf, sinf)


def _cmp_kernel(x_ref, pet_ref, peb_ref, wt_ref, wb_ref, w2_ref, kc_ref, vct_ref):
    x = x_ref[...]
    n_rows = x.shape[0]
    a = _dot((x + pet_ref[...]).astype(BF16), wt_ref[...])
    b = _dot((x + peb_ref[...]).astype(BF16), wb_ref[...])
    hid = a + pltpu.roll(b, n_rows - 1, 0)
    out = _dot(_silu(hid).astype(BF16), w2_ref[...])
    out = jnp.where(_iota(out.shape, 0) < n_rows - 1, out, 0.0)
    grp_w = NSA_KV_HEADS * LANES
    kc_ref[...] = out[:, 0:grp_w].astype(BF16)
    for c, piece in enumerate(_transposed_pieces(out[:, grp_w:grp_w + NSA_KV_W])):
        vct_ref[:, c * LANES:(c + 1) * LANES] = piece.astype(BF16)


def _cmp_call(kcvc, pe_top, pe_bot, w_top, w_bot, w2, *, layer, batch, seq):
    rows = seq // CMP_STRIDE
    width = CMP_STRIDE * 2 * NSA_KV_W
    x = kcvc.reshape(batch, rows, width)
    hid_w = 4 * CMP_HIDDEN
    grp_w = NSA_KV_HEADS * LANES

    def wspec(shape):
        return pl.BlockSpec((None,) + shape, lambda b: (layer, 0, 0))

    return pl.pallas_call(
        _cmp_kernel,
        grid=(batch,),
        in_specs=[pl.BlockSpec((None, rows, width), lambda b: (b, 0, 0)),
                  wspec((1, width)), wspec((1, width)),
                  wspec((width, hid_w)), wspec((width, hid_w)), wspec((hid_w, grp_w + NSA_KV_W))],
        out_specs=[pl.BlockSpec((None, rows, grp_w), lambda b: (b, 0, 0)),
                   pl.BlockSpec((None, NSA_KV_W, rows), lambda b: (b, 0, 0))],
        out_shape=[jax.ShapeDtypeStruct((batch, rows, grp_w), BF16),
                   jax.ShapeDtypeStruct((batch, NSA_KV_W, rows), BF16)],
        compiler_params=_params(1),
        name=f"nsa_compress_l{layer}",
    )(x, pe_top, pe_bot, w_top, w_bot, w2)


def _heads_tiled(x):
    return jnp.concatenate([x] * NSA_REP, axis=1)


def _q_aug(x, extra_rows):
    x = x.astype(F32)
    parts = [jnp.concatenate([jnp.transpose(x[:, r * LANES:(r + 1) * LANES])[0:HEAD_DIM, :], extra_rows], axis=0)
             for r in range(NSA_REP)]
    return jnp.concatenate(parts, axis=1).astype(BF16)


def _nsa_compressed(q_raw, kc, vc_t, t0, tq, seq):
    n_cp = seq // CMP_STRIDE
    n_cmp = (seq - CMP_LEN) // CMP_STRIDE + 1
    s_c = _dot(kc, q_raw)
    nn = _iota((n_cp, tq), 0)
    tt = t0 + _iota((n_cp, tq), 1)
    bias_c = jnp.where((nn * CMP_STRIDE + (CMP_LEN - 1) <= tt) & (nn < n_cmp), 0.0, NEG_INF)
    has_cmp = t0 + _iota((1, tq), 1) >= CMP_LEN - 1
    p_heads = []
    for r in range(NSA_REP):
        s = s_c[:, r * tq:(r + 1) * tq] + bias_c
        e = jnp.exp(s - jnp.max(s, axis=0, keepdims=True))
        p_heads.append(e * jnp.where(has_cmp, 1.0 / jnp.sum(e, axis=0, keepdims=True), 0.0))
    o_cmp = _dot(vc_t, jnp.concatenate(p_heads, axis=1).astype(BF16))
    p_sum = p_heads[0]
    for r in range(1, NSA_REP):
        p_sum = p_sum + p_heads[r]
    return o_cmp, p_sum


def _nsa_selection_rows(p_sum, imp_ref, t0, tq, seq):
    n_cp = seq // CMP_STRIDE
    n_cmp = (seq - CMP_LEN) // CMP_STRIDE + 1
    n_slc = seq // SLC_LEN
    n_sel = min(SLC_TOPK, n_slc)
    sb = _iota((LANES, n_cp), 0)
    cb = _iota((LANES, n_cp), 1)
    overlap_t = ((cb * CMP_STRIDE < sb * SLC_LEN + SLC_LEN) & (cb * CMP_STRIDE + CMP_LEN > sb * SLC_LEN)
                 & (sb < n_slc) & (cb < n_cmp))
    imp_t = _dot3_exact_lhs(jnp.where(overlap_t, 1.0, 0.0).astype(BF16), p_sum)
    blk = _iota((LANES, tq), 0)
    cur = (t0 + _iota((LANES, tq), 1)) >> SLC_SHIFT
    forced = (blk == 0) | (blk == cur) | (blk == cur - 1)
    imp_ref[...] = jnp.where(forced, FORCED_SCORE, jnp.where(blk <= cur, imp_t, -1.0))

    n_grp = n_slc // 8
    vals = [imp_ref[8 * j:8 * (j + 1), :] for j in range(n_grp)]
    ranks = [jnp.zeros((8, tq), F32) for _ in range(n_grp)]
    sub = _iota((8, tq), 0)
    for i in range(n_slc):
        row = jnp.broadcast_to(imp_ref[i:i + 1, :], (8, tq))
        for j in range(n_grp):
            if 8 * j > i:
                beats = jnp.where(row >= vals[j], 1.0, 0.0)
            elif 8 * j + 7 <= i:
                beats = jnp.where(row > vals[j], 1.0, 0.0)
            else:
                beats = jnp.where(sub + 8 * j > i, jnp.where(row >= vals[j], 1.0, 0.0),
                                  jnp.where(row > vals[j], 1.0, 0.0))
            ranks[j] = ranks[j] + beats
    return jnp.concatenate([jnp.where(rk < n_sel, 0.0, NEG_INF) for rk in ranks]
                           + [jnp.zeros((HEAD_DIM - n_slc, tq), F32)] * (n_slc < HEAD_DIM), axis=0)


def _online_step(s, v_t, carry):
    m_i, l_i, acc = carry
    m_new = jnp.maximum(m_i, jnp.max(s, axis=0, keepdims=True))
    alpha = jnp.exp(m_i - m_new)
    p = jnp.exp(s - m_new)
    l_new = alpha * l_i + jnp.sum(p, axis=0, keepdims=True)
    return m_new, l_new, alpha * acc + _dot(v_t, p.astype(BF16))


def _nsa_window(q_rot, kw_ref, vwt_ref, g, qb, tq):
    n_wb = WINDOW // tq + 1
    ki = _iota((tq, tq), 0)
    ci = _iota((tq, tq), 1)
    blocks = []
    for j in range(n_wb):
        kb = qb - (n_wb - 1) + j
        kb_read = jnp.maximum(kb, 0)
        s = _dot(kw_ref[pl.ds(pl.multiple_of(kb_read * tq, tq), tq), g * LANES:(g + 1) * LANES], q_rot)
        if j == 0:
            s = s + _heads_tiled(jnp.where(ki > ci, 0.0, NEG_INF))
        if j == n_wb - 1:
            s = s + _heads_tiled(jnp.where(ki <= ci, 0.0, NEG_INF))
        else:
            s = s + jnp.where(kb >= 0, 0.0, NEG_INF)
        blocks.append((s, kb_read))
    m_w = blocks[0][0].max(axis=0, keepdims=True)
    for s, _ in blocks[1:]:
        m_w = jnp.maximum(m_w, s.max(axis=0, keepdims=True))
    den_w = jnp.zeros_like(m_w)
    o_win = jnp.zeros((LANES, NSA_REP * tq), F32)
    for s, kb_read in blocks:
        e = jnp.exp(s - m_w)
        den_w = den_w + jnp.sum(e, axis=0, keepdims=True)
        o_win = o_win + _dot(vwt_ref[kb_read], e.astype(BF16))
    return o_win * (1.0 / den_w)


def _nsa_kernel(qraw_ref, qrot_ref, kc_ref, vct_ref, ks_ref, vst_ref, kw_ref, vwt_ref, gate_ref,
                o_ref, imp_ref, *, seq, tq, tk):
    qb = pl.program_id(1)
    t0 = qb * tq
    groups = range(NSA_KV_HEADS)
    n_cols = NSA_REP * tq

    def grp(g):
        return slice(g * LANES, (g + 1) * LANES)

    def q_cols(g):
        return slice(g * NSA_REP * LANES, (g + 1) * NSA_REP * LANES)

    o_cmp, q_rot = [], []
    for g in groups:
        q_raw = _q_aug(qraw_ref[:, q_cols(g)], jnp.zeros((HEAD_DIM, tq), F32))
        o_c, p_sum = _nsa_compressed(q_raw, kc_ref[:, grp(g)], vct_ref[...], t0, tq, seq)
        sel_neg = _nsa_selection_rows(p_sum, imp_ref.at[g], t0, tq, seq)
        o_cmp.append(o_c)
        q_rot.append(_q_aug(qrot_ref[:, q_cols(g)], sel_neg))

    blocks_per_tile = tk // LANES

    def key_tile(kt):
        r0 = pl.multiple_of(kt * tk, tk)
        v_t = jnp.concatenate([vst_ref[kt * blocks_per_tile + c] for c in range(blocks_per_tile)], axis=1)
        return [_dot(ks_ref[pl.ds(r0, tk), grp(g)], q_rot[g]) for g in groups], v_t

    def slc_step(kt, carries):
        scores, v_t = key_tile(kt)
        return tuple(_online_step(scores[g], v_t, carries[g]) for g in groups)

    n_full = t0 // tk
    init = (jnp.full((1, n_cols), NEG_INF, F32), jnp.zeros((1, n_cols), F32), jnp.zeros((LANES, n_cols), F32))
    carries = lax.fori_loop(0, n_full, slc_step, (init,) * NSA_KV_HEADS)
    scores, v_t = key_tile(n_full)
    causal = n_full * tk + _iota((tk, tq), 0) <= t0 + _iota((tk, tq), 1)
    causal_bias = _heads_tiled(jnp.where(causal, 0.0, NEG_INF))

    gate_all = gate_ref[...]
    low = _iota((tq, LANES), 1) < HEAD_DIM
    for g in groups:
        _, l_s, acc_s = _online_step(scores[g] + causal_bias, v_t, carries[g])
        o_slc = acc_s * (1.0 / l_s)
        o_win = _nsa_window(q_rot[g], kw_ref, vwt_ref, g, qb, tq)
        gate_t = jnp.transpose(gate_all[:, grp(g)])
        o_heads = []
        for r in range(NSA_REP):
            cols = slice(r * tq, (r + 1) * tq)
            o_t = (gate_t[3 * r:3 * r + 1] * o_cmp[g][:, cols] + gate_t[3 * r + 1:3 * r + 2] * o_slc[:, cols]
                   + gate_t[3 * r + 2:3 * r + 3] * o_win[:, cols])
            o_heads.append(jnp.transpose(o_t))
        for pair in range(NSA_REP // 2):
            a, b = o_heads[2 * pair], o_heads[2 * pair + 1]
            if g == 0:
                slab = jnp.where(low, a, pltpu.roll(b, HEAD_DIM, 1))
            else:
                slab = jnp.where(low, pltpu.roll(a, HEAD_DIM, 1), b)
            c0 = (g * NSA_REP // 2 + pair) * LANES
            o_ref[:, c0:c0 + LANES] = slab.astype(BF16)


def _nsa_call(qraw, qrot, kc, vct, ks, vst, kw, vwt, gate, *, layer, batch, seq, tq=Q_BLOCK, tk=512):
    assert tq == LANES and tk % tq == 0 and seq % tk == 0 and seq // SLC_LEN <= HEAD_DIM
    nq = seq // tq
    n_cp = seq // CMP_STRIDE
    grp_w = NSA_KV_HEADS * LANES

    def q_spec():
        return pl.BlockSpec((tq, NSA_HEADS * LANES), lambda b, i: (b * nq + i, 0))

    def k_spec(rows):
        return pl.BlockSpec((None, rows, grp_w), lambda b, i: (b, 0, 0))

    def vt_spec():
        return pl.BlockSpec((None, seq // LANES, NSA_KV_W, LANES), lambda b, i: (b, 0, 0, 0))

    return pl.pallas_call(
        functools.partial(_nsa_kernel, seq=seq, tq=tq, tk=tk),
        grid=(batch, nq),
        in_specs=[q_spec(), q_spec(), k_spec(n_cp),
                  pl.BlockSpec((None, NSA_KV_W, n_cp), lambda b, i: (b, 0, 0)),
                  k_spec(seq), vt_spec(), k_spec(seq), vt_spec(),
                  pl.BlockSpec((tq, grp_w), lambda b, i: (b * nq + i, 0))],
        out_specs=pl.BlockSpec((tq, NSA_W), lambda b, i: (b * nq + i, 0)),
        out_shape=jax.ShapeDtypeStruct((batch * seq, NSA_W), BF16),
        scratch_shapes=[pltpu.VMEM((NSA_KV_HEADS, LANES, tq), F32)],
        compiler_params=_params(2),
        name=f"nsa_attn_l{layer}",
    )(qraw, qrot, kc, vct, ks.reshape(batch, seq, grp_w), vst, kw.reshape(batch, seq, grp_w), vwt, gate)


LIN_BLOCK = 2 * CHUNK


def _lin_block_consts():
    r = _iota((LIN_BLOCK, LANES), 0)
    c = _iota((LIN_BLOCK, LANES), 1)
    same_chunk = (r >= CHUNK) == (c >= CHUNK)
    causal = same_chunk & (r >= c)
    head_diag = same_chunk
    first_rows = r < CHUNK
    low_lanes = c < HEAD_DIM
    return causal, head_diag, first_rows, low_lanes


def _head_stats(x, head_diag_mean):
    hi = x.astype(BF16)
    lo = (x - hi.astype(F32)).astype(BF16)
    return _dot(hi, head_diag_mean) + _dot(lo, head_diag_mean)


def _lin_core(q_dec, k_inv, k_dec, v, q_int, intra_scale, dec_a, dec_b, st_prev, consts):
    causal, head_diag, first_rows, low_lanes = consts
    zero = jnp.zeros_like(q_dec)
    q2 = jnp.concatenate([jnp.where(low_lanes, q_dec, zero), jnp.where(low_lanes, zero, q_dec)], axis=0)
    a2 = _dot_nt(q2.astype(BF16), k_inv.astype(BF16))
    mask2 = jnp.concatenate([causal, causal], axis=0)
    a2 = jnp.where(mask2, a2, 0.0)
    if intra_scale is not None:
        a2 = a2 * intra_scale
    o2 = _dot(a2.astype(BF16), v.astype(BF16))
    o_intra = jnp.where(low_lanes, o2[0:LIN_BLOCK], o2[LIN_BLOCK:])

    v_t = jnp.transpose(v).astype(BF16)
    kd_a = jnp.where(first_rows, k_dec, 0.0).astype(BF16)
    kd_b = jnp.where(first_rows, 0.0, k_dec).astype(BF16)
    st_a = st_prev * dec_a + jnp.where(head_diag, _dot(v_t, kd_a), 0.0)
    st_b = st_a * dec_b + jnp.where(head_diag, _dot(v_t, kd_b), 0.0)
    qi = q_int.astype(BF16)
    o_inter = jnp.where(first_rows, _dot_nt(qi, st_prev.astype(BF16)), _dot_nt(qi, st_a.astype(BF16)))
    return o_intra + o_inter, st_b


def _gla_kernel(x_ref, lr_ref, a2_ref, ab_ref, ng_ref, o_ref, st_ref, *, n_blocks, batch):
    @pl.when(pl.program_id(0) == 0)
    def _():
        st_ref[...] = jnp.zeros_like(st_ref)

    consts = _lin_block_consts()
    causal, head_diag, first_rows, low_lanes = consts
    tri = jnp.where(causal, 1.0, 0.0).astype(BF16)
    mean_op = jnp.where(head_diag, 1.0 / HEAD_DIM, 0.0).astype(BF16)
    a2h = a2_ref[...].astype(BF16)
    a2l = (a2_ref[...] - a2h.astype(F32)).astype(BF16)

    def block(i, carry):
        r0 = pl.multiple_of(i * LIN_BLOCK, LIN_BLOCK)
        for bi in range(batch):
            lr = lr_ref[bi, pl.ds(r0, LIN_BLOCK), :]
            lh = lr.astype(BF16)
            ll = (lr - lh.astype(F32)).astype(BF16)
            z = _dot(lh, a2h) + _dot(lh, a2l) + _dot(ll, a2h) + ab_ref[...]
            log_a = (jnp.minimum(z, 0.0) - jnp.log(1.0 + jnp.exp(-jnp.abs(z)))) * (1.0 / GLA_TAU)
            cum = _dot3_exact_lhs(tri, log_a)
            for s in range(GLA_W // LANES):
                cols = slice(s * LANES, (s + 1) * LANES)

                def ld(k):
                    return x_ref[bi, pl.ds(r0, LIN_BLOCK), k * GLA_W + s * LANES:k * GLA_W + (s + 1) * LANES]

                q, k, v, gg = ld(0), ld(1), ld(2), ld(3)
                b = cum[:, cols]
                bl_a = b[CHUNK - 1:CHUNK, :]
                bl_b = b[LIN_BLOCK - 1:LIN_BLOCK, :]
                bl = jnp.where(first_rows, bl_a, bl_b)
                q_dec = q * QK_SCALE * jnp.exp(b)
                o, st_new = _lin_core(q_dec, k * jnp.exp(-b), k * jnp.exp(bl - b), v, q_dec, None,
                                      jnp.exp(bl_a), jnp.exp(bl_b), st_ref[bi, s], consts)
                st_ref[bi, s] = st_new
                ms = _head_stats(o * o, mean_op)
                y = o * lax.rsqrt(ms + NORM_EPS) * ng_ref[:, cols] * _silu(gg)
                o_ref[bi, pl.ds(r0, LIN_BLOCK), cols] = y.astype(BF16)
        return carry

    lax.fori_loop(0, n_blocks, block, 0)


def _ret_kernel(x_ref, dmat_ref, qf_ref, kf_ref, cd_ref, ng_ref, o_ref, st_ref, *, n_blocks, batch):
    @pl.when(pl.program_id(0) == 0)
    def _():
        st_ref[...] = jnp.zeros_like(st_ref)

    consts = _lin_block_consts()
    causal, head_diag, first_rows, low_lanes = consts
    mean_op = jnp.where(head_diag, 1.0 / HEAD_DIM, 0.0).astype(BF16)

    def block(i, carry):
        r0 = pl.multiple_of(i * LIN_BLOCK, LIN_BLOCK)
        for bi in range(batch):
            for s in range(RET_W // LANES):
                cols = slice(s * LANES, (s + 1) * LANES)

                def ld(k):
                    return x_ref[bi, pl.ds(r0, LIN_BLOCK), k * RET_W + s * LANES:k * RET_W + (s + 1) * LANES]

                q, k, v, gg = ld(0), ld(1), ld(2), ld(3)
                cd = cd_ref[:, cols]
                o, st_new = _lin_core(q, k, k * kf_ref[:, cols], v, q * qf_ref[:, cols], dmat_ref[s],
                                      cd, cd, st_ref[bi, s], consts)
                st_ref[bi, s] = st_new
                mu = _head_stats(o, mean_op)
                oc = o - mu
                var = _head_stats(oc * oc, mean_op)
                y = oc * lax.rsqrt(var + NORM_EPS) * ng_ref[:, cols] * _silu(gg)
                o_ref[bi, pl.ds(r0, LIN_BLOCK), cols] = y.astype(BF16)
        return carry

    lax.fori_loop(0, n_blocks, block, 0)


def _gla_call(gla, lr, a2_pad, a_bias, norm_g, *, layer, batch, seq, tb=256):
    n_slab = GLA_W // LANES

    def lspec(shape):
        return pl.BlockSpec((None,) + shape, lambda t: (layer, 0, 0))

    return pl.pallas_call(
        functools.partial(_gla_kernel, n_blocks=tb // LIN_BLOCK, batch=batch),
        grid=(seq // tb,),
        in_specs=[pl.BlockSpec((batch, tb, 4 * GLA_W), lambda t: (0, t, 0)),
                  pl.BlockSpec((batch, tb, LANES), lambda t: (0, t, 0)),
                  lspec((LANES, GLA_W)), lspec((1, GLA_W)), lspec((1, GLA_W))],
        out_specs=pl.BlockSpec((batch, tb, GLA_W), lambda t: (0, t, 0)),
        out_shape=jax.ShapeDtypeStruct((batch, seq, GLA_W), BF16),
        scratch_shapes=[pltpu.VMEM((batch, n_slab, LANES, LANES), F32)],
        compiler_params=_params(1),
        name=f"gla_l{layer}",
    )(gla.reshape(batch, seq, 4 * GLA_W), lr.reshape(batch, seq, LANES), a2_pad, a_bias, norm_g
      ).reshape(batch * seq, GLA_W)


def _ret_call(ret, dmat, qf, kf, cd, norm_g, *, layer, batch, seq, tb=256):
    n_slab = RET_W // LANES
    full = lambda a: pl.BlockSpec(a.shape, lambda t: (0,) * a.ndim)
    return pl.pallas_call(
        functools.partial(_ret_kernel, n_blocks=tb // LIN_BLOCK, batch=batch),
        grid=(seq // tb,),
        in_specs=[pl.BlockSpec((batch, tb, 4 * RET_W), lambda t: (0, t, 0)),
                  full(dmat), full(qf), full(kf), full(cd),
                  pl.BlockSpec((None, 1, RET_W), lambda t: (layer, 0, 0))],
        out_specs=pl.BlockSpec((batch, tb, RET_W), lambda t: (0, t, 0)),
        out_shape=jax.ShapeDtypeStruct((batch, seq, RET_W), BF16),
        scratch_shapes=[pltpu.VMEM((batch, n_slab, LANES, LANES), F32)],
        compiler_params=_params(1),
        name=f"ret_l{layer}",
    )(ret.reshape(batch, seq, 4 * RET_W), dmat, qf, kf, cd, norm_g).reshape(batch * seq, RET_W)


def _out_kernel(x_ref, og_ref, on_ref, or_ref, gt_ref, w_ref, o_ref):
    y = (_dot(og_ref[...], w_ref[0:GLA_W, :])
         + _dot(on_ref[...], w_ref[GLA_W:GLA_W + NSA_W, :])
         + _dot(or_ref[...], w_ref[GLA_W + NSA_W:, :]))
    o_ref[...] = x_ref[...] + gt_ref[...] * y


def _out_call(x, o_gla, o_nsa, o_ret, mod, w_out, *, layer, seq, tm=1024):
    m, d = x.shape
    tiles_per_seq = seq // tm
    row = lambda w: pl.BlockSpec((tm, w), lambda i: (i, 0))
    return pl.pallas_call(
        _out_kernel,
        grid=(m // tm,),
        in_specs=[row(d), row(GLA_W), row(NSA_W), row(RET_W),
                  pl.BlockSpec((None, 1, d), lambda i: ((layer * 8 + i // tiles_per_seq) * N_ADA + 5, 0, 0)),
                  pl.BlockSpec((None, GLA_W + NSA_W + RET_W, d), lambda i: (layer, 0, 0))],
        out_specs=row(d),
        out_shape=jax.ShapeDtypeStruct((m, d), F32),
        compiler_params=_params(1),
        name=f"out_proj_l{layer}",
    )(x, o_gla, o_nsa, o_ret, mod, w_out)


def _layout_proj_weights(w_in, nsa_gate_bias):
    n_layers, d, _ = w_in.shape
    (g_q, g_k, g_v, g_g, g_lr, n_q, n_kc, n_vc, n_ks, n_vs, n_kw, n_vw, n_gate,
     r_q, r_k, r_v, r_g) = jnp.split(w_in, IN_SPLITS, axis=-1)
    zeros = lambda w: jnp.zeros((n_layers, d, w), w_in.dtype)
    def pad_heads(w, n_heads):
        w = w.reshape(n_layers, d, n_heads, HEAD_DIM)
        return jnp.concatenate([w, jnp.zeros_like(w)], axis=-1).reshape(n_layers, d, n_heads * LANES)

    per_grp = 3 * NSA_REP
    gate_cols = []
    bias_cols = []
    for g in range(NSA_KV_HEADS):
        gate_cols += [n_gate[..., g * per_grp:(g + 1) * per_grp], zeros(LANES - per_grp)]
        bias_cols += [nsa_gate_bias[:, g * per_grp:(g + 1) * per_grp],
                      jnp.zeros((n_layers, LANES - per_grp), nsa_gate_bias.dtype)]
    w = jnp.concatenate([g_q, g_k, g_v, g_g, g_lr, zeros(LANES - GLA_LOWRANK), pad_heads(n_q, NSA_HEADS),
                         n_kc, n_vc, pad_heads(n_ks, NSA_KV_HEADS), pad_heads(n_kw, NSA_KV_HEADS), n_vs, n_vw]
                        + gate_cols + [r_q, r_k, r_v, r_g], axis=-1)
    assert w.shape[-1] == PROJ_W
    bias = jnp.concatenate(bias_cols, axis=-1).reshape(n_layers, 1, NSA_KV_HEADS * LANES)
    return w.astype(BF16), bias


def _layout_cmp_weights(pe_k, pe_v, w1_k, w2_k, w1_v, w2_v):
    n_layers = pe_k.shape[0]
    eye = jnp.eye(2, dtype=F32)
    w1 = jnp.stack([w1_k, w1_v], axis=1).reshape(n_layers, 2, CMP_LEN, HEAD_DIM, CMP_HIDDEN)
    w1_full = jnp.einsum('zkldn,kq,gh->zlkgdqhn', w1, eye, eye)
    w1_full = w1_full.reshape(n_layers, CMP_LEN * 2 * NSA_KV_W, 4 * CMP_HIDDEN).astype(BF16)
    half = CMP_STRIDE * 2 * NSA_KV_W
    w2 = jnp.stack([w2_k, w2_v], axis=1)
    w2_full = jnp.einsum('zknd,kq,gh->zqhnkgd', w2, eye, eye)
    w2_full = w2_full.reshape(n_layers, 4 * CMP_HIDDEN, 2 * NSA_KV_W)
    w2_k = w2_full[..., :NSA_KV_W].reshape(n_layers, 4 * CMP_HIDDEN, NSA_KV_HEADS, HEAD_DIM)
    w2_k = jnp.concatenate([w2_k, jnp.zeros_like(w2_k)], axis=-1).reshape(n_layers, 4 * CMP_HIDDEN, -1)
    w2_full = jnp.concatenate([w2_k, w2_full[..., NSA_KV_W:]], axis=-1).astype(BF16)
    pe = jnp.stack([pe_k, pe_v], axis=1)
    pe = jnp.broadcast_to(pe.transpose(0, 2, 1, 3)[:, :, :, None, :],
                          (n_layers, CMP_LEN, 2, NSA_KV_HEADS, HEAD_DIM))
    pe = pe.reshape(n_layers, 1, CMP_LEN * 2 * NSA_KV_W)
    return pe[:, :, :half], pe[:, :, half:], w1_full[:, :half], w1_full[:, half:], w2_full


def _rotary_tables(seq):
    half = HEAD_DIM // 2
    inv_freq = ROPE_THETA ** (-jnp.arange(half, dtype=F32) / half)
    ang = jnp.arange(seq).astype(F32)[:, None] * inv_freq[None, :]
    cos, sin = jnp.cos(ang), jnp.sin(ang)
    reps = LANES // HEAD_DIM
    return jnp.tile(jnp.concatenate([cos, cos], axis=-1), (1, reps)), \
        jnp.tile(jnp.concatenate([-sin, sin], axis=-1), (1, reps))


def _retention_tables():
    log_gamma = jnp.log1p(-jnp.exp2(-5.0 - jnp.arange(RET_HEADS, dtype=F32)))
    lg_lane = jnp.repeat(log_gamma, HEAD_DIM)[None, :]
    pos = (jnp.arange(LIN_BLOCK) % CHUNK).astype(F32)[:, None]
    qf = jnp.exp(lg_lane * (pos + 1.0))
    kf = jnp.exp(lg_lane * (CHUNK - 1.0 - pos))
    cd = jnp.exp(lg_lane * CHUNK)
    r = jnp.arange(LIN_BLOCK)
    rel = (r[:, None] - r[None, :]).astype(F32)
    ok = ((r[:, None] // CHUNK) == (r[None, :] // CHUNK)) & (rel >= 0)
    dm = jnp.where(ok[None], jnp.exp(log_gamma[:, None, None] * rel[None]), 0.0)
    dmat = dm.reshape(RET_W // LANES, 2 * LIN_BLOCK, LIN_BLOCK)
    return dmat, qf, kf, cd


def kernel(x, c, w_ada, b_ada, norm_g, ffn1_in, ffn1_out, w_in, gla_a2, gla_a_bias, gla_norm_g,
           nsa_pe_k, nsa_pe_v, nsa_w1_k, nsa_w2_k, nsa_w1_v, nsa_w2_v, nsa_gate_bias, ret_norm_g,
           w_out, ffn2_in, ffn2_out, final_norm_g):
    batch, seq, d = x.shape
    n_layers = w_ada.shape[0]
    assert batch <= 8 and seq % 512 == 0 and seq >= WINDOW + Q_BLOCK

    c_pad = jnp.zeros((8, d), F32).at[:batch].set(c)
    mod = _ada_call(c_pad, w_ada, b_ada).reshape(n_layers * 8 * N_ADA, 1, d)
    ng = norm_g.reshape(n_layers * 3, 1, d)

    w_proj, gate_bias = _layout_proj_weights(w_in, nsa_gate_bias)
    pe_top, pe_bot, w1_top, w1_bot, w2_cmp = _layout_cmp_weights(nsa_pe_k, nsa_pe_v, nsa_w1_k, nsa_w2_k,
                                                                  nsa_w1_v, nsa_w2_v)
    cosf, sinf = _rotary_tables(seq)
    dmat, qf, kf, cd = _retention_tables()
    a2_pad = jnp.zeros((n_layers, LANES, GLA_W), F32).at[:, :GLA_LOWRANK].set(gla_a2)
    a_bias = gla_a_bias.reshape(n_layers, 1, GLA_W)
    gla_g = jnp.tile(gla_norm_g, (1, GLA_HEADS)).reshape(n_layers, 1, GLA_W)
    ret_g = jnp.tile(ret_norm_g, (1, RET_HEADS)).reshape(n_layers, 1, RET_W)
    f1_in, f1_out = ffn1_in.astype(BF16), ffn1_out.astype(BF16)
    f2_in, f2_out = ffn2_in.astype(BF16), ffn2_out.astype(BF16)
    w_o = w_out.astype(BF16)

    xs = x.reshape(batch * seq, d)
    for l in range(n_layers):
        xs = _ffn_call(xs, ng, mod, f1_in, f1_out, layer=l, sub=0, seq=seq)
        (gla, lr, qraw, qrot, kcvc, ks, vs, kw, vw, gate, ret) = _proj_call(
            xs, ng, mod, w_proj, gate_bias, cosf, sinf, layer=l, seq=seq)
        kc, vc = _cmp_call(kcvc, pe_top, pe_bot, w1_top, w1_bot, w2_cmp, layer=l, batch=batch, seq=seq)
        o_gla = _gla_call(gla, lr, a2_pad, a_bias, gla_g, layer=l, batch=batch, seq=seq)
        o_ret = _ret_call(ret, dmat, qf, kf, cd, ret_g, layer=l, batch=batch, seq=seq)
        o_nsa = _nsa_call(qraw, qrot, kc, vc, ks, vs, kw, vw, gate, layer=l, batch=batch, seq=seq)
        xs = _out_call(xs, o_gla, o_nsa, o_ret, mod, w_o, layer=l, seq=seq)
        final_g = final_norm_g.reshape(1, d) if l == n_layers - 1 else None
        xs = _ffn_call(xs, ng, mod, f2_in, f2_out, layer=l, sub=2, seq=seq, final_g=final_g)
    return xs.reshape(batch, seq, d)
```

```python
import functools
import math

import numpy as np
import jax
import jax.numpy as jnp
from jax import lax
from jax.experimental import pallas as pl
from jax.experimental.pallas import tpu as pltpu

F32 = jnp.float32
BF16 = jnp.bfloat16

HEAD_DIM = 64
LANES = 128
GLA_HEADS = 4
NSA_HEADS = 8
NSA_KV_HEADS = 2
NSA_REP = NSA_HEADS // NSA_KV_HEADS
RET_HEADS = 4
GLA_LOWRANK = 16
GLA_TAU = 16.0
CHUNK = 64
CMP_LEN = 32
CMP_STRIDE = 16
CMP_HIDDEN = 128
SLC_LEN = 64
SLC_TOPK = 16
WINDOW = 512
Q_BLOCK = 128
ROPE_THETA = 10000.0
FFN_HALF = 0.5
NORM_EPS = 1e-6
NEG_INF = -1e30
FORCED_SCORE = 1e4
N_ADA = 9
QK_SCALE = HEAD_DIM ** -0.5
SLC_SHIFT = int(math.log2(SLC_LEN))
LOG2_E = math.log2(math.e)

GLA_W = GLA_HEADS * HEAD_DIM
NSA_W = NSA_HEADS * HEAD_DIM
NSA_KV_W = NSA_KV_HEADS * HEAD_DIM
RET_W = RET_HEADS * HEAD_DIM
IN_SIZES = (GLA_W, GLA_W, GLA_W, GLA_W, GLA_LOWRANK,
            NSA_W, NSA_KV_W, NSA_KV_W, NSA_KV_W, NSA_KV_W, NSA_KV_W, NSA_KV_W, 3 * NSA_HEADS,
            RET_W, RET_W, RET_W, RET_W)
IN_SPLITS = tuple(int(s) for s in np.cumsum(IN_SIZES)[:-1])

SEG_GLA = 0
SEG_LR = SEG_GLA + 4 * GLA_W
SEG_NQ = SEG_LR + LANES
SEG_KCVC = SEG_NQ + NSA_HEADS * LANES
SEG_KS = SEG_KCVC + 2 * NSA_KV_W
SEG_KW = SEG_KS + NSA_KV_HEADS * LANES
SEG_VS = SEG_KW + NSA_KV_HEADS * LANES
SEG_VW = SEG_VS + NSA_KV_W
SEG_GATE = SEG_VW + NSA_KV_W
SEG_RET = SEG_GATE + NSA_KV_HEADS * LANES
PROJ_W = SEG_RET + 4 * RET_W

VMEM_LIMIT_BYTES = 56 * 1024 * 1024

NT_DIMS = (((1,), (1,)), ((), ()))


def _dot(a, b):
    return jnp.dot(a, b, preferred_element_type=F32)


def _dot_nt(a, b):
    return lax.dot_general(a, b, NT_DIMS, preferred_element_type=F32)


def _split3(x):
    hi = x.astype(BF16)
    r1 = x - hi.astype(F32)
    mid = r1.astype(BF16)
    lo = (r1 - mid.astype(F32)).astype(BF16)
    return hi, mid, lo


def _dot3_exact_rhs(x, w_bf16):
    hi, mid, lo = _split3(x)
    return _dot(hi, w_bf16) + _dot(mid, w_bf16) + _dot(lo, w_bf16)


def _dot3_exact_lhs(w_bf16, x):
    hi, mid, lo = _split3(x)
    return _dot(w_bf16, hi) + _dot(w_bf16, mid) + _dot(w_bf16, lo)


def _sigmoid(x):
    return 1.0 / (1.0 + jnp.exp(-x))


def _silu(x):
    return x * _sigmoid(x)


def _iota(shape, dim):
    return lax.broadcasted_iota(jnp.int32, shape, dim)


def _params(n_grid):
    return pltpu.CompilerParams(dimension_semantics=("arbitrary",) * n_grid,
                                vmem_limit_bytes=VMEM_LIMIT_BYTES)


def _norm_mod(x, g, sh, sc):
    ms = jnp.mean(x * x, axis=-1, keepdims=True)
    return x * lax.rsqrt(ms + NORM_EPS) * (g * (1.0 + sc)) + sh


def _ada_kernel(c_ref, w_ref, b_ref, o_ref):
    c = c_ref[...]
    o_ref[...] = _dot(_silu(c).astype(BF16), w_ref[...].astype(BF16)) + b_ref[...]


def _ada_call(c_pad, w_ada, b_ada):
    n_layers, d, n_out = w_ada.shape
    tn = 2304 if n_out % 2304 == 0 else n_out
    return pl.pallas_call(
        _ada_kernel,
        grid=(n_layers, n_out // tn),
        in_specs=[pl.BlockSpec((8, d), lambda l, j: (0, 0)),
                  pl.BlockSpec((None, d, tn), lambda l, j: (l, 0, j)),
                  pl.BlockSpec((None, 1, tn), lambda l, j: (l, 0, j))],
        out_specs=pl.BlockSpec((None, 8, tn), lambda l, j: (l, 0, j)),
        out_shape=jax.ShapeDtypeStruct((n_layers, 8, n_out), F32),
        compiler_params=_params(2),
        name="ada_mod",
    )(c_pad, w_ada, b_ada.reshape(n_layers, 1, n_out))


def _ffn_kernel(x_ref, g_ref, sh_ref, sc_ref, gt_ref, win_ref, wout_ref, *rest, d_ff, chunks, final):
    if final:
        fg_ref, o_ref, act_ref = rest
    else:
        o_ref, act_ref = rest
    x = x_ref[...]
    h = _norm_mod(x, g_ref[...], sh_ref[...], sc_ref[...]).astype(BF16)
    for (c0, cw) in chunks:
        gate = _dot(h, win_ref[:, c0:c0 + cw])
        up = _dot(h, win_ref[:, d_ff + c0:d_ff + c0 + cw])
        act_ref[:, c0:c0 + cw] = (_silu(gate) * up).astype(BF16)
    y = _dot(act_ref[...], wout_ref[...])
    xn = x + (FFN_HALF * gt_ref[...]) * y
    if final:
        ms = jnp.mean(xn * xn, axis=-1, keepdims=True)
        xn = xn * lax.rsqrt(ms + NORM_EPS) * fg_ref[...]
    o_ref[...] = xn


def _ffn_call(x, ng, mod, w_in, w_out, *, layer, sub, seq, final_g=None, tm=512):
    m, d = x.shape
    d_ff = w_out.shape[1]
    tiles_per_seq = seq // tm
    chunks, c0 = [], 0
    while c0 < d_ff:
        cw = min(512, d_ff - c0)
        chunks.append((c0, cw))
        c0 += cw

    def mod_spec(k):
        return pl.BlockSpec((None, 1, d), lambda i: ((layer * 8 + i // tiles_per_seq) * N_ADA + k, 0, 0))

    in_specs = [pl.BlockSpec((tm, d), lambda i: (i, 0)),
                pl.BlockSpec((None, 1, d), lambda i: (layer * 3 + sub, 0, 0)),
                mod_spec(3 * sub), mod_spec(3 * sub + 1), mod_spec(3 * sub + 2),
                pl.BlockSpec((None, d, 2 * d_ff), lambda i: (layer, 0, 0), pipeline_mode=pl.Buffered(1)),
                pl.BlockSpec((None, d_ff, d), lambda i: (layer, 0, 0), pipeline_mode=pl.Buffered(1))]
    args = [x, ng, mod, mod, mod, w_in, w_out]
    if final_g is not None:
        in_specs.append(pl.BlockSpec((1, d), lambda i: (0, 0)))
        args.append(final_g)
    return pl.pallas_call(
        functools.partial(_ffn_kernel, d_ff=d_ff, chunks=tuple(chunks), final=final_g is not None),
        grid=(m // tm,),
        in_specs=in_specs,
        out_specs=pl.BlockSpec((tm, d), lambda i: (i, 0)),
        out_shape=jax.ShapeDtypeStruct((m, d), F32),
        scratch_shapes=[pltpu.VMEM((tm, d_ff), BF16)],
        compiler_params=_params(1),
        name=f"ffn_l{layer}_s{sub}",
    )(*args)


def _rotate(x, cosf, sinf):
    rows, width = x.shape
    first = (_iota((rows, LANES), 1) & (HEAD_DIM - 1)) < HEAD_DIM // 2
    outs = []
    for j in range(width // LANES):
        xs = x[:, j * LANES:(j + 1) * LANES]
        swapped = jnp.where(first, pltpu.roll(xs, LANES - HEAD_DIM // 2, 1), pltpu.roll(xs, HEAD_DIM // 2, 1))
        outs.append(xs * cosf + swapped * sinf)
    return outs[0] if len(outs) == 1 else jnp.concatenate(outs, axis=1)


def _store_group_transposed(dst_ref, x, fill):
    top = _iota((LANES, LANES), 0) < HEAD_DIM
    for c in range(x.shape[0] // LANES):
        piece = jnp.transpose(x[c * LANES:(c + 1) * LANES])
        dst_ref[0, c] = jnp.where(top, piece, fill).astype(dst_ref.dtype)
        dst_ref[1, c] = jnp.where(top, pltpu.roll(piece, HEAD_DIM, 0), fill).astype(dst_ref.dtype)


def _proj_kernel(x_ref, g_ref, sh_ref, sc_ref, w_ref, gb_ref, cos_ref, sin_ref,
                 gla_ref, lr_ref, qraw_ref, qrot_ref, kcvc_ref, ks_ref, vst_ref, kw_ref, vwt_ref,
                 gate_ref, ret_ref, *, tiles_per_seq):
    h = _norm_mod(x_ref[...], g_ref[...], sh_ref[...], sc_ref[...]).astype(BF16)
    cosf = cos_ref[...]
    sinf = sin_ref[...]
    tm = h.shape[0]

    def seg(c0, width):
        return _dot(h, w_ref[:, c0:c0 + width])

    gla_ref[...] = seg(SEG_GLA, 4 * GLA_W)
    lr_ref[...] = seg(SEG_LR, LANES)
    nq = seg(SEG_NQ, NSA_HEADS * LANES)
    qraw_ref[...] = (nq * (QK_SCALE * LOG2_E)).astype(BF16)
    qrot_ref[...] = (_rotate(nq, cosf, sinf) * (QK_SCALE * LOG2_E)).astype(BF16)
    kcvc_ref[...] = seg(SEG_KCVC, 2 * NSA_KV_W)
    grp_w = NSA_KV_HEADS * LANES
    pos = (pl.program_id(0) % tiles_per_seq) * tm + _iota((tm, grp_w), 0)
    spare = (_iota((tm, grp_w), 1) & (LANES - 1)) - HEAD_DIM
    blk_onehot = jnp.where(spare == (pos >> SLC_SHIFT), 1.0, 0.0)
    ks_ref[...] = (_rotate(seg(SEG_KS, grp_w), cosf, sinf) + blk_onehot).astype(BF16)
    kw_ref[...] = _rotate(seg(SEG_KW, grp_w), cosf, sinf).astype(BF16)
    _store_group_transposed(vst_ref, seg(SEG_VS, NSA_KV_W), 1.0)
    _store_group_transposed(vwt_ref, seg(SEG_VW, NSA_KV_W), 1.0)
    gate_ref[...] = _sigmoid(seg(SEG_GATE, NSA_KV_HEADS * LANES) + gb_ref[...])
    ret_ref[:, 0:RET_W] = _rotate(seg(SEG_RET, RET_W), cosf, sinf)
    ret_ref[:, RET_W:2 * RET_W] = _rotate(seg(SEG_RET + RET_W, RET_W), cosf, sinf) * QK_SCALE
    ret_ref[:, 2 * RET_W:4 * RET_W] = seg(SEG_RET + 2 * RET_W, 2 * RET_W)


def _proj_call(x, ng, mod, w_proj, gate_bias, cosf, sinf, *, layer, seq, tm=512):
    m, d = x.shape
    tiles_per_seq = seq // tm

    def mod_spec(k):
        return pl.BlockSpec((None, 1, d), lambda i: ((layer * 8 + i // tiles_per_seq) * N_ADA + k, 0, 0))

    def row_spec(width):
        return pl.BlockSpec((tm, width), lambda i: (i, 0))

    batch = m // seq
    grp_w = NSA_KV_HEADS * LANES

    def vt_spec():
        return pl.BlockSpec((None, NSA_KV_HEADS, tm // LANES, LANES, LANES),
                            lambda i: (i // tiles_per_seq, 0, i % tiles_per_seq, 0, 0))

    vt_shape = jax.ShapeDtypeStruct((batch, NSA_KV_HEADS, seq // LANES, LANES, LANES), BF16)
    out_widths = [(4 * GLA_W, F32), (LANES, F32), (NSA_HEADS * LANES, BF16), (NSA_HEADS * LANES, BF16),
                  (2 * NSA_KV_W, F32), (grp_w, BF16), None, (grp_w, BF16), None,
                  (NSA_KV_HEADS * LANES, F32), (4 * RET_W, F32)]
    return pl.pallas_call(
        functools.partial(_proj_kernel, tiles_per_seq=tiles_per_seq),
        grid=(m // tm,),
        in_specs=[row_spec(d),
                  pl.BlockSpec((None, 1, d), lambda i: (layer * 3 + 1, 0, 0)),
                  mod_spec(3), mod_spec(4),
                  pl.BlockSpec((None, d, PROJ_W), lambda i: (layer, 0, 0), pipeline_mode=pl.Buffered(1)),
                  pl.BlockSpec((None, 1, NSA_KV_HEADS * LANES), lambda i: (layer, 0, 0)),
                  pl.BlockSpec((tm, LANES), lambda i: (i % tiles_per_seq, 0)),
                  pl.BlockSpec((tm, LANES), lambda i: (i % tiles_per_seq, 0))],
        out_specs=[vt_spec() if ow is None else row_spec(ow[0]) for ow in out_widths],
        out_shape=[vt_shape if ow is None else jax.ShapeDtypeStruct((m, ow[0]), ow[1]) for ow in out_widths],
        compiler_params=_params(1),
        name=f"proj_l{layer}",
    )(x, ng, mod, mod, w_proj, gate_bias, cosf, sinf)


def _cmp_kernel(x_ref, pet_ref, peb_ref, wt_ref, wb_ref, w2_ref, kc_ref, vct_ref):
    x = x_ref[...]
    n_rows = x.shape[0]
    a = _dot((x + pet_ref[...]).astype(BF16), wt_ref[...])
    b = _dot((x + peb_ref[...]).astype(BF16), wb_ref[...])
    hid = a + pltpu.roll(b, n_rows - 1, 0)
    out = _dot(_silu(hid).astype(BF16), w2_ref[...])
    out = jnp.where(_iota(out.shape, 0) < n_rows - 1, out, 0.0)
    grp_w = NSA_KV_HEADS * LANES
    kc_ref[...] = out[:, 0:grp_w].astype(BF16)
    _store_group_transposed(vct_ref, out[:, grp_w:grp_w + NSA_KV_W], 0.0)


def _cmp_call(kcvc, pe_top, pe_bot, w_top, w_bot, w2, *, layer, batch, seq):
    rows = seq // CMP_STRIDE
    width = CMP_STRIDE * 2 * NSA_KV_W
    x = kcvc.reshape(batch, rows, width)
    hid_w = 4 * CMP_HIDDEN
    grp_w = NSA_KV_HEADS * LANES

    def wspec(shape):
        return pl.BlockSpec((None,) + shape, lambda b: (layer, 0, 0))

    return pl.pallas_call(
        _cmp_kernel,
        grid=(batch,),
        in_specs=[pl.BlockSpec((None, rows, width), lambda b: (b, 0, 0)),
                  wspec((1, width)), wspec((1, width)),
                  wspec((width, hid_w)), wspec((width, hid_w)), wspec((hid_w, grp_w + NSA_KV_W))],
        out_specs=[pl.BlockSpec((None, rows, grp_w), lambda b: (b, 0, 0)),
                   pl.BlockSpec((None, NSA_KV_HEADS, rows // LANES, LANES, LANES), lambda b: (b, 0, 0, 0, 0))],
        out_shape=[jax.ShapeDtypeStruct((batch, rows, grp_w), BF16),
                   jax.ShapeDtypeStruct((batch, NSA_KV_HEADS, rows // LANES, LANES, LANES), BF16)],
        compiler_params=_params(1),
        name=f"nsa_compress_l{layer}",
    )(x, pe_top, pe_bot, w_top, w_bot, w2)


def _heads_tiled(x):
    return jnp.concatenate([x] * NSA_REP, axis=1)


SLC_UNROLL = 2


def _q_transposed(x):
    x = x.astype(F32)
    return jnp.concatenate([jnp.transpose(x[:, r * LANES:(r + 1) * LANES])[0:HEAD_DIM, :] for r in range(NSA_REP)],
                           axis=1)


def _nsa_compressed(q_raw, kc, vc_t, t0, tq, seq):
    n_cp = seq // CMP_STRIDE
    n_cmp = (seq - CMP_LEN) // CMP_STRIDE + 1
    s_c = _dot(kc, q_raw)
    nn = _iota((n_cp, tq), 0)
    tt = t0 + _iota((n_cp, tq), 1)
    bias_c = jnp.where((nn * CMP_STRIDE + (CMP_LEN - 1) <= tt) & (nn < n_cmp), 0.0, NEG_INF)
    has_cmp = t0 + _iota((1, tq), 1) >= CMP_LEN - 1
    p_heads = []
    for r in range(NSA_REP):
        s = s_c[:, r * tq:(r + 1) * tq] + bias_c
        e = jnp.exp2(s - jnp.max(s, axis=0, keepdims=True))
        p_heads.append(e * jnp.where(has_cmp, 1.0 / jnp.sum(e, axis=0, keepdims=True), 0.0))
    o_cmp = _dot(vc_t, jnp.concatenate(p_heads, axis=1).astype(BF16))
    p_sum = p_heads[0]
    for r in range(1, NSA_REP):
        p_sum = p_sum + p_heads[r]
    return o_cmp, p_sum


def _nsa_selection_rows(p_sum, imp_ref, t0, tq, seq):
    n_cp = seq // CMP_STRIDE
    n_cmp = (seq - CMP_LEN) // CMP_STRIDE + 1
    n_slc = seq // SLC_LEN
    n_sel = min(SLC_TOPK, n_slc)
    sb = _iota((LANES, n_cp), 0)
    cb = _iota((LANES, n_cp), 1)
    overlap_t = ((cb * CMP_STRIDE < sb * SLC_LEN + SLC_LEN) & (cb * CMP_STRIDE + CMP_LEN > sb * SLC_LEN)
                 & (sb < n_slc) & (cb < n_cmp))
    imp_t = _dot3_exact_lhs(jnp.where(overlap_t, 1.0, 0.0).astype(BF16), p_sum)
    blk = _iota((LANES, tq), 0)
    cur = (t0 + _iota((LANES, tq), 1)) >> SLC_SHIFT
    forced = (blk == 0) | (blk == cur) | (blk == cur - 1)
    imp_ref[...] = jnp.where(forced, FORCED_SCORE, jnp.where(blk <= cur, imp_t, -1.0))

    n_grp = n_slc // 8
    vals = [imp_ref[8 * j:8 * (j + 1), :] for j in range(n_grp)]
    ranks = [jnp.zeros((8, tq), F32) for _ in range(n_grp)]
    sub = _iota((8, tq), 0)
    for i in range(n_slc):
        row = jnp.broadcast_to(imp_ref[i:i + 1, :], (8, tq))
        for j in range(n_grp):
            if 8 * j > i:
                beats = jnp.where(row >= vals[j], 1.0, 0.0)
            elif 8 * j + 7 <= i:
                beats = jnp.where(row > vals[j], 1.0, 0.0)
            else:
                beats = jnp.where(sub + 8 * j > i, jnp.where(row >= vals[j], 1.0, 0.0),
                                  jnp.where(row > vals[j], 1.0, 0.0))
            ranks[j] = ranks[j] + beats
    return jnp.concatenate([jnp.where(rk < n_sel, 0.0, NEG_INF) for rk in ranks]
                           + [jnp.zeros((HEAD_DIM - n_slc, tq), F32)] * (n_slc < HEAD_DIM), axis=0)


def _nsa_window(q_rot, kw_ref, vwt_ref, g, qb, tq):
    n_wb = WINDOW // tq + 1
    ki = _iota((tq, tq), 0)
    ci = _iota((tq, tq), 1)
    blocks = []
    for j in range(n_wb):
        kb = qb - (n_wb - 1) + j
        kb_read = jnp.maximum(kb, 0)
        s = _dot(kw_ref[pl.ds(pl.multiple_of(kb_read * tq, tq), tq), g * LANES:(g + 1) * LANES], q_rot)
        if j == 0:
            s = s + _heads_tiled(jnp.where(ki > ci, 0.0, NEG_INF))
        if j == n_wb - 1:
            s = s + _heads_tiled(jnp.where(ki <= ci, 0.0, NEG_INF))
        else:
            s = s + jnp.where(kb >= 0, 0.0, NEG_INF)
        blocks.append((s, kb_read))
    m_w = blocks[0][0].max(axis=0, keepdims=True)
    for s, _ in blocks[1:]:
        m_w = jnp.maximum(m_w, s.max(axis=0, keepdims=True))
    o_aug = jnp.zeros((LANES, NSA_REP * tq), F32)
    for s, kb_read in blocks:
        o_aug = o_aug + _dot(vwt_ref[g, kb_read], jnp.exp2(s - m_w).astype(BF16))
    return o_aug[0:HEAD_DIM] * (1.0 / o_aug[HEAD_DIM:HEAD_DIM + 1])


def _nsa_kernel(qraw_ref, qrot_ref, kc_ref, vct_ref, ks_ref, vst_ref, kw_ref, vwt_ref, gate_ref,
                o_ref, imp_ref, *, seq, tq, tk):
    qb = pl.program_id(1)
    t0 = qb * tq
    groups = range(NSA_KV_HEADS)
    n_cols = NSA_REP * tq

    def grp(g):
        return slice(g * LANES, (g + 1) * LANES)

    def q_cols(g):
        return slice(g * NSA_REP * LANES, (g + 1) * NSA_REP * LANES)

    def q_aug(q_t, extra_rows):
        return jnp.concatenate([q_t, extra_rows], axis=0).astype(BF16)

    o_cmp, q_sel = [], []
    for g in groups:
        q_raw = q_aug(_q_transposed(qraw_ref[:, q_cols(g)]), jnp.zeros((HEAD_DIM, n_cols), F32))
        vc_t = jnp.concatenate([vct_ref[g, c] for c in range(seq // CMP_STRIDE // LANES)], axis=1)
        o_c, p_sum = _nsa_compressed(q_raw, kc_ref[:, grp(g)], vc_t, t0, tq, seq)
        o_cmp.append(o_c[0:HEAD_DIM])
        sel_neg = _heads_tiled(_nsa_selection_rows(p_sum, imp_ref.at[g], t0, tq, seq))
        q_sel.append(q_aug(_q_transposed(qrot_ref[:, q_cols(g)]), sel_neg))

    blocks_per_tile = tk // LANES
    n_full = t0 // tk
    n_iter = (n_full + SLC_UNROLL - 1) // SLC_UNROLL

    def k_tile(kt, g):
        return ks_ref[pl.ds(pl.multiple_of(kt * tk, tk), tk), grp(g)]

    def v_tile(kt, g):
        return jnp.concatenate([vst_ref[g, kt * blocks_per_tile + c] for c in range(blocks_per_tile)], axis=1)

    def online_update(scores, values, m_i, acc):
        m_new = m_i
        for s in scores:
            m_new = jnp.maximum(m_new, s.max(axis=0, keepdims=True))
        p = jnp.concatenate([jnp.exp2(s - m_new).astype(BF16) for s in scores], axis=0)
        return m_new, jnp.exp2(m_i - m_new) * acc + _dot(jnp.concatenate(values, axis=1), p)

    def slc_step(j, stats):
        out = []
        for g in groups:
            scores, values = [], []
            for u in range(SLC_UNROLL):
                kt_raw = j * SLC_UNROLL + u
                kt = jnp.minimum(kt_raw, n_full - 1)
                v_t = v_tile(kt, g)
                values.append(jnp.where(kt_raw < n_full, v_t, jnp.zeros_like(v_t)))
                scores.append(_dot(k_tile(kt, g), q_sel[g]))
            out.append(online_update(scores, values, *stats[g]))
        return tuple(out)

    causal = n_full * tk + _iota((tk, tq), 0) <= t0 + _iota((tk, tq), 1)
    causal_bias = _heads_tiled(jnp.where(causal, 0.0, NEG_INF))
    init = tuple(online_update([_dot(k_tile(n_full, g), q_sel[g]) + causal_bias], [v_tile(n_full, g)],
                               jnp.full((1, n_cols), NEG_INF, F32), jnp.zeros((LANES, n_cols), F32))
                 for g in groups)
    stats = lax.fori_loop(0, n_iter, slc_step, init)

    gate_all = gate_ref[...]
    for g in groups:
        acc = stats[g][1]
        o_slc = acc[0:HEAD_DIM] * (1.0 / acc[HEAD_DIM:HEAD_DIM + 1])
        o_win = _nsa_window(q_sel[g], kw_ref, vwt_ref, g, qb, tq)
        gate_t = jnp.transpose(gate_all[:, grp(g)])
        o_heads = []
        for r in range(NSA_REP):
            cols = slice(r * tq, (r + 1) * tq)
            o_heads.append(gate_t[3 * r:3 * r + 1] * o_cmp[g][:, cols] + gate_t[3 * r + 1:3 * r + 2] * o_slc[:, cols]
                           + gate_t[3 * r + 2:3 * r + 3] * o_win[:, cols])
        for pair in range(NSA_REP // 2):
            slab = jnp.transpose(jnp.concatenate(o_heads[2 * pair:2 * pair + 2], axis=0))
            c0 = (g * NSA_REP // 2 + pair) * LANES
            o_ref[:, c0:c0 + LANES] = slab.astype(BF16)


def _nsa_call(qraw, qrot, kc, vct, ks, vst, kw, vwt, gate, *, layer, batch, seq, tq=Q_BLOCK, tk=512):
    assert tq == LANES and tk % tq == 0 and seq % tk == 0 and seq // SLC_LEN <= HEAD_DIM
    nq = seq // tq
    n_cp = seq // CMP_STRIDE
    grp_w = NSA_KV_HEADS * LANES

    def q_spec():
        return pl.BlockSpec((tq, NSA_HEADS * LANES), lambda b, i: (b * nq + i, 0))

    def k_spec(rows):
        return pl.BlockSpec((None, rows, grp_w), lambda b, i: (b, 0, 0))

    def vt_spec():
        return pl.BlockSpec((None, NSA_KV_HEADS, seq // LANES, LANES, LANES), lambda b, i: (b, 0, 0, 0, 0))

    return pl.pallas_call(
        functools.partial(_nsa_kernel, seq=seq, tq=tq, tk=tk),
        grid=(batch, nq),
        in_specs=[q_spec(), q_spec(), k_spec(n_cp),
                  pl.BlockSpec((None, NSA_KV_HEADS, n_cp // LANES, LANES, LANES), lambda b, i: (b, 0, 0, 0, 0)),
                  k_spec(seq), vt_spec(), k_spec(seq), vt_spec(),
                  pl.BlockSpec((tq, grp_w), lambda b, i: (b * nq + i, 0))],
        out_specs=pl.BlockSpec((tq, NSA_W), lambda b, i: (b * nq + i, 0)),
        out_shape=jax.ShapeDtypeStruct((batch * seq, NSA_W), BF16),
        scratch_shapes=[pltpu.VMEM((NSA_KV_HEADS, LANES, tq), F32)],
        compiler_params=_params(2),
        name=f"nsa_attn_l{layer}",
    )(qraw, qrot, kc, vct, ks.reshape(batch, seq, grp_w), vst, kw.reshape(batch, seq, grp_w), vwt, gate)


LIN_BLOCK = 2 * CHUNK


def _lin_block_consts():
    r = _iota((LIN_BLOCK, LANES), 0)
    c = _iota((LIN_BLOCK, LANES), 1)
    same_chunk = (r >= CHUNK) == (c >= CHUNK)
    causal = same_chunk & (r >= c)
    head_diag = same_chunk
    first_rows = r < CHUNK
    low_lanes = c < HEAD_DIM
    return causal, head_diag, first_rows, low_lanes


def _head_stats(x, head_diag_mean):
    hi = x.astype(BF16)
    lo = (x - hi.astype(F32)).astype(BF16)
    return _dot(hi, head_diag_mean) + _dot(lo, head_diag_mean)


def _lin_core(q_dec, k_inv, k_dec, v, q_int, intra_scale, dec_a, dec_b, st_prev, consts):
    causal, head_diag, first_rows, low_lanes = consts
    zero = jnp.zeros_like(q_dec)
    q2 = jnp.concatenate([jnp.where(low_lanes, q_dec, zero), jnp.where(low_lanes, zero, q_dec)], axis=0)
    a2 = _dot_nt(q2.astype(BF16), k_inv.astype(BF16))
    mask2 = jnp.concatenate([causal, causal], axis=0)
    a2 = jnp.where(mask2, a2, 0.0)
    if intra_scale is not None:
        a2 = a2 * intra_scale
    o2 = _dot(a2.astype(BF16), v.astype(BF16))
    o_intra = jnp.where(low_lanes, o2[0:LIN_BLOCK], o2[LIN_BLOCK:])

    v_t = jnp.transpose(v).astype(BF16)
    kd_ab = jnp.concatenate([jnp.where(first_rows, k_dec, 0.0), jnp.where(first_rows, 0.0, k_dec)], axis=1)
    upd = _dot(v_t, kd_ab.astype(BF16))
    st_a = st_prev * dec_a + jnp.where(head_diag, upd[:, 0:LANES], 0.0)
    st_b = st_a * dec_b + jnp.where(head_diag, upd[:, LANES:], 0.0)
    o_ab = _dot_nt(q_int.astype(BF16), jnp.concatenate([st_prev, st_a], axis=0).astype(BF16))
    o_inter = jnp.where(first_rows, o_ab[:, 0:LANES], o_ab[:, LANES:])
    return o_intra + o_inter, st_b


def _gla_kernel(x_ref, lr_ref, a2_ref, ab_ref, ng_ref, o_ref, st_ref, *, n_blocks, batch):
    @pl.when(pl.program_id(0) == 0)
    def _():
        st_ref[...] = jnp.zeros_like(st_ref)

    consts = _lin_block_consts()
    causal, head_diag, first_rows, low_lanes = consts
    ri = _iota((LIN_BLOCK, LANES), 0)
    ci = _iota((LIN_BLOCK, LANES), 1)
    tri_t = jnp.where(((ri >= CHUNK) == (ci >= CHUNK)) & (ri <= ci), 1.0, 0.0).astype(BF16)
    mean_op = jnp.where(head_diag, 1.0 / HEAD_DIM, 0.0).astype(BF16)
    a2h = a2_ref[...].astype(BF16)
    a2l = (a2_ref[...] - a2h.astype(F32)).astype(BF16)

    def block(i):
        rows = pl.ds(i * LIN_BLOCK, LIN_BLOCK)
        lr = jnp.concatenate([lr_ref[bi, rows, :] for bi in range(batch)], axis=0)
        lh = lr.astype(BF16)
        ll = (lr - lh.astype(F32)).astype(BF16)
        z = _dot(lh, a2h) + _dot(lh, a2l) + _dot(ll, a2h) + ab_ref[...]
        log_a = (jnp.minimum(z, 0.0) - jnp.log(1.0 + jnp.exp(-jnp.abs(z)))) * (1.0 / GLA_TAU)
        n_slab = GLA_W // LANES
        la_t = jnp.concatenate([jnp.transpose(log_a[bi * LIN_BLOCK:(bi + 1) * LIN_BLOCK, s * LANES:(s + 1) * LANES])
                                for bi in range(batch) for s in range(n_slab)], axis=0)
        cum_t = _dot3_exact_rhs(la_t, tri_t)
        chains = []
        for bi in range(batch):
            for s in range(n_slab):
                cols = slice(s * LANES, (s + 1) * LANES)
                n = bi * n_slab + s

                def ld(k):
                    return x_ref[bi, rows, k * GLA_W + s * LANES:k * GLA_W + (s + 1) * LANES]

                q, k, v, gg = ld(0), ld(1), ld(2), ld(3)
                b = jnp.transpose(cum_t[n * LANES:(n + 1) * LANES])
                bl_a = b[CHUNK - 1:CHUNK, :]
                bl_b = b[LIN_BLOCK - 1:LIN_BLOCK, :]
                bl = jnp.where(first_rows, bl_a, bl_b)
                q_dec = q * QK_SCALE * jnp.exp(b)
                o, st_new = _lin_core(q_dec, k * jnp.exp(-b), k * jnp.exp(bl - b), v, q_dec, None,
                                      jnp.exp(bl_a), jnp.exp(bl_b), st_ref[bi, s], consts)
                st_ref[bi, s] = st_new
                chains.append((bi, cols, o, gg))
        ms = _head_stats(jnp.concatenate([o * o for _, _, o, _ in chains], axis=0), mean_op)
        for n, (bi, cols, o, gg) in enumerate(chains):
            y = o * lax.rsqrt(ms[n * LIN_BLOCK:(n + 1) * LIN_BLOCK] + NORM_EPS) * ng_ref[:, cols] * _silu(gg)
            o_ref[bi, rows, cols] = y.astype(BF16)

    for i in range(n_blocks):
        block(i)


def _ret_kernel(x_ref, dmat_ref, qf_ref, kf_ref, cd_ref, ng_ref, o_ref, st_ref, *, n_blocks, batch):
    @pl.when(pl.program_id(0) == 0)
    def _():
        st_ref[...] = jnp.zeros_like(st_ref)

    consts = _lin_block_consts()
    causal, head_diag, first_rows, low_lanes = consts
    mean_op = jnp.where(head_diag, 1.0 / HEAD_DIM, 0.0).astype(BF16)

    def block(i):
        rows = pl.ds(i * LIN_BLOCK, LIN_BLOCK)
        chains = []
        for bi in range(batch):
            for s in range(RET_W // LANES):
                cols = slice(s * LANES, (s + 1) * LANES)

                def ld(k):
                    return x_ref[bi, rows, k * RET_W + s * LANES:k * RET_W + (s + 1) * LANES]

                q, k, v, gg = ld(0), ld(1), ld(2), ld(3)
                cd = cd_ref[:, cols]
                o, st_new = _lin_core(q, k, k * kf_ref[:, cols], v, q * qf_ref[:, cols], dmat_ref[s],
                                      cd, cd, st_ref[bi, s], consts)
                st_ref[bi, s] = st_new
                chains.append((bi, cols, o, gg))
        o_all = jnp.concatenate([o for _, _, o, _ in chains], axis=0)
        oc_all = o_all - _head_stats(o_all, mean_op)
        var_all = _head_stats(oc_all * oc_all, mean_op)
        y_all = oc_all * lax.rsqrt(var_all + NORM_EPS)
        for n, (bi, cols, _, gg) in enumerate(chains):
            y = y_all[n * LIN_BLOCK:(n + 1) * LIN_BLOCK] * ng_ref[:, cols] * _silu(gg)
            o_ref[bi, rows, cols] = y.astype(BF16)

    for i in range(n_blocks):
        block(i)


def _gla_call(gla, lr, a2_pad, a_bias, norm_g, *, layer, batch, seq, tb=LIN_BLOCK):
    n_slab = GLA_W // LANES

    def lspec(shape):
        return pl.BlockSpec((None,) + shape, lambda t: (layer, 0, 0))

    return pl.pallas_call(
        functools.partial(_gla_kernel, n_blocks=tb // LIN_BLOCK, batch=batch),
        grid=(seq // tb,),
        in_specs=[pl.BlockSpec((batch, tb, 4 * GLA_W), lambda t: (0, t, 0)),
                  pl.BlockSpec((batch, tb, LANES), lambda t: (0, t, 0)),
                  lspec((LANES, GLA_W)), lspec((1, GLA_W)), lspec((1, GLA_W))],
        out_specs=pl.BlockSpec((batch, tb, GLA_W), lambda t: (0, t, 0)),
        out_shape=jax.ShapeDtypeStruct((batch, seq, GLA_W), BF16),
        scratch_shapes=[pltpu.VMEM((batch, n_slab, LANES, LANES), F32)],
        compiler_params=_params(1),
        name=f"gla_l{layer}",
    )(gla.reshape(batch, seq, 4 * GLA_W), lr.reshape(batch, seq, LANES), a2_pad, a_bias, norm_g
      ).reshape(batch * seq, GLA_W)


def _ret_call(ret, dmat, qf, kf, cd, norm_g, *, layer, batch, seq, tb=LIN_BLOCK):
    n_slab = RET_W // LANES
    full = lambda a: pl.BlockSpec(a.shape, lambda t: (0,) * a.ndim)
    return pl.pallas_call(
        functools.partial(_ret_kernel, n_blocks=tb // LIN_BLOCK, batch=batch),
        grid=(seq // tb,),
        in_specs=[pl.BlockSpec((batch, tb, 4 * RET_W), lambda t: (0, t, 0)),
                  full(dmat), full(qf), full(kf), full(cd),
                  pl.BlockSpec((None, 1, RET_W), lambda t: (layer, 0, 0))],
        out_specs=pl.BlockSpec((batch, tb, RET_W), lambda t: (0, t, 0)),
        out_shape=jax.ShapeDtypeStruct((batch, seq, RET_W), BF16),
        scratch_shapes=[pltpu.VMEM((batch, n_slab, LANES, LANES), F32)],
        compiler_params=_params(1),
        name=f"ret_l{layer}",
    )(ret.reshape(batch, seq, 4 * RET_W), dmat, qf, kf, cd, norm_g).reshape(batch * seq, RET_W)


def _out_kernel(x_ref, og_ref, on_ref, or_ref, gt_ref, w_ref, o_ref):
    y = (_dot(og_ref[...], w_ref[0:GLA_W, :])
         + _dot(on_ref[...], w_ref[GLA_W:GLA_W + NSA_W, :])
         + _dot(or_ref[...], w_ref[GLA_W + NSA_W:, :]))
    o_ref[...] = x_ref[...] + gt_ref[...] * y


def _out_call(x, o_gla, o_nsa, o_ret, mod, w_out, *, layer, seq, tm=1024):
    m, d = x.shape
    tiles_per_seq = seq // tm
    row = lambda w: pl.BlockSpec((tm, w), lambda i: (i, 0))
    return pl.pallas_call(
        _out_kernel,
        grid=(m // tm,),
        in_specs=[row(d), row(GLA_W), row(NSA_W), row(RET_W),
                  pl.BlockSpec((None, 1, d), lambda i: ((layer * 8 + i // tiles_per_seq) * N_ADA + 5, 0, 0)),
                  pl.BlockSpec((None, GLA_W + NSA_W + RET_W, d), lambda i: (layer, 0, 0))],
        out_specs=row(d),
        out_shape=jax.ShapeDtypeStruct((m, d), F32),
        compiler_params=_params(1),
        name=f"out_proj_l{layer}",
    )(x, o_gla, o_nsa, o_ret, mod, w_out)


def _layout_proj_weights(w_in, nsa_gate_bias):
    n_layers, d, _ = w_in.shape
    w_in = w_in.astype(BF16)
    (g_q, g_k, g_v, g_g, g_lr, n_q, n_kc, n_vc, n_ks, n_vs, n_kw, n_vw, n_gate,
     r_q, r_k, r_v, r_g) = jnp.split(w_in, IN_SPLITS, axis=-1)
    zeros = lambda w: jnp.zeros((n_layers, d, w), w_in.dtype)
    def pad_heads(w, n_heads):
        w = w.reshape(n_layers, d, n_heads, HEAD_DIM)
        return jnp.concatenate([w, jnp.zeros_like(w)], axis=-1).reshape(n_layers, d, n_heads * LANES)

    per_grp = 3 * NSA_REP
    gate_cols = []
    bias_cols = []
    for g in range(NSA_KV_HEADS):
        gate_cols += [n_gate[..., g * per_grp:(g + 1) * per_grp], zeros(LANES - per_grp)]
        bias_cols += [nsa_gate_bias[:, g * per_grp:(g + 1) * per_grp],
                      jnp.zeros((n_layers, LANES - per_grp), nsa_gate_bias.dtype)]
    w = jnp.concatenate([g_q, g_k, g_v, g_g, g_lr, zeros(LANES - GLA_LOWRANK), pad_heads(n_q, NSA_HEADS),
                         n_kc, n_vc, pad_heads(n_ks, NSA_KV_HEADS), pad_heads(n_kw, NSA_KV_HEADS), n_vs, n_vw]
                        + gate_cols + [r_q, r_k, r_v, r_g], axis=-1)
    assert w.shape[-1] == PROJ_W
    bias = jnp.concatenate(bias_cols, axis=-1).reshape(n_layers, 1, NSA_KV_HEADS * LANES)
    return w.astype(BF16), bias


def _layout_cmp_weights(pe_k, pe_v, w1_k, w2_k, w1_v, w2_v):
    n_layers = pe_k.shape[0]
    eye = jnp.eye(2, dtype=BF16)
    w1 = jnp.stack([w1_k, w1_v], axis=1).reshape(n_layers, 2, CMP_LEN, HEAD_DIM, CMP_HIDDEN).astype(BF16)
    w1_full = jnp.einsum('zkldn,kq,gh->zlkgdqhn', w1, eye, eye)
    w1_full = w1_full.reshape(n_layers, CMP_LEN * 2 * NSA_KV_W, 4 * CMP_HIDDEN)
    half = CMP_STRIDE * 2 * NSA_KV_W
    w2 = jnp.stack([w2_k, w2_v], axis=1).astype(BF16)
    w2_full = jnp.einsum('zknd,kq,gh->zqhnkgd', w2, eye, eye)
    w2_full = w2_full.reshape(n_layers, 4 * CMP_HIDDEN, 2 * NSA_KV_W)
    w2_k = w2_full[..., :NSA_KV_W].reshape(n_layers, 4 * CMP_HIDDEN, NSA_KV_HEADS, HEAD_DIM)
    w2_k = jnp.concatenate([w2_k, jnp.zeros_like(w2_k)], axis=-1).reshape(n_layers, 4 * CMP_HIDDEN, -1)
    w2_full = jnp.concatenate([w2_k, w2_full[..., NSA_KV_W:]], axis=-1).astype(BF16)
    pe = jnp.stack([pe_k, pe_v], axis=1)
    pe = jnp.broadcast_to(pe.transpose(0, 2, 1, 3)[:, :, :, None, :],
                          (n_layers, CMP_LEN, 2, NSA_KV_HEADS, HEAD_DIM))
    pe = pe.reshape(n_layers, 1, CMP_LEN * 2 * NSA_KV_W)
    return pe[:, :, :half], pe[:, :, half:], w1_full[:, :half], w1_full[:, half:], w2_full


def _rotary_tables(seq):
    half = HEAD_DIM // 2
    inv_freq = ROPE_THETA ** (-jnp.arange(half, dtype=F32) / half)
    ang = jnp.arange(seq).astype(F32)[:, None] * inv_freq[None, :]
    cos, sin = jnp.cos(ang), jnp.sin(ang)
    reps = LANES // HEAD_DIM
    return jnp.tile(jnp.concatenate([cos, cos], axis=-1), (1, reps)), \
        jnp.tile(jnp.concatenate([-sin, sin], axis=-1), (1, reps))


def _retention_tables():
    log_gamma = jnp.log1p(-jnp.exp2(-5.0 - jnp.arange(RET_HEADS, dtype=F32)))
    lg_lane = jnp.repeat(log_gamma, HEAD_DIM)[None, :]
    pos = (jnp.arange(LIN_BLOCK) % CHUNK).astype(F32)[:, None]
    qf = jnp.exp(lg_lane * (pos + 1.0))
    kf = jnp.exp(lg_lane * (CHUNK - 1.0 - pos))
    cd = jnp.exp(lg_lane * CHUNK)
    r = jnp.arange(LIN_BLOCK)
    rel = (r[:, None] - r[None, :]).astype(F32)
    ok = ((r[:, None] // CHUNK) == (r[None, :] // CHUNK)) & (rel >= 0)
    dm = jnp.where(ok[None], jnp.exp(log_gamma[:, None, None] * rel[None]), 0.0)
    dmat = dm.reshape(RET_W // LANES, 2 * LIN_BLOCK, LIN_BLOCK)
    return dmat, qf, kf, cd


def kernel(x, c, w_ada, b_ada, norm_g, ffn1_in, ffn1_out, w_in, gla_a2, gla_a_bias, gla_norm_g,
           nsa_pe_k, nsa_pe_v, nsa_w1_k, nsa_w2_k, nsa_w1_v, nsa_w2_v, nsa_gate_bias, ret_norm_g,
           w_out, ffn2_in, ffn2_out, final_norm_g):
    batch, seq, d = x.shape
    n_layers = w_ada.shape[0]
    assert batch <= 8 and seq % 512 == 0 and seq >= WINDOW + Q_BLOCK

    c_pad = jnp.zeros((8, d), F32).at[:batch].set(c)
    mod = _ada_call(c_pad, w_ada, b_ada).reshape(n_layers * 8 * N_ADA, 1, d)
    ng = norm_g.reshape(n_layers * 3, 1, d)

    w_proj, gate_bias = _layout_proj_weights(w_in, nsa_gate_bias)
    pe_top, pe_bot, w1_top, w1_bot, w2_cmp = _layout_cmp_weights(nsa_pe_k, nsa_pe_v, nsa_w1_k, nsa_w2_k,
                                                                  nsa_w1_v, nsa_w2_v)
    cosf, sinf = _rotary_tables(seq)
    dmat, qf, kf, cd = _retention_tables()
    a2_pad = jnp.zeros((n_layers, LANES, GLA_W), F32).at[:, :GLA_LOWRANK].set(gla_a2)
    a_bias = gla_a_bias.reshape(n_layers, 1, GLA_W)
    gla_g = jnp.tile(gla_norm_g, (1, GLA_HEADS)).reshape(n_layers, 1, GLA_W)
    ret_g = jnp.tile(ret_norm_g, (1, RET_HEADS)).reshape(n_layers, 1, RET_W)
    f1_in, f1_out = ffn1_in.astype(BF16), ffn1_out.astype(BF16)
    f2_in, f2_out = ffn2_in.astype(BF16), ffn2_out.astype(BF16)
    w_o = w_out.astype(BF16)

    xs = x.reshape(batch * seq, d)
    for l in range(n_layers):
        xs = _ffn_call(xs, ng, mod, f1_in, f1_out, layer=l, sub=0, seq=seq)
        (gla, lr, qraw, qrot, kcvc, ks, vs, kw, vw, gate, ret) = _proj_call(
            xs, ng, mod, w_proj, gate_bias, cosf, sinf, layer=l, seq=seq)
        kc, vc = _cmp_call(kcvc, pe_top, pe_bot, w1_top, w1_bot, w2_cmp, layer=l, batch=batch, seq=seq)
        o_gla = _gla_call(gla, lr, a2_pad, a_bias, gla_g, layer=l, batch=batch, seq=seq)
        o_ret = _ret_call(ret, dmat, qf, kf, cd, ret_g, layer=l, batch=batch, seq=seq)
        o_nsa = _nsa_call(qraw, qrot, kc, vc, ks, vs, kw, vw, gate, layer=l, batch=batch, seq=seq)
        xs = _out_call(xs, o_gla, o_nsa, o_ret, mod, w_o, layer=l, seq=seq)
        final_g = final_norm_g.reshape(1, d) if l == n_layers - 1 else None
        xs = _ffn_call(xs, ng, mod, f2_in, f2_out, layer=l, sub=2, seq=seq, final_g=final_g)
    return xs.reshape(batch, seq, d)
```

```python
import functools
import math

import numpy as np
import jax
import jax.numpy as jnp
from jax import lax
from jax.experimental import pallas as pl
from jax.experimental.pallas import tpu as pltpu

F32 = jnp.float32
BF16 = jnp.bfloat16

HEAD_DIM = 64
LANES = 128
GLA_HEADS = 4
NSA_HEADS = 8
NSA_KV_HEADS = 2
NSA_REP = NSA_HEADS // NSA_KV_HEADS
RET_HEADS = 4
GLA_LOWRANK = 16
GLA_TAU = 16.0
CHUNK = 64
CMP_LEN = 32
CMP_STRIDE = 16
CMP_HIDDEN = 128
SLC_LEN = 64
SLC_TOPK = 16
WINDOW = 512
Q_BLOCK = 128
ROPE_THETA = 10000.0
FFN_HALF = 0.5
NORM_EPS = 1e-6
NEG_INF = -1e30
FORCED_SCORE = 1e4
N_ADA = 9
QK_SCALE = HEAD_DIM ** -0.5
SLC_SHIFT = int(math.log2(SLC_LEN))
LOG2_E = math.log2(math.e)

GLA_W = GLA_HEADS * HEAD_DIM
NSA_W = NSA_HEADS * HEAD_DIM
NSA_KV_W = NSA_KV_HEADS * HEAD_DIM
RET_W = RET_HEADS * HEAD_DIM
IN_SIZES = (GLA_W, GLA_W, GLA_W, GLA_W, GLA_LOWRANK,
            NSA_W, NSA_KV_W, NSA_KV_W, NSA_KV_W, NSA_KV_W, NSA_KV_W, NSA_KV_W, 3 * NSA_HEADS,
            RET_W, RET_W, RET_W, RET_W)
IN_SPLITS = tuple(int(s) for s in np.cumsum(IN_SIZES)[:-1])

SEG_GLA = 0
SEG_LR = SEG_GLA + 4 * GLA_W
SEG_NQ = SEG_LR + LANES
SEG_KCVC = SEG_NQ + NSA_HEADS * LANES
SEG_KS = SEG_KCVC + 2 * NSA_KV_W
SEG_KW = SEG_KS + NSA_KV_HEADS * LANES
SEG_VS = SEG_KW + NSA_KV_HEADS * LANES
SEG_VW = SEG_VS + NSA_KV_W
SEG_GATE = SEG_VW + NSA_KV_W
SEG_RET = SEG_GATE + NSA_KV_HEADS * LANES
PROJ_W = SEG_RET + 4 * RET_W

VMEM_LIMIT_BYTES = 56 * 1024 * 1024

NT_DIMS = (((1,), (1,)), ((), ()))


def _dot(a, b):
    return jnp.dot(a, b, preferred_element_type=F32)


def _dot_nt(a, b):
    return lax.dot_general(a, b, NT_DIMS, preferred_element_type=F32)


def _split3(x):
    hi = x.astype(BF16)
    r1 = x - hi.astype(F32)
    mid = r1.astype(BF16)
    lo = (r1 - mid.astype(F32)).astype(BF16)
    return hi, mid, lo


def _dot3_exact_rhs(x, w_bf16):
    hi, mid, lo = _split3(x)
    return _dot(hi, w_bf16) + _dot(mid, w_bf16) + _dot(lo, w_bf16)


def _dot3_exact_lhs(w_bf16, x):
    hi, mid, lo = _split3(x)
    return _dot(w_bf16, hi) + _dot(w_bf16, mid) + _dot(w_bf16, lo)


def _sigmoid(x):
    return 1.0 / (1.0 + jnp.exp(-x))


def _silu(x):
    return x * _sigmoid(x)


def _iota(shape, dim):
    return lax.broadcasted_iota(jnp.int32, shape, dim)


def _params(n_grid):
    return pltpu.CompilerParams(dimension_semantics=("arbitrary",) * n_grid,
                                vmem_limit_bytes=VMEM_LIMIT_BYTES)


def _norm_mod(x, g, sh, sc):
    ms = jnp.mean(x * x, axis=-1, keepdims=True)
    return x * lax.rsqrt(ms + NORM_EPS) * (g * (1.0 + sc)) + sh


def _ada_kernel(c_ref, w_ref, b_ref, o_ref):
    c = c_ref[...]
    o_ref[...] = _dot(_silu(c).astype(BF16), w_ref[...].astype(BF16)) + b_ref[...]


def _ada_call(c_pad, w_ada, b_ada):
    n_layers, d, n_out = w_ada.shape
    tn = 2304 if n_out % 2304 == 0 else n_out
    return pl.pallas_call(
        _ada_kernel,
        grid=(n_layers, n_out // tn),
        in_specs=[pl.BlockSpec((8, d), lambda l, j: (0, 0)),
                  pl.BlockSpec((None, d, tn), lambda l, j: (l, 0, j)),
                  pl.BlockSpec((None, 1, tn), lambda l, j: (l, 0, j))],
        out_specs=pl.BlockSpec((None, 8, tn), lambda l, j: (l, 0, j)),
        out_shape=jax.ShapeDtypeStruct((n_layers, 8, n_out), F32),
        compiler_params=_params(2),
        name="ada_mod",
    )(c_pad, w_ada, b_ada.reshape(n_layers, 1, n_out))


def _ffn_kernel(x_ref, g_ref, sh_ref, sc_ref, gt_ref, win_ref, wout_ref, *rest, d_ff, chunks, final):
    if final:
        fg_ref, o_ref, act_ref = rest
    else:
        o_ref, act_ref = rest
    x = x_ref[...]
    h = _norm_mod(x, g_ref[...], sh_ref[...], sc_ref[...]).astype(BF16)
    for (c0, cw) in chunks:
        gate = _dot(h, win_ref[:, c0:c0 + cw])
        up = _dot(h, win_ref[:, d_ff + c0:d_ff + c0 + cw])
        act_ref[:, c0:c0 + cw] = (_silu(gate) * up).astype(BF16)
    y = _dot(act_ref[...], wout_ref[...])
    xn = x + (FFN_HALF * gt_ref[...]) * y
    if final:
        ms = jnp.mean(xn * xn, axis=-1, keepdims=True)
        xn = xn * lax.rsqrt(ms + NORM_EPS) * fg_ref[...]
    o_ref[...] = xn


def _ffn_call(x, ng, mod, w_in, w_out, *, layer, sub, seq, final_g=None, tm=1024):
    m, d = x.shape
    d_ff = w_out.shape[1]
    tiles_per_seq = seq // tm
    chunks, c0 = [], 0
    while c0 < d_ff:
        cw = min(512, d_ff - c0)
        chunks.append((c0, cw))
        c0 += cw

    def mod_spec(k):
        return pl.BlockSpec((None, 1, d), lambda i: ((layer * 8 + i // tiles_per_seq) * N_ADA + k, 0, 0))

    in_specs = [pl.BlockSpec((tm, d), lambda i: (i, 0)),
                pl.BlockSpec((None, 1, d), lambda i: (layer * 3 + sub, 0, 0)),
                mod_spec(3 * sub), mod_spec(3 * sub + 1), mod_spec(3 * sub + 2),
                pl.BlockSpec((None, d, 2 * d_ff), lambda i: (layer, 0, 0), pipeline_mode=pl.Buffered(1)),
                pl.BlockSpec((None, d_ff, d), lambda i: (layer, 0, 0), pipeline_mode=pl.Buffered(1))]
    args = [x, ng, mod, mod, mod, w_in, w_out]
    if final_g is not None:
        in_specs.append(pl.BlockSpec((1, d), lambda i: (0, 0)))
        args.append(final_g)
    return pl.pallas_call(
        functools.partial(_ffn_kernel, d_ff=d_ff, chunks=tuple(chunks), final=final_g is not None),
        grid=(m // tm,),
        in_specs=in_specs,
        out_specs=pl.BlockSpec((tm, d), lambda i: (i, 0)),
        out_shape=jax.ShapeDtypeStruct((m, d), F32),
        scratch_shapes=[pltpu.VMEM((tm, d_ff), BF16)],
        compiler_params=_params(1),
        name=f"ffn_l{layer}_s{sub}",
    )(*args)


def _rotate(x, cosf, sinf):
    rows, width = x.shape
    first = (_iota((rows, LANES), 1) & (HEAD_DIM - 1)) < HEAD_DIM // 2
    outs = []
    for j in range(width // LANES):
        xs = x[:, j * LANES:(j + 1) * LANES]
        swapped = jnp.where(first, pltpu.roll(xs, LANES - HEAD_DIM // 2, 1), pltpu.roll(xs, HEAD_DIM // 2, 1))
        outs.append(xs * cosf + swapped * sinf)
    return outs[0] if len(outs) == 1 else jnp.concatenate(outs, axis=1)


def _store_group_transposed(dst_ref, x, fill):
    top = _iota((LANES, LANES), 0) < HEAD_DIM
    for c in range(x.shape[0] // LANES):
        piece = jnp.transpose(x[c * LANES:(c + 1) * LANES])
        dst_ref[0, c] = jnp.where(top, piece, fill).astype(dst_ref.dtype)
        dst_ref[1, c] = jnp.where(top, pltpu.roll(piece, HEAD_DIM, 0), fill).astype(dst_ref.dtype)


def _proj_kernel(x_ref, g_ref, sh_ref, sc_ref, w_ref, gb_ref, cos_ref, sin_ref,
                 gla_ref, lr_ref, qraw_ref, qrot_ref, kcvc_ref, ks_ref, vst_ref, kw_ref, vwt_ref,
                 gate_ref, ret_ref, *, tiles_per_seq):
    h = _norm_mod(x_ref[...], g_ref[...], sh_ref[...], sc_ref[...]).astype(BF16)
    cosf = cos_ref[...]
    sinf = sin_ref[...]
    tm = h.shape[0]

    def seg(c0, width):
        return _dot(h, w_ref[:, c0:c0 + width])

    gla_ref[...] = seg(SEG_GLA, 4 * GLA_W)
    lr_ref[...] = seg(SEG_LR, LANES)
    nq = seg(SEG_NQ, NSA_HEADS * LANES)
    qraw_ref[...] = (nq * (QK_SCALE * LOG2_E)).astype(BF16)
    qrot_ref[...] = (_rotate(nq, cosf, sinf) * (QK_SCALE * LOG2_E)).astype(BF16)
    kcvc_ref[...] = seg(SEG_KCVC, 2 * NSA_KV_W)
    grp_w = NSA_KV_HEADS * LANES
    pos = (pl.program_id(0) % tiles_per_seq) * tm + _iota((tm, grp_w), 0)
    spare = (_iota((tm, grp_w), 1) & (LANES - 1)) - HEAD_DIM
    blk_onehot = jnp.where(spare == (pos >> SLC_SHIFT), 1.0, 0.0)
    ks_ref[...] = (_rotate(seg(SEG_KS, grp_w), cosf, sinf) + blk_onehot).astype(BF16)
    kw_ref[...] = _rotate(seg(SEG_KW, grp_w), cosf, sinf).astype(BF16)
    _store_group_transposed(vst_ref, seg(SEG_VS, NSA_KV_W), 1.0)
    _store_group_transposed(vwt_ref, seg(SEG_VW, NSA_KV_W), 1.0)
    gate_ref[...] = _sigmoid(seg(SEG_GATE, NSA_KV_HEADS * LANES) + gb_ref[...])
    ret_ref[:, 0:RET_W] = _rotate(seg(SEG_RET, RET_W), cosf, sinf)
    ret_ref[:, RET_W:2 * RET_W] = _rotate(seg(SEG_RET + RET_W, RET_W), cosf, sinf) * QK_SCALE
    ret_ref[:, 2 * RET_W:4 * RET_W] = seg(SEG_RET + 2 * RET_W, 2 * RET_W)


def _proj_call(x, ng, mod, w_proj, gate_bias, cosf, sinf, *, layer, seq, tm=512):
    m, d = x.shape
    tiles_per_seq = seq // tm

    def mod_spec(k):
        return pl.BlockSpec((None, 1, d), lambda i: ((layer * 8 + i // tiles_per_seq) * N_ADA + k, 0, 0))

    def row_spec(width):
        return pl.BlockSpec((tm, width), lambda i: (i, 0))

    batch = m // seq
    grp_w = NSA_KV_HEADS * LANES

    def vt_spec():
        return pl.BlockSpec((None, NSA_KV_HEADS, tm // LANES, LANES, LANES),
                            lambda i: (i // tiles_per_seq, 0, i % tiles_per_seq, 0, 0))

    vt_shape = jax.ShapeDtypeStruct((batch, NSA_KV_HEADS, seq // LANES, LANES, LANES), BF16)
    out_widths = [(4 * GLA_W, F32), (LANES, F32), (NSA_HEADS * LANES, BF16), (NSA_HEADS * LANES, BF16),
                  (2 * NSA_KV_W, F32), (grp_w, BF16), None, (grp_w, BF16), None,
                  (NSA_KV_HEADS * LANES, F32), (4 * RET_W, F32)]
    return pl.pallas_call(
        functools.partial(_proj_kernel, tiles_per_seq=tiles_per_seq),
        grid=(m // tm,),
        in_specs=[row_spec(d),
                  pl.BlockSpec((None, 1, d), lambda i: (layer * 3 + 1, 0, 0)),
                  mod_spec(3), mod_spec(4),
                  pl.BlockSpec((None, d, PROJ_W), lambda i: (layer, 0, 0), pipeline_mode=pl.Buffered(1)),
                  pl.BlockSpec((None, 1, NSA_KV_HEADS * LANES), lambda i: (layer, 0, 0)),
                  pl.BlockSpec((tm, LANES), lambda i: (i % tiles_per_seq, 0)),
                  pl.BlockSpec((tm, LANES), lambda i: (i % tiles_per_seq, 0))],
        out_specs=[vt_spec() if ow is None else row_spec(ow[0]) for ow in out_widths],
        out_shape=[vt_shape if ow is None else jax.ShapeDtypeStruct((m, ow[0]), ow[1]) for ow in out_widths],
        compiler_params=_params(1),
        name=f"proj_l{layer}",
    )(x, ng, mod, mod, w_proj, gate_bias, cosf, sinf)


def _cmp_kernel(xk_ref, xv_ref, pet_ref, peb_ref, wt_ref, wb_ref, w2_ref, kc_ref, vct_ref):
    n_rows = xk_ref.shape[0] // CMP_STRIDE
    x = jnp.concatenate([ref[pl.ds(l, n_rows, stride=CMP_STRIDE), :]
                         for l in range(CMP_STRIDE) for ref in (xk_ref, xv_ref)], axis=1)
    a = _dot((x + pet_ref[...]).astype(BF16), wt_ref[...])
    b = _dot((x + peb_ref[...]).astype(BF16), wb_ref[...])
    hid = a + pltpu.roll(b, n_rows - 1, 0)
    out = _dot(_silu(hid).astype(BF16), w2_ref[...])
    out = jnp.where(_iota(out.shape, 0) < n_rows - 1, out, 0.0)
    grp_w = NSA_KV_HEADS * LANES
    kc_ref[...] = out[:, 0:grp_w].astype(BF16)
    _store_group_transposed(vct_ref, out[:, grp_w:grp_w + NSA_KV_W], 0.0)


def _cmp_call(kcvc, pe_top, pe_bot, w1_halves, w2, *, layer, batch, seq):
    rows = seq // CMP_STRIDE
    width = CMP_STRIDE * 2 * NSA_KV_W
    x = kcvc.reshape(batch, seq, 2 * NSA_KV_W)
    hid_w = 4 * CMP_HIDDEN
    grp_w = NSA_KV_HEADS * LANES

    def wspec(shape):
        return pl.BlockSpec((None,) + shape, lambda b: (layer, 0, 0))

    return pl.pallas_call(
        _cmp_kernel,
        grid=(batch,),
        in_specs=[pl.BlockSpec((None, seq, NSA_KV_W), lambda b: (b, 0, 0)),
                  pl.BlockSpec((None, seq, NSA_KV_W), lambda b: (b, 0, 1)),
                  wspec((1, width)), wspec((1, width)),
                  pl.BlockSpec((None, None, width, hid_w), lambda b: (layer, 0, 0, 0)),
                  pl.BlockSpec((None, None, width, hid_w), lambda b: (layer, 1, 0, 0)),
                  wspec((hid_w, grp_w + NSA_KV_W))],
        out_specs=[pl.BlockSpec((None, rows, grp_w), lambda b: (b, 0, 0)),
                   pl.BlockSpec((None, NSA_KV_HEADS, rows // LANES, LANES, LANES), lambda b: (b, 0, 0, 0, 0))],
        out_shape=[jax.ShapeDtypeStruct((batch, rows, grp_w), BF16),
                   jax.ShapeDtypeStruct((batch, NSA_KV_HEADS, rows // LANES, LANES, LANES), BF16)],
        compiler_params=_params(1),
        name=f"nsa_compress_l{layer}",
    )(x, x, pe_top, pe_bot, w1_halves, w1_halves, w2)


def _heads_tiled(x):
    return jnp.concatenate([x] * NSA_REP, axis=1)


SLC_UNROLL = 2
RANK_STEP = 16


def _q_transposed(x):
    x = x.astype(F32)
    return jnp.concatenate([jnp.transpose(x[:, r * LANES:(r + 1) * LANES])[0:HEAD_DIM, :] for r in range(NSA_REP)],
                           axis=1)


def _nsa_compressed(q_raw, kc, vc_t, t0, tq, seq):
    n_cp = seq // CMP_STRIDE
    n_cmp = (seq - CMP_LEN) // CMP_STRIDE + 1
    s_c = _dot(kc, q_raw)
    nn = _iota((n_cp, tq), 0)
    tt = t0 + _iota((n_cp, tq), 1)
    bias_c = jnp.where((nn * CMP_STRIDE + (CMP_LEN - 1) <= tt) & (nn < n_cmp), 0.0, NEG_INF)
    has_cmp = t0 + _iota((1, tq), 1) >= CMP_LEN - 1
    p_heads = []
    for r in range(NSA_REP):
        s = s_c[:, r * tq:(r + 1) * tq] + bias_c
        e = jnp.exp2(s - jnp.max(s, axis=0, keepdims=True))
        p_heads.append(e * jnp.where(has_cmp, 1.0 / jnp.sum(e, axis=0, keepdims=True), 0.0))
    o_cmp = _dot(vc_t, jnp.concatenate(p_heads, axis=1).astype(BF16))
    p_sum = p_heads[0]
    for r in range(1, NSA_REP):
        p_sum = p_sum + p_heads[r]
    return o_cmp, p_sum


def _nsa_selection_rows(p_sum, imp_ref, t0, tq, seq):
    n_cp = seq // CMP_STRIDE
    n_cmp = (seq - CMP_LEN) // CMP_STRIDE + 1
    n_slc = seq // SLC_LEN
    n_sel = min(SLC_TOPK, n_slc)
    sb = _iota((LANES, n_cp), 0)
    cb = _iota((LANES, n_cp), 1)
    overlap_t = ((cb * CMP_STRIDE < sb * SLC_LEN + SLC_LEN) & (cb * CMP_STRIDE + CMP_LEN > sb * SLC_LEN)
                 & (sb < n_slc) & (cb < n_cmp))
    imp_t = _dot3_exact_lhs(jnp.where(overlap_t, 1.0, 0.0).astype(BF16), p_sum)
    blk = _iota((LANES, tq), 0)
    cur = (t0 + _iota((LANES, tq), 1)) >> SLC_SHIFT
    forced = (blk == 0) | (blk == cur) | (blk == cur - 1)
    imp_ref[...] = jnp.where(forced, FORCED_SCORE, jnp.where(blk <= cur, imp_t, -1.0))

    def ranked_rows(n_rows):
        if n_rows <= n_sel:
            return jnp.zeros((HEAD_DIM, tq), F32)
        n_grp = n_rows // 8
        vals = [imp_ref[8 * j:8 * (j + 1), :] for j in range(n_grp)]
        ranks = [jnp.zeros((8, tq), F32) for _ in range(n_grp)]
        sub = _iota((8, tq), 0)
        for i in range(n_rows):
            row = jnp.broadcast_to(imp_ref[i:i + 1, :], (8, tq))
            for j in range(n_grp):
                if 8 * j > i:
                    beats = jnp.where(row >= vals[j], 1.0, 0.0)
                elif 8 * j + 7 <= i:
                    beats = jnp.where(row > vals[j], 1.0, 0.0)
                else:
                    beats = jnp.where(sub + 8 * j > i, jnp.where(row >= vals[j], 1.0, 0.0),
                                      jnp.where(row > vals[j], 1.0, 0.0))
                ranks[j] = ranks[j] + beats
        return jnp.concatenate([jnp.where(rk < n_sel, 0.0, NEG_INF) for rk in ranks]
                               + [jnp.zeros((HEAD_DIM - n_rows, tq), F32)] * (n_rows < HEAD_DIM), axis=0)

    sizes = list(range(RANK_STEP, n_slc, RANK_STEP)) + [n_slc]
    last_causal = (t0 + tq - 1) >> SLC_SHIFT
    return lax.switch(jnp.minimum(last_causal >> int(math.log2(RANK_STEP)), len(sizes) - 1),
                      [functools.partial(ranked_rows, n) for n in sizes])


def _nsa_window(q_rot, kw_ref, vwt_ref, g, qb, tq):
    n_wb = WINDOW // tq + 1
    ki = _iota((tq, tq), 0)
    ci = _iota((tq, tq), 1)
    blocks = []
    for j in range(n_wb):
        kb = qb - (n_wb - 1) + j
        kb_read = jnp.maximum(kb, 0)
        s = _dot(kw_ref[pl.ds(pl.multiple_of(kb_read * tq, tq), tq), g * LANES:(g + 1) * LANES], q_rot)
        if j == 0:
            s = s + _heads_tiled(jnp.where(ki > ci, 0.0, NEG_INF))
        if j == n_wb - 1:
            s = s + _heads_tiled(jnp.where(ki <= ci, 0.0, NEG_INF))
        else:
            s = s + jnp.where(kb >= 0, 0.0, NEG_INF)
        blocks.append((s, kb_read))
    m_w = blocks[0][0].max(axis=0, keepdims=True)
    for s, _ in blocks[1:]:
        m_w = jnp.maximum(m_w, s.max(axis=0, keepdims=True))
    o_aug = jnp.zeros((LANES, NSA_REP * tq), F32)
    for s, kb_read in blocks:
        o_aug = o_aug + _dot(vwt_ref[g, kb_read], jnp.exp2(s - m_w).astype(BF16))
    return o_aug[0:HEAD_DIM] * (1.0 / o_aug[HEAD_DIM:HEAD_DIM + 1])


def _nsa_kernel(qraw_ref, qrot_ref, kc_ref, vct_ref, ks_ref, vst_ref, kw_ref, vwt_ref, gate_ref,
                o_ref, imp_ref, *, seq, tq, tk):
    qb = pl.program_id(1)
    t0 = qb * tq
    groups = range(NSA_KV_HEADS)
    n_cols = NSA_REP * tq

    def grp(g):
        return slice(g * LANES, (g + 1) * LANES)

    def q_cols(g):
        return slice(g * NSA_REP * LANES, (g + 1) * NSA_REP * LANES)

    def q_aug(q_t, extra_rows):
        return jnp.concatenate([q_t, extra_rows], axis=0).astype(BF16)

    o_cmp, q_sel = [], []
    for g in groups:
        q_raw = q_aug(_q_transposed(qraw_ref[:, q_cols(g)]), jnp.zeros((HEAD_DIM, n_cols), F32))
        vc_t = jnp.concatenate([vct_ref[g, c] for c in range(seq // CMP_STRIDE // LANES)], axis=1)
        o_c, p_sum = _nsa_compressed(q_raw, kc_ref[:, grp(g)], vc_t, t0, tq, seq)
        o_cmp.append(o_c[0:HEAD_DIM])
        sel_neg = _heads_tiled(_nsa_selection_rows(p_sum, imp_ref.at[g], t0, tq, seq))
        q_sel.append(q_aug(_q_transposed(qrot_ref[:, q_cols(g)]), sel_neg))

    blocks_per_tile = tk // LANES
    n_full = t0 // tk
    n_iter = (n_full + SLC_UNROLL - 1) // SLC_UNROLL

    def k_tile(kt, g):
        return ks_ref[pl.ds(pl.multiple_of(kt * tk, tk), tk), grp(g)]

    def v_tile(kt, g):
        return jnp.concatenate([vst_ref[g, kt * blocks_per_tile + c] for c in range(blocks_per_tile)], axis=1)

    def online_update(scores, values, m_i, acc):
        m_new = m_i
        for s in scores:
            m_new = jnp.maximum(m_new, s.max(axis=0, keepdims=True))
        p = jnp.concatenate([jnp.exp2(s - m_new).astype(BF16) for s in scores], axis=0)
        return m_new, jnp.exp2(m_i - m_new) * acc + _dot(jnp.concatenate(values, axis=1), p)

    def slc_step(j, stats):
        out = []
        for g in groups:
            scores, values = [], []
            for u in range(SLC_UNROLL):
                kt_raw = j * SLC_UNROLL + u
                kt = jnp.minimum(kt_raw, n_full - 1)
                v_t = v_tile(kt, g)
                values.append(jnp.where(kt_raw < n_full, v_t, jnp.zeros_like(v_t)))
                scores.append(_dot(k_tile(kt, g), q_sel[g]))
            out.append(online_update(scores, values, *stats[g]))
        return tuple(out)

    causal = n_full * tk + _iota((tk, tq), 0) <= t0 + _iota((tk, tq), 1)
    causal_bias = _heads_tiled(jnp.where(causal, 0.0, NEG_INF))
    init = tuple(online_update([_dot(k_tile(n_full, g), q_sel[g]) + causal_bias], [v_tile(n_full, g)],
                               jnp.full((1, n_cols), NEG_INF, F32), jnp.zeros((LANES, n_cols), F32))
                 for g in groups)
    stats = lax.fori_loop(0, n_iter, slc_step, init)

    gate_all = gate_ref[...]
    for g in groups:
        acc = stats[g][1]
        o_slc = acc[0:HEAD_DIM] * (1.0 / acc[HEAD_DIM:HEAD_DIM + 1])
        o_win = _nsa_window(q_sel[g], kw_ref, vwt_ref, g, qb, tq)
        gate_t = jnp.transpose(gate_all[:, grp(g)])
        o_heads = []
        for r in range(NSA_REP):
            cols = slice(r * tq, (r + 1) * tq)
            o_heads.append(gate_t[3 * r:3 * r + 1] * o_cmp[g][:, cols] + gate_t[3 * r + 1:3 * r + 2] * o_slc[:, cols]
                           + gate_t[3 * r + 2:3 * r + 3] * o_win[:, cols])
        for pair in range(NSA_REP // 2):
            slab = jnp.transpose(jnp.concatenate(o_heads[2 * pair:2 * pair + 2], axis=0))
            c0 = (g * NSA_REP // 2 + pair) * LANES
            o_ref[:, c0:c0 + LANES] = slab.astype(BF16)


def _nsa_call(qraw, qrot, kc, vct, ks, vst, kw, vwt, gate, *, layer, batch, seq, tq=Q_BLOCK, tk=512):
    assert tq == LANES and tk % tq == 0 and seq % tk == 0 and seq // SLC_LEN <= HEAD_DIM
    nq = seq // tq
    n_cp = seq // CMP_STRIDE
    grp_w = NSA_KV_HEADS * LANES

    def q_spec():
        return pl.BlockSpec((tq, NSA_HEADS * LANES), lambda b, i: (b * nq + i, 0))

    def k_spec(rows):
        return pl.BlockSpec((None, rows, grp_w), lambda b, i: (b, 0, 0))

    def vt_spec():
        return pl.BlockSpec((None, NSA_KV_HEADS, seq // LANES, LANES, LANES), lambda b, i: (b, 0, 0, 0, 0))

    return pl.pallas_call(
        functools.partial(_nsa_kernel, seq=seq, tq=tq, tk=tk),
        grid=(batch, nq),
        in_specs=[q_spec(), q_spec(), k_spec(n_cp),
                  pl.BlockSpec((None, NSA_KV_HEADS, n_cp // LANES, LANES, LANES), lambda b, i: (b, 0, 0, 0, 0)),
                  k_spec(seq), vt_spec(), k_spec(seq), vt_spec(),
                  pl.BlockSpec((tq, grp_w), lambda b, i: (b * nq + i, 0))],
        out_specs=pl.BlockSpec((tq, NSA_W), lambda b, i: (b * nq + i, 0)),
        out_shape=jax.ShapeDtypeStruct((batch * seq, NSA_W), BF16),
        scratch_shapes=[pltpu.VMEM((NSA_KV_HEADS, LANES, tq), F32)],
        compiler_params=_params(2),
        name=f"nsa_attn_l{layer}",
    )(qraw, qrot, kc, vct, ks.reshape(batch, seq, grp_w), vst, kw.reshape(batch, seq, grp_w), vwt, gate)


LIN_BLOCK = 2 * CHUNK


def _lin_block_consts():
    r = _iota((LIN_BLOCK, LANES), 0)
    c = _iota((LIN_BLOCK, LANES), 1)
    same_chunk = (r >= CHUNK) == (c >= CHUNK)
    causal = same_chunk & (r >= c)
    head_diag = same_chunk
    first_rows = r < CHUNK
    low_lanes = c < HEAD_DIM
    return causal, head_diag, first_rows, low_lanes


def _head_stats(x, head_diag_mean):
    hi = x.astype(BF16)
    lo = (x - hi.astype(F32)).astype(BF16)
    return _dot(hi, head_diag_mean) + _dot(lo, head_diag_mean)


def _lin_core(q_dec, k_inv, k_dec, v, q_int, intra_scale, dec_a, dec_b, st_prev, consts):
    causal, head_diag, first_rows, low_lanes = consts
    zero = jnp.zeros_like(q_dec)
    q2 = jnp.concatenate([jnp.where(low_lanes, q_dec, zero), jnp.where(low_lanes, zero, q_dec)], axis=0)
    a2 = _dot_nt(q2.astype(BF16), k_inv.astype(BF16))
    mask2 = jnp.concatenate([causal, causal], axis=0)
    a2 = jnp.where(mask2, a2, 0.0)
    if intra_scale is not None:
        a2 = a2 * intra_scale
    o2 = _dot(a2.astype(BF16), v.astype(BF16))
    o_intra = jnp.where(low_lanes, o2[0:LIN_BLOCK], o2[LIN_BLOCK:])

    v_t = jnp.transpose(v).astype(BF16)
    kd_ab = jnp.concatenate([jnp.where(first_rows, k_dec, 0.0), jnp.where(first_rows, 0.0, k_dec)], axis=1)
    upd = _dot(v_t, kd_ab.astype(BF16))
    st_a = st_prev * dec_a + jnp.where(head_diag, upd[:, 0:LANES], 0.0)
    st_b = st_a * dec_b + jnp.where(head_diag, upd[:, LANES:], 0.0)
    o_ab = _dot_nt(q_int.astype(BF16), jnp.concatenate([st_prev, st_a], axis=0).astype(BF16))
    o_inter = jnp.where(first_rows, o_ab[:, 0:LANES], o_ab[:, LANES:])
    return o_intra + o_inter, st_b


def _gla_kernel(x_ref, lr_ref, a2_ref, ab_ref, ng_ref, o_ref, st_ref, *, n_blocks, batch):
    @pl.when(pl.program_id(0) == 0)
    def _():
        st_ref[...] = jnp.zeros_like(st_ref)

    consts = _lin_block_consts()
    causal, head_diag, first_rows, low_lanes = consts
    ri = _iota((LIN_BLOCK, LANES), 0)
    ci = _iota((LIN_BLOCK, LANES), 1)
    tri_t = jnp.where(((ri >= CHUNK) == (ci >= CHUNK)) & (ri <= ci), 1.0, 0.0).astype(BF16)
    mean_op = jnp.where(head_diag, 1.0 / HEAD_DIM, 0.0).astype(BF16)
    a2h = a2_ref[...].astype(BF16)
    a2l = (a2_ref[...] - a2h.astype(F32)).astype(BF16)

    def block(i):
        rows = pl.ds(i * LIN_BLOCK, LIN_BLOCK)
        lr = jnp.concatenate([lr_ref[bi, rows, :] for bi in range(batch)], axis=0)
        lh = lr.astype(BF16)
        ll = (lr - lh.astype(F32)).astype(BF16)
        z = _dot(lh, a2h) + _dot(lh, a2l) + _dot(ll, a2h) + ab_ref[...]
        log_a = (jnp.minimum(z, 0.0) - jnp.log(1.0 + jnp.exp(-jnp.abs(z)))) * (1.0 / GLA_TAU)
        n_slab = GLA_W // LANES
        la_t = jnp.concatenate([jnp.transpose(log_a[bi * LIN_BLOCK:(bi + 1) * LIN_BLOCK, s * LANES:(s + 1) * LANES])
                                for bi in range(batch) for s in range(n_slab)], axis=0)
        cum_t = _dot3_exact_rhs(la_t, tri_t)
        chains = []
        for bi in range(batch):
            for s in range(n_slab):
                cols = slice(s * LANES, (s + 1) * LANES)
                n = bi * n_slab + s

                def ld(k):
                    return x_ref[bi, rows, k * GLA_W + s * LANES:k * GLA_W + (s + 1) * LANES]

                q, k, v, gg = ld(0), ld(1), ld(2), ld(3)
                b = jnp.transpose(cum_t[n * LANES:(n + 1) * LANES])
                bl_a = b[CHUNK - 1:CHUNK, :]
                bl_b = b[LIN_BLOCK - 1:LIN_BLOCK, :]
                bl = jnp.where(first_rows, bl_a, bl_b)
                q_dec = q * QK_SCALE * jnp.exp(b)
                o, st_new = _lin_core(q_dec, k * jnp.exp(-b), k * jnp.exp(bl - b), v, q_dec, None,
                                      jnp.exp(bl_a), jnp.exp(bl_b), st_ref[bi, s], consts)
                st_ref[bi, s] = st_new
                chains.append((bi, cols, o, gg))
        ms = _head_stats(jnp.concatenate([o * o for _, _, o, _ in chains], axis=0), mean_op)
        for n, (bi, cols, o, gg) in enumerate(chains):
            y = o * lax.rsqrt(ms[n * LIN_BLOCK:(n + 1) * LIN_BLOCK] + NORM_EPS) * ng_ref[:, cols] * _silu(gg)
            o_ref[bi, rows, cols] = y.astype(BF16)

    for i in range(n_blocks):
        block(i)


def _ret_kernel(x_ref, dmat_ref, qf_ref, kf_ref, cd_ref, ng_ref, o_ref, st_ref, *, n_blocks, batch):
    @pl.when(pl.program_id(0) == 0)
    def _():
        st_ref[...] = jnp.zeros_like(st_ref)

    consts = _lin_block_consts()
    causal, head_diag, first_rows, low_lanes = consts
    mean_op = jnp.where(head_diag, 1.0 / HEAD_DIM, 0.0).astype(BF16)

    def block(i):
        rows = pl.ds(i * LIN_BLOCK, LIN_BLOCK)
        chains = []
        for bi in range(batch):
            for s in range(RET_W // LANES):
                cols = slice(s * LANES, (s + 1) * LANES)

                def ld(k):
                    return x_ref[bi, rows, k * RET_W + s * LANES:k * RET_W + (s + 1) * LANES]

                q, k, v, gg = ld(0), ld(1), ld(2), ld(3)
                cd = cd_ref[:, cols]
                o, st_new = _lin_core(q, k, k * kf_ref[:, cols], v, q * qf_ref[:, cols], dmat_ref[s],
                                      cd, cd, st_ref[bi, s], consts)
                st_ref[bi, s] = st_new
                chains.append((bi, cols, o, gg))
        o_all = jnp.concatenate([o for _, _, o, _ in chains], axis=0)
        oc_all = o_all - _head_stats(o_all, mean_op)
        var_all = _head_stats(oc_all * oc_all, mean_op)
        y_all = oc_all * lax.rsqrt(var_all + NORM_EPS)
        for n, (bi, cols, _, gg) in enumerate(chains):
            y = y_all[n * LIN_BLOCK:(n + 1) * LIN_BLOCK] * ng_ref[:, cols] * _silu(gg)
            o_ref[bi, rows, cols] = y.astype(BF16)

    for i in range(n_blocks):
        block(i)


def _gla_call(gla, lr, a2_pad, a_bias, norm_g, *, layer, batch, seq, tb=LIN_BLOCK):
    n_slab = GLA_W // LANES

    def lspec(shape):
        return pl.BlockSpec((None,) + shape, lambda t: (layer, 0, 0))

    return pl.pallas_call(
        functools.partial(_gla_kernel, n_blocks=tb // LIN_BLOCK, batch=batch),
        grid=(seq // tb,),
        in_specs=[pl.BlockSpec((batch, tb, 4 * GLA_W), lambda t: (0, t, 0)),
                  pl.BlockSpec((batch, tb, LANES), lambda t: (0, t, 0)),
                  lspec((LANES, GLA_W)), lspec((1, GLA_W)), lspec((1, GLA_W))],
        out_specs=pl.BlockSpec((batch, tb, GLA_W), lambda t: (0, t, 0)),
        out_shape=jax.ShapeDtypeStruct((batch, seq, GLA_W), BF16),
        scratch_shapes=[pltpu.VMEM((batch, n_slab, LANES, LANES), F32)],
        compiler_params=_params(1),
        name=f"gla_l{layer}",
    )(gla.reshape(batch, seq, 4 * GLA_W), lr.reshape(batch, seq, LANES), a2_pad, a_bias, norm_g
      ).reshape(batch * seq, GLA_W)


def _ret_call(ret, dmat, qf, kf, cd, norm_g, *, layer, batch, seq, tb=LIN_BLOCK):
    n_slab = RET_W // LANES
    full = lambda a: pl.BlockSpec(a.shape, lambda t: (0,) * a.ndim)
    return pl.pallas_call(
        functools.partial(_ret_kernel, n_blocks=tb // LIN_BLOCK, batch=batch),
        grid=(seq // tb,),
        in_specs=[pl.BlockSpec((batch, tb, 4 * RET_W), lambda t: (0, t, 0)),
                  full(dmat), full(qf), full(kf), full(cd),
                  pl.BlockSpec((None, 1, RET_W), lambda t: (layer, 0, 0))],
        out_specs=pl.BlockSpec((batch, tb, RET_W), lambda t: (0, t, 0)),
        out_shape=jax.ShapeDtypeStruct((batch, seq, RET_W), BF16),
        scratch_shapes=[pltpu.VMEM((batch, n_slab, LANES, LANES), F32)],
        compiler_params=_params(1),
        name=f"ret_l{layer}",
    )(ret.reshape(batch, seq, 4 * RET_W), dmat, qf, kf, cd, norm_g).reshape(batch * seq, RET_W)


def _out_kernel(x_ref, og_ref, on_ref, or_ref, gt_ref, w_ref, o_ref):
    y = (_dot(og_ref[...], w_ref[0:GLA_W, :])
         + _dot(on_ref[...], w_ref[GLA_W:GLA_W + NSA_W, :])
         + _dot(or_ref[...], w_ref[GLA_W + NSA_W:, :]))
    o_ref[...] = x_ref[...] + gt_ref[...] * y


def _out_call(x, o_gla, o_nsa, o_ret, mod, w_out, *, layer, seq, tm=1024):
    m, d = x.shape
    tiles_per_seq = seq // tm
    row = lambda w: pl.BlockSpec((tm, w), lambda i: (i, 0))
    return pl.pallas_call(
        _out_kernel,
        grid=(m // tm,),
        in_specs=[row(d), row(GLA_W), row(NSA_W), row(RET_W),
                  pl.BlockSpec((None, 1, d), lambda i: ((layer * 8 + i // tiles_per_seq) * N_ADA + 5, 0, 0)),
                  pl.BlockSpec((None, GLA_W + NSA_W + RET_W, d), lambda i: (layer, 0, 0))],
        out_specs=row(d),
        out_shape=jax.ShapeDtypeStruct((m, d), F32),
        compiler_params=_params(1),
        name=f"out_proj_l{layer}",
    )(x, o_gla, o_nsa, o_ret, mod, w_out)


def _layout_proj_weights(w_in, nsa_gate_bias):
    n_layers, d, _ = w_in.shape
    w_in = w_in.astype(BF16)
    (g_q, g_k, g_v, g_g, g_lr, n_q, n_kc, n_vc, n_ks, n_vs, n_kw, n_vw, n_gate,
     r_q, r_k, r_v, r_g) = jnp.split(w_in, IN_SPLITS, axis=-1)
    zeros = lambda w: jnp.zeros((n_layers, d, w), w_in.dtype)
    def pad_heads(w, n_heads):
        w = w.reshape(n_layers, d, n_heads, HEAD_DIM)
        return jnp.concatenate([w, jnp.zeros_like(w)], axis=-1).reshape(n_layers, d, n_heads * LANES)

    per_grp = 3 * NSA_REP
    gate_cols = []
    bias_cols = []
    for g in range(NSA_KV_HEADS):
        gate_cols += [n_gate[..., g * per_grp:(g + 1) * per_grp], zeros(LANES - per_grp)]
        bias_cols += [nsa_gate_bias[:, g * per_grp:(g + 1) * per_grp],
                      jnp.zeros((n_layers, LANES - per_grp), nsa_gate_bias.dtype)]
    w = jnp.concatenate([g_q, g_k, g_v, g_g, g_lr, zeros(LANES - GLA_LOWRANK), pad_heads(n_q, NSA_HEADS),
                         n_kc, n_vc, pad_heads(n_ks, NSA_KV_HEADS), pad_heads(n_kw, NSA_KV_HEADS), n_vs, n_vw]
                        + gate_cols + [r_q, r_k, r_v, r_g], axis=-1)
    assert w.shape[-1] == PROJ_W
    bias = jnp.concatenate(bias_cols, axis=-1).reshape(n_layers, 1, NSA_KV_HEADS * LANES)
    return w.astype(BF16), bias


def _layout_cmp_weights(pe_k, pe_v, w1_k, w2_k, w1_v, w2_v):
    n_layers = pe_k.shape[0]
    eye = jnp.eye(2, dtype=BF16)
    w1 = jnp.stack([w1_k, w1_v], axis=1).reshape(n_layers, 2, CMP_LEN, HEAD_DIM, CMP_HIDDEN).astype(BF16)
    blocks = [jnp.pad(w1[:, kv], ((0, 0), (0, 0), (0, 0), ((2 * kv + g) * CMP_HIDDEN, (3 - 2 * kv - g) * CMP_HIDDEN)))
              for kv in range(2) for g in range(NSA_KV_HEADS)]
    w1_full = jnp.stack(blocks, axis=2).reshape(n_layers, CMP_LEN * 2 * NSA_KV_W, 4 * CMP_HIDDEN)
    half = CMP_STRIDE * 2 * NSA_KV_W
    w2 = jnp.stack([w2_k, w2_v], axis=1).astype(BF16)
    w2_full = jnp.einsum('zknd,kq,gh->zqhnkgd', w2, eye, eye)
    w2_full = w2_full.reshape(n_layers, 4 * CMP_HIDDEN, 2 * NSA_KV_W)
    w2_k = w2_full[..., :NSA_KV_W].reshape(n_layers, 4 * CMP_HIDDEN, NSA_KV_HEADS, HEAD_DIM)
    w2_k = jnp.concatenate([w2_k, jnp.zeros_like(w2_k)], axis=-1).reshape(n_layers, 4 * CMP_HIDDEN, -1)
    w2_full = jnp.concatenate([w2_k, w2_full[..., NSA_KV_W:]], axis=-1).astype(BF16)
    pe = jnp.stack([pe_k, pe_v], axis=1)
    pe = jnp.broadcast_to(pe.transpose(0, 2, 1, 3)[:, :, :, None, :],
                          (n_layers, CMP_LEN, 2, NSA_KV_HEADS, HEAD_DIM))
    pe = pe.reshape(n_layers, 1, CMP_LEN * 2 * NSA_KV_W)
    return pe[:, :, :half], pe[:, :, half:], w1_full.reshape(n_layers, 2, half, 4 * CMP_HIDDEN), w2_full


def _rotary_tables(seq):
    half = HEAD_DIM // 2
    inv_freq = ROPE_THETA ** (-jnp.arange(half, dtype=F32) / half)
    ang = jnp.arange(seq).astype(F32)[:, None] * inv_freq[None, :]
    cos, sin = jnp.cos(ang), jnp.sin(ang)
    reps = LANES // HEAD_DIM
    return jnp.tile(jnp.concatenate([cos, cos], axis=-1), (1, reps)), \
        jnp.tile(jnp.concatenate([-sin, sin], axis=-1), (1, reps))


def _retention_tables():
    log_gamma = jnp.log1p(-jnp.exp2(-5.0 - jnp.arange(RET_HEADS, dtype=F32)))
    lg_lane = jnp.repeat(log_gamma, HEAD_DIM)[None, :]
    pos = (jnp.arange(LIN_BLOCK) % CHUNK).astype(F32)[:, None]
    qf = jnp.exp(lg_lane * (pos + 1.0))
    kf = jnp.exp(lg_lane * (CHUNK - 1.0 - pos))
    cd = jnp.exp(lg_lane * CHUNK)
    r = jnp.arange(LIN_BLOCK)
    rel = (r[:, None] - r[None, :]).astype(F32)
    ok = ((r[:, None] // CHUNK) == (r[None, :] // CHUNK)) & (rel >= 0)
    dm = jnp.where(ok[None], jnp.exp(log_gamma[:, None, None] * rel[None]), 0.0)
    dmat = dm.reshape(RET_W // LANES, 2 * LIN_BLOCK, LIN_BLOCK)
    return dmat, qf, kf, cd


def kernel(x, c, w_ada, b_ada, norm_g, ffn1_in, ffn1_out, w_in, gla_a2, gla_a_bias, gla_norm_g,
           nsa_pe_k, nsa_pe_v, nsa_w1_k, nsa_w2_k, nsa_w1_v, nsa_w2_v, nsa_gate_bias, ret_norm_g,
           w_out, ffn2_in, ffn2_out, final_norm_g):
    batch, seq, d = x.shape
    n_layers = w_ada.shape[0]
    assert batch <= 8 and seq % 512 == 0 and seq >= WINDOW + Q_BLOCK

    c_pad = jnp.zeros((8, d), F32).at[:batch].set(c)
    mod = _ada_call(c_pad, w_ada, b_ada).reshape(n_layers * 8 * N_ADA, 1, d)
    ng = norm_g.reshape(n_layers * 3, 1, d)

    w_proj, gate_bias = _layout_proj_weights(w_in, nsa_gate_bias)
    pe_top, pe_bot, w1_cmp, w2_cmp = _layout_cmp_weights(nsa_pe_k, nsa_pe_v, nsa_w1_k, nsa_w2_k, nsa_w1_v, nsa_w2_v)
    cosf, sinf = _rotary_tables(seq)
    dmat, qf, kf, cd = _retention_tables()
    a2_pad = jnp.zeros((n_layers, LANES, GLA_W), F32).at[:, :GLA_LOWRANK].set(gla_a2)
    a_bias = gla_a_bias.reshape(n_layers, 1, GLA_W)
    gla_g = jnp.tile(gla_norm_g, (1, GLA_HEADS)).reshape(n_layers, 1, GLA_W)
    ret_g = jnp.tile(ret_norm_g, (1, RET_HEADS)).reshape(n_layers, 1, RET_W)
    f1_in, f1_out = ffn1_in.astype(BF16), ffn1_out.astype(BF16)
    f2_in, f2_out = ffn2_in.astype(BF16), ffn2_out.astype(BF16)
    w_o = w_out.astype(BF16)

    xs = x.reshape(batch * seq, d)
    for l in range(n_layers):
        xs = _ffn_call(xs, ng, mod, f1_in, f1_out, layer=l, sub=0, seq=seq)
        (gla, lr, qraw, qrot, kcvc, ks, vs, kw, vw, gate, ret) = _proj_call(
            xs, ng, mod, w_proj, gate_bias, cosf, sinf, layer=l, seq=seq)
        kc, vc = _cmp_call(kcvc, pe_top, pe_bot, w1_cmp, w2_cmp, layer=l, batch=batch, seq=seq)
        o_gla = _gla_call(gla, lr, a2_pad, a_bias, gla_g, layer=l, batch=batch, seq=seq)
        o_ret = _ret_call(ret, dmat, qf, kf, cd, ret_g, layer=l, batch=batch, seq=seq)
        o_nsa = _nsa_call(qraw, qrot, kc, vc, ks, vs, kw, vw, gate, layer=l, batch=batch, seq=seq)
        xs = _out_call(xs, o_gla, o_nsa, o_ret, mod, w_o, layer=l, seq=seq)
        final_g = final_norm_g.reshape(1, d) if l == n_layers - 1 else None
        xs = _ffn_call(xs, ng, mod, f2_in, f2_out, layer=l, sub=2, seq=seq, final_g=final_g)
    return xs.reshape(batch, seq, d)
```

```python
import functools
import math

import numpy as np
import jax
import jax.numpy as jnp
from jax import lax
from jax.experimental import pallas as pl
from jax.experimental.pallas import tpu as pltpu

F32 = jnp.float32
BF16 = jnp.bfloat16

HEAD_DIM = 64
LANES = 128
GLA_HEADS = 4
NSA_HEADS = 8
NSA_KV_HEADS = 2
NSA_REP = NSA_HEADS // NSA_KV_HEADS
RET_HEADS = 4
GLA_LOWRANK = 16
GLA_TAU = 16.0
CHUNK = 64
CMP_LEN = 32
CMP_STRIDE = 16
CMP_HIDDEN = 128
SLC_LEN = 64
SLC_TOPK = 16
WINDOW = 512
Q_BLOCK = 128
ROPE_THETA = 10000.0
FFN_HALF = 0.5
NORM_EPS = 1e-6
NEG_INF = -1e30
FORCED_SCORE = 1e4
N_ADA = 9
QK_SCALE = HEAD_DIM ** -0.5
SLC_SHIFT = int(math.log2(SLC_LEN))
LOG2_E = math.log2(math.e)

GLA_W = GLA_HEADS * HEAD_DIM
NSA_W = NSA_HEADS * HEAD_DIM
NSA_KV_W = NSA_KV_HEADS * HEAD_DIM
RET_W = RET_HEADS * HEAD_DIM
IN_SIZES = (GLA_W, GLA_W, GLA_W, GLA_W, GLA_LOWRANK,
            NSA_W, NSA_KV_W, NSA_KV_W, NSA_KV_W, NSA_KV_W, NSA_KV_W, NSA_KV_W, 3 * NSA_HEADS,
            RET_W, RET_W, RET_W, RET_W)
IN_SPLITS = tuple(int(s) for s in np.cumsum(IN_SIZES)[:-1])

SEG_GLA = 0
SEG_LR = SEG_GLA + 4 * GLA_W
SEG_NQ = SEG_LR + LANES
SEG_KCVC = SEG_NQ + NSA_HEADS * LANES
SEG_KS = SEG_KCVC + 2 * NSA_KV_W
SEG_KW = SEG_KS + NSA_KV_HEADS * LANES
SEG_VS = SEG_KW + NSA_KV_HEADS * LANES
SEG_VW = SEG_VS + NSA_KV_W
SEG_GATE = SEG_VW + NSA_KV_W
SEG_RET = SEG_GATE + NSA_KV_HEADS * LANES
PROJ_W = SEG_RET + 4 * RET_W

VMEM_LIMIT_BYTES = 56 * 1024 * 1024

NT_DIMS = (((1,), (1,)), ((), ()))


def _dot(a, b):
    return jnp.dot(a, b, preferred_element_type=F32)


def _dot_nt(a, b):
    return lax.dot_general(a, b, NT_DIMS, preferred_element_type=F32)


def _split3(x):
    hi = x.astype(BF16)
    r1 = x - hi.astype(F32)
    mid = r1.astype(BF16)
    lo = (r1 - mid.astype(F32)).astype(BF16)
    return hi, mid, lo


def _dot3_exact_rhs(x, w_bf16):
    hi, mid, lo = _split3(x)
    return _dot(hi, w_bf16) + _dot(mid, w_bf16) + _dot(lo, w_bf16)


def _dot3_exact_lhs(w_bf16, x):
    hi, mid, lo = _split3(x)
    return _dot(w_bf16, hi) + _dot(w_bf16, mid) + _dot(w_bf16, lo)


def _sigmoid(x):
    return 1.0 / (1.0 + jnp.exp(-x))


def _silu(x):
    return x * _sigmoid(x)


def _iota(shape, dim):
    return lax.broadcasted_iota(jnp.int32, shape, dim)


def _params(n_grid):
    return pltpu.CompilerParams(dimension_semantics=("arbitrary",) * n_grid,
                                vmem_limit_bytes=VMEM_LIMIT_BYTES)


def _norm_mod(x, g, sh, sc):
    ms = jnp.mean(x * x, axis=-1, keepdims=True)
    return x * lax.rsqrt(ms + NORM_EPS) * (g * (1.0 + sc)) + sh


def _ada_kernel(c_ref, w_ref, b_ref, o_ref):
    c = c_ref[...]
    o_ref[...] = _dot(_silu(c).astype(BF16), w_ref[...].astype(BF16)) + b_ref[...]


def _ada_call(c_pad, w_ada, b_ada):
    n_layers, d, n_out = w_ada.shape
    tn = 2304 if n_out % 2304 == 0 else n_out
    return pl.pallas_call(
        _ada_kernel,
        grid=(n_layers, n_out // tn),
        in_specs=[pl.BlockSpec((8, d), lambda l, j: (0, 0)),
                  pl.BlockSpec((None, d, tn), lambda l, j: (l, 0, j)),
                  pl.BlockSpec((None, 1, tn), lambda l, j: (l, 0, j))],
        out_specs=pl.BlockSpec((None, 8, tn), lambda l, j: (l, 0, j)),
        out_shape=jax.ShapeDtypeStruct((n_layers, 8, n_out), F32),
        compiler_params=_params(2),
        name="ada_mod",
    )(c_pad, w_ada, b_ada.reshape(n_layers, 1, n_out))


def _ffn_kernel(x_ref, g_ref, sh_ref, sc_ref, gt_ref, win_ref, wout_ref, *rest, d_ff, chunks, final):
    if final:
        fg_ref, o_ref, act_ref = rest
    else:
        o_ref, act_ref = rest
    x = x_ref[...]
    h = _norm_mod(x, g_ref[...], sh_ref[...], sc_ref[...]).astype(BF16)
    for (c0, cw) in chunks:
        gate = _dot(h, win_ref[:, c0:c0 + cw])
        up = _dot(h, win_ref[:, d_ff + c0:d_ff + c0 + cw])
        act_ref[:, c0:c0 + cw] = (_silu(gate) * up).astype(BF16)
    y = _dot(act_ref[...], wout_ref[...])
    xn = x + (FFN_HALF * gt_ref[...]) * y
    if final:
        ms = jnp.mean(xn * xn, axis=-1, keepdims=True)
        xn = xn * lax.rsqrt(ms + NORM_EPS) * fg_ref[...]
    o_ref[...] = xn


def _ffn_call(x, ng, mod, w_in, w_out, *, layer, sub, seq, final_g=None, tm=1024):
    m, d = x.shape
    d_ff = w_out.shape[1]
    tiles_per_seq = seq // tm
    chunks, c0 = [], 0
    while c0 < d_ff:
        cw = min(512, d_ff - c0)
        chunks.append((c0, cw))
        c0 += cw

    def mod_spec(k):
        return pl.BlockSpec((None, 1, d), lambda i: ((layer * 8 + i // tiles_per_seq) * N_ADA + k, 0, 0))

    in_specs = [pl.BlockSpec((tm, d), lambda i: (i, 0)),
                pl.BlockSpec((None, 1, d), lambda i: (layer * 3 + sub, 0, 0)),
                mod_spec(3 * sub), mod_spec(3 * sub + 1), mod_spec(3 * sub + 2),
                pl.BlockSpec((None, d, 2 * d_ff), lambda i: (layer, 0, 0), pipeline_mode=pl.Buffered(1)),
                pl.BlockSpec((None, d_ff, d), lambda i: (layer, 0, 0), pipeline_mode=pl.Buffered(1))]
    args = [x, ng, mod, mod, mod, w_in, w_out]
    if final_g is not None:
        in_specs.append(pl.BlockSpec((1, d), lambda i: (0, 0)))
        args.append(final_g)
    return pl.pallas_call(
        functools.partial(_ffn_kernel, d_ff=d_ff, chunks=tuple(chunks), final=final_g is not None),
        grid=(m // tm,),
        in_specs=in_specs,
        out_specs=pl.BlockSpec((tm, d), lambda i: (i, 0)),
        out_shape=jax.ShapeDtypeStruct((m, d), F32),
        scratch_shapes=[pltpu.VMEM((tm, d_ff), BF16)],
        compiler_params=_params(1),
        name=f"ffn_l{layer}_s{sub}",
    )(*args)


def _rotate(x, cosf, sinf):
    rows, width = x.shape
    first = (_iota((rows, LANES), 1) & (HEAD_DIM - 1)) < HEAD_DIM // 2
    outs = []
    for j in range(width // LANES):
        xs = x[:, j * LANES:(j + 1) * LANES]
        swapped = jnp.where(first, pltpu.roll(xs, LANES - HEAD_DIM // 2, 1), pltpu.roll(xs, HEAD_DIM // 2, 1))
        outs.append(xs * cosf + swapped * sinf)
    return outs[0] if len(outs) == 1 else jnp.concatenate(outs, axis=1)


def _store_group_transposed(dst_ref, x, fill):
    top = _iota((LANES, LANES), 0) < HEAD_DIM
    for c in range(x.shape[0] // LANES):
        piece = jnp.transpose(x[c * LANES:(c + 1) * LANES])
        dst_ref[0, c] = jnp.where(top, piece, fill).astype(dst_ref.dtype)
        dst_ref[1, c] = jnp.where(top, pltpu.roll(piece, HEAD_DIM, 0), fill).astype(dst_ref.dtype)


def _proj_kernel(x_ref, g_ref, sh_ref, sc_ref, w_ref, gb_ref, cos_ref, sin_ref,
                 gla_ref, lr_ref, qraw_ref, qrot_ref, kcvc_ref, ks_ref, vst_ref, kw_ref, vwt_ref,
                 gate_ref, ret_ref, *, tiles_per_seq):
    h = _norm_mod(x_ref[...], g_ref[...], sh_ref[...], sc_ref[...]).astype(BF16)
    cosf = cos_ref[...]
    sinf = sin_ref[...]
    tm = h.shape[0]

    def seg(c0, width):
        return _dot(h, w_ref[:, c0:c0 + width])

    gla_ref[...] = seg(SEG_GLA, 4 * GLA_W)
    lr_ref[...] = seg(SEG_LR, LANES)
    nq = seg(SEG_NQ, NSA_HEADS * LANES)
    qraw_ref[...] = (nq * (QK_SCALE * LOG2_E)).astype(BF16)
    qrot_ref[...] = (_rotate(nq, cosf, sinf) * (QK_SCALE * LOG2_E)).astype(BF16)
    kcvc_ref[...] = seg(SEG_KCVC, 2 * NSA_KV_W)
    grp_w = NSA_KV_HEADS * LANES
    pos = (pl.program_id(0) % tiles_per_seq) * tm + _iota((tm, grp_w), 0)
    spare = (_iota((tm, grp_w), 1) & (LANES - 1)) - HEAD_DIM
    blk_onehot = jnp.where(spare == (pos >> SLC_SHIFT), 1.0, 0.0)
    ks_ref[...] = (_rotate(seg(SEG_KS, grp_w), cosf, sinf) + blk_onehot).astype(BF16)
    kw_ref[...] = _rotate(seg(SEG_KW, grp_w), cosf, sinf).astype(BF16)
    _store_group_transposed(vst_ref, seg(SEG_VS, NSA_KV_W), 1.0)
    _store_group_transposed(vwt_ref, seg(SEG_VW, NSA_KV_W), 1.0)
    gate_ref[...] = _sigmoid(seg(SEG_GATE, NSA_KV_HEADS * LANES) + gb_ref[...])
    ret_ref[:, 0:RET_W] = _rotate(seg(SEG_RET, RET_W), cosf, sinf)
    ret_ref[:, RET_W:2 * RET_W] = _rotate(seg(SEG_RET + RET_W, RET_W), cosf, sinf) * QK_SCALE
    ret_ref[:, 2 * RET_W:4 * RET_W] = seg(SEG_RET + 2 * RET_W, 2 * RET_W)


def _proj_call(x, ng, mod, w_proj, gate_bias, cosf, sinf, *, layer, seq, tm=512):
    m, d = x.shape
    tiles_per_seq = seq // tm

    def mod_spec(k):
        return pl.BlockSpec((None, 1, d), lambda i: ((layer * 8 + i // tiles_per_seq) * N_ADA + k, 0, 0))

    def row_spec(width):
        return pl.BlockSpec((tm, width), lambda i: (i, 0))

    batch = m // seq
    grp_w = NSA_KV_HEADS * LANES

    def vt_spec():
        return pl.BlockSpec((None, NSA_KV_HEADS, tm // LANES, LANES, LANES),
                            lambda i: (i // tiles_per_seq, 0, i % tiles_per_seq, 0, 0))

    vt_shape = jax.ShapeDtypeStruct((batch, NSA_KV_HEADS, seq // LANES, LANES, LANES), BF16)
    out_widths = [(4 * GLA_W, F32), (LANES, F32), (NSA_HEADS * LANES, BF16), (NSA_HEADS * LANES, BF16),
                  (2 * NSA_KV_W, F32), (grp_w, BF16), None, (grp_w, BF16), None,
                  (NSA_KV_HEADS * LANES, F32), (4 * RET_W, F32)]
    return pl.pallas_call(
        functools.partial(_proj_kernel, tiles_per_seq=tiles_per_seq),
        grid=(m // tm,),
        in_specs=[row_spec(d),
                  pl.BlockSpec((None, 1, d), lambda i: (layer * 3 + 1, 0, 0)),
                  mod_spec(3), mod_spec(4),
                  pl.BlockSpec((None, d, PROJ_W), lambda i: (layer, 0, 0), pipeline_mode=pl.Buffered(1)),
                  pl.BlockSpec((None, 1, NSA_KV_HEADS * LANES), lambda i: (layer, 0, 0)),
                  pl.BlockSpec((tm, LANES), lambda i: (i % tiles_per_seq, 0)),
                  pl.BlockSpec((tm, LANES), lambda i: (i % tiles_per_seq, 0))],
        out_specs=[vt_spec() if ow is None else row_spec(ow[0]) for ow in out_widths],
        out_shape=[vt_shape if ow is None else jax.ShapeDtypeStruct((m, ow[0]), ow[1]) for ow in out_widths],
        compiler_params=_params(1),
        name=f"proj_l{layer}",
    )(x, ng, mod, mod, w_proj, gate_bias, cosf, sinf)


def _cmp_kernel(xk_ref, xv_ref, pet_ref, peb_ref, wt_ref, wb_ref, w2_ref, kc_ref, vct_ref):
    n_rows = xk_ref.shape[0] // CMP_STRIDE
    x = jnp.concatenate([ref[pl.ds(l, n_rows, stride=CMP_STRIDE), :]
                         for l in range(CMP_STRIDE) for ref in (xk_ref, xv_ref)], axis=1)
    a = _dot((x + pet_ref[...]).astype(BF16), wt_ref[...])
    b = _dot((x + peb_ref[...]).astype(BF16), wb_ref[...])
    hid = a + pltpu.roll(b, n_rows - 1, 0)
    out = _dot(_silu(hid).astype(BF16), w2_ref[...])
    out = jnp.where(_iota(out.shape, 0) < n_rows - 1, out, 0.0)
    grp_w = NSA_KV_HEADS * LANES
    kc_ref[...] = out[:, 0:grp_w].astype(BF16)
    _store_group_transposed(vct_ref, out[:, grp_w:grp_w + NSA_KV_W], 0.0)


def _cmp_call(kcvc, pe_top, pe_bot, w1_halves, w2, *, layer, batch, seq):
    rows = seq // CMP_STRIDE
    width = CMP_STRIDE * 2 * NSA_KV_W
    x = kcvc.reshape(batch, seq, 2 * NSA_KV_W)
    hid_w = 4 * CMP_HIDDEN
    grp_w = NSA_KV_HEADS * LANES

    def wspec(shape):
        return pl.BlockSpec((None,) + shape, lambda b: (layer, 0, 0))

    return pl.pallas_call(
        _cmp_kernel,
        grid=(batch,),
        in_specs=[pl.BlockSpec((None, seq, NSA_KV_W), lambda b: (b, 0, 0)),
                  pl.BlockSpec((None, seq, NSA_KV_W), lambda b: (b, 0, 1)),
                  wspec((1, width)), wspec((1, width)),
                  pl.BlockSpec((None, None, width, hid_w), lambda b: (layer, 0, 0, 0)),
                  pl.BlockSpec((None, None, width, hid_w), lambda b: (layer, 1, 0, 0)),
                  wspec((hid_w, grp_w + NSA_KV_W))],
        out_specs=[pl.BlockSpec((None, rows, grp_w), lambda b: (b, 0, 0)),
                   pl.BlockSpec((None, NSA_KV_HEADS, rows // LANES, LANES, LANES), lambda b: (b, 0, 0, 0, 0))],
        out_shape=[jax.ShapeDtypeStruct((batch, rows, grp_w), BF16),
                   jax.ShapeDtypeStruct((batch, NSA_KV_HEADS, rows // LANES, LANES, LANES), BF16)],
        compiler_params=_params(1),
        name=f"nsa_compress_l{layer}",
    )(x, x, pe_top, pe_bot, w1_halves, w1_halves, w2)


def _heads_tiled(x):
    return jnp.concatenate([x] * NSA_REP, axis=1)


SLC_UNROLL = 2

def _q_transposed(x):
    x = x.astype(F32)
    return jnp.concatenate([jnp.transpose(x[:, r * LANES:(r + 1) * LANES])[0:HEAD_DIM, :] for r in range(NSA_REP)],
                           axis=1)


def _nsa_compressed(s_c, vc_t, t0, tq, seq):
    n_cp = seq // CMP_STRIDE
    n_cmp = (seq - CMP_LEN) // CMP_STRIDE + 1
    nn = _iota((n_cp, tq), 0)
    tt = t0 + _iota((n_cp, tq), 1)
    bias_c = jnp.where((nn * CMP_STRIDE + (CMP_LEN - 1) <= tt) & (nn < n_cmp), 0.0, NEG_INF)
    has_cmp = t0 + _iota((1, tq), 1) >= CMP_LEN - 1
    p_heads = []
    for r in range(NSA_REP):
        s = s_c[:, r * tq:(r + 1) * tq] + bias_c
        e = jnp.exp2(s - jnp.max(s, axis=0, keepdims=True))
        p_heads.append(e * jnp.where(has_cmp, 1.0 / jnp.sum(e, axis=0, keepdims=True), 0.0))
    o_cmp = _dot(vc_t, jnp.concatenate(p_heads, axis=1).astype(BF16))
    p_sum = p_heads[0]
    for r in range(1, NSA_REP):
        p_sum = p_sum + p_heads[r]
    return o_cmp, p_sum


def _nsa_selection_rows(p_sum, imp_ref, t0, tq, seq):
    n_cp = seq // CMP_STRIDE
    n_cmp = (seq - CMP_LEN) // CMP_STRIDE + 1
    n_slc = seq // SLC_LEN
    n_sel = min(SLC_TOPK, n_slc)
    sb = _iota((LANES, n_cp), 0)
    cb = _iota((LANES, n_cp), 1)
    overlap_t = ((cb * CMP_STRIDE < sb * SLC_LEN + SLC_LEN) & (cb * CMP_STRIDE + CMP_LEN > sb * SLC_LEN)
                 & (sb < n_slc) & (cb < n_cmp))
    imp_t = _dot3_exact_lhs(jnp.where(overlap_t, 1.0, 0.0).astype(BF16), p_sum)
    blk = _iota((LANES, tq), 0)
    cur = (t0 + _iota((LANES, tq), 1)) >> SLC_SHIFT
    forced = (blk == 0) | (blk == cur) | (blk == cur - 1)
    imp_ref[...] = jnp.where(forced, FORCED_SCORE, jnp.where(blk <= cur, imp_t, -1.0))

    def ranked_rows(n_rows):
        if n_rows <= n_sel:
            return jnp.zeros((HEAD_DIM, tq), F32)
        n_grp = n_rows // 8
        vals = [imp_ref[8 * j:8 * (j + 1), :] for j in range(n_grp)]
        ranks = [jnp.zeros((8, tq), F32) for _ in range(n_grp)]
        sub = _iota((8, tq), 0)
        for i in range(n_rows):
            row = jnp.broadcast_to(imp_ref[i:i + 1, :], (8, tq))
            for j in range(n_grp):
                if 8 * j > i:
                    beats = jnp.where(row >= vals[j], 1.0, 0.0)
                elif 8 * j + 7 <= i:
                    beats = jnp.where(row > vals[j], 1.0, 0.0)
                else:
                    beats = jnp.where(sub + 8 * j > i, jnp.where(row >= vals[j], 1.0, 0.0),
                                      jnp.where(row > vals[j], 1.0, 0.0))
                ranks[j] = ranks[j] + beats
        return jnp.concatenate([jnp.where(rk < n_sel, 0.0, NEG_INF) for rk in ranks]
                               + [jnp.zeros((HEAD_DIM - n_rows, tq), F32)] * (n_rows < HEAD_DIM), axis=0)

    return ranked_rows(n_slc)


def _window_scores(q_rot, kw_ref, g, qb, tq):
    n_wb = WINDOW // tq + 1
    ki = _iota((tq, tq), 0)
    ci = _iota((tq, tq), 1)
    blocks = []
    for j in range(n_wb):
        kb = qb - (n_wb - 1) + j
        kb_read = jnp.maximum(kb, 0)
        s = _dot(kw_ref[pl.ds(pl.multiple_of(kb_read * tq, tq), tq), g * LANES:(g + 1) * LANES], q_rot)
        if j == 0:
            s = s + _heads_tiled(jnp.where(ki > ci, 0.0, NEG_INF))
        if j == n_wb - 1:
            s = s + _heads_tiled(jnp.where(ki <= ci, 0.0, NEG_INF))
        else:
            s = s + jnp.where(kb >= 0, 0.0, NEG_INF)
        blocks.append((s, kb_read))
    return blocks


def _window_finish(blocks, vwt_ref, g, tq):
    m_w = blocks[0][0].max(axis=0, keepdims=True)
    for s, _ in blocks[1:]:
        m_w = jnp.maximum(m_w, s.max(axis=0, keepdims=True))
    o_aug = jnp.zeros((LANES, NSA_REP * tq), F32)
    for s, kb_read in blocks:
        o_aug = o_aug + _dot(vwt_ref[g, kb_read], jnp.exp2(s - m_w).astype(BF16))
    return o_aug[0:HEAD_DIM] * (1.0 / o_aug[HEAD_DIM:HEAD_DIM + 1])


def _nsa_kernel(qraw_ref, qrot_ref, kc_ref, vct_ref, ks_ref, vst_ref, kw_ref, vwt_ref, gate_ref,
                o_ref, imp_ref, *, seq, tq, tk):
    qb = pl.program_id(1)
    t0 = qb * tq
    groups = range(NSA_KV_HEADS)
    n_cols = NSA_REP * tq

    def grp(g):
        return slice(g * LANES, (g + 1) * LANES)

    def q_cols(g):
        return slice(g * NSA_REP * LANES, (g + 1) * NSA_REP * LANES)

    def q_aug(q_t, extra_rows):
        return jnp.concatenate([q_t, extra_rows], axis=0).astype(BF16)

    no_extra = jnp.zeros((HEAD_DIM, n_cols), F32)
    q_rot_t = [_q_transposed(qrot_ref[:, q_cols(g)]) for g in groups]
    s_cmp = [_dot(kc_ref[:, grp(g)], q_aug(_q_transposed(qraw_ref[:, q_cols(g)]), no_extra)) for g in groups]
    win_blocks = [_window_scores(q_aug(q_rot_t[g], no_extra), kw_ref, g, qb, tq) for g in groups]
    o_cmp, q_sel = [], []
    for g in groups:
        vc_t = jnp.concatenate([vct_ref[g, c] for c in range(seq // CMP_STRIDE // LANES)], axis=1)
        o_c, p_sum = _nsa_compressed(s_cmp[g], vc_t, t0, tq, seq)
        o_cmp.append(o_c[0:HEAD_DIM])
        sel_neg = _heads_tiled(_nsa_selection_rows(p_sum, imp_ref.at[g], t0, tq, seq))
        q_sel.append(q_aug(q_rot_t[g], sel_neg))
    o_win = [_window_finish(win_blocks[g], vwt_ref, g, tq) for g in groups]

    blocks_per_tile = tk // LANES
    n_full = t0 // tk
    n_iter = (n_full + SLC_UNROLL - 1) // SLC_UNROLL

    def k_tile(kt, g):
        return ks_ref[pl.ds(pl.multiple_of(kt * tk, tk), tk), grp(g)]

    def v_tile(kt, g):
        return jnp.concatenate([vst_ref[g, kt * blocks_per_tile + c] for c in range(blocks_per_tile)], axis=1)

    def online_update(scores, values, m_i, acc):
        m_new = m_i
        for s in scores:
            m_new = jnp.maximum(m_new, s.max(axis=0, keepdims=True))
        p = jnp.concatenate([jnp.exp2(s - m_new).astype(BF16) for s in scores], axis=0)
        return m_new, jnp.exp2(m_i - m_new) * acc + _dot(jnp.concatenate(values, axis=1), p)

    def slc_step(j, stats):
        tiles = [(jnp.minimum(j * SLC_UNROLL + u, n_full - 1), j * SLC_UNROLL + u < n_full)
                 for u in range(SLC_UNROLL)]
        scores = [[_dot(k_tile(kt, g), q_sel[g]) for kt, _ in tiles] for g in groups]
        out = []
        for g in groups:
            values = [jnp.where(live, v_tile(kt, g), jnp.zeros((LANES, tk), BF16)) for kt, live in tiles]
            out.append(online_update(scores[g], values, *stats[g]))
        return tuple(out)

    causal = n_full * tk + _iota((tk, tq), 0) <= t0 + _iota((tk, tq), 1)
    causal_bias = _heads_tiled(jnp.where(causal, 0.0, NEG_INF))
    s_diag = [_dot(k_tile(n_full, g), q_sel[g]) for g in groups]
    init = tuple(online_update([s_diag[g] + causal_bias], [v_tile(n_full, g)],
                               jnp.full((1, n_cols), NEG_INF, F32), jnp.zeros((LANES, n_cols), F32))
                 for g in groups)
    stats = lax.fori_loop(0, n_iter, slc_step, init)

    gate_all = gate_ref[...]
    for g in groups:
        acc = stats[g][1]
        o_slc = acc[0:HEAD_DIM] * (1.0 / acc[HEAD_DIM:HEAD_DIM + 1])
        gate_t = jnp.transpose(gate_all[:, grp(g)])
        o_heads = []
        for r in range(NSA_REP):
            cols = slice(r * tq, (r + 1) * tq)
            o_heads.append(gate_t[3 * r:3 * r + 1] * o_cmp[g][:, cols] + gate_t[3 * r + 1:3 * r + 2] * o_slc[:, cols]
                           + gate_t[3 * r + 2:3 * r + 3] * o_win[g][:, cols])
        for pair in range(NSA_REP // 2):
            slab = jnp.transpose(jnp.concatenate(o_heads[2 * pair:2 * pair + 2], axis=0))
            c0 = (g * NSA_REP // 2 + pair) * LANES
            o_ref[:, c0:c0 + LANES] = slab.astype(BF16)


def _nsa_call(qraw, qrot, kc, vct, ks, vst, kw, vwt, gate, *, layer, batch, seq, tq=Q_BLOCK, tk=512):
    assert tq == LANES and tk % tq == 0 and seq % tk == 0 and seq // SLC_LEN <= HEAD_DIM
    nq = seq // tq
    n_cp = seq // CMP_STRIDE
    grp_w = NSA_KV_HEADS * LANES

    def q_spec():
        return pl.BlockSpec((tq, NSA_HEADS * LANES), lambda b, i: (b * nq + i, 0))

    def k_spec(rows):
        return pl.BlockSpec((None, rows, grp_w), lambda b, i: (b, 0, 0))

    def vt_spec():
        return pl.BlockSpec((None, NSA_KV_HEADS, seq // LANES, LANES, LANES), lambda b, i: (b, 0, 0, 0, 0))

    return pl.pallas_call(
        functools.partial(_nsa_kernel, seq=seq, tq=tq, tk=tk),
        grid=(batch, nq),
        in_specs=[q_spec(), q_spec(), k_spec(n_cp),
                  pl.BlockSpec((None, NSA_KV_HEADS, n_cp // LANES, LANES, LANES), lambda b, i: (b, 0, 0, 0, 0)),
                  k_spec(seq), vt_spec(), k_spec(seq), vt_spec(),
                  pl.BlockSpec((tq, grp_w), lambda b, i: (b * nq + i, 0))],
        out_specs=pl.BlockSpec((tq, NSA_W), lambda b, i: (b * nq + i, 0)),
        out_shape=jax.ShapeDtypeStruct((batch * seq, NSA_W), BF16),
        scratch_shapes=[pltpu.VMEM((NSA_KV_HEADS, LANES, tq), F32)],
        compiler_params=_params(2),
        name=f"nsa_attn_l{layer}",
    )(qraw, qrot, kc, vct, ks.reshape(batch, seq, grp_w), vst, kw.reshape(batch, seq, grp_w), vwt, gate)


LIN_BLOCK = 2 * CHUNK


def _lin_block_consts():
    r = _iota((LIN_BLOCK, LANES), 0)
    c = _iota((LIN_BLOCK, LANES), 1)
    same_chunk = (r >= CHUNK) == (c >= CHUNK)
    causal = same_chunk & (r >= c)
    head_diag = same_chunk
    first_rows = r < CHUNK
    low_lanes = c < HEAD_DIM
    return causal, head_diag, first_rows, low_lanes


def _head_stats(x, head_diag_mean):
    hi = x.astype(BF16)
    lo = (x - hi.astype(F32)).astype(BF16)
    return _dot(hi, head_diag_mean) + _dot(lo, head_diag_mean)


def _lin_core(chains, consts):
    causal, head_diag, first_rows, low_lanes = consts
    mask2 = jnp.concatenate([causal, causal], axis=0)
    scores, updates = [], []
    for c in chains:
        zero = jnp.zeros_like(c["q_dec"])
        q2 = jnp.concatenate([jnp.where(low_lanes, c["q_dec"], zero), jnp.where(low_lanes, zero, c["q_dec"])], axis=0)
        scores.append(_dot_nt(q2.astype(BF16), c["k_inv"].astype(BF16)))
    for c in chains:
        v_t = jnp.transpose(c["v"]).astype(BF16)
        kd_ab = jnp.concatenate([jnp.where(first_rows, c["k_dec"], 0.0), jnp.where(first_rows, 0.0, c["k_dec"])],
                                axis=1)
        updates.append(_dot(v_t, kd_ab.astype(BF16)))
    intra, states = [], []
    for c, a2 in zip(chains, scores):
        a2 = jnp.where(mask2, a2, 0.0)
        if c["intra_scale"] is not None:
            a2 = a2 * c["intra_scale"]
        o2 = _dot(a2.astype(BF16), c["v"].astype(BF16))
        intra.append(jnp.where(low_lanes, o2[0:LIN_BLOCK], o2[LIN_BLOCK:]))
    out = []
    for c, upd, o_intra in zip(chains, updates, intra):
        st_a = c["st_prev"] * c["dec_a"] + jnp.where(head_diag, upd[:, 0:LANES], 0.0)
        st_b = st_a * c["dec_b"] + jnp.where(head_diag, upd[:, LANES:], 0.0)
        o_ab = _dot_nt(c["q_int"].astype(BF16), jnp.concatenate([c["st_prev"], st_a], axis=0).astype(BF16))
        out.append((o_intra + jnp.where(first_rows, o_ab[:, 0:LANES], o_ab[:, LANES:]), st_b))
    return out


def _gla_kernel(x_ref, lr_ref, a2_ref, ab_ref, ng_ref, o_ref, st_ref, *, n_blocks, batch):
    @pl.when(pl.program_id(0) == 0)
    def _():
        st_ref[...] = jnp.zeros_like(st_ref)

    consts = _lin_block_consts()
    causal, head_diag, first_rows, low_lanes = consts
    ri = _iota((LIN_BLOCK, LANES), 0)
    ci = _iota((LIN_BLOCK, LANES), 1)
    tri_t = jnp.where(((ri >= CHUNK) == (ci >= CHUNK)) & (ri <= ci), 1.0, 0.0).astype(BF16)
    mean_op = jnp.where(head_diag, 1.0 / HEAD_DIM, 0.0).astype(BF16)
    a2h = a2_ref[...].astype(BF16)
    a2l = (a2_ref[...] - a2h.astype(F32)).astype(BF16)

    def block(i):
        rows = pl.ds(i * LIN_BLOCK, LIN_BLOCK)
        lr = jnp.concatenate([lr_ref[bi, rows, :] for bi in range(batch)], axis=0)
        lh = lr.astype(BF16)
        ll = (lr - lh.astype(F32)).astype(BF16)
        z = _dot(lh, a2h) + _dot(lh, a2l) + _dot(ll, a2h) + ab_ref[...]
        log_a = (jnp.minimum(z, 0.0) - jnp.log(1.0 + jnp.exp(-jnp.abs(z)))) * (1.0 / GLA_TAU)
        n_slab = GLA_W // LANES
        la_t = jnp.concatenate([jnp.transpose(log_a[bi * LIN_BLOCK:(bi + 1) * LIN_BLOCK, s * LANES:(s + 1) * LANES])
                                for bi in range(batch) for s in range(n_slab)], axis=0)
        cum_t = _dot3_exact_rhs(la_t, tri_t)
        chains = []
        for bi in range(batch):
            for s in range(n_slab):
                cols = slice(s * LANES, (s + 1) * LANES)
                n = bi * n_slab + s

                def ld(k):
                    return x_ref[bi, rows, k * GLA_W + s * LANES:k * GLA_W + (s + 1) * LANES]

                q, k, v, gg = ld(0), ld(1), ld(2), ld(3)
                b = jnp.transpose(cum_t[n * LANES:(n + 1) * LANES])
                bl_a = b[CHUNK - 1:CHUNK, :]
                bl_b = b[LIN_BLOCK - 1:LIN_BLOCK, :]
                bl = jnp.where(first_rows, bl_a, bl_b)
                q_dec = q * QK_SCALE * jnp.exp(b)
                chains.append(dict(bi=bi, s=s, cols=cols, gate=gg, q_dec=q_dec, q_int=q_dec, k_inv=k * jnp.exp(-b),
                                   k_dec=k * jnp.exp(bl - b), v=v, intra_scale=None, dec_a=jnp.exp(bl_a),
                                   dec_b=jnp.exp(bl_b), st_prev=st_ref[bi, s]))
        results = _lin_core(chains, consts)
        for c, (_, st_new) in zip(chains, results):
            st_ref[c["bi"], c["s"]] = st_new
        ms = _head_stats(jnp.concatenate([o * o for o, _ in results], axis=0), mean_op)
        for n, (c, (o, _)) in enumerate(zip(chains, results)):
            y = (o * lax.rsqrt(ms[n * LIN_BLOCK:(n + 1) * LIN_BLOCK] + NORM_EPS) * ng_ref[:, c["cols"]]
                 * _silu(c["gate"]))
            o_ref[c["bi"], rows, c["cols"]] = y.astype(BF16)

    for i in range(n_blocks):
        block(i)


def _ret_kernel(x_ref, dmat_ref, qf_ref, kf_ref, cd_ref, ng_ref, o_ref, st_ref, *, n_blocks, batch):
    @pl.when(pl.program_id(0) == 0)
    def _():
        st_ref[...] = jnp.zeros_like(st_ref)

    consts = _lin_block_consts()
    causal, head_diag, first_rows, low_lanes = consts
    mean_op = jnp.where(head_diag, 1.0 / HEAD_DIM, 0.0).astype(BF16)

    def block(i):
        rows = pl.ds(i * LIN_BLOCK, LIN_BLOCK)
        chains = []
        for bi in range(batch):
            for s in range(RET_W // LANES):
                cols = slice(s * LANES, (s + 1) * LANES)

                def ld(k):
                    return x_ref[bi, rows, k * RET_W + s * LANES:k * RET_W + (s + 1) * LANES]

                q, k, v, gg = ld(0), ld(1), ld(2), ld(3)
                cd = cd_ref[:, cols]
                chains.append(dict(bi=bi, s=s, cols=cols, gate=gg, q_dec=q, q_int=q * qf_ref[:, cols], k_inv=k,
                                   k_dec=k * kf_ref[:, cols], v=v, intra_scale=dmat_ref[s], dec_a=cd, dec_b=cd,
                                   st_prev=st_ref[bi, s]))
        results = _lin_core(chains, consts)
        for c, (_, st_new) in zip(chains, results):
            st_ref[c["bi"], c["s"]] = st_new
        o_all = jnp.concatenate([o for o, _ in results], axis=0)
        oc_all = o_all - _head_stats(o_all, mean_op)
        var_all = _head_stats(oc_all * oc_all, mean_op)
        y_all = oc_all * lax.rsqrt(var_all + NORM_EPS)
        for n, c in enumerate(chains):
            y = y_all[n * LIN_BLOCK:(n + 1) * LIN_BLOCK] * ng_ref[:, c["cols"]] * _silu(c["gate"])
            o_ref[c["bi"], rows, c["cols"]] = y.astype(BF16)

    for i in range(n_blocks):
        block(i)


def _gla_call(gla, lr, a2_pad, a_bias, norm_g, *, layer, batch, seq, tb=LIN_BLOCK):
    n_slab = GLA_W // LANES

    def lspec(shape):
        return pl.BlockSpec((None,) + shape, lambda t: (layer, 0, 0))

    return pl.pallas_call(
        functools.partial(_gla_kernel, n_blocks=tb // LIN_BLOCK, batch=batch),
        grid=(seq // tb,),
        in_specs=[pl.BlockSpec((batch, tb, 4 * GLA_W), lambda t: (0, t, 0)),
                  pl.BlockSpec((batch, tb, LANES), lambda t: (0, t, 0)),
                  lspec((LANES, GLA_W)), lspec((1, GLA_W)), lspec((1, GLA_W))],
        out_specs=pl.BlockSpec((batch, tb, GLA_W), lambda t: (0, t, 0)),
        out_shape=jax.ShapeDtypeStruct((batch, seq, GLA_W), BF16),
        scratch_shapes=[pltpu.VMEM((batch, n_slab, LANES, LANES), F32)],
        compiler_params=_params(1),
        name=f"gla_l{layer}",
    )(gla.reshape(batch, seq, 4 * GLA_W), lr.reshape(batch, seq, LANES), a2_pad, a_bias, norm_g
      ).reshape(batch * seq, GLA_W)


def _ret_call(ret, dmat, qf, kf, cd, norm_g, *, layer, batch, seq, tb=LIN_BLOCK):
    n_slab = RET_W // LANES
    full = lambda a: pl.BlockSpec(a.shape, lambda t: (0,) * a.ndim)
    return pl.pallas_call(
        functools.partial(_ret_kernel, n_blocks=tb // LIN_BLOCK, batch=batch),
        grid=(seq // tb,),
        in_specs=[pl.BlockSpec((batch, tb, 4 * RET_W), lambda t: (0, t, 0)),
                  full(dmat), full(qf), full(kf), full(cd),
                  pl.BlockSpec((None, 1, RET_W), lambda t: (layer, 0, 0))],
        out_specs=pl.BlockSpec((batch, tb, RET_W), lambda t: (0, t, 0)),
        out_shape=jax.ShapeDtypeStruct((batch, seq, RET_W), BF16),
        scratch_shapes=[pltpu.VMEM((batch, n_slab, LANES, LANES), F32)],
        compiler_params=_params(1),
        name=f"ret_l{layer}",
    )(ret.reshape(batch, seq, 4 * RET_W), dmat, qf, kf, cd, norm_g).reshape(batch * seq, RET_W)


def _out_kernel(x_ref, og_ref, on_ref, or_ref, gt_ref, w_ref, o_ref):
    y = (_dot(og_ref[...], w_ref[0:GLA_W, :])
         + _dot(on_ref[...], w_ref[GLA_W:GLA_W + NSA_W, :])
         + _dot(or_ref[...], w_ref[GLA_W + NSA_W:, :]))
    o_ref[...] = x_ref[...] + gt_ref[...] * y


def _out_call(x, o_gla, o_nsa, o_ret, mod, w_out, *, layer, seq, tm=1024):
    m, d = x.shape
    tiles_per_seq = seq // tm
    row = lambda w: pl.BlockSpec((tm, w), lambda i: (i, 0))
    return pl.pallas_call(
        _out_kernel,
        grid=(m // tm,),
        in_specs=[row(d), row(GLA_W), row(NSA_W), row(RET_W),
                  pl.BlockSpec((None, 1, d), lambda i: ((layer * 8 + i // tiles_per_seq) * N_ADA + 5, 0, 0)),
                  pl.BlockSpec((None, GLA_W + NSA_W + RET_W, d), lambda i: (layer, 0, 0))],
        out_specs=row(d),
        out_shape=jax.ShapeDtypeStruct((m, d), F32),
        compiler_params=_params(1),
        name=f"out_proj_l{layer}",
    )(x, o_gla, o_nsa, o_ret, mod, w_out)


def _layout_proj_weights(w_in, nsa_gate_bias):
    n_layers, d, _ = w_in.shape
    w_in = w_in.astype(BF16)
    (g_q, g_k, g_v, g_g, g_lr, n_q, n_kc, n_vc, n_ks, n_vs, n_kw, n_vw, n_gate,
     r_q, r_k, r_v, r_g) = jnp.split(w_in, IN_SPLITS, axis=-1)
    zeros = lambda w: jnp.zeros((n_layers, d, w), w_in.dtype)
    def pad_heads(w, n_heads):
        w = w.reshape(n_layers, d, n_heads, HEAD_DIM)
        return jnp.concatenate([w, jnp.zeros_like(w)], axis=-1).reshape(n_layers, d, n_heads * LANES)

    per_grp = 3 * NSA_REP
    gate_cols = []
    bias_cols = []
    for g in range(NSA_KV_HEADS):
        gate_cols += [n_gate[..., g * per_grp:(g + 1) * per_grp], zeros(LANES - per_grp)]
        bias_cols += [nsa_gate_bias[:, g * per_grp:(g + 1) * per_grp],
                      jnp.zeros((n_layers, LANES - per_grp), nsa_gate_bias.dtype)]
    w = jnp.concatenate([g_q, g_k, g_v, g_g, g_lr, zeros(LANES - GLA_LOWRANK), pad_heads(n_q, NSA_HEADS),
                         n_kc, n_vc, pad_heads(n_ks, NSA_KV_HEADS), pad_heads(n_kw, NSA_KV_HEADS), n_vs, n_vw]
                        + gate_cols + [r_q, r_k, r_v, r_g], axis=-1)
    assert w.shape[-1] == PROJ_W
    bias = jnp.concatenate(bias_cols, axis=-1).reshape(n_layers, 1, NSA_KV_HEADS * LANES)
    return w.astype(BF16), bias


def _layout_cmp_weights(pe_k, pe_v, w1_k, w2_k, w1_v, w2_v):
    n_layers = pe_k.shape[0]
    eye = jnp.eye(2, dtype=BF16)
    w1 = jnp.stack([w1_k, w1_v], axis=1).reshape(n_layers, 2, CMP_LEN, HEAD_DIM, CMP_HIDDEN).astype(BF16)
    blocks = [jnp.pad(w1[:, kv], ((0, 0), (0, 0), (0, 0), ((2 * kv + g) * CMP_HIDDEN, (3 - 2 * kv - g) * CMP_HIDDEN)))
              for kv in range(2) for g in range(NSA_KV_HEADS)]
    w1_full = jnp.stack(blocks, axis=2).reshape(n_layers, CMP_LEN * 2 * NSA_KV_W, 4 * CMP_HIDDEN)
    half = CMP_STRIDE * 2 * NSA_KV_W
    w2 = jnp.stack([w2_k, w2_v], axis=1).astype(BF16)
    w2_full = jnp.einsum('zknd,kq,gh->zqhnkgd', w2, eye, eye)
    w2_full = w2_full.reshape(n_layers, 4 * CMP_HIDDEN, 2 * NSA_KV_W)
    w2_k = w2_full[..., :NSA_KV_W].reshape(n_layers, 4 * CMP_HIDDEN, NSA_KV_HEADS, HEAD_DIM)
    w2_k = jnp.concatenate([w2_k, jnp.zeros_like(w2_k)], axis=-1).reshape(n_layers, 4 * CMP_HIDDEN, -1)
    w2_full = jnp.concatenate([w2_k, w2_full[..., NSA_KV_W:]], axis=-1).astype(BF16)
    pe = jnp.stack([pe_k, pe_v], axis=1)
    pe = jnp.broadcast_to(pe.transpose(0, 2, 1, 3)[:, :, :, None, :],
                          (n_layers, CMP_LEN, 2, NSA_KV_HEADS, HEAD_DIM))
    pe = pe.reshape(n_layers, 1, CMP_LEN * 2 * NSA_KV_W)
    return pe[:, :, :half], pe[:, :, half:], w1_full.reshape(n_layers, 2, half, 4 * CMP_HIDDEN), w2_full


def _rotary_tables(seq):
    half = HEAD_DIM // 2
    inv_freq = ROPE_THETA ** (-jnp.arange(half, dtype=F32) / half)
    ang = jnp.arange(seq).astype(F32)[:, None] * inv_freq[None, :]
    cos, sin = jnp.cos(ang), jnp.sin(ang)
    reps = LANES // HEAD_DIM
    return jnp.tile(jnp.concatenate([cos, cos], axis=-1), (1, reps)), \
        jnp.tile(jnp.concatenate([-sin, sin], axis=-1), (1, reps))


def _retention_tables():
    log_gamma = jnp.log1p(-jnp.exp2(-5.0 - jnp.arange(RET_HEADS, dtype=F32)))
    lg_lane = jnp.repeat(log_gamma, HEAD_DIM)[None, :]
    pos = (jnp.arange(LIN_BLOCK) % CHUNK).astype(F32)[:, None]
    qf = jnp.exp(lg_lane * (pos + 1.0))
    kf = jnp.exp(lg_lane * (CHUNK - 1.0 - pos))
    cd = jnp.exp(lg_lane * CHUNK)
    r = jnp.arange(LIN_BLOCK)
    rel = (r[:, None] - r[None, :]).astype(F32)
    ok = ((r[:, None] // CHUNK) == (r[None, :] // CHUNK)) & (rel >= 0)
    dm = jnp.where(ok[None], jnp.exp(log_gamma[:, None, None] * rel[None]), 0.0)
    dmat = dm.reshape(RET_W // LANES, 2 * LIN_BLOCK, LIN_BLOCK)
    return dmat, qf, kf, cd


def kernel(x, c, w_ada, b_ada, norm_g, ffn1_in, ffn1_out, w_in, gla_a2, gla_a_bias, gla_norm_g,
           nsa_pe_k, nsa_pe_v, nsa_w1_k, nsa_w2_k, nsa_w1_v, nsa_w2_v, nsa_gate_bias, ret_norm_g,
           w_out, ffn2_in, ffn2_out, final_norm_g):
    batch, seq, d = x.shape
    n_layers = w_ada.shape[0]
    assert batch <= 8 and seq % 512 == 0 and seq >= WINDOW + Q_BLOCK

    c_pad = jnp.zeros((8, d), F32).at[:batch].set(c)
    mod = _ada_call(c_pad, w_ada, b_ada).reshape(n_layers * 8 * N_ADA, 1, d)
    ng = norm_g.reshape(n_layers * 3, 1, d)

    w_proj, gate_bias = _layout_proj_weights(w_in, nsa_gate_bias)
    pe_top, pe_bot, w1_cmp, w2_cmp = _layout_cmp_weights(nsa_pe_k, nsa_pe_v, nsa_w1_k, nsa_w2_k, nsa_w1_v, nsa_w2_v)
    cosf, sinf = _rotary_tables(seq)
    dmat, qf, kf, cd = _retention_tables()
    a2_pad = jnp.zeros((n_layers, LANES, GLA_W), F32).at[:, :GLA_LOWRANK].set(gla_a2)
    a_bias = gla_a_bias.reshape(n_layers, 1, GLA_W)
    gla_g = jnp.tile(gla_norm_g, (1, GLA_HEADS)).reshape(n_layers, 1, GLA_W)
    ret_g = jnp.tile(ret_norm_g, (1, RET_HEADS)).reshape(n_layers, 1, RET_W)
    f1_in, f1_out = ffn1_in.astype(BF16), ffn1_out.astype(BF16)
    f2_in, f2_out = ffn2_in.astype(BF16), ffn2_out.astype(BF16)
    w_o = w_out.astype(BF16)

    xs = x.reshape(batch * seq, d)
    for l in range(n_layers):
        xs = _ffn_call(xs, ng, mod, f1_in, f1_out, layer=l, sub=0, seq=seq)
        (gla, lr, qraw, qrot, kcvc, ks, vs, kw, vw, gate, ret) = _proj_call(
            xs, ng, mod, w_proj, gate_bias, cosf, sinf, layer=l, seq=seq)
        kc, vc = _cmp_call(kcvc, pe_top, pe_bot, w1_cmp, w2_cmp, layer=l, batch=batch, seq=seq)
        o_gla = _gla_call(gla, lr, a2_pad, a_bias, gla_g, layer=l, batch=batch, seq=seq)
        o_ret = _ret_call(ret, dmat, qf, kf, cd, ret_g, layer=l, batch=batch, seq=seq)
        o_nsa = _nsa_call(qraw, qrot, kc, vc, ks, vs, kw, vw, gate, layer=l, batch=batch, seq=seq)
        xs = _out_call(xs, o_gla, o_nsa, o_ret, mod, w_o, layer=l, seq=seq)
        final_g = final_norm_g.reshape(1, d) if l == n_layers - 1 else None
        xs = _ffn_call(xs, ng, mod, f2_in, f2_out, layer=l, sub=2, seq=seq, final_g=final_g)
    return xs.reshape(batch, seq, d)
```

```python
import functools
import math

import numpy as np
import jax
import jax.numpy as jnp
from jax import lax
from jax.experimental import pallas as pl
from jax.experimental.pallas import tpu as pltpu

F32 = jnp.float32
BF16 = jnp.bfloat16

HEAD_DIM = 64
LANES = 128
GLA_HEADS = 4
NSA_HEADS = 8
NSA_KV_HEADS = 2
NSA_REP = NSA_HEADS // NSA_KV_HEADS
RET_HEADS = 4
GLA_LOWRANK = 16
GLA_TAU = 16.0
CHUNK = 64
CMP_LEN = 32
CMP_STRIDE = 16
CMP_HIDDEN = 128
SLC_LEN = 64
SLC_TOPK = 16
WINDOW = 512
Q_BLOCK = 128
ROPE_THETA = 10000.0
FFN_HALF = 0.5
NORM_EPS = 1e-6
NEG_INF = -1e30
FORCED_SCORE = 1e4
N_ADA = 9
QK_SCALE = HEAD_DIM ** -0.5
SLC_SHIFT = int(math.log2(SLC_LEN))
LOG2_E = math.log2(math.e)

GLA_W = GLA_HEADS * HEAD_DIM
NSA_W = NSA_HEADS * HEAD_DIM
NSA_KV_W = NSA_KV_HEADS * HEAD_DIM
RET_W = RET_HEADS * HEAD_DIM
IN_SIZES = (GLA_W, GLA_W, GLA_W, GLA_W, GLA_LOWRANK,
            NSA_W, NSA_KV_W, NSA_KV_W, NSA_KV_W, NSA_KV_W, NSA_KV_W, NSA_KV_W, 3 * NSA_HEADS,
            RET_W, RET_W, RET_W, RET_W)
IN_SPLITS = tuple(int(s) for s in np.cumsum(IN_SIZES)[:-1])

SEG_GLA = 0
SEG_LR = SEG_GLA + 4 * GLA_W
SEG_NQ = SEG_LR + LANES
SEG_KCVC = SEG_NQ + NSA_HEADS * LANES
SEG_KS = SEG_KCVC + 2 * NSA_KV_W
SEG_KW = SEG_KS + NSA_KV_HEADS * LANES
SEG_VS = SEG_KW + NSA_KV_HEADS * LANES
SEG_VW = SEG_VS + NSA_KV_W
SEG_GATE = SEG_VW + NSA_KV_W
SEG_RET = SEG_GATE + NSA_KV_HEADS * LANES
PROJ_W = SEG_RET + 4 * RET_W

VMEM_LIMIT_BYTES = 56 * 1024 * 1024

NT_DIMS = (((1,), (1,)), ((), ()))


def _dot(a, b):
    return jnp.dot(a, b, preferred_element_type=F32)


def _dot_nt(a, b):
    return lax.dot_general(a, b, NT_DIMS, preferred_element_type=F32)


def _split3(x):
    hi = x.astype(BF16)
    r1 = x - hi.astype(F32)
    mid = r1.astype(BF16)
    lo = (r1 - mid.astype(F32)).astype(BF16)
    return hi, mid, lo


def _dot3_exact_rhs(x, w_bf16):
    hi, mid, lo = _split3(x)
    return _dot(hi, w_bf16) + _dot(mid, w_bf16) + _dot(lo, w_bf16)


def _dot3_exact_lhs(w_bf16, x):
    hi, mid, lo = _split3(x)
    return _dot(w_bf16, hi) + _dot(w_bf16, mid) + _dot(w_bf16, lo)


def _sigmoid(x):
    return 1.0 / (1.0 + jnp.exp(-x))


def _silu(x):
    return x * _sigmoid(x)


def _iota(shape, dim):
    return lax.broadcasted_iota(jnp.int32, shape, dim)


def _params(n_grid):
    return pltpu.CompilerParams(dimension_semantics=("arbitrary",) * n_grid,
                                vmem_limit_bytes=VMEM_LIMIT_BYTES)


def _norm_mod(x, g, sh, sc):
    ms = jnp.mean(x * x, axis=-1, keepdims=True)
    return x * lax.rsqrt(ms + NORM_EPS) * (g * (1.0 + sc)) + sh


def _ada_kernel(c_ref, w_ref, b_ref, o_ref):
    c = c_ref[...]
    o_ref[...] = _dot(_silu(c).astype(BF16), w_ref[...].astype(BF16)) + b_ref[...]


def _ada_call(c_pad, w_ada, b_ada):
    n_layers, d, n_out = w_ada.shape
    tn = 2304 if n_out % 2304 == 0 else n_out
    return pl.pallas_call(
        _ada_kernel,
        grid=(n_layers, n_out // tn),
        in_specs=[pl.BlockSpec((8, d), lambda l, j: (0, 0)),
                  pl.BlockSpec((None, d, tn), lambda l, j: (l, 0, j)),
                  pl.BlockSpec((None, 1, tn), lambda l, j: (l, 0, j))],
        out_specs=pl.BlockSpec((None, 8, tn), lambda l, j: (l, 0, j)),
        out_shape=jax.ShapeDtypeStruct((n_layers, 8, n_out), F32),
        compiler_params=_params(2),
        name="ada_mod",
    )(c_pad, w_ada, b_ada.reshape(n_layers, 1, n_out))


def _ffn_kernel(x_ref, g_ref, sh_ref, sc_ref, gt_ref, win_ref, wout_ref, *rest, d_ff, chunks, mixer, final):
    rest = list(rest)
    x = x_ref[...]
    if mixer:
        og_ref, on_ref, or_ref, gm_ref, wo_ref = rest[:5]
        rest = rest[5:]
        x = x + gm_ref[...] * (_dot(og_ref[...], wo_ref[0:GLA_W, :])
                               + _dot(on_ref[...], wo_ref[GLA_W:GLA_W + NSA_W, :])
                               + _dot(or_ref[...], wo_ref[GLA_W + NSA_W:, :]))
    if final:
        fg_ref, o_ref, act_ref = rest
    else:
        o_ref, act_ref = rest
    h = _norm_mod(x, g_ref[...], sh_ref[...], sc_ref[...]).astype(BF16)
    for (c0, cw) in chunks:
        gate = _dot(h, win_ref[:, c0:c0 + cw])
        up = _dot(h, win_ref[:, d_ff + c0:d_ff + c0 + cw])
        act_ref[:, c0:c0 + cw] = (_silu(gate) * up).astype(BF16)
    y = _dot(act_ref[...], wout_ref[...])
    xn = x + (FFN_HALF * gt_ref[...]) * y
    if final:
        ms = jnp.mean(xn * xn, axis=-1, keepdims=True)
        xn = xn * lax.rsqrt(ms + NORM_EPS) * fg_ref[...]
    o_ref[...] = xn


def _ffn_call(x, ng, mod, w_in, w_out, *, layer, sub, seq, mixer=None, final_g=None, tm=1024):
    m, d = x.shape
    d_ff = w_out.shape[1]
    tiles_per_seq = seq // tm
    chunks, c0 = [], 0
    while c0 < d_ff:
        cw = min(512, d_ff - c0)
        chunks.append((c0, cw))
        c0 += cw

    def mod_spec(k):
        return pl.BlockSpec((None, 1, d), lambda i: ((layer * 8 + i // tiles_per_seq) * N_ADA + k, 0, 0))

    in_specs = [pl.BlockSpec((tm, d), lambda i: (i, 0)),
                pl.BlockSpec((None, 1, d), lambda i: (layer * 3 + sub, 0, 0)),
                mod_spec(3 * sub), mod_spec(3 * sub + 1), mod_spec(3 * sub + 2),
                pl.BlockSpec((None, d, 2 * d_ff), lambda i: (layer, 0, 0), pipeline_mode=pl.Buffered(1)),
                pl.BlockSpec((None, d_ff, d), lambda i: (layer, 0, 0), pipeline_mode=pl.Buffered(1))]
    args = [x, ng, mod, mod, mod, w_in, w_out]
    if mixer is not None:
        o_gla, o_nsa, o_ret, w_mix = mixer
        in_specs += [pl.BlockSpec((tm, o.shape[1]), lambda i: (i, 0)) for o in (o_gla, o_nsa, o_ret)]
        in_specs += [mod_spec(5), pl.BlockSpec((None,) + w_mix.shape[1:], lambda i: (layer, 0, 0),
                                               pipeline_mode=pl.Buffered(1))]
        args += [o_gla, o_nsa, o_ret, mod, w_mix]
    if final_g is not None:
        in_specs.append(pl.BlockSpec((1, d), lambda i: (0, 0)))
        args.append(final_g)
    return pl.pallas_call(
        functools.partial(_ffn_kernel, d_ff=d_ff, chunks=tuple(chunks), mixer=mixer is not None,
                          final=final_g is not None),
        grid=(m // tm,),
        in_specs=in_specs,
        out_specs=pl.BlockSpec((tm, d), lambda i: (i, 0)),
        out_shape=jax.ShapeDtypeStruct((m, d), F32),
        scratch_shapes=[pltpu.VMEM((tm, d_ff), BF16)],
        compiler_params=_params(1),
        name=f"ffn_l{layer}_s{sub}",
    )(*args)


def _rotate(x, cosf, sinf):
    rows, width = x.shape
    first = (_iota((rows, LANES), 1) & (HEAD_DIM - 1)) < HEAD_DIM // 2
    outs = []
    for j in range(width // LANES):
        xs = x[:, j * LANES:(j + 1) * LANES]
        swapped = jnp.where(first, pltpu.roll(xs, LANES - HEAD_DIM // 2, 1), pltpu.roll(xs, HEAD_DIM // 2, 1))
        outs.append(xs * cosf + swapped * sinf)
    return outs[0] if len(outs) == 1 else jnp.concatenate(outs, axis=1)


def _store_group_transposed(dst_ref, x, fill):
    top = _iota((LANES, LANES), 0) < HEAD_DIM
    for c in range(x.shape[0] // LANES):
        piece = jnp.transpose(x[c * LANES:(c + 1) * LANES])
        dst_ref[0, c] = jnp.where(top, piece, fill).astype(dst_ref.dtype)
        dst_ref[1, c] = jnp.where(top, pltpu.roll(piece, HEAD_DIM, 0), fill).astype(dst_ref.dtype)


def _proj_kernel(x_ref, g_ref, sh_ref, sc_ref, w_ref, gb_ref, cos_ref, sin_ref,
                 gla_ref, lr_ref, qraw_ref, qrot_ref, kcvc_ref, ks_ref, vst_ref, kw_ref, vwt_ref,
                 gate_ref, ret_ref, *, tiles_per_seq):
    h = _norm_mod(x_ref[...], g_ref[...], sh_ref[...], sc_ref[...]).astype(BF16)
    cosf = cos_ref[...]
    sinf = sin_ref[...]
    tm = h.shape[0]

    def seg(c0, width):
        return _dot(h, w_ref[:, c0:c0 + width])

    nq = seg(SEG_NQ, NSA_HEADS * LANES)
    qraw_ref[...] = (nq * (QK_SCALE * LOG2_E)).astype(BF16)
    qrot_ref[...] = (_rotate(nq, cosf, sinf) * (QK_SCALE * LOG2_E)).astype(BF16)
    ret_ref[:, 0:RET_W] = _rotate(seg(SEG_RET, RET_W), cosf, sinf)
    ret_ref[:, RET_W:2 * RET_W] = _rotate(seg(SEG_RET + RET_W, RET_W), cosf, sinf) * QK_SCALE
    grp_w = NSA_KV_HEADS * LANES
    pos = (pl.program_id(0) % tiles_per_seq) * tm + _iota((tm, grp_w), 0)
    spare = (_iota((tm, grp_w), 1) & (LANES - 1)) - HEAD_DIM
    blk_onehot = jnp.where(spare == (pos >> SLC_SHIFT), 1.0, 0.0)
    ks_ref[...] = (_rotate(seg(SEG_KS, grp_w), cosf, sinf) + blk_onehot).astype(BF16)
    kw_ref[...] = _rotate(seg(SEG_KW, grp_w), cosf, sinf).astype(BF16)
    _store_group_transposed(vst_ref, seg(SEG_VS, NSA_KV_W), 1.0)
    _store_group_transposed(vwt_ref, seg(SEG_VW, NSA_KV_W), 1.0)
    gate_ref[...] = _sigmoid(seg(SEG_GATE, NSA_KV_HEADS * LANES) + gb_ref[...])
    kcvc_ref[...] = seg(SEG_KCVC, 2 * NSA_KV_W)
    lr_ref[...] = seg(SEG_LR, LANES)
    ret_ref[:, 2 * RET_W:4 * RET_W] = seg(SEG_RET + 2 * RET_W, 2 * RET_W)
    gla_ref[...] = seg(SEG_GLA, 4 * GLA_W)


def _proj_call(x, ng, mod, w_proj, gate_bias, cosf, sinf, *, layer, seq, tm=512):
    m, d = x.shape
    tiles_per_seq = seq // tm

    def mod_spec(k):
        return pl.BlockSpec((None, 1, d), lambda i: ((layer * 8 + i // tiles_per_seq) * N_ADA + k, 0, 0))

    def row_spec(width):
        return pl.BlockSpec((tm, width), lambda i: (i, 0))

    batch = m // seq
    grp_w = NSA_KV_HEADS * LANES

    def vt_spec():
        return pl.BlockSpec((None, NSA_KV_HEADS, tm // LANES, LANES, LANES),
                            lambda i: (i // tiles_per_seq, 0, i % tiles_per_seq, 0, 0))

    vt_shape = jax.ShapeDtypeStruct((batch, NSA_KV_HEADS, seq // LANES, LANES, LANES), BF16)
    out_widths = [(4 * GLA_W, F32), (LANES, F32), (NSA_HEADS * LANES, BF16), (NSA_HEADS * LANES, BF16),
                  (2 * NSA_KV_W, F32), (grp_w, BF16), None, (grp_w, BF16), None,
                  (NSA_KV_HEADS * LANES, F32), (4 * RET_W, F32)]
    return pl.pallas_call(
        functools.partial(_proj_kernel, tiles_per_seq=tiles_per_seq),
        grid=(m // tm,),
        in_specs=[row_spec(d),
                  pl.BlockSpec((None, 1, d), lambda i: (layer * 3 + 1, 0, 0)),
                  mod_spec(3), mod_spec(4),
                  pl.BlockSpec((None, d, PROJ_W), lambda i: (layer, 0, 0), pipeline_mode=pl.Buffered(1)),
                  pl.BlockSpec((None, 1, NSA_KV_HEADS * LANES), lambda i: (layer, 0, 0)),
                  pl.BlockSpec((tm, LANES), lambda i: (i % tiles_per_seq, 0)),
                  pl.BlockSpec((tm, LANES), lambda i: (i % tiles_per_seq, 0))],
        out_specs=[vt_spec() if ow is None else row_spec(ow[0]) for ow in out_widths],
        out_shape=[vt_shape if ow is None else jax.ShapeDtypeStruct((m, ow[0]), ow[1]) for ow in out_widths],
        compiler_params=_params(1),
        name=f"proj_l{layer}",
    )(x, ng, mod, mod, w_proj, gate_bias, cosf, sinf)


def _cmp_kernel(xk_ref, xv_ref, pet_ref, peb_ref, wt_ref, wb_ref, w2_ref, kc_ref, vct_ref):
    n_rows = xk_ref.shape[0] // CMP_STRIDE
    x = jnp.concatenate([ref[pl.ds(l, n_rows, stride=CMP_STRIDE), :]
                         for l in range(CMP_STRIDE) for ref in (xk_ref, xv_ref)], axis=1)
    a = _dot((x + pet_ref[...]).astype(BF16), wt_ref[...])
    b = _dot((x + peb_ref[...]).astype(BF16), wb_ref[...])
    hid = a + pltpu.roll(b, n_rows - 1, 0)
    out = _dot(_silu(hid).astype(BF16), w2_ref[...])
    out = jnp.where(_iota(out.shape, 0) < n_rows - 1, out, 0.0)
    grp_w = NSA_KV_HEADS * LANES
    kc_ref[...] = out[:, 0:grp_w].astype(BF16)
    _store_group_transposed(vct_ref, out[:, grp_w:grp_w + NSA_KV_W], 0.0)


def _cmp_call(kcvc, pe_top, pe_bot, w1_halves, w2, *, layer, batch, seq):
    rows = seq // CMP_STRIDE
    width = CMP_STRIDE * 2 * NSA_KV_W
    x = kcvc.reshape(batch, seq, 2 * NSA_KV_W)
    hid_w = 4 * CMP_HIDDEN
    grp_w = NSA_KV_HEADS * LANES

    def wspec(shape):
        return pl.BlockSpec((None,) + shape, lambda b: (layer, 0, 0))

    return pl.pallas_call(
        _cmp_kernel,
        grid=(batch,),
        in_specs=[pl.BlockSpec((None, seq, NSA_KV_W), lambda b: (b, 0, 0)),
                  pl.BlockSpec((None, seq, NSA_KV_W), lambda b: (b, 0, 1)),
                  wspec((1, width)), wspec((1, width)),
                  pl.BlockSpec((None, None, width, hid_w), lambda b: (layer, 0, 0, 0)),
                  pl.BlockSpec((None, None, width, hid_w), lambda b: (layer, 1, 0, 0)),
                  wspec((hid_w, grp_w + NSA_KV_W))],
        out_specs=[pl.BlockSpec((None, rows, grp_w), lambda b: (b, 0, 0)),
                   pl.BlockSpec((None, NSA_KV_HEADS, rows // LANES, LANES, LANES), lambda b: (b, 0, 0, 0, 0))],
        out_shape=[jax.ShapeDtypeStruct((batch, rows, grp_w), BF16),
                   jax.ShapeDtypeStruct((batch, NSA_KV_HEADS, rows // LANES, LANES, LANES), BF16)],
        compiler_params=_params(1),
        name=f"nsa_compress_l{layer}",
    )(x, x, pe_top, pe_bot, w1_halves, w1_halves, w2)


def _heads_tiled(x):
    return jnp.concatenate([x] * NSA_REP, axis=1)


SLC_UNROLL = 2
NSA_PARTS = 4

def _q_transposed(x):
    x = x.astype(F32)
    return jnp.concatenate([jnp.transpose(x[:, r * LANES:(r + 1) * LANES])[0:HEAD_DIM, :] for r in range(NSA_REP)],
                           axis=1)


def _nsa_compressed(s_c, vc_t, t0, tq, seq):
    n_cp = s_c.shape[0]
    n_cmp = (seq - CMP_LEN) // CMP_STRIDE + 1
    nn = _iota((n_cp, tq), 0)
    tt = t0 + _iota((n_cp, tq), 1)
    bias_c = jnp.where((nn * CMP_STRIDE + (CMP_LEN - 1) <= tt) & (nn < n_cmp), 0.0, NEG_INF)
    has_cmp = t0 + _iota((1, tq), 1) >= CMP_LEN - 1
    p_heads = []
    for r in range(NSA_REP):
        s = s_c[:, r * tq:(r + 1) * tq] + bias_c
        e = jnp.exp2(s - jnp.max(s, axis=0, keepdims=True))
        p_heads.append(e * jnp.where(has_cmp, 1.0 / jnp.sum(e, axis=0, keepdims=True), 0.0))
    o_cmp = _dot(vc_t, jnp.concatenate(p_heads, axis=1).astype(BF16))
    p_sum = p_heads[0]
    for r in range(1, NSA_REP):
        p_sum = p_sum + p_heads[r]
    return o_cmp, p_sum


def _nsa_selection_rows(p_sum, imp_ref, t0, tq, seq, n_rank):
    n_cp = p_sum.shape[0]
    n_cmp = (seq - CMP_LEN) // CMP_STRIDE + 1
    n_slc = seq // SLC_LEN
    n_sel = min(SLC_TOPK, n_slc)
    sb = _iota((LANES, n_cp), 0)
    cb = _iota((LANES, n_cp), 1)
    overlap_t = ((cb * CMP_STRIDE < sb * SLC_LEN + SLC_LEN) & (cb * CMP_STRIDE + CMP_LEN > sb * SLC_LEN)
                 & (sb < n_slc) & (cb < n_cmp))
    imp_t = _dot3_exact_lhs(jnp.where(overlap_t, 1.0, 0.0).astype(BF16), p_sum)
    blk = _iota((LANES, tq), 0)
    cur = (t0 + _iota((LANES, tq), 1)) >> SLC_SHIFT
    forced = (blk == 0) | (blk == cur) | (blk == cur - 1)
    imp_ref[...] = jnp.where(forced, FORCED_SCORE, jnp.where(blk <= cur, imp_t, -1.0))

    def ranked_rows(n_rows):
        if n_rows <= n_sel:
            return jnp.zeros((HEAD_DIM, tq), F32)
        n_grp = n_rows // 8
        vals = [imp_ref[8 * j:8 * (j + 1), :] for j in range(n_grp)]
        ranks = [jnp.zeros((8, tq), F32) for _ in range(n_grp)]
        sub = _iota((8, tq), 0)
        for i in range(n_rows):
            row = jnp.broadcast_to(imp_ref[i:i + 1, :], (8, tq))
            for j in range(n_grp):
                if 8 * j > i:
                    beats = jnp.where(row >= vals[j], 1.0, 0.0)
                elif 8 * j + 7 <= i:
                    beats = jnp.where(row > vals[j], 1.0, 0.0)
                else:
                    beats = jnp.where(sub + 8 * j > i, jnp.where(row >= vals[j], 1.0, 0.0),
                                      jnp.where(row > vals[j], 1.0, 0.0))
                ranks[j] = ranks[j] + beats
        return jnp.concatenate([jnp.where(rk < n_sel, 0.0, NEG_INF) for rk in ranks]
                               + [jnp.zeros((HEAD_DIM - n_rows, tq), F32)] * (n_rows < HEAD_DIM), axis=0)

    return ranked_rows(n_rank)


def _window_scores(q_rot, kw_ref, g, qb, tq, may_precede_start):
    n_wb = WINDOW // tq + 1
    ki = _iota((tq, tq), 0)
    ci = _iota((tq, tq), 1)
    blocks = []
    for j in range(n_wb):
        kb = qb - (n_wb - 1) + j
        kb_read = jnp.maximum(kb, 0) if may_precede_start else kb
        s = _dot(kw_ref[pl.ds(pl.multiple_of(kb_read * tq, tq), tq), g * LANES:(g + 1) * LANES], q_rot)
        if j == 0:
            s = s + _heads_tiled(jnp.where(ki > ci, 0.0, NEG_INF))
        if j == n_wb - 1:
            s = s + _heads_tiled(jnp.where(ki <= ci, 0.0, NEG_INF))
        elif may_precede_start:
            s = s + jnp.where(kb >= 0, 0.0, NEG_INF)
        blocks.append((s, kb_read))
    return blocks


def _window_finish(blocks, vwt_ref, g, tq):
    m_w = blocks[0][0].max(axis=0, keepdims=True)
    for s, _ in blocks[1:]:
        m_w = jnp.maximum(m_w, s.max(axis=0, keepdims=True))
    o_aug = jnp.zeros((LANES, NSA_REP * tq), F32)
    for s, kb_read in blocks:
        o_aug = o_aug + _dot(vwt_ref[g, kb_read], jnp.exp2(s - m_w).astype(BF16))
    return o_aug[0:HEAD_DIM] * (1.0 / o_aug[HEAD_DIM:HEAD_DIM + 1])


def _nsa_kernel(qraw_ref, qrot_ref, kc_ref, vct_ref, ks_ref, vst_ref, kw_ref, vwt_ref, gate_ref,
                o_ref, imp_ref, *, seq, tq, tk, q_lo, n_rank, n_cmp_rows):
    qb = pl.program_id(1) + q_lo
    t0 = qb * tq
    groups = range(NSA_KV_HEADS)
    n_cols = NSA_REP * tq

    def grp(g):
        return slice(g * LANES, (g + 1) * LANES)

    def q_cols(g):
        return slice(g * NSA_REP * LANES, (g + 1) * NSA_REP * LANES)

    def q_aug(q_t, extra_rows):
        return jnp.concatenate([q_t, extra_rows], axis=0).astype(BF16)

    no_extra = jnp.zeros((HEAD_DIM, n_cols), F32)
    q_rot_t = [_q_transposed(qrot_ref[:, q_cols(g)]) for g in groups]
    s_cmp = [_dot(kc_ref[0:n_cmp_rows, grp(g)], q_aug(_q_transposed(qraw_ref[:, q_cols(g)]), no_extra))
             for g in groups]
    win_blocks = [_window_scores(q_aug(q_rot_t[g], no_extra), kw_ref, g, qb, tq, q_lo * tq < WINDOW)
                  for g in groups]
    o_cmp, q_sel = [], []
    for g in groups:
        vc_t = jnp.concatenate([vct_ref[g, c] for c in range(n_cmp_rows // LANES)], axis=1)
        o_c, p_sum = _nsa_compressed(s_cmp[g], vc_t, t0, tq, seq)
        o_cmp.append(o_c[0:HEAD_DIM])
        sel_neg = _heads_tiled(_nsa_selection_rows(p_sum, imp_ref.at[g], t0, tq, seq, n_rank))
        q_sel.append(q_aug(q_rot_t[g], sel_neg))
    o_win = [_window_finish(win_blocks[g], vwt_ref, g, tq) for g in groups]

    blocks_per_tile = tk // LANES
    n_full = t0 // tk
    n_iter = (n_full + SLC_UNROLL - 1) // SLC_UNROLL

    def k_tile(kt, g):
        return ks_ref[pl.ds(pl.multiple_of(kt * tk, tk), tk), grp(g)]

    def v_tile(kt, g):
        return jnp.concatenate([vst_ref[g, kt * blocks_per_tile + c] for c in range(blocks_per_tile)], axis=1)

    def online_update(scores, values, m_i, acc):
        m_new = m_i
        for s in scores:
            m_new = jnp.maximum(m_new, s.max(axis=0, keepdims=True))
        p = jnp.concatenate([jnp.exp2(s - m_new).astype(BF16) for s in scores], axis=0)
        return m_new, jnp.exp2(m_i - m_new) * acc + _dot(jnp.concatenate(values, axis=1), p)

    pair_w = 2 * tq
    chains = [(g, h) for g in groups for h in range(NSA_REP // 2)]

    def q_pair(g, h):
        return q_sel[g][:, h * pair_w:(h + 1) * pair_w]

    def slc_step(j, stats):
        tiles = [(jnp.minimum(j * SLC_UNROLL + u, n_full - 1), j * SLC_UNROLL + u < n_full)
                 for u in range(SLC_UNROLL)]
        scores = [[_dot(k_tile(kt, g), q_pair(g, h)) for kt, _ in tiles] for g, h in chains]
        out = []
        for c, (g, h) in enumerate(chains):
            values = [jnp.where(live, v_tile(kt, g), jnp.zeros((LANES, tk), BF16)) for kt, live in tiles]
            out.append(online_update(scores[c], values, *stats[c]))
        return tuple(out)

    causal = n_full * tk + _iota((tk, tq), 0) <= t0 + _iota((tk, tq), 1)
    causal_bias = _heads_tiled(jnp.where(causal, 0.0, NEG_INF))
    s_diag = [_dot(k_tile(n_full, g), q_sel[g]) for g in groups]
    first = [online_update([s_diag[g] + causal_bias], [v_tile(n_full, g)],
                           jnp.full((1, n_cols), NEG_INF, F32), jnp.zeros((LANES, n_cols), F32)) for g in groups]
    init = tuple(tuple(x[:, h * pair_w:(h + 1) * pair_w] for x in first[g]) for g, h in chains)
    stats = lax.fori_loop(0, n_iter, slc_step, init)

    gate_all = gate_ref[...]
    for g in groups:
        acc = jnp.concatenate([stats[c][1] for c, (cg, _) in enumerate(chains) if cg == g], axis=1)
        o_slc = acc[0:HEAD_DIM] * (1.0 / acc[HEAD_DIM:HEAD_DIM + 1])
        gate_t = jnp.transpose(gate_all[:, grp(g)])
        o_heads = []
        for r in range(NSA_REP):
            cols = slice(r * tq, (r + 1) * tq)
            o_heads.append(gate_t[3 * r:3 * r + 1] * o_cmp[g][:, cols] + gate_t[3 * r + 1:3 * r + 2] * o_slc[:, cols]
                           + gate_t[3 * r + 2:3 * r + 3] * o_win[g][:, cols])
        for pair in range(NSA_REP // 2):
            slab = jnp.transpose(jnp.concatenate(o_heads[2 * pair:2 * pair + 2], axis=0))
            c0 = (g * NSA_REP // 2 + pair) * LANES
            o_ref[:, c0:c0 + LANES] = slab.astype(BF16)


def _nsa_call(qraw, qrot, kc, vct, ks, vst, kw, vwt, gate, *, layer, batch, seq, tq=Q_BLOCK, tk=512):
    assert tq == LANES and tk % tq == 0 and seq % tk == 0 and seq // SLC_LEN <= HEAD_DIM
    nq = seq // tq
    n_cp = seq // CMP_STRIDE
    grp_w = NSA_KV_HEADS * LANES

    def k_spec(rows):
        return pl.BlockSpec((None, rows, grp_w), lambda b, i: (b, 0, 0))

    def vt_spec():
        return pl.BlockSpec((None, NSA_KV_HEADS, seq // LANES, LANES, LANES), lambda b, i: (b, 0, 0, 0, 0))

    ks3, kw3 = ks.reshape(batch, seq, grp_w), kw.reshape(batch, seq, grp_w)
    part = nq // NSA_PARTS
    outs = []
    for p in range(NSA_PARTS):
        q_lo, last_token = p * part, (p + 1) * part * tq - 1
        n_rank = min(seq // SLC_LEN, -(-(last_token // SLC_LEN + 1) // 8) * 8)
        n_cmp_rows = min(n_cp, -(-(last_token // CMP_STRIDE + 1) // LANES) * LANES)

        def row_spec(width, q_lo=q_lo):
            return pl.BlockSpec((tq, width), lambda b, i: (b * nq + q_lo + i, 0))

        outs.append(pl.pallas_call(
            functools.partial(_nsa_kernel, seq=seq, tq=tq, tk=tk, q_lo=q_lo, n_rank=n_rank, n_cmp_rows=n_cmp_rows),
            grid=(batch, part),
            in_specs=[row_spec(NSA_HEADS * LANES), row_spec(NSA_HEADS * LANES), k_spec(n_cp),
                      pl.BlockSpec((None, NSA_KV_HEADS, n_cp // LANES, LANES, LANES), lambda b, i: (b, 0, 0, 0, 0)),
                      k_spec(seq), vt_spec(), k_spec(seq), vt_spec(), row_spec(grp_w)],
            out_specs=pl.BlockSpec((None, tq, NSA_W), lambda b, i: (b, i, 0)),
            out_shape=jax.ShapeDtypeStruct((batch, part * tq, NSA_W), BF16),
            scratch_shapes=[pltpu.VMEM((NSA_KV_HEADS, LANES, tq), F32)],
            compiler_params=_params(2),
            name=f"nsa_attn_l{layer}_p{p}",
        )(qraw, qrot, kc, vct, ks3, vst, kw3, vwt, gate))
    return jnp.concatenate(outs, axis=1).reshape(batch * seq, NSA_W)


LIN_BLOCK = 2 * CHUNK


def _lin_block_consts():
    r = _iota((LIN_BLOCK, LANES), 0)
    c = _iota((LIN_BLOCK, LANES), 1)
    same_chunk = (r >= CHUNK) == (c >= CHUNK)
    causal = same_chunk & (r >= c)
    head_diag = same_chunk
    first_rows = r < CHUNK
    low_lanes = c < HEAD_DIM
    return causal, head_diag, first_rows, low_lanes


def _head_stats(x, head_diag_mean):
    hi = x.astype(BF16)
    lo = (x - hi.astype(F32)).astype(BF16)
    return _dot(hi, head_diag_mean) + _dot(lo, head_diag_mean)


def _lin_core(chains, consts):
    causal, head_diag, first_rows, low_lanes = consts
    mask2 = jnp.concatenate([causal, causal], axis=0)
    scores, updates = [], []
    for c in chains:
        zero = jnp.zeros_like(c["q_dec"])
        q2 = jnp.concatenate([jnp.where(low_lanes, c["q_dec"], zero), jnp.where(low_lanes, zero, c["q_dec"])], axis=0)
        scores.append(_dot_nt(q2.astype(BF16), c["k_inv"].astype(BF16)))
    for c in chains:
        v_t = jnp.transpose(c["v"]).astype(BF16)
        kd_ab = jnp.concatenate([jnp.where(first_rows, c["k_dec"], 0.0), jnp.where(first_rows, 0.0, c["k_dec"])],
                                axis=1)
        updates.append(_dot(v_t, kd_ab.astype(BF16)))
    intra, states = [], []
    for c, a2 in zip(chains, scores):
        a2 = jnp.where(mask2, a2, 0.0)
        if c["intra_scale"] is not None:
            a2 = a2 * c["intra_scale"]
        o2 = _dot(a2.astype(BF16), c["v"].astype(BF16))
        intra.append(jnp.where(low_lanes, o2[0:LIN_BLOCK], o2[LIN_BLOCK:]))
    out = []
    for c, upd, o_intra in zip(chains, updates, intra):
        st_a = c["st_prev"] * c["dec_a"] + jnp.where(head_diag, upd[:, 0:LANES], 0.0)
        st_b = st_a * c["dec_b"] + jnp.where(head_diag, upd[:, LANES:], 0.0)
        o_ab = _dot_nt(c["q_int"].astype(BF16), jnp.concatenate([c["st_prev"], st_a], axis=0).astype(BF16))
        out.append((o_intra + jnp.where(first_rows, o_ab[:, 0:LANES], o_ab[:, LANES:]), st_b))
    return out


def _gla_kernel(x_ref, lr_ref, a2_ref, ab_ref, ng_ref, o_ref, st_ref, *, n_blocks, batch):
    @pl.when(pl.program_id(0) == 0)
    def _():
        st_ref[...] = jnp.zeros_like(st_ref)

    consts = _lin_block_consts()
    causal, head_diag, first_rows, low_lanes = consts
    ri = _iota((LIN_BLOCK, LANES), 0)
    ci = _iota((LIN_BLOCK, LANES), 1)
    tri_t = jnp.where(((ri >= CHUNK) == (ci >= CHUNK)) & (ri <= ci), 1.0, 0.0).astype(BF16)
    mean_op = jnp.where(head_diag, 1.0 / HEAD_DIM, 0.0).astype(BF16)
    a2h = a2_ref[...].astype(BF16)
    a2l = (a2_ref[...] - a2h.astype(F32)).astype(BF16)

    def block(i):
        rows = pl.ds(i * LIN_BLOCK, LIN_BLOCK)
        lr = jnp.concatenate([lr_ref[bi, rows, :] for bi in range(batch)], axis=0)
        lh = lr.astype(BF16)
        ll = (lr - lh.astype(F32)).astype(BF16)
        z = _dot(lh, a2h) + _dot(lh, a2l) + _dot(ll, a2h) + ab_ref[...]
        log_a = (jnp.minimum(z, 0.0) - jnp.log(1.0 + jnp.exp(-jnp.abs(z)))) * (1.0 / GLA_TAU)
        n_slab = GLA_W // LANES
        la_t = jnp.concatenate([jnp.transpose(log_a[bi * LIN_BLOCK:(bi + 1) * LIN_BLOCK, s * LANES:(s + 1) * LANES])
                                for bi in range(batch) for s in range(n_slab)], axis=0)
        cum_t = _dot3_exact_rhs(la_t, tri_t)
        chains = []
        for bi in range(batch):
            for s in range(n_slab):
                cols = slice(s * LANES, (s + 1) * LANES)
                n = bi * n_slab + s

                def ld(k):
                    return x_ref[bi, rows, k * GLA_W + s * LANES:k * GLA_W + (s + 1) * LANES]

                q, k, v, gg = ld(0), ld(1), ld(2), ld(3)
                b = jnp.transpose(cum_t[n * LANES:(n + 1) * LANES])
                bl_a = b[CHUNK - 1:CHUNK, :]
                bl_b = b[LIN_BLOCK - 1:LIN_BLOCK, :]
                bl = jnp.where(first_rows, bl_a, bl_b)
                q_dec = q * QK_SCALE * jnp.exp(b)
                chains.append(dict(bi=bi, s=s, cols=cols, gate=gg, q_dec=q_dec, q_int=q_dec, k_inv=k * jnp.exp(-b),
                                   k_dec=k * jnp.exp(bl - b), v=v, intra_scale=None, dec_a=jnp.exp(bl_a),
                                   dec_b=jnp.exp(bl_b), st_prev=st_ref[bi, s]))
        results = _lin_core(chains, consts)
        for c, (_, st_new) in zip(chains, results):
            st_ref[c["bi"], c["s"]] = st_new
        ms = _head_stats(jnp.concatenate([o * o for o, _ in results], axis=0), mean_op)
        for n, (c, (o, _)) in enumerate(zip(chains, results)):
            y = (o * lax.rsqrt(ms[n * LIN_BLOCK:(n + 1) * LIN_BLOCK] + NORM_EPS) * ng_ref[:, c["cols"]]
                 * _silu(c["gate"]))
            o_ref[c["bi"], rows, c["cols"]] = y.astype(BF16)

    for i in range(n_blocks):
        block(i)


def _ret_kernel(x_ref, dmat_ref, qf_ref, kf_ref, cd_ref, ng_ref, o_ref, st_ref, *, n_blocks, batch):
    @pl.when(pl.program_id(0) == 0)
    def _():
        st_ref[...] = jnp.zeros_like(st_ref)

    consts = _lin_block_consts()
    causal, head_diag, first_rows, low_lanes = consts
    mean_op = jnp.where(head_diag, 1.0 / HEAD_DIM, 0.0).astype(BF16)

    def block(i):
        rows = pl.ds(i * LIN_BLOCK, LIN_BLOCK)
        chains = []
        for bi in range(batch):
            for s in range(RET_W // LANES):
                cols = slice(s * LANES, (s + 1) * LANES)

                def ld(k):
                    return x_ref[bi, rows, k * RET_W + s * LANES:k * RET_W + (s + 1) * LANES]

                q, k, v, gg = ld(0), ld(1), ld(2), ld(3)
                cd = cd_ref[:, cols]
                chains.append(dict(bi=bi, s=s, cols=cols, gate=gg, q_dec=q, q_int=q * qf_ref[:, cols], k_inv=k,
                                   k_dec=k * kf_ref[:, cols], v=v, intra_scale=dmat_ref[s], dec_a=cd, dec_b=cd,
                                   st_prev=st_ref[bi, s]))
        results = _lin_core(chains, consts)
        for c, (_, st_new) in zip(chains, results):
            st_ref[c["bi"], c["s"]] = st_new
        o_all = jnp.concatenate([o for o, _ in results], axis=0)
        oc_all = o_all - _head_stats(o_all, mean_op)
        var_all = _head_stats(oc_all * oc_all, mean_op)
        y_all = oc_all * lax.rsqrt(var_all + NORM_EPS)
        for n, c in enumerate(chains):
            y = y_all[n * LIN_BLOCK:(n + 1) * LIN_BLOCK] * ng_ref[:, c["cols"]] * _silu(c["gate"])
            o_ref[c["bi"], rows, c["cols"]] = y.astype(BF16)

    for i in range(n_blocks):
        block(i)


def _gla_call(gla, lr, a2_pad, a_bias, norm_g, *, layer, batch, seq, tb=LIN_BLOCK):
    n_slab = GLA_W // LANES

    def lspec(shape):
        return pl.BlockSpec((None,) + shape, lambda t: (layer, 0, 0))

    return pl.pallas_call(
        functools.partial(_gla_kernel, n_blocks=tb // LIN_BLOCK, batch=batch),
        grid=(seq // tb,),
        in_specs=[pl.BlockSpec((batch, tb, 4 * GLA_W), lambda t: (0, t, 0)),
                  pl.BlockSpec((batch, tb, LANES), lambda t: (0, t, 0)),
                  lspec((LANES, GLA_W)), lspec((1, GLA_W)), lspec((1, GLA_W))],
        out_specs=pl.BlockSpec((batch, tb, GLA_W), lambda t: (0, t, 0)),
        out_shape=jax.ShapeDtypeStruct((batch, seq, GLA_W), BF16),
        scratch_shapes=[pltpu.VMEM((batch, n_slab, LANES, LANES), F32)],
        compiler_params=_params(1),
        name=f"gla_l{layer}",
    )(gla.reshape(batch, seq, 4 * GLA_W), lr.reshape(batch, seq, LANES), a2_pad, a_bias, norm_g
      ).reshape(batch * seq, GLA_W)


def _ret_call(ret, dmat, qf, kf, cd, norm_g, *, layer, batch, seq, tb=LIN_BLOCK):
    n_slab = RET_W // LANES
    full = lambda a: pl.BlockSpec(a.shape, lambda t: (0,) * a.ndim)
    return pl.pallas_call(
        functools.partial(_ret_kernel, n_blocks=tb // LIN_BLOCK, batch=batch),
        grid=(seq // tb,),
        in_specs=[pl.BlockSpec((batch, tb, 4 * RET_W), lambda t: (0, t, 0)),
                  full(dmat), full(qf), full(kf), full(cd),
                  pl.BlockSpec((None, 1, RET_W), lambda t: (layer, 0, 0))],
        out_specs=pl.BlockSpec((batch, tb, RET_W), lambda t: (0, t, 0)),
        out_shape=jax.ShapeDtypeStruct((batch, seq, RET_W), BF16),
        scratch_shapes=[pltpu.VMEM((batch, n_slab, LANES, LANES), F32)],
        compiler_params=_params(1),
        name=f"ret_l{layer}",
    )(ret.reshape(batch, seq, 4 * RET_W), dmat, qf, kf, cd, norm_g).reshape(batch * seq, RET_W)


def _layout_proj_weights(w_in, nsa_gate_bias):
    n_layers, d, _ = w_in.shape
    w_in = w_in.astype(BF16)
    (g_q, g_k, g_v, g_g, g_lr, n_q, n_kc, n_vc, n_ks, n_vs, n_kw, n_vw, n_gate,
     r_q, r_k, r_v, r_g) = jnp.split(w_in, IN_SPLITS, axis=-1)
    zeros = lambda w: jnp.zeros((n_layers, d, w), w_in.dtype)
    def pad_heads(w, n_heads):
        w = w.reshape(n_layers, d, n_heads, HEAD_DIM)
        return jnp.concatenate([w, jnp.zeros_like(w)], axis=-1).reshape(n_layers, d, n_heads * LANES)

    per_grp = 3 * NSA_REP
    gate_cols = []
    bias_cols = []
    for g in range(NSA_KV_HEADS):
        gate_cols += [n_gate[..., g * per_grp:(g + 1) * per_grp], zeros(LANES - per_grp)]
        bias_cols += [nsa_gate_bias[:, g * per_grp:(g + 1) * per_grp],
                      jnp.zeros((n_layers, LANES - per_grp), nsa_gate_bias.dtype)]
    w = jnp.concatenate([g_q, g_k, g_v, g_g, g_lr, zeros(LANES - GLA_LOWRANK), pad_heads(n_q, NSA_HEADS),
                         n_kc, n_vc, pad_heads(n_ks, NSA_KV_HEADS), pad_heads(n_kw, NSA_KV_HEADS), n_vs, n_vw]
                        + gate_cols + [r_q, r_k, r_v, r_g], axis=-1)
    assert w.shape[-1] == PROJ_W
    bias = jnp.concatenate(bias_cols, axis=-1).reshape(n_layers, 1, NSA_KV_HEADS * LANES)
    return w.astype(BF16), bias


def _layout_cmp_weights(pe_k, pe_v, w1_k, w2_k, w1_v, w2_v):
    n_layers = pe_k.shape[0]
    eye = jnp.eye(2, dtype=BF16)
    w1 = jnp.stack([w1_k, w1_v], axis=1).reshape(n_layers, 2, CMP_LEN, HEAD_DIM, CMP_HIDDEN).astype(BF16)
    blocks = [jnp.pad(w1[:, kv], ((0, 0), (0, 0), (0, 0), ((2 * kv + g) * CMP_HIDDEN, (3 - 2 * kv - g) * CMP_HIDDEN)))
              for kv in range(2) for g in range(NSA_KV_HEADS)]
    w1_full = jnp.stack(blocks, axis=2).reshape(n_layers, CMP_LEN * 2 * NSA_KV_W, 4 * CMP_HIDDEN)
    half = CMP_STRIDE * 2 * NSA_KV_W
    w2 = jnp.stack([w2_k, w2_v], axis=1).astype(BF16)
    w2_full = jnp.einsum('zknd,kq,gh->zqhnkgd', w2, eye, eye)
    w2_full = w2_full.reshape(n_layers, 4 * CMP_HIDDEN, 2 * NSA_KV_W)
    w2_k = w2_full[..., :NSA_KV_W].reshape(n_layers, 4 * CMP_HIDDEN, NSA_KV_HEADS, HEAD_DIM)
    w2_k = jnp.concatenate([w2_k, jnp.zeros_like(w2_k)], axis=-1).reshape(n_layers, 4 * CMP_HIDDEN, -1)
    w2_full = jnp.concatenate([w2_k, w2_full[..., NSA_KV_W:]], axis=-1).astype(BF16)
    pe = jnp.stack([pe_k, pe_v], axis=1)
    pe = jnp.broadcast_to(pe.transpose(0, 2, 1, 3)[:, :, :, None, :],
                          (n_layers, CMP_LEN, 2, NSA_KV_HEADS, HEAD_DIM))
    pe = pe.reshape(n_layers, 1, CMP_LEN * 2 * NSA_KV_W)
    return pe[:, :, :half], pe[:, :, half:], w1_full.reshape(n_layers, 2, half, 4 * CMP_HIDDEN), w2_full


def _rotary_tables(seq):
    half = HEAD_DIM // 2
    inv_freq = ROPE_THETA ** (-jnp.arange(half, dtype=F32) / half)
    ang = jnp.arange(seq).astype(F32)[:, None] * inv_freq[None, :]
    cos, sin = jnp.cos(ang), jnp.sin(ang)
    reps = LANES // HEAD_DIM
    return jnp.tile(jnp.concatenate([cos, cos], axis=-1), (1, reps)), \
        jnp.tile(jnp.concatenate([-sin, sin], axis=-1), (1, reps))


def _retention_tables():
    log_gamma = jnp.log1p(-jnp.exp2(-5.0 - jnp.arange(RET_HEADS, dtype=F32)))
    lg_lane = jnp.repeat(log_gamma, HEAD_DIM)[None, :]
    pos = (jnp.arange(LIN_BLOCK) % CHUNK).astype(F32)[:, None]
    qf = jnp.exp(lg_lane * (pos + 1.0))
    kf = jnp.exp(lg_lane * (CHUNK - 1.0 - pos))
    cd = jnp.exp(lg_lane * CHUNK)
    r = jnp.arange(LIN_BLOCK)
    rel = (r[:, None] - r[None, :]).astype(F32)
    ok = ((r[:, None] // CHUNK) == (r[None, :] // CHUNK)) & (rel >= 0)
    dm = jnp.where(ok[None], jnp.exp(log_gamma[:, None, None] * rel[None]), 0.0)
    dmat = dm.reshape(RET_W // LANES, 2 * LIN_BLOCK, LIN_BLOCK)
    return dmat, qf, kf, cd


def kernel(x, c, w_ada, b_ada, norm_g, ffn1_in, ffn1_out, w_in, gla_a2, gla_a_bias, gla_norm_g,
           nsa_pe_k, nsa_pe_v, nsa_w1_k, nsa_w2_k, nsa_w1_v, nsa_w2_v, nsa_gate_bias, ret_norm_g,
           w_out, ffn2_in, ffn2_out, final_norm_g):
    batch, seq, d = x.shape
    n_layers = w_ada.shape[0]
    assert batch <= 8 and seq % 512 == 0 and seq >= WINDOW + Q_BLOCK

    c_pad = jnp.zeros((8, d), F32).at[:batch].set(c)
    mod = _ada_call(c_pad, w_ada, b_ada).reshape(n_layers * 8 * N_ADA, 1, d)
    ng = norm_g.reshape(n_layers * 3, 1, d)

    w_proj, gate_bias = _layout_proj_weights(w_in, nsa_gate_bias)
    pe_top, pe_bot, w1_cmp, w2_cmp = _layout_cmp_weights(nsa_pe_k, nsa_pe_v, nsa_w1_k, nsa_w2_k, nsa_w1_v, nsa_w2_v)
    cosf, sinf = _rotary_tables(seq)
    dmat, qf, kf, cd = _retention_tables()
    a2_pad = jnp.zeros((n_layers, LANES, GLA_W), F32).at[:, :GLA_LOWRANK].set(gla_a2)
    a_bias = gla_a_bias.reshape(n_layers, 1, GLA_W)
    gla_g = jnp.tile(gla_norm_g, (1, GLA_HEADS)).reshape(n_layers, 1, GLA_W)
    ret_g = jnp.tile(ret_norm_g, (1, RET_HEADS)).reshape(n_layers, 1, RET_W)
    f1_in, f1_out = ffn1_in.astype(BF16), ffn1_out.astype(BF16)
    f2_in, f2_out = ffn2_in.astype(BF16), ffn2_out.astype(BF16)
    w_o = w_out.astype(BF16)

    xs = x.reshape(batch * seq, d)
    for l in range(n_layers):
        xs = _ffn_call(xs, ng, mod, f1_in, f1_out, layer=l, sub=0, seq=seq)
        (gla, lr, qraw, qrot, kcvc, ks, vs, kw, vw, gate, ret) = _proj_call(
            xs, ng, mod, w_proj, gate_bias, cosf, sinf, layer=l, seq=seq)
        kc, vc = _cmp_call(kcvc, pe_top, pe_bot, w1_cmp, w2_cmp, layer=l, batch=batch, seq=seq)
        o_gla = _gla_call(gla, lr, a2_pad, a_bias, gla_g, layer=l, batch=batch, seq=seq)
        o_ret = _ret_call(ret, dmat, qf, kf, cd, ret_g, layer=l, batch=batch, seq=seq)
        o_nsa = _nsa_call(qraw, qrot, kc, vc, ks, vs, kw, vw, gate, layer=l, batch=batch, seq=seq)
        final_g = final_norm_g.reshape(1, d) if l == n_layers - 1 else None
        xs = _ffn_call(xs, ng, mod, f2_in, f2_out, layer=l, sub=2, seq=seq, mixer=(o_gla, o_nsa, o_ret, w_o),
                       final_g=final_g)
    return xs.reshape(batch, seq, d)
```

```python
import functools
import math

import numpy as np
import jax
import jax.numpy as jnp
from jax import lax
from jax.experimental import pallas as pl
from jax.experimental.pallas import tpu as pltpu

F32 = jnp.float32
BF16 = jnp.bfloat16

HEAD_DIM = 64
LANES = 128
GLA_HEADS = 4
NSA_HEADS = 8
NSA_KV_HEADS = 2
NSA_REP = NSA_HEADS // NSA_KV_HEADS
RET_HEADS = 4
GLA_LOWRANK = 16
GLA_TAU = 16.0
CHUNK = 64
CMP_LEN = 32
CMP_STRIDE = 16
CMP_HIDDEN = 128
SLC_LEN = 64
SLC_TOPK = 16
WINDOW = 512
Q_BLOCK = 128
ROPE_THETA = 10000.0
FFN_HALF = 0.5
NORM_EPS = 1e-6
NEG_INF = -1e30
FORCED_SCORE = 1e4
N_ADA = 9
QK_SCALE = HEAD_DIM ** -0.5
SLC_SHIFT = int(math.log2(SLC_LEN))
LOG2_E = math.log2(math.e)

GLA_W = GLA_HEADS * HEAD_DIM
NSA_W = NSA_HEADS * HEAD_DIM
NSA_KV_W = NSA_KV_HEADS * HEAD_DIM
RET_W = RET_HEADS * HEAD_DIM
IN_SIZES = (GLA_W, GLA_W, GLA_W, GLA_W, GLA_LOWRANK,
            NSA_W, NSA_KV_W, NSA_KV_W, NSA_KV_W, NSA_KV_W, NSA_KV_W, NSA_KV_W, 3 * NSA_HEADS,
            RET_W, RET_W, RET_W, RET_W)
IN_SPLITS = tuple(int(s) for s in np.cumsum(IN_SIZES)[:-1])

SEG_GLA = 0
SEG_LR = SEG_GLA + 4 * GLA_W
SEG_NQ = SEG_LR + LANES
SEG_KCVC = SEG_NQ + NSA_HEADS * LANES
SEG_KS = SEG_KCVC + 2 * NSA_KV_W
SEG_KW = SEG_KS + NSA_KV_HEADS * LANES
SEG_VS = SEG_KW + NSA_KV_HEADS * LANES
SEG_VW = SEG_VS + NSA_KV_W
SEG_GATE = SEG_VW + NSA_KV_W
SEG_RET = SEG_GATE + NSA_KV_HEADS * LANES
PROJ_W = SEG_RET + 4 * RET_W

VMEM_LIMIT_BYTES = 56 * 1024 * 1024

NT_DIMS = (((1,), (1,)), ((), ()))


def _dot(a, b):
    return jnp.dot(a, b, preferred_element_type=F32)


def _dot_nt(a, b):
    return lax.dot_general(a, b, NT_DIMS, preferred_element_type=F32)


def _split3(x):
    hi = x.astype(BF16)
    r1 = x - hi.astype(F32)
    mid = r1.astype(BF16)
    lo = (r1 - mid.astype(F32)).astype(BF16)
    return hi, mid, lo


def _dot3_exact_rhs(x, w_bf16):
    hi, mid, lo = _split3(x)
    return _dot(hi, w_bf16) + _dot(mid, w_bf16) + _dot(lo, w_bf16)


def _dot3_exact_lhs(w_bf16, x):
    hi, mid, lo = _split3(x)
    return _dot(w_bf16, hi) + _dot(w_bf16, mid) + _dot(w_bf16, lo)


def _sigmoid(x):
    return 1.0 / (1.0 + jnp.exp(-x))


def _silu(x):
    return x * _sigmoid(x)


def _iota(shape, dim):
    return lax.broadcasted_iota(jnp.int32, shape, dim)


def _params(n_grid):
    return pltpu.CompilerParams(dimension_semantics=("arbitrary",) * n_grid,
                                vmem_limit_bytes=VMEM_LIMIT_BYTES)


def _norm_mod(x, g, sh, sc):
    ms = jnp.mean(x * x, axis=-1, keepdims=True)
    return x * lax.rsqrt(ms + NORM_EPS) * (g * (1.0 + sc)) + sh


def _ada_kernel(c_ref, w_ref, b_ref, o_ref):
    c = c_ref[...]
    o_ref[...] = _dot(_silu(c).astype(BF16), w_ref[...].astype(BF16)) + b_ref[...]


def _ada_call(c_pad, w_ada, b_ada):
    n_layers, d, n_out = w_ada.shape
    tn = 2304 if n_out % 2304 == 0 else n_out
    return pl.pallas_call(
        _ada_kernel,
        grid=(n_layers, n_out // tn),
        in_specs=[pl.BlockSpec((8, d), lambda l, j: (0, 0)),
                  pl.BlockSpec((None, d, tn), lambda l, j: (l, 0, j)),
                  pl.BlockSpec((None, 1, tn), lambda l, j: (l, 0, j))],
        out_specs=pl.BlockSpec((None, 8, tn), lambda l, j: (l, 0, j)),
        out_shape=jax.ShapeDtypeStruct((n_layers, 8, n_out), F32),
        compiler_params=_params(2),
        name="ada_mod",
    )(c_pad, w_ada, b_ada.reshape(n_layers, 1, n_out))


def _ffn_kernel(x_ref, g_ref, sh_ref, sc_ref, gt_ref, win_ref, wout_ref, *rest, d_ff, chunks, mixer, final):
    rest = list(rest)
    x = x_ref[...]
    if mixer:
        og_ref, on_ref, or_ref, gm_ref, wo_ref = rest[:5]
        rest = rest[5:]
        x = x + gm_ref[...] * (_dot(og_ref[...], wo_ref[0:GLA_W, :])
                               + _dot(on_ref[...], wo_ref[GLA_W:GLA_W + NSA_W, :])
                               + _dot(or_ref[...], wo_ref[GLA_W + NSA_W:, :]))
    if final:
        fg_ref, o_ref, act_ref = rest
    else:
        o_ref, act_ref = rest
    h = _norm_mod(x, g_ref[...], sh_ref[...], sc_ref[...]).astype(BF16)
    for (c0, cw) in chunks:
        gate = _dot(h, win_ref[:, c0:c0 + cw])
        up = _dot(h, win_ref[:, d_ff + c0:d_ff + c0 + cw])
        act_ref[:, c0:c0 + cw] = (_silu(gate) * up).astype(BF16)
    y = _dot(act_ref[...], wout_ref[...])
    xn = x + (FFN_HALF * gt_ref[...]) * y
    if final:
        ms = jnp.mean(xn * xn, axis=-1, keepdims=True)
        xn = xn * lax.rsqrt(ms + NORM_EPS) * fg_ref[...]
    o_ref[...] = xn


def _ffn_call(x, ng, mod, w_in, w_out, *, layer, sub, seq, mixer=None, final_g=None, tm=1024):
    m, d = x.shape
    d_ff = w_out.shape[1]
    tiles_per_seq = seq // tm
    chunks, c0 = [], 0
    while c0 < d_ff:
        cw = min(512, d_ff - c0)
        chunks.append((c0, cw))
        c0 += cw

    def mod_spec(k):
        return pl.BlockSpec((None, 1, d), lambda i: ((layer * 8 + i // tiles_per_seq) * N_ADA + k, 0, 0))

    in_specs = [pl.BlockSpec((tm, d), lambda i: (i, 0)),
                pl.BlockSpec((None, 1, d), lambda i: (layer * 3 + sub, 0, 0)),
                mod_spec(3 * sub), mod_spec(3 * sub + 1), mod_spec(3 * sub + 2),
                pl.BlockSpec((None, d, 2 * d_ff), lambda i: (layer, 0, 0), pipeline_mode=pl.Buffered(1)),
                pl.BlockSpec((None, d_ff, d), lambda i: (layer, 0, 0), pipeline_mode=pl.Buffered(1))]
    args = [x, ng, mod, mod, mod, w_in, w_out]
    if mixer is not None:
        o_gla, o_nsa, o_ret, w_mix = mixer
        in_specs += [pl.BlockSpec((tm, o.shape[1]), lambda i: (i, 0)) for o in (o_gla, o_nsa, o_ret)]
        in_specs += [mod_spec(5), pl.BlockSpec((None,) + w_mix.shape[1:], lambda i: (layer, 0, 0),
                                               pipeline_mode=pl.Buffered(1))]
        args += [o_gla, o_nsa, o_ret, mod, w_mix]
    if final_g is not None:
        in_specs.append(pl.BlockSpec((1, d), lambda i: (0, 0)))
        args.append(final_g)
    return pl.pallas_call(
        functools.partial(_ffn_kernel, d_ff=d_ff, chunks=tuple(chunks), mixer=mixer is not None,
                          final=final_g is not None),
        grid=(m // tm,),
        in_specs=in_specs,
        out_specs=pl.BlockSpec((tm, d), lambda i: (i, 0)),
        out_shape=jax.ShapeDtypeStruct((m, d), F32),
        scratch_shapes=[pltpu.VMEM((tm, d_ff), BF16)],
        compiler_params=_params(1),
        name=f"ffn_l{layer}_s{sub}",
    )(*args)


def _rotate(x, cosf, sinf):
    rows, width = x.shape
    first = (_iota((rows, LANES), 1) & (HEAD_DIM - 1)) < HEAD_DIM // 2
    outs = []
    for j in range(width // LANES):
        xs = x[:, j * LANES:(j + 1) * LANES]
        swapped = jnp.where(first, pltpu.roll(xs, LANES - HEAD_DIM // 2, 1), pltpu.roll(xs, HEAD_DIM // 2, 1))
        outs.append(xs * cosf + swapped * sinf)
    return outs[0] if len(outs) == 1 else jnp.concatenate(outs, axis=1)


def _store_group_transposed(dst_ref, x, fill):
    top = _iota((LANES, LANES), 0) < HEAD_DIM
    for c in range(x.shape[0] // LANES):
        piece = jnp.transpose(x[c * LANES:(c + 1) * LANES])
        dst_ref[0, c] = jnp.where(top, piece, fill).astype(dst_ref.dtype)
        dst_ref[1, c] = jnp.where(top, pltpu.roll(piece, HEAD_DIM, 0), fill).astype(dst_ref.dtype)


def _proj_kernel(x_ref, g_ref, sh_ref, sc_ref, w_ref, gb_ref, cos_ref, sin_ref,
                 gla_ref, lr_ref, qraw_ref, qrot_ref, kcvc_ref, ks_ref, vst_ref, kw_ref, vwt_ref,
                 gate_ref, ret_ref, *, tiles_per_seq):
    h = _norm_mod(x_ref[...], g_ref[...], sh_ref[...], sc_ref[...]).astype(BF16)
    cosf = cos_ref[...]
    sinf = sin_ref[...]
    tm = h.shape[0]

    def seg(c0, width):
        return _dot(h, w_ref[:, c0:c0 + width])

    nq = seg(SEG_NQ, NSA_HEADS * LANES)
    qraw_ref[...] = (nq * (QK_SCALE * LOG2_E)).astype(BF16)
    qrot_ref[...] = (_rotate(nq, cosf, sinf) * (QK_SCALE * LOG2_E)).astype(BF16)
    ret_ref[:, 0:RET_W] = _rotate(seg(SEG_RET, RET_W), cosf, sinf)
    ret_ref[:, RET_W:2 * RET_W] = _rotate(seg(SEG_RET + RET_W, RET_W), cosf, sinf) * QK_SCALE
    grp_w = NSA_KV_HEADS * LANES
    pos = (pl.program_id(0) % tiles_per_seq) * tm + _iota((tm, grp_w), 0)
    spare = (_iota((tm, grp_w), 1) & (LANES - 1)) - HEAD_DIM
    blk_onehot = jnp.where(spare == (pos >> SLC_SHIFT), 1.0, 0.0)
    ks_ref[...] = (_rotate(seg(SEG_KS, grp_w), cosf, sinf) + blk_onehot).astype(BF16)
    kw_ref[...] = _rotate(seg(SEG_KW, grp_w), cosf, sinf).astype(BF16)
    _store_group_transposed(vst_ref, seg(SEG_VS, NSA_KV_W), 1.0)
    _store_group_transposed(vwt_ref, seg(SEG_VW, NSA_KV_W), 1.0)
    gate_ref[...] = _sigmoid(seg(SEG_GATE, NSA_KV_HEADS * LANES) + gb_ref[...])
    kcvc_ref[...] = seg(SEG_KCVC, 2 * NSA_KV_W)
    lr_ref[...] = seg(SEG_LR, LANES)
    ret_ref[:, 2 * RET_W:4 * RET_W] = seg(SEG_RET + 2 * RET_W, 2 * RET_W)
    gla_ref[...] = seg(SEG_GLA, 4 * GLA_W)


def _proj_call(x, ng, mod, w_proj, gate_bias, cosf, sinf, *, layer, seq, tm=512):
    m, d = x.shape
    tiles_per_seq = seq // tm

    def mod_spec(k):
        return pl.BlockSpec((None, 1, d), lambda i: ((layer * 8 + i // tiles_per_seq) * N_ADA + k, 0, 0))

    def row_spec(width):
        return pl.BlockSpec((tm, width), lambda i: (i, 0))

    batch = m // seq
    grp_w = NSA_KV_HEADS * LANES

    def vt_spec():
        return pl.BlockSpec((None, NSA_KV_HEADS, tm // LANES, LANES, LANES),
                            lambda i: (i // tiles_per_seq, 0, i % tiles_per_seq, 0, 0))

    vt_shape = jax.ShapeDtypeStruct((batch, NSA_KV_HEADS, seq // LANES, LANES, LANES), BF16)
    out_widths = [(4 * GLA_W, F32), (LANES, F32), (NSA_HEADS * LANES, BF16), (NSA_HEADS * LANES, BF16),
                  (2 * NSA_KV_W, F32), (grp_w, BF16), None, (grp_w, BF16), None,
                  (NSA_KV_HEADS * LANES, F32), (4 * RET_W, F32)]
    return pl.pallas_call(
        functools.partial(_proj_kernel, tiles_per_seq=tiles_per_seq),
        grid=(m // tm,),
        in_specs=[row_spec(d),
                  pl.BlockSpec((None, 1, d), lambda i: (layer * 3 + 1, 0, 0)),
                  mod_spec(3), mod_spec(4),
                  pl.BlockSpec((None, d, PROJ_W), lambda i: (layer, 0, 0), pipeline_mode=pl.Buffered(1)),
                  pl.BlockSpec((None, 1, NSA_KV_HEADS * LANES), lambda i: (layer, 0, 0)),
                  pl.BlockSpec((tm, LANES), lambda i: (i % tiles_per_seq, 0)),
                  pl.BlockSpec((tm, LANES), lambda i: (i % tiles_per_seq, 0))],
        out_specs=[vt_spec() if ow is None else row_spec(ow[0]) for ow in out_widths],
        out_shape=[vt_shape if ow is None else jax.ShapeDtypeStruct((m, ow[0]), ow[1]) for ow in out_widths],
        compiler_params=_params(1),
        name=f"proj_l{layer}",
    )(x, ng, mod, mod, w_proj, gate_bias, cosf, sinf)


def _cmp_kernel(xk_ref, xv_ref, pet_ref, peb_ref, wt_ref, wb_ref, w2_ref, kc_ref, vct_ref):
    n_rows = xk_ref.shape[0] // CMP_STRIDE
    x = jnp.concatenate([ref[pl.ds(l, n_rows, stride=CMP_STRIDE), :]
                         for l in range(CMP_STRIDE) for ref in (xk_ref, xv_ref)], axis=1)
    a = _dot((x + pet_ref[...]).astype(BF16), wt_ref[...])
    b = _dot((x + peb_ref[...]).astype(BF16), wb_ref[...])
    hid = a + pltpu.roll(b, n_rows - 1, 0)
    out = _dot(_silu(hid).astype(BF16), w2_ref[...])
    out = jnp.where(_iota(out.shape, 0) < n_rows - 1, out, 0.0)
    grp_w = NSA_KV_HEADS * LANES
    kc_ref[...] = out[:, 0:grp_w].astype(BF16)
    _store_group_transposed(vct_ref, out[:, grp_w:grp_w + NSA_KV_W], 0.0)


def _cmp_call(kcvc, pe_top, pe_bot, w1_halves, w2, *, layer, batch, seq):
    rows = seq // CMP_STRIDE
    width = CMP_STRIDE * 2 * NSA_KV_W
    x = kcvc.reshape(batch, seq, 2 * NSA_KV_W)
    hid_w = 4 * CMP_HIDDEN
    grp_w = NSA_KV_HEADS * LANES

    def wspec(shape):
        return pl.BlockSpec((None,) + shape, lambda b: (layer, 0, 0))

    return pl.pallas_call(
        _cmp_kernel,
        grid=(batch,),
        in_specs=[pl.BlockSpec((None, seq, NSA_KV_W), lambda b: (b, 0, 0)),
                  pl.BlockSpec((None, seq, NSA_KV_W), lambda b: (b, 0, 1)),
                  wspec((1, width)), wspec((1, width)),
                  pl.BlockSpec((None, None, width, hid_w), lambda b: (layer, 0, 0, 0)),
                  pl.BlockSpec((None, None, width, hid_w), lambda b: (layer, 1, 0, 0)),
                  wspec((hid_w, grp_w + NSA_KV_W))],
        out_specs=[pl.BlockSpec((None, rows, grp_w), lambda b: (b, 0, 0)),
                   pl.BlockSpec((None, NSA_KV_HEADS, rows // LANES, LANES, LANES), lambda b: (b, 0, 0, 0, 0))],
        out_shape=[jax.ShapeDtypeStruct((batch, rows, grp_w), BF16),
                   jax.ShapeDtypeStruct((batch, NSA_KV_HEADS, rows // LANES, LANES, LANES), BF16)],
        compiler_params=_params(1),
        name=f"nsa_compress_l{layer}",
    )(x, x, pe_top, pe_bot, w1_halves, w1_halves, w2)


def _heads_tiled(x):
    return jnp.concatenate([x] * NSA_REP, axis=1)


SLC_UNROLL = 2
NSA_PARTS = 4

def _q_transposed(x):
    x = x.astype(F32)
    return jnp.concatenate([jnp.transpose(x[:, r * LANES:(r + 1) * LANES])[0:HEAD_DIM, :] for r in range(NSA_REP)],
                           axis=1)


def _nsa_compressed(s_c, vc_t, t0, tq, seq):
    n_cp = s_c.shape[0]
    n_cmp = (seq - CMP_LEN) // CMP_STRIDE + 1
    nn = _iota((n_cp, tq), 0)
    tt = t0 + _iota((n_cp, tq), 1)
    bias_c = jnp.where((nn * CMP_STRIDE + (CMP_LEN - 1) <= tt) & (nn < n_cmp), 0.0, NEG_INF)
    has_cmp = t0 + _iota((1, tq), 1) >= CMP_LEN - 1
    p_heads = []
    for r in range(NSA_REP):
        s = s_c[:, r * tq:(r + 1) * tq] + bias_c
        e = jnp.exp2(s - jnp.max(s, axis=0, keepdims=True))
        p_heads.append(e * jnp.where(has_cmp, 1.0 / jnp.sum(e, axis=0, keepdims=True), 0.0))
    o_cmp = _dot(vc_t, jnp.concatenate(p_heads, axis=1).astype(BF16))
    p_sum = p_heads[0]
    for r in range(1, NSA_REP):
        p_sum = p_sum + p_heads[r]
    return o_cmp, p_sum


def _nsa_selection_rows(p_sum, imp_ref, t0, tq, seq, n_rank):
    n_cp = p_sum.shape[0]
    n_cmp = (seq - CMP_LEN) // CMP_STRIDE + 1
    n_slc = seq // SLC_LEN
    n_sel = min(SLC_TOPK, n_slc)
    sb = _iota((LANES, n_cp), 0)
    cb = _iota((LANES, n_cp), 1)
    overlap_t = ((cb * CMP_STRIDE < sb * SLC_LEN + SLC_LEN) & (cb * CMP_STRIDE + CMP_LEN > sb * SLC_LEN)
                 & (sb < n_slc) & (cb < n_cmp))
    imp_t = _dot3_exact_lhs(jnp.where(overlap_t, 1.0, 0.0).astype(BF16), p_sum)
    blk = _iota((LANES, tq), 0)
    cur = (t0 + _iota((LANES, tq), 1)) >> SLC_SHIFT
    forced = (blk == 0) | (blk == cur) | (blk == cur - 1)
    imp_ref[...] = jnp.where(forced, FORCED_SCORE, jnp.where(blk <= cur, imp_t, -1.0))

    def ranked_rows(n_rows):
        if n_rows <= n_sel:
            return jnp.zeros((HEAD_DIM, tq), F32)
        n_grp = n_rows // 8
        vals = [imp_ref[8 * j:8 * (j + 1), :] for j in range(n_grp)]
        ranks = [jnp.zeros((8, tq), F32) for _ in range(n_grp)]
        sub = _iota((8, tq), 0)
        for i in range(n_rows):
            row = jnp.broadcast_to(imp_ref[i:i + 1, :], (8, tq))
            for j in range(n_grp):
                if 8 * j > i:
                    beats = jnp.where(row >= vals[j], 1.0, 0.0)
                elif 8 * j + 7 <= i:
                    beats = jnp.where(row > vals[j], 1.0, 0.0)
                else:
                    beats = jnp.where(sub + 8 * j > i, jnp.where(row >= vals[j], 1.0, 0.0),
                                      jnp.where(row > vals[j], 1.0, 0.0))
                ranks[j] = ranks[j] + beats
        return jnp.concatenate([jnp.where(rk < n_sel, 0.0, NEG_INF) for rk in ranks]
                               + [jnp.zeros((HEAD_DIM - n_rows, tq), F32)] * (n_rows < HEAD_DIM), axis=0)

    return ranked_rows(n_rank)


def _window_scores(q_rot, kw_ref, g, qb, tq, may_precede_start):
    n_wb = WINDOW // tq + 1
    ki = _iota((tq, tq), 0)
    ci = _iota((tq, tq), 1)
    blocks = []
    for j in range(n_wb):
        kb = qb - (n_wb - 1) + j
        kb_read = jnp.maximum(kb, 0) if may_precede_start else kb
        s = _dot(kw_ref[pl.ds(pl.multiple_of(kb_read * tq, tq), tq), g * LANES:(g + 1) * LANES], q_rot)
        if j == 0:
            s = s + _heads_tiled(jnp.where(ki > ci, 0.0, NEG_INF))
        if j == n_wb - 1:
            s = s + _heads_tiled(jnp.where(ki <= ci, 0.0, NEG_INF))
        elif may_precede_start:
            s = s + jnp.where(kb >= 0, 0.0, NEG_INF)
        blocks.append((s, kb_read))
    return blocks


def _window_finish(blocks, vwt_ref, g, tq):
    m_w = blocks[0][0].max(axis=0, keepdims=True)
    for s, _ in blocks[1:]:
        m_w = jnp.maximum(m_w, s.max(axis=0, keepdims=True))
    o_aug = jnp.zeros((LANES, NSA_REP * tq), F32)
    for s, kb_read in blocks:
        o_aug = o_aug + _dot(vwt_ref[g, kb_read], jnp.exp2(s - m_w).astype(BF16))
    return o_aug[0:HEAD_DIM] * (1.0 / o_aug[HEAD_DIM:HEAD_DIM + 1])


def _nsa_kernel(qraw_ref, qrot_ref, kc_ref, vct_ref, ks_ref, vst_ref, kw_ref, vwt_ref, gate_ref, *rest,
                seq, tq, tk, q_lo, n_rank, n_cmp_rows, has_prev):
    o_ref, imp_ref = rest[1:] if has_prev else rest
    qb = pl.program_id(1) + q_lo
    t0 = qb * tq
    groups = range(NSA_KV_HEADS)
    n_cols = NSA_REP * tq

    def grp(g):
        return slice(g * LANES, (g + 1) * LANES)

    def q_cols(g):
        return slice(g * NSA_REP * LANES, (g + 1) * NSA_REP * LANES)

    def q_aug(q_t, extra_rows):
        return jnp.concatenate([q_t, extra_rows], axis=0).astype(BF16)

    no_extra = jnp.zeros((HEAD_DIM, n_cols), F32)
    q_rot_t = [_q_transposed(qrot_ref[:, q_cols(g)]) for g in groups]
    s_cmp = [_dot(kc_ref[0:n_cmp_rows, grp(g)], q_aug(_q_transposed(qraw_ref[:, q_cols(g)]), no_extra))
             for g in groups]
    win_blocks = [_window_scores(q_aug(q_rot_t[g], no_extra), kw_ref, g, qb, tq, q_lo * tq < WINDOW)
                  for g in groups]
    o_cmp, q_sel = [], []
    for g in groups:
        vc_t = jnp.concatenate([vct_ref[g, c] for c in range(n_cmp_rows // LANES)], axis=1)
        o_c, p_sum = _nsa_compressed(s_cmp[g], vc_t, t0, tq, seq)
        o_cmp.append(o_c[0:HEAD_DIM])
        sel_neg = _heads_tiled(_nsa_selection_rows(p_sum, imp_ref.at[g], t0, tq, seq, n_rank))
        q_sel.append(q_aug(q_rot_t[g], sel_neg))
    o_win = [_window_finish(win_blocks[g], vwt_ref, g, tq) for g in groups]

    blocks_per_tile = tk // LANES
    n_full = t0 // tk
    n_iter = (n_full + SLC_UNROLL - 1) // SLC_UNROLL

    def k_tile(kt, g):
        return ks_ref[pl.ds(pl.multiple_of(kt * tk, tk), tk), grp(g)]

    def v_tile(kt, g):
        return jnp.concatenate([vst_ref[g, kt * blocks_per_tile + c] for c in range(blocks_per_tile)], axis=1)

    def online_update(scores, values, m_i, acc):
        m_new = m_i
        for s in scores:
            m_new = jnp.maximum(m_new, s.max(axis=0, keepdims=True))
        p = jnp.concatenate([jnp.exp2(s - m_new).astype(BF16) for s in scores], axis=0)
        return m_new, jnp.exp2(m_i - m_new) * acc + _dot(jnp.concatenate(values, axis=1), p)

    pair_w = 2 * tq
    chains = [(g, h) for g in groups for h in range(NSA_REP // 2)]

    def q_pair(g, h):
        return q_sel[g][:, h * pair_w:(h + 1) * pair_w]

    def slc_step(j, stats):
        tiles = [(jnp.minimum(j * SLC_UNROLL + u, n_full - 1), j * SLC_UNROLL + u < n_full)
                 for u in range(SLC_UNROLL)]
        scores = [[_dot(k_tile(kt, g), q_pair(g, h)) for kt, _ in tiles] for g, h in chains]
        out = []
        for c, (g, h) in enumerate(chains):
            values = [jnp.where(live, v_tile(kt, g), jnp.zeros((LANES, tk), BF16)) for kt, live in tiles]
            out.append(online_update(scores[c], values, *stats[c]))
        return tuple(out)

    causal = n_full * tk + _iota((tk, tq), 0) <= t0 + _iota((tk, tq), 1)
    causal_bias = _heads_tiled(jnp.where(causal, 0.0, NEG_INF))
    s_diag = [_dot(k_tile(n_full, g), q_sel[g]) for g in groups]
    first = [online_update([s_diag[g] + causal_bias], [v_tile(n_full, g)],
                           jnp.full((1, n_cols), NEG_INF, F32), jnp.zeros((LANES, n_cols), F32)) for g in groups]
    init = tuple(tuple(x[:, h * pair_w:(h + 1) * pair_w] for x in first[g]) for g, h in chains)
    stats = lax.fori_loop(0, n_iter, slc_step, init)

    gate_all = gate_ref[...]
    for g in groups:
        acc = jnp.concatenate([stats[c][1] for c, (cg, _) in enumerate(chains) if cg == g], axis=1)
        o_slc = acc[0:HEAD_DIM] * (1.0 / acc[HEAD_DIM:HEAD_DIM + 1])
        gate_t = jnp.transpose(gate_all[:, grp(g)])
        o_heads = []
        for r in range(NSA_REP):
            cols = slice(r * tq, (r + 1) * tq)
            o_heads.append(gate_t[3 * r:3 * r + 1] * o_cmp[g][:, cols] + gate_t[3 * r + 1:3 * r + 2] * o_slc[:, cols]
                           + gate_t[3 * r + 2:3 * r + 3] * o_win[g][:, cols])
        for pair in range(NSA_REP // 2):
            slab = jnp.transpose(jnp.concatenate(o_heads[2 * pair:2 * pair + 2], axis=0))
            c0 = (g * NSA_REP // 2 + pair) * LANES
            o_ref[:, c0:c0 + LANES] = slab.astype(BF16)


def _nsa_call(qraw, qrot, kc, vct, ks, vst, kw, vwt, gate, *, layer, batch, seq, tq=Q_BLOCK, tk=512):
    assert tq == LANES and tk % tq == 0 and seq % tk == 0 and seq // SLC_LEN <= HEAD_DIM
    nq = seq // tq
    n_cp = seq // CMP_STRIDE
    grp_w = NSA_KV_HEADS * LANES

    def k_spec(rows):
        return pl.BlockSpec((None, rows, grp_w), lambda b, i: (b, 0, 0))

    def vt_spec():
        return pl.BlockSpec((None, NSA_KV_HEADS, seq // LANES, LANES, LANES), lambda b, i: (b, 0, 0, 0, 0))

    ks3, kw3 = ks.reshape(batch, seq, grp_w), kw.reshape(batch, seq, grp_w)
    part = nq // NSA_PARTS
    out = None
    for p in range(NSA_PARTS):
        q_lo, last_token = p * part, (p + 1) * part * tq - 1
        n_rank = min(seq // SLC_LEN, -(-(last_token // SLC_LEN + 1) // 8) * 8)
        n_cmp_rows = min(n_cp, -(-(last_token // CMP_STRIDE + 1) // LANES) * LANES)

        def row_spec(width, q_lo=q_lo):
            return pl.BlockSpec((tq, width), lambda b, i: (b * nq + q_lo + i, 0))

        in_specs = [row_spec(NSA_HEADS * LANES), row_spec(NSA_HEADS * LANES), k_spec(n_cp),
                    pl.BlockSpec((None, NSA_KV_HEADS, n_cp // LANES, LANES, LANES), lambda b, i: (b, 0, 0, 0, 0)),
                    k_spec(seq), vt_spec(), k_spec(seq), vt_spec(), row_spec(grp_w)]
        args = [qraw, qrot, kc, vct, ks3, vst, kw3, vwt, gate]
        if out is not None:
            in_specs.append(pl.BlockSpec(memory_space=pl.ANY))
            args.append(out)
        out = pl.pallas_call(
            functools.partial(_nsa_kernel, seq=seq, tq=tq, tk=tk, q_lo=q_lo, n_rank=n_rank, n_cmp_rows=n_cmp_rows,
                              has_prev=out is not None),
            grid=(batch, part),
            in_specs=in_specs,
            out_specs=pl.BlockSpec((tq, NSA_W), lambda b, i, q_lo=q_lo: (b * nq + q_lo + i, 0)),
            out_shape=jax.ShapeDtypeStruct((batch * seq, NSA_W), BF16),
            input_output_aliases={} if out is None else {len(args) - 1: 0},
            scratch_shapes=[pltpu.VMEM((NSA_KV_HEADS, LANES, tq), F32)],
            compiler_params=_params(2),
            name=f"nsa_attn_l{layer}_p{p}",
        )(*args)
    return out


LIN_BLOCK = 2 * CHUNK


def _lin_block_consts():
    r = _iota((LIN_BLOCK, LANES), 0)
    c = _iota((LIN_BLOCK, LANES), 1)
    same_chunk = (r >= CHUNK) == (c >= CHUNK)
    causal = same_chunk & (r >= c)
    head_diag = same_chunk
    first_rows = r < CHUNK
    low_lanes = c < HEAD_DIM
    return causal, head_diag, first_rows, low_lanes


def _head_stats(x, head_diag_mean):
    hi = x.astype(BF16)
    lo = (x - hi.astype(F32)).astype(BF16)
    return _dot(hi, head_diag_mean) + _dot(lo, head_diag_mean)


def _lin_core(chains, consts):
    causal, head_diag, first_rows, low_lanes = consts
    mask2 = jnp.concatenate([causal, causal], axis=0)
    scores, updates = [], []
    for c in chains:
        zero = jnp.zeros_like(c["q_dec"])
        q2 = jnp.concatenate([jnp.where(low_lanes, c["q_dec"], zero), jnp.where(low_lanes, zero, c["q_dec"])], axis=0)
        scores.append(_dot_nt(q2.astype(BF16), c["k_inv"].astype(BF16)))
    for c in chains:
        v_t = jnp.transpose(c["v"]).astype(BF16)
        kd_ab = jnp.concatenate([jnp.where(first_rows, c["k_dec"], 0.0), jnp.where(first_rows, 0.0, c["k_dec"])],
                                axis=1)
        updates.append(_dot(v_t, kd_ab.astype(BF16)))
    intra, states = [], []
    for c, a2 in zip(chains, scores):
        a2 = jnp.where(mask2, a2, 0.0)
        if c["intra_scale"] is not None:
            a2 = a2 * c["intra_scale"]
        o2 = _dot(a2.astype(BF16), c["v"].astype(BF16))
        intra.append(jnp.where(low_lanes, o2[0:LIN_BLOCK], o2[LIN_BLOCK:]))
    out = []
    for c, upd, o_intra in zip(chains, updates, intra):
        st_a = c["st_prev"] * c["dec_a"] + jnp.where(head_diag, upd[:, 0:LANES], 0.0)
        st_b = st_a * c["dec_b"] + jnp.where(head_diag, upd[:, LANES:], 0.0)
        o_ab = _dot_nt(c["q_int"].astype(BF16), jnp.concatenate([c["st_prev"], st_a], axis=0).astype(BF16))
        out.append((o_intra + jnp.where(first_rows, o_ab[:, 0:LANES], o_ab[:, LANES:]), st_b))
    return out


def _gla_kernel(x_ref, lr_ref, a2_ref, ab_ref, ng_ref, o_ref, st_ref, *, n_blocks, batch):
    @pl.when(pl.program_id(0) == 0)
    def _():
        st_ref[...] = jnp.zeros_like(st_ref)

    consts = _lin_block_consts()
    causal, head_diag, first_rows, low_lanes = consts
    ri = _iota((LIN_BLOCK, LANES), 0)
    ci = _iota((LIN_BLOCK, LANES), 1)
    tri_t = jnp.where(((ri >= CHUNK) == (ci >= CHUNK)) & (ri <= ci), 1.0, 0.0).astype(BF16)
    mean_op = jnp.where(head_diag, 1.0 / HEAD_DIM, 0.0).astype(BF16)
    a2h = a2_ref[...].astype(BF16)
    a2l = (a2_ref[...] - a2h.astype(F32)).astype(BF16)

    def block(i):
        rows = pl.ds(i * LIN_BLOCK, LIN_BLOCK)
        lr = jnp.concatenate([lr_ref[bi, rows, :] for bi in range(batch)], axis=0)
        lh = lr.astype(BF16)
        ll = (lr - lh.astype(F32)).astype(BF16)
        z = _dot(lh, a2h) + _dot(lh, a2l) + _dot(ll, a2h) + ab_ref[...]
        log_a = (jnp.minimum(z, 0.0) - jnp.log(1.0 + jnp.exp(-jnp.abs(z)))) * (1.0 / GLA_TAU)
        n_slab = GLA_W // LANES
        la_t = jnp.concatenate([jnp.transpose(log_a[bi * LIN_BLOCK:(bi + 1) * LIN_BLOCK, s * LANES:(s + 1) * LANES])
                                for bi in range(batch) for s in range(n_slab)], axis=0)
        cum_t = _dot3_exact_rhs(la_t, tri_t)
        chains = []
        for bi in range(batch):
            for s in range(n_slab):
                cols = slice(s * LANES, (s + 1) * LANES)
                n = bi * n_slab + s

                def ld(k):
                    return x_ref[bi, rows, k * GLA_W + s * LANES:k * GLA_W + (s + 1) * LANES]

                q, k, v, gg = ld(0), ld(1), ld(2), ld(3)
                b = jnp.transpose(cum_t[n * LANES:(n + 1) * LANES])
                bl_a = b[CHUNK - 1:CHUNK, :]
                bl_b = b[LIN_BLOCK - 1:LIN_BLOCK, :]
                bl = jnp.where(first_rows, bl_a, bl_b)
                q_dec = q * QK_SCALE * jnp.exp(b)
                chains.append(dict(bi=bi, s=s, cols=cols, gate=gg, q_dec=q_dec, q_int=q_dec, k_inv=k * jnp.exp(-b),
                                   k_dec=k * jnp.exp(bl - b), v=v, intra_scale=None, dec_a=jnp.exp(bl_a),
                                   dec_b=jnp.exp(bl_b), st_prev=st_ref[bi, s]))
        results = _lin_core(chains, consts)
        for c, (_, st_new) in zip(chains, results):
            st_ref[c["bi"], c["s"]] = st_new
        ms = _head_stats(jnp.concatenate([o * o for o, _ in results], axis=0), mean_op)
        for n, (c, (o, _)) in enumerate(zip(chains, results)):
            y = (o * lax.rsqrt(ms[n * LIN_BLOCK:(n + 1) * LIN_BLOCK] + NORM_EPS) * ng_ref[:, c["cols"]]
                 * _silu(c["gate"]))
            o_ref[c["bi"], rows, c["cols"]] = y.astype(BF16)

    for i in range(n_blocks):
        block(i)


def _ret_kernel(x_ref, dmat_ref, qf_ref, kf_ref, cd_ref, ng_ref, o_ref, st_ref, *, n_blocks, batch):
    @pl.when(pl.program_id(0) == 0)
    def _():
        st_ref[...] = jnp.zeros_like(st_ref)

    consts = _lin_block_consts()
    causal, head_diag, first_rows, low_lanes = consts
    mean_op = jnp.where(head_diag, 1.0 / HEAD_DIM, 0.0).astype(BF16)

    def block(i):
        rows = pl.ds(i * LIN_BLOCK, LIN_BLOCK)
        chains = []
        for bi in range(batch):
            for s in range(RET_W // LANES):
                cols = slice(s * LANES, (s + 1) * LANES)

                def ld(k):
                    return x_ref[bi, rows, k * RET_W + s * LANES:k * RET_W + (s + 1) * LANES]

                q, k, v, gg = ld(0), ld(1), ld(2), ld(3)
                cd = cd_ref[:, cols]
                chains.append(dict(bi=bi, s=s, cols=cols, gate=gg, q_dec=q, q_int=q * qf_ref[:, cols], k_inv=k,
                                   k_dec=k * kf_ref[:, cols], v=v, intra_scale=dmat_ref[s], dec_a=cd, dec_b=cd,
                                   st_prev=st_ref[bi, s]))
        results = _lin_core(chains, consts)
        for c, (_, st_new) in zip(chains, results):
            st_ref[c["bi"], c["s"]] = st_new
        o_all = jnp.concatenate([o for o, _ in results], axis=0)
        oc_all = o_all - _head_stats(o_all, mean_op)
        var_all = _head_stats(oc_all * oc_all, mean_op)
        y_all = oc_all * lax.rsqrt(var_all + NORM_EPS)
        for n, c in enumerate(chains):
            y = y_all[n * LIN_BLOCK:(n + 1) * LIN_BLOCK] * ng_ref[:, c["cols"]] * _silu(c["gate"])
            o_ref[c["bi"], rows, c["cols"]] = y.astype(BF16)

    for i in range(n_blocks):
        block(i)


def _gla_call(gla, lr, a2_pad, a_bias, norm_g, *, layer, batch, seq, tb=LIN_BLOCK):
    n_slab = GLA_W // LANES

    def lspec(shape):
        return pl.BlockSpec((None,) + shape, lambda t: (layer, 0, 0))

    return pl.pallas_call(
        functools.partial(_gla_kernel, n_blocks=tb // LIN_BLOCK, batch=batch),
        grid=(seq // tb,),
        in_specs=[pl.BlockSpec((batch, tb, 4 * GLA_W), lambda t: (0, t, 0)),
                  pl.BlockSpec((batch, tb, LANES), lambda t: (0, t, 0)),
                  lspec((LANES, GLA_W)), lspec((1, GLA_W)), lspec((1, GLA_W))],
        out_specs=pl.BlockSpec((batch, tb, GLA_W), lambda t: (0, t, 0)),
        out_shape=jax.ShapeDtypeStruct((batch, seq, GLA_W), BF16),
        scratch_shapes=[pltpu.VMEM((batch, n_slab, LANES, LANES), F32)],
        compiler_params=_params(1),
        name=f"gla_l{layer}",
    )(gla.reshape(batch, seq, 4 * GLA_W), lr.reshape(batch, seq, LANES), a2_pad, a_bias, norm_g
      ).reshape(batch * seq, GLA_W)


def _ret_call(ret, dmat, qf, kf, cd, norm_g, *, layer, batch, seq, tb=LIN_BLOCK):
    n_slab = RET_W // LANES
    full = lambda a: pl.BlockSpec(a.shape, lambda t: (0,) * a.ndim)
    return pl.pallas_call(
        functools.partial(_ret_kernel, n_blocks=tb // LIN_BLOCK, batch=batch),
        grid=(seq // tb,),
        in_specs=[pl.BlockSpec((batch, tb, 4 * RET_W), lambda t: (0, t, 0)),
                  full(dmat), full(qf), full(kf), full(cd),
                  pl.BlockSpec((None, 1, RET_W), lambda t: (layer, 0, 0))],
        out_specs=pl.BlockSpec((batch, tb, RET_W), lambda t: (0, t, 0)),
        out_shape=jax.ShapeDtypeStruct((batch, seq, RET_W), BF16),
        scratch_shapes=[pltpu.VMEM((batch, n_slab, LANES, LANES), F32)],
        compiler_params=_params(1),
        name=f"ret_l{layer}",
    )(ret.reshape(batch, seq, 4 * RET_W), dmat, qf, kf, cd, norm_g).reshape(batch * seq, RET_W)


def _layout_proj_weights(w_in, nsa_gate_bias):
    n_layers, d, _ = w_in.shape
    w_in = w_in.astype(BF16)
    (g_q, g_k, g_v, g_g, g_lr, n_q, n_kc, n_vc, n_ks, n_vs, n_kw, n_vw, n_gate,
     r_q, r_k, r_v, r_g) = jnp.split(w_in, IN_SPLITS, axis=-1)
    zeros = lambda w: jnp.zeros((n_layers, d, w), w_in.dtype)
    def pad_heads(w, n_heads):
        w = w.reshape(n_layers, d, n_heads, HEAD_DIM)
        return jnp.concatenate([w, jnp.zeros_like(w)], axis=-1).reshape(n_layers, d, n_heads * LANES)

    per_grp = 3 * NSA_REP
    gate_cols = []
    bias_cols = []
    for g in range(NSA_KV_HEADS):
        gate_cols += [n_gate[..., g * per_grp:(g + 1) * per_grp], zeros(LANES - per_grp)]
        bias_cols += [nsa_gate_bias[:, g * per_grp:(g + 1) * per_grp],
                      jnp.zeros((n_layers, LANES - per_grp), nsa_gate_bias.dtype)]
    w = jnp.concatenate([g_q, g_k, g_v, g_g, g_lr, zeros(LANES - GLA_LOWRANK), pad_heads(n_q, NSA_HEADS),
                         n_kc, n_vc, pad_heads(n_ks, NSA_KV_HEADS), pad_heads(n_kw, NSA_KV_HEADS), n_vs, n_vw]
                        + gate_cols + [r_q, r_k, r_v, r_g], axis=-1)
    assert w.shape[-1] == PROJ_W
    bias = jnp.concatenate(bias_cols, axis=-1).reshape(n_layers, 1, NSA_KV_HEADS * LANES)
    return w.astype(BF16), bias


def _layout_cmp_weights(pe_k, pe_v, w1_k, w2_k, w1_v, w2_v):
    n_layers = pe_k.shape[0]
    eye = jnp.eye(2, dtype=BF16)
    w1 = jnp.stack([w1_k, w1_v], axis=1).reshape(n_layers, 2, CMP_LEN, HEAD_DIM, CMP_HIDDEN).astype(BF16)
    blocks = [jnp.pad(w1[:, kv], ((0, 0), (0, 0), (0, 0), ((2 * kv + g) * CMP_HIDDEN, (3 - 2 * kv - g) * CMP_HIDDEN)))
              for kv in range(2) for g in range(NSA_KV_HEADS)]
    w1_full = jnp.stack(blocks, axis=2).reshape(n_layers, CMP_LEN * 2 * NSA_KV_W, 4 * CMP_HIDDEN)
    half = CMP_STRIDE * 2 * NSA_KV_W
    w2 = jnp.stack([w2_k, w2_v], axis=1).astype(BF16)
    w2_full = jnp.einsum('zknd,kq,gh->zqhnkgd', w2, eye, eye)
    w2_full = w2_full.reshape(n_layers, 4 * CMP_HIDDEN, 2 * NSA_KV_W)
    w2_k = w2_full[..., :NSA_KV_W].reshape(n_layers, 4 * CMP_HIDDEN, NSA_KV_HEADS, HEAD_DIM)
    w2_k = jnp.concatenate([w2_k, jnp.zeros_like(w2_k)], axis=-1).reshape(n_layers, 4 * CMP_HIDDEN, -1)
    w2_full = jnp.concatenate([w2_k, w2_full[..., NSA_KV_W:]], axis=-1).astype(BF16)
    pe = jnp.stack([pe_k, pe_v], axis=1)
    pe = jnp.broadcast_to(pe.transpose(0, 2, 1, 3)[:, :, :, None, :],
                          (n_layers, CMP_LEN, 2, NSA_KV_HEADS, HEAD_DIM))
    pe = pe.reshape(n_layers, 1, CMP_LEN * 2 * NSA_KV_W)
    return pe[:, :, :half], pe[:, :, half:], w1_full.reshape(n_layers, 2, half, 4 * CMP_HIDDEN), w2_full


def _rotary_tables(seq):
    half = HEAD_DIM // 2
    inv_freq = ROPE_THETA ** (-jnp.arange(half, dtype=F32) / half)
    ang = jnp.arange(seq).astype(F32)[:, None] * inv_freq[None, :]
    cos, sin = jnp.cos(ang), jnp.sin(ang)
    reps = LANES // HEAD_DIM
    return jnp.tile(jnp.concatenate([cos, cos], axis=-1), (1, reps)), \
        jnp.tile(jnp.concatenate([-sin, sin], axis=-1), (1, reps))


def _retention_tables():
    log_gamma = jnp.log1p(-jnp.exp2(-5.0 - jnp.arange(RET_HEADS, dtype=F32)))
    lg_lane = jnp.repeat(log_gamma, HEAD_DIM)[None, :]
    pos = (jnp.arange(LIN_BLOCK) % CHUNK).astype(F32)[:, None]
    qf = jnp.exp(lg_lane * (pos + 1.0))
    kf = jnp.exp(lg_lane * (CHUNK - 1.0 - pos))
    cd = jnp.exp(lg_lane * CHUNK)
    r = jnp.arange(LIN_BLOCK)
    rel = (r[:, None] - r[None, :]).astype(F32)
    ok = ((r[:, None] // CHUNK) == (r[None, :] // CHUNK)) & (rel >= 0)
    dm = jnp.where(ok[None], jnp.exp(log_gamma[:, None, None] * rel[None]), 0.0)
    dmat = dm.reshape(RET_W // LANES, 2 * LIN_BLOCK, LIN_BLOCK)
    return dmat, qf, kf, cd


def kernel(x, c, w_ada, b_ada, norm_g, ffn1_in, ffn1_out, w_in, gla_a2, gla_a_bias, gla_norm_g,
           nsa_pe_k, nsa_pe_v, nsa_w1_k, nsa_w2_k, nsa_w1_v, nsa_w2_v, nsa_gate_bias, ret_norm_g,
           w_out, ffn2_in, ffn2_out, final_norm_g):
    batch, seq, d = x.shape
    n_layers = w_ada.shape[0]
    assert batch <= 8 and seq % 512 == 0 and seq >= WINDOW + Q_BLOCK

    c_pad = jnp.zeros((8, d), F32).at[:batch].set(c)
    mod = _ada_call(c_pad, w_ada, b_ada).reshape(n_layers * 8 * N_ADA, 1, d)
    ng = norm_g.reshape(n_layers * 3, 1, d)

    w_proj, gate_bias = _layout_proj_weights(w_in, nsa_gate_bias)
    pe_top, pe_bot, w1_cmp, w2_cmp = _layout_cmp_weights(nsa_pe_k, nsa_pe_v, nsa_w1_k, nsa_w2_k, nsa_w1_v, nsa_w2_v)
    cosf, sinf = _rotary_tables(seq)
    dmat, qf, kf, cd = _retention_tables()
    a2_pad = jnp.zeros((n_layers, LANES, GLA_W), F32).at[:, :GLA_LOWRANK].set(gla_a2)
    a_bias = gla_a_bias.reshape(n_layers, 1, GLA_W)
    gla_g = jnp.tile(gla_norm_g, (1, GLA_HEADS)).reshape(n_layers, 1, GLA_W)
    ret_g = jnp.tile(ret_norm_g, (1, RET_HEADS)).reshape(n_layers, 1, RET_W)
    f1_in, f1_out = ffn1_in.astype(BF16), ffn1_out.astype(BF16)
    f2_in, f2_out = ffn2_in.astype(BF16), ffn2_out.astype(BF16)
    w_o = w_out.astype(BF16)

    xs = x.reshape(batch * seq, d)
    for l in range(n_layers):
        xs = _ffn_call(xs, ng, mod, f1_in, f1_out, layer=l, sub=0, seq=seq)
        (gla, lr, qraw, qrot, kcvc, ks, vs, kw, vw, gate, ret) = _proj_call(
            xs, ng, mod, w_proj, gate_bias, cosf, sinf, layer=l, seq=seq)
        kc, vc = _cmp_call(kcvc, pe_top, pe_bot, w1_cmp, w2_cmp, layer=l, batch=batch, seq=seq)
        o_gla = _gla_call(gla, lr, a2_pad, a_bias, gla_g, layer=l, batch=batch, seq=seq)
        o_ret = _ret_call(ret, dmat, qf, kf, cd, ret_g, layer=l, batch=batch, seq=seq)
        o_nsa = _nsa_call(qraw, qrot, kc, vc, ks, vs, kw, vw, gate, layer=l, batch=batch, seq=seq)
        final_g = final_norm_g.reshape(1, d) if l == n_layers - 1 else None
        xs = _ffn_call(xs, ng, mod, f2_in, f2_out, layer=l, sub=2, seq=seq, mixer=(o_gla, o_nsa, o_ret, w_o),
                       final_g=final_g)
    return xs.reshape(batch, seq, d)
```

```python
import functools
import math

import numpy as np
import jax
import jax.numpy as jnp
from jax import lax
from jax.experimental import pallas as pl
from jax.experimental.pallas import tpu as pltpu

F32 = jnp.float32
BF16 = jnp.bfloat16

HEAD_DIM = 64
LANES = 128
GLA_HEADS = 4
NSA_HEADS = 8
NSA_KV_HEADS = 2
NSA_REP = NSA_HEADS // NSA_KV_HEADS
RET_HEADS = 4
GLA_LOWRANK = 16
GLA_TAU = 16.0
CHUNK = 64
CMP_LEN = 32
CMP_STRIDE = 16
CMP_HIDDEN = 128
SLC_LEN = 64
SLC_TOPK = 16
WINDOW = 512
Q_BLOCK = 128
ROPE_THETA = 10000.0
FFN_HALF = 0.5
NORM_EPS = 1e-6
NEG_INF = -1e30
FORCED_SCORE = 1e4
N_ADA = 9
QK_SCALE = HEAD_DIM ** -0.5
SLC_SHIFT = int(math.log2(SLC_LEN))
LOG2_E = math.log2(math.e)

GLA_W = GLA_HEADS * HEAD_DIM
NSA_W = NSA_HEADS * HEAD_DIM
NSA_KV_W = NSA_KV_HEADS * HEAD_DIM
RET_W = RET_HEADS * HEAD_DIM
IN_SIZES = (GLA_W, GLA_W, GLA_W, GLA_W, GLA_LOWRANK,
            NSA_W, NSA_KV_W, NSA_KV_W, NSA_KV_W, NSA_KV_W, NSA_KV_W, NSA_KV_W, 3 * NSA_HEADS,
            RET_W, RET_W, RET_W, RET_W)
IN_SPLITS = tuple(int(s) for s in np.cumsum(IN_SIZES)[:-1])

SEG_GLA = 0
SEG_LR = SEG_GLA + 4 * GLA_W
SEG_NQ = SEG_LR + LANES
SEG_KCVC = SEG_NQ + NSA_HEADS * LANES
SEG_KS = SEG_KCVC + 2 * NSA_KV_W
SEG_KW = SEG_KS + NSA_KV_HEADS * LANES
SEG_VS = SEG_KW + NSA_KV_HEADS * LANES
SEG_VW = SEG_VS + NSA_KV_W
SEG_GATE = SEG_VW + NSA_KV_W
SEG_RET = SEG_GATE + NSA_KV_HEADS * LANES
PROJ_W = SEG_RET + 4 * RET_W

VMEM_LIMIT_BYTES = 56 * 1024 * 1024

NT_DIMS = (((1,), (1,)), ((), ()))


def _dot(a, b):
    return jnp.dot(a, b, preferred_element_type=F32)


def _dot_nt(a, b):
    return lax.dot_general(a, b, NT_DIMS, preferred_element_type=F32)


def _split3(x):
    hi = x.astype(BF16)
    r1 = x - hi.astype(F32)
    mid = r1.astype(BF16)
    lo = (r1 - mid.astype(F32)).astype(BF16)
    return hi, mid, lo


def _dot3_exact_rhs(x, w_bf16):
    hi, mid, lo = _split3(x)
    return _dot(hi, w_bf16) + _dot(mid, w_bf16) + _dot(lo, w_bf16)


def _dot3_exact_lhs(w_bf16, x):
    hi, mid, lo = _split3(x)
    return _dot(w_bf16, hi) + _dot(w_bf16, mid) + _dot(w_bf16, lo)


def _sigmoid(x):
    return 1.0 / (1.0 + jnp.exp(-x))


def _silu(x):
    return x * _sigmoid(x)


def _iota(shape, dim):
    return lax.broadcasted_iota(jnp.int32, shape, dim)


def _params(n_grid):
    return pltpu.CompilerParams(dimension_semantics=("arbitrary",) * n_grid,
                                vmem_limit_bytes=VMEM_LIMIT_BYTES)


def _norm_mod(x, g, sh, sc):
    ms = jnp.mean(x * x, axis=-1, keepdims=True)
    return x * lax.rsqrt(ms + NORM_EPS) * (g * (1.0 + sc)) + sh


def _ada_kernel(c_ref, w_ref, b_ref, o_ref):
    c = c_ref[...]
    o_ref[...] = _dot(_silu(c).astype(BF16), w_ref[...].astype(BF16)) + b_ref[...]


def _ada_call(c_pad, w_ada, b_ada):
    n_layers, d, n_out = w_ada.shape
    tn = 2304 if n_out % 2304 == 0 else n_out
    return pl.pallas_call(
        _ada_kernel,
        grid=(n_layers, n_out // tn),
        in_specs=[pl.BlockSpec((8, d), lambda l, j: (0, 0)),
                  pl.BlockSpec((None, d, tn), lambda l, j: (l, 0, j)),
                  pl.BlockSpec((None, 1, tn), lambda l, j: (l, 0, j))],
        out_specs=pl.BlockSpec((None, 8, tn), lambda l, j: (l, 0, j)),
        out_shape=jax.ShapeDtypeStruct((n_layers, 8, n_out), F32),
        compiler_params=_params(2),
        name="ada_mod",
    )(c_pad, w_ada, b_ada.reshape(n_layers, 1, n_out))


def _ffn_kernel(x_ref, g_ref, sh_ref, sc_ref, gt_ref, win_ref, wout_ref, *rest, d_ff, chunks, mixer, final):
    rest = list(rest)
    x = x_ref[...]
    if mixer:
        og_ref, on_ref, or_ref, gm_ref, wo_ref = rest[:5]
        rest = rest[5:]
        x = x + gm_ref[...] * (_dot(og_ref[...], wo_ref[0:GLA_W, :])
                               + _dot(on_ref[...], wo_ref[GLA_W:GLA_W + NSA_W, :])
                               + _dot(or_ref[...], wo_ref[GLA_W + NSA_W:, :]))
    if final:
        fg_ref, o_ref, act_ref = rest
    else:
        o_ref, act_ref = rest
    h = _norm_mod(x, g_ref[...], sh_ref[...], sc_ref[...]).astype(BF16)
    for (c0, cw) in chunks:
        gate = _dot(h, win_ref[:, c0:c0 + cw])
        up = _dot(h, win_ref[:, d_ff + c0:d_ff + c0 + cw])
        act_ref[:, c0:c0 + cw] = (_silu(gate) * up).astype(BF16)
    y = _dot(act_ref[...], wout_ref[...])
    xn = x + (FFN_HALF * gt_ref[...]) * y
    if final:
        ms = jnp.mean(xn * xn, axis=-1, keepdims=True)
        xn = xn * lax.rsqrt(ms + NORM_EPS) * fg_ref[...]
    o_ref[...] = xn


def _ffn_call(x, ng, mod, w_in, w_out, *, layer, sub, seq, mixer=None, final_g=None, tm=1024):
    m, d = x.shape
    d_ff = w_out.shape[1]
    tiles_per_seq = seq // tm
    chunks, c0 = [], 0
    while c0 < d_ff:
        cw = min(512, d_ff - c0)
        chunks.append((c0, cw))
        c0 += cw

    def mod_spec(k):
        return pl.BlockSpec((None, 1, d), lambda i: ((layer * 8 + i // tiles_per_seq) * N_ADA + k, 0, 0))

    in_specs = [pl.BlockSpec((tm, d), lambda i: (i, 0)),
                pl.BlockSpec((None, 1, d), lambda i: (layer * 3 + sub, 0, 0)),
                mod_spec(3 * sub), mod_spec(3 * sub + 1), mod_spec(3 * sub + 2),
                pl.BlockSpec((None, d, 2 * d_ff), lambda i: (layer, 0, 0), pipeline_mode=pl.Buffered(1)),
                pl.BlockSpec((None, d_ff, d), lambda i: (layer, 0, 0), pipeline_mode=pl.Buffered(1))]
    args = [x, ng, mod, mod, mod, w_in, w_out]
    if mixer is not None:
        o_gla, o_nsa, o_ret, w_mix = mixer
        in_specs += [pl.BlockSpec((tm, o.shape[1]), lambda i: (i, 0)) for o in (o_gla, o_nsa, o_ret)]
        in_specs += [mod_spec(5), pl.BlockSpec((None,) + w_mix.shape[1:], lambda i: (layer, 0, 0),
                                               pipeline_mode=pl.Buffered(1))]
        args += [o_gla, o_nsa, o_ret, mod, w_mix]
    if final_g is not None:
        in_specs.append(pl.BlockSpec((1, d), lambda i: (0, 0)))
        args.append(final_g)
    return pl.pallas_call(
        functools.partial(_ffn_kernel, d_ff=d_ff, chunks=tuple(chunks), mixer=mixer is not None,
                          final=final_g is not None),
        grid=(m // tm,),
        in_specs=in_specs,
        out_specs=pl.BlockSpec((tm, d), lambda i: (i, 0)),
        out_shape=jax.ShapeDtypeStruct((m, d), F32),
        scratch_shapes=[pltpu.VMEM((tm, d_ff), BF16)],
        compiler_params=_params(1),
        name=f"ffn_l{layer}_s{sub}",
    )(*args)


def _rotate(x, cosf, sinf):
    rows, width = x.shape
    first = (_iota((rows, LANES), 1) & (HEAD_DIM - 1)) < HEAD_DIM // 2
    outs = []
    for j in range(width // LANES):
        xs = x[:, j * LANES:(j + 1) * LANES]
        swapped = jnp.where(first, pltpu.roll(xs, LANES - HEAD_DIM // 2, 1), pltpu.roll(xs, HEAD_DIM // 2, 1))
        outs.append(xs * cosf + swapped * sinf)
    return outs[0] if len(outs) == 1 else jnp.concatenate(outs, axis=1)


def _store_group_transposed(dst_ref, x, fill):
    top = _iota((LANES, LANES), 0) < HEAD_DIM
    for c in range(x.shape[0] // LANES):
        piece = jnp.transpose(x[c * LANES:(c + 1) * LANES])
        dst_ref[0, c] = jnp.where(top, piece, fill).astype(dst_ref.dtype)
        dst_ref[1, c] = jnp.where(top, pltpu.roll(piece, HEAD_DIM, 0), fill).astype(dst_ref.dtype)


def _proj_kernel(x_ref, g_ref, sh_ref, sc_ref, w_ref, gb_ref, cos_ref, sin_ref,
                 gla_ref, lr_ref, qraw_ref, qrot_ref, kcvc_ref, ks_ref, vst_ref, kw_ref, vwt_ref,
                 gate_ref, ret_ref, *, tiles_per_seq):
    h = _norm_mod(x_ref[...], g_ref[...], sh_ref[...], sc_ref[...]).astype(BF16)
    cosf = cos_ref[...]
    sinf = sin_ref[...]
    tm = h.shape[0]

    def seg(c0, width):
        return _dot(h, w_ref[:, c0:c0 + width])

    nq = seg(SEG_NQ, NSA_HEADS * LANES)
    qraw_ref[...] = (nq * (QK_SCALE * LOG2_E)).astype(BF16)
    qrot_ref[...] = (_rotate(nq, cosf, sinf) * (QK_SCALE * LOG2_E)).astype(BF16)
    ret_ref[:, 0:RET_W] = _rotate(seg(SEG_RET, RET_W), cosf, sinf)
    ret_ref[:, RET_W:2 * RET_W] = _rotate(seg(SEG_RET + RET_W, RET_W), cosf, sinf) * QK_SCALE
    grp_w = NSA_KV_HEADS * LANES
    pos = (pl.program_id(0) % tiles_per_seq) * tm + _iota((tm, grp_w), 0)
    spare = (_iota((tm, grp_w), 1) & (LANES - 1)) - HEAD_DIM
    blk_onehot = jnp.where(spare == (pos >> SLC_SHIFT), 1.0, 0.0)
    ks_ref[...] = (_rotate(seg(SEG_KS, grp_w), cosf, sinf) + blk_onehot).astype(BF16)
    kw_ref[...] = _rotate(seg(SEG_KW, grp_w), cosf, sinf).astype(BF16)
    _store_group_transposed(vst_ref, seg(SEG_VS, NSA_KV_W), 1.0)
    _store_group_transposed(vwt_ref, seg(SEG_VW, NSA_KV_W), 1.0)
    gate_ref[...] = _sigmoid(seg(SEG_GATE, NSA_KV_HEADS * LANES) + gb_ref[...])
    kcvc_ref[...] = seg(SEG_KCVC, 2 * NSA_KV_W)
    lr_ref[...] = seg(SEG_LR, LANES)
    ret_ref[:, 2 * RET_W:4 * RET_W] = seg(SEG_RET + 2 * RET_W, 2 * RET_W)
    gla_ref[...] = seg(SEG_GLA, 4 * GLA_W)


def _proj_call(x, ng, mod, w_proj, gate_bias, cosf, sinf, *, layer, seq, tm=512):
    m, d = x.shape
    tiles_per_seq = seq // tm

    def mod_spec(k):
        return pl.BlockSpec((None, 1, d), lambda i: ((layer * 8 + i // tiles_per_seq) * N_ADA + k, 0, 0))

    def row_spec(width):
        return pl.BlockSpec((tm, width), lambda i: (i, 0))

    batch = m // seq
    grp_w = NSA_KV_HEADS * LANES

    def vt_spec():
        return pl.BlockSpec((None, NSA_KV_HEADS, tm // LANES, LANES, LANES),
                            lambda i: (i // tiles_per_seq, 0, i % tiles_per_seq, 0, 0))

    vt_shape = jax.ShapeDtypeStruct((batch, NSA_KV_HEADS, seq // LANES, LANES, LANES), BF16)
    out_widths = [(4 * GLA_W, F32), (LANES, F32), (NSA_HEADS * LANES, BF16), (NSA_HEADS * LANES, BF16),
                  (2 * NSA_KV_W, F32), (grp_w, BF16), None, (grp_w, BF16), None,
                  (NSA_KV_HEADS * LANES, F32), (4 * RET_W, F32)]
    return pl.pallas_call(
        functools.partial(_proj_kernel, tiles_per_seq=tiles_per_seq),
        grid=(m // tm,),
        in_specs=[row_spec(d),
                  pl.BlockSpec((None, 1, d), lambda i: (layer * 3 + 1, 0, 0)),
                  mod_spec(3), mod_spec(4),
                  pl.BlockSpec((None, d, PROJ_W), lambda i: (layer, 0, 0), pipeline_mode=pl.Buffered(1)),
                  pl.BlockSpec((None, 1, NSA_KV_HEADS * LANES), lambda i: (layer, 0, 0)),
                  pl.BlockSpec((tm, LANES), lambda i: (i % tiles_per_seq, 0)),
                  pl.BlockSpec((tm, LANES), lambda i: (i % tiles_per_seq, 0))],
        out_specs=[vt_spec() if ow is None else row_spec(ow[0]) for ow in out_widths],
        out_shape=[vt_shape if ow is None else jax.ShapeDtypeStruct((m, ow[0]), ow[1]) for ow in out_widths],
        compiler_params=_params(1),
        name=f"proj_l{layer}",
    )(x, ng, mod, mod, w_proj, gate_bias, cosf, sinf)


def _cmp_kernel(xk_ref, xv_ref, pet_ref, peb_ref, wt_ref, wb_ref, w2_ref, kc_ref, vct_ref):
    n_rows = xk_ref.shape[0] // CMP_STRIDE
    x = jnp.concatenate([ref[pl.ds(l, n_rows, stride=CMP_STRIDE), :]
                         for l in range(CMP_STRIDE) for ref in (xk_ref, xv_ref)], axis=1)
    a = _dot((x + pet_ref[...]).astype(BF16), wt_ref[...])
    b = _dot((x + peb_ref[...]).astype(BF16), wb_ref[...])
    hid = a + pltpu.roll(b, n_rows - 1, 0)
    out = _dot(_silu(hid).astype(BF16), w2_ref[...])
    out = jnp.where(_iota(out.shape, 0) < n_rows - 1, out, 0.0)
    grp_w = NSA_KV_HEADS * LANES
    kc_ref[...] = out[:, 0:grp_w].astype(BF16)
    _store_group_transposed(vct_ref, out[:, grp_w:grp_w + NSA_KV_W], 0.0)


def _cmp_call(kcvc, pe_top, pe_bot, w1_halves, w2, *, layer, batch, seq):
    rows = seq // CMP_STRIDE
    width = CMP_STRIDE * 2 * NSA_KV_W
    x = kcvc.reshape(batch, seq, 2 * NSA_KV_W)
    hid_w = 4 * CMP_HIDDEN
    grp_w = NSA_KV_HEADS * LANES

    def wspec(shape):
        return pl.BlockSpec((None,) + shape, lambda b: (layer, 0, 0))

    return pl.pallas_call(
        _cmp_kernel,
        grid=(batch,),
        in_specs=[pl.BlockSpec((None, seq, NSA_KV_W), lambda b: (b, 0, 0)),
                  pl.BlockSpec((None, seq, NSA_KV_W), lambda b: (b, 0, 1)),
                  wspec((1, width)), wspec((1, width)),
                  pl.BlockSpec((None, None, width, hid_w), lambda b: (layer, 0, 0, 0)),
                  pl.BlockSpec((None, None, width, hid_w), lambda b: (layer, 1, 0, 0)),
                  wspec((hid_w, grp_w + NSA_KV_W))],
        out_specs=[pl.BlockSpec((None, rows, grp_w), lambda b: (b, 0, 0)),
                   pl.BlockSpec((None, NSA_KV_HEADS, rows // LANES, LANES, LANES), lambda b: (b, 0, 0, 0, 0))],
        out_shape=[jax.ShapeDtypeStruct((batch, rows, grp_w), BF16),
                   jax.ShapeDtypeStruct((batch, NSA_KV_HEADS, rows // LANES, LANES, LANES), BF16)],
        compiler_params=_params(1),
        name=f"nsa_compress_l{layer}",
    )(x, x, pe_top, pe_bot, w1_halves, w1_halves, w2)


def _heads_tiled(x):
    return jnp.concatenate([x] * NSA_REP, axis=1)


SLC_UNROLL = 2
NSA_PARTS = 8

def _q_transposed(x):
    x = x.astype(F32)
    return jnp.concatenate([jnp.transpose(x[:, r * LANES:(r + 1) * LANES])[0:HEAD_DIM, :] for r in range(NSA_REP)],
                           axis=1)


def _nsa_compressed(s_c, vc_t, t0, tq, seq):
    n_cp = s_c.shape[0]
    n_cmp = (seq - CMP_LEN) // CMP_STRIDE + 1
    nn = _iota((n_cp, tq), 0)
    tt = t0 + _iota((n_cp, tq), 1)
    bias_c = jnp.where((nn * CMP_STRIDE + (CMP_LEN - 1) <= tt) & (nn < n_cmp), 0.0, NEG_INF)
    has_cmp = t0 + _iota((1, tq), 1) >= CMP_LEN - 1
    p_heads = []
    for r in range(NSA_REP):
        s = s_c[:, r * tq:(r + 1) * tq] + bias_c
        e = jnp.exp2(s - jnp.max(s, axis=0, keepdims=True))
        p_heads.append(e * jnp.where(has_cmp, 1.0 / jnp.sum(e, axis=0, keepdims=True), 0.0))
    o_cmp = _dot(vc_t, jnp.concatenate(p_heads, axis=1).astype(BF16))
    p_sum = p_heads[0]
    for r in range(1, NSA_REP):
        p_sum = p_sum + p_heads[r]
    return o_cmp, p_sum


def _nsa_selection_rows(p_sum, imp_ref, t0, tq, seq, n_rank):
    n_cp = p_sum.shape[0]
    n_cmp = (seq - CMP_LEN) // CMP_STRIDE + 1
    n_slc = seq // SLC_LEN
    n_sel = min(SLC_TOPK, n_slc)
    sb = _iota((LANES, n_cp), 0)
    cb = _iota((LANES, n_cp), 1)
    overlap_t = ((cb * CMP_STRIDE < sb * SLC_LEN + SLC_LEN) & (cb * CMP_STRIDE + CMP_LEN > sb * SLC_LEN)
                 & (sb < n_slc) & (cb < n_cmp))
    imp_t = _dot3_exact_lhs(jnp.where(overlap_t, 1.0, 0.0).astype(BF16), p_sum)
    blk = _iota((LANES, tq), 0)
    cur = (t0 + _iota((LANES, tq), 1)) >> SLC_SHIFT
    forced = (blk == 0) | (blk == cur) | (blk == cur - 1)
    imp_ref[...] = jnp.where(forced, FORCED_SCORE, jnp.where(blk <= cur, imp_t, -1.0))

    def ranked_rows(n_rows):
        if n_rows <= n_sel:
            return jnp.zeros((HEAD_DIM, tq), F32)
        n_grp = n_rows // 8
        vals = [imp_ref[8 * j:8 * (j + 1), :] for j in range(n_grp)]
        ranks = [jnp.zeros((8, tq), F32) for _ in range(n_grp)]
        sub = _iota((8, tq), 0)
        for i in range(n_rows):
            row = jnp.broadcast_to(imp_ref[i:i + 1, :], (8, tq))
            for j in range(n_grp):
                if 8 * j > i:
                    beats = jnp.where(row >= vals[j], 1.0, 0.0)
                elif 8 * j + 7 <= i:
                    beats = jnp.where(row > vals[j], 1.0, 0.0)
                else:
                    beats = jnp.where(sub + 8 * j > i, jnp.where(row >= vals[j], 1.0, 0.0),
                                      jnp.where(row > vals[j], 1.0, 0.0))
                ranks[j] = ranks[j] + beats
        return jnp.concatenate([jnp.where(rk < n_sel, 0.0, NEG_INF) for rk in ranks]
                               + [jnp.zeros((HEAD_DIM - n_rows, tq), F32)] * (n_rows < HEAD_DIM), axis=0)

    return ranked_rows(n_rank)


def _window_scores(q_rot, kw_ref, g, qb, tq, may_precede_start):
    n_wb = WINDOW // tq + 1
    ki = _iota((tq, tq), 0)
    ci = _iota((tq, tq), 1)
    blocks = []
    for j in range(n_wb):
        kb = qb - (n_wb - 1) + j
        kb_read = jnp.maximum(kb, 0) if may_precede_start else kb
        s = _dot(kw_ref[pl.ds(pl.multiple_of(kb_read * tq, tq), tq), g * LANES:(g + 1) * LANES], q_rot)
        if j == 0:
            s = s + _heads_tiled(jnp.where(ki > ci, 0.0, NEG_INF))
        if j == n_wb - 1:
            s = s + _heads_tiled(jnp.where(ki <= ci, 0.0, NEG_INF))
        elif may_precede_start:
            s = s + jnp.where(kb >= 0, 0.0, NEG_INF)
        blocks.append((s, kb_read))
    return blocks


def _window_finish(blocks, vwt_ref, g, tq):
    m_w = blocks[0][0].max(axis=0, keepdims=True)
    for s, _ in blocks[1:]:
        m_w = jnp.maximum(m_w, s.max(axis=0, keepdims=True))
    o_aug = jnp.zeros((LANES, NSA_REP * tq), F32)
    for s, kb_read in blocks:
        o_aug = o_aug + _dot(vwt_ref[g, kb_read], jnp.exp2(s - m_w).astype(BF16))
    return o_aug[0:HEAD_DIM] * (1.0 / o_aug[HEAD_DIM:HEAD_DIM + 1])


def _nsa_kernel(qraw_ref, qrot_ref, kc_ref, vct_ref, ks_ref, vst_ref, kw_ref, vwt_ref, gate_ref, *rest,
                seq, tq, tk, q_lo, n_qb, n_rank, n_cmp_rows, has_prev):
    o_ref, imp_ref = rest[1:] if has_prev else rest
    qb = pl.program_id(1) + q_lo
    t0 = qb * tq
    groups = range(NSA_KV_HEADS)
    n_cols = NSA_REP * tq

    def grp(g):
        return slice(g * LANES, (g + 1) * LANES)

    def q_cols(g):
        return slice(g * NSA_REP * LANES, (g + 1) * NSA_REP * LANES)

    def q_aug(q_t, extra_rows):
        return jnp.concatenate([q_t, extra_rows], axis=0).astype(BF16)

    no_extra = jnp.zeros((HEAD_DIM, n_cols), F32)
    q_rot_t = [_q_transposed(qrot_ref[:, q_cols(g)]) for g in groups]
    s_cmp = [_dot(kc_ref[0:n_cmp_rows, grp(g)], q_aug(_q_transposed(qraw_ref[:, q_cols(g)]), no_extra))
             for g in groups]
    win_blocks = [_window_scores(q_aug(q_rot_t[g], no_extra), kw_ref, g, qb, tq, q_lo * tq < WINDOW)
                  for g in groups]
    o_cmp, q_sel = [], []
    for g in groups:
        vc_t = jnp.concatenate([vct_ref[g, c] for c in range(n_cmp_rows // LANES)], axis=1)
        o_c, p_sum = _nsa_compressed(s_cmp[g], vc_t, t0, tq, seq)
        o_cmp.append(o_c[0:HEAD_DIM])
        sel_neg = _heads_tiled(_nsa_selection_rows(p_sum, imp_ref.at[g], t0, tq, seq, n_rank))
        q_sel.append(q_aug(q_rot_t[g], sel_neg))
    o_win = [_window_finish(win_blocks[g], vwt_ref, g, tq) for g in groups]

    blocks_per_tile = tk // LANES
    n_full = t0 // tk
    min_full = (q_lo * tq) // tk
    max_full = ((q_lo + n_qb - 1) * tq) // tk

    def k_tile(kt, g):
        if isinstance(kt, int):
            return ks_ref[kt * tk:(kt + 1) * tk, grp(g)]
        return ks_ref[pl.ds(pl.multiple_of(kt * tk, tk), tk), grp(g)]

    def v_tile(kt, g):
        return jnp.concatenate([vst_ref[g, kt * blocks_per_tile + c] for c in range(blocks_per_tile)], axis=1)

    def online_update(scores, values, m_i, acc):
        m_new = m_i
        for s in scores:
            m_new = jnp.maximum(m_new, s.max(axis=0, keepdims=True))
        p = jnp.concatenate([jnp.exp2(s - m_new).astype(BF16) for s in scores], axis=0)
        return m_new, jnp.exp2(m_i - m_new) * acc + _dot(jnp.concatenate(values, axis=1), p)

    pair_w = 2 * tq
    chains = [(g, h) for g in groups for h in range(NSA_REP // 2)]

    def q_pair(g, h):
        return q_sel[g][:, h * pair_w:(h + 1) * pair_w]

    tiles = list(range(max_full))
    trips = [tiles[i:i + SLC_UNROLL] for i in range(0, len(tiles), SLC_UNROLL)]

    def trip_scores(trip):
        scores = []
        for g, h in chains:
            per_tile = []
            for kt in trip:
                s = _dot(k_tile(kt, g), q_pair(g, h))
                per_tile.append(s if kt < min_full else s + jnp.where(kt < n_full, 0.0, NEG_INF))
            scores.append(per_tile)
        return scores

    def trip_update(trip, scores, stats):
        return [online_update(scores[c], [v_tile(kt, g) for kt in trip], *stats[c])
                for c, (g, h) in enumerate(chains)]

    causal = n_full * tk + _iota((tk, tq), 0) <= t0 + _iota((tk, tq), 1)
    causal_bias = _heads_tiled(jnp.where(causal, 0.0, NEG_INF))
    s_diag = [_dot(k_tile(n_full, g), q_sel[g]) for g in groups]
    pending = trip_scores(trips[0]) if trips else None
    first = [online_update([s_diag[g] + causal_bias], [v_tile(n_full, g)],
                           jnp.full((1, n_cols), NEG_INF, F32), jnp.zeros((LANES, n_cols), F32)) for g in groups]
    stats = [tuple(x[:, h * pair_w:(h + 1) * pair_w] for x in first[g]) for g, h in chains]
    for i, trip in enumerate(trips):
        upcoming = trip_scores(trips[i + 1]) if i + 1 < len(trips) else None
        stats = trip_update(trip, pending, stats)
        pending = upcoming

    gate_all = gate_ref[...]
    for g in groups:
        acc = jnp.concatenate([stats[c][1] for c, (cg, _) in enumerate(chains) if cg == g], axis=1)
        o_slc = acc[0:HEAD_DIM] * (1.0 / acc[HEAD_DIM:HEAD_DIM + 1])
        gate_t = jnp.transpose(gate_all[:, grp(g)])
        o_heads = []
        for r in range(NSA_REP):
            cols = slice(r * tq, (r + 1) * tq)
            o_heads.append(gate_t[3 * r:3 * r + 1] * o_cmp[g][:, cols] + gate_t[3 * r + 1:3 * r + 2] * o_slc[:, cols]
                           + gate_t[3 * r + 2:3 * r + 3] * o_win[g][:, cols])
        for pair in range(NSA_REP // 2):
            slab = jnp.transpose(jnp.concatenate(o_heads[2 * pair:2 * pair + 2], axis=0))
            c0 = (g * NSA_REP // 2 + pair) * LANES
            o_ref[:, c0:c0 + LANES] = slab.astype(BF16)


def _nsa_call(qraw, qrot, kc, vct, ks, vst, kw, vwt, gate, *, layer, batch, seq, tq=Q_BLOCK, tk=512):
    assert tq == LANES and tk % tq == 0 and seq % tk == 0 and seq // SLC_LEN <= HEAD_DIM
    nq = seq // tq
    n_cp = seq // CMP_STRIDE
    grp_w = NSA_KV_HEADS * LANES

    def k_spec(rows):
        return pl.BlockSpec((None, rows, grp_w), lambda b, i: (b, 0, 0))

    def vt_spec(rows):
        return pl.BlockSpec((None, NSA_KV_HEADS, rows // LANES, LANES, LANES), lambda b, i: (b, 0, 0, 0, 0))

    ks3, kw3 = ks.reshape(batch, seq, grp_w), kw.reshape(batch, seq, grp_w)
    part = nq // NSA_PARTS
    out = None
    for p in range(NSA_PARTS):
        q_lo, last_token = p * part, (p + 1) * part * tq - 1
        n_rank = min(seq // SLC_LEN, -(-(last_token // SLC_LEN + 1) // 8) * 8)
        n_cmp_rows = min(n_cp, -(-(last_token // CMP_STRIDE + 1) // LANES) * LANES)

        def row_spec(width, q_lo=q_lo):
            return pl.BlockSpec((tq, width), lambda b, i: (b * nq + q_lo + i, 0))

        n_keys = -(-(last_token + 1) // tk) * tk
        in_specs = [row_spec(NSA_HEADS * LANES), row_spec(NSA_HEADS * LANES), k_spec(n_cmp_rows),
                    vt_spec(n_cmp_rows), k_spec(n_keys), vt_spec(n_keys), k_spec(n_keys), vt_spec(n_keys),
                    row_spec(grp_w)]
        args = [qraw, qrot, kc, vct, ks3, vst, kw3, vwt, gate]
        if out is not None:
            in_specs.append(pl.BlockSpec(memory_space=pl.ANY))
            args.append(out)
        out = pl.pallas_call(
            functools.partial(_nsa_kernel, seq=seq, tq=tq, tk=tk, q_lo=q_lo, n_qb=part, n_rank=n_rank,
                              n_cmp_rows=n_cmp_rows,
                              has_prev=out is not None),
            grid=(batch, part),
            in_specs=in_specs,
            out_specs=pl.BlockSpec((tq, NSA_W), lambda b, i, q_lo=q_lo: (b * nq + q_lo + i, 0)),
            out_shape=jax.ShapeDtypeStruct((batch * seq, NSA_W), BF16),
            input_output_aliases={} if out is None else {len(args) - 1: 0},
            scratch_shapes=[pltpu.VMEM((NSA_KV_HEADS, LANES, tq), F32)],
            compiler_params=_params(2),
            name=f"nsa_attn_l{layer}_p{p}",
        )(*args)
    return out


LIN_BLOCK = 2 * CHUNK


def _lin_block_consts():
    r = _iota((LIN_BLOCK, LANES), 0)
    c = _iota((LIN_BLOCK, LANES), 1)
    same_chunk = (r >= CHUNK) == (c >= CHUNK)
    causal = same_chunk & (r >= c)
    head_diag = same_chunk
    first_rows = r < CHUNK
    low_lanes = c < HEAD_DIM
    return causal, head_diag, first_rows, low_lanes


def _head_stats(x, head_diag_mean):
    hi = x.astype(BF16)
    lo = (x - hi.astype(F32)).astype(BF16)
    return _dot(hi, head_diag_mean) + _dot(lo, head_diag_mean)


def _lin_core(chains, consts):
    causal, head_diag, first_rows, low_lanes = consts
    mask2 = jnp.concatenate([causal, causal], axis=0)
    scores, updates = [], []
    for c in chains:
        zero = jnp.zeros_like(c["q_dec"])
        q2 = jnp.concatenate([jnp.where(low_lanes, c["q_dec"], zero), jnp.where(low_lanes, zero, c["q_dec"])], axis=0)
        scores.append(_dot_nt(q2.astype(BF16), c["k_inv"].astype(BF16)))
    for c in chains:
        v_t = jnp.transpose(c["v"]).astype(BF16)
        kd_ab = jnp.concatenate([jnp.where(first_rows, c["k_dec"], 0.0), jnp.where(first_rows, 0.0, c["k_dec"])],
                                axis=1)
        updates.append(_dot(v_t, kd_ab.astype(BF16)))
    intra, states = [], []
    for c, a2 in zip(chains, scores):
        a2 = jnp.where(mask2, a2, 0.0)
        if c["intra_scale"] is not None:
            a2 = a2 * c["intra_scale"]
        o2 = _dot(a2.astype(BF16), c["v"].astype(BF16))
        intra.append(jnp.where(low_lanes, o2[0:LIN_BLOCK], o2[LIN_BLOCK:]))
    out = []
    for c, upd, o_intra in zip(chains, updates, intra):
        st_a = c["st_prev"] * c["dec_a"] + jnp.where(head_diag, upd[:, 0:LANES], 0.0)
        st_b = st_a * c["dec_b"] + jnp.where(head_diag, upd[:, LANES:], 0.0)
        o_ab = _dot_nt(c["q_int"].astype(BF16), jnp.concatenate([c["st_prev"], st_a], axis=0).astype(BF16))
        out.append((o_intra + jnp.where(first_rows, o_ab[:, 0:LANES], o_ab[:, LANES:]), st_b))
    return out


def _gla_kernel(x_ref, lr_ref, a2_ref, ab_ref, ng_ref, o_ref, st_ref, *, n_blocks, batch):
    @pl.when(pl.program_id(0) == 0)
    def _():
        st_ref[...] = jnp.zeros_like(st_ref)

    consts = _lin_block_consts()
    causal, head_diag, first_rows, low_lanes = consts
    ri = _iota((LIN_BLOCK, LANES), 0)
    ci = _iota((LIN_BLOCK, LANES), 1)
    tri_t = jnp.where(((ri >= CHUNK) == (ci >= CHUNK)) & (ri <= ci), 1.0, 0.0).astype(BF16)
    mean_op = jnp.where(head_diag, 1.0 / HEAD_DIM, 0.0).astype(BF16)
    a2h = a2_ref[...].astype(BF16)
    a2l = (a2_ref[...] - a2h.astype(F32)).astype(BF16)

    def block(i):
        rows = pl.ds(i * LIN_BLOCK, LIN_BLOCK)
        lr = jnp.concatenate([lr_ref[bi, rows, :] for bi in range(batch)], axis=0)
        lh = lr.astype(BF16)
        ll = (lr - lh.astype(F32)).astype(BF16)
        z = _dot(lh, a2h) + _dot(lh, a2l) + _dot(ll, a2h) + ab_ref[...]
        log_a = (jnp.minimum(z, 0.0) - jnp.log(1.0 + jnp.exp(-jnp.abs(z)))) * (1.0 / GLA_TAU)
        n_slab = GLA_W // LANES
        la_t = jnp.concatenate([jnp.transpose(log_a[bi * LIN_BLOCK:(bi + 1) * LIN_BLOCK, s * LANES:(s + 1) * LANES])
                                for bi in range(batch) for s in range(n_slab)], axis=0)
        cum_t = _dot3_exact_rhs(la_t, tri_t)
        chains = []
        for bi in range(batch):
            for s in range(n_slab):
                cols = slice(s * LANES, (s + 1) * LANES)
                n = bi * n_slab + s

                def ld(k):
                    return x_ref[bi, rows, k * GLA_W + s * LANES:k * GLA_W + (s + 1) * LANES]

                q, k, v, gg = ld(0), ld(1), ld(2), ld(3)
                b = jnp.transpose(cum_t[n * LANES:(n + 1) * LANES])
                bl_a = b[CHUNK - 1:CHUNK, :]
                bl_b = b[LIN_BLOCK - 1:LIN_BLOCK, :]
                bl = jnp.where(first_rows, bl_a, bl_b)
                q_dec = q * QK_SCALE * jnp.exp(b)
                chains.append(dict(bi=bi, s=s, cols=cols, gate=gg, q_dec=q_dec, q_int=q_dec, k_inv=k * jnp.exp(-b),
                                   k_dec=k * jnp.exp(bl - b), v=v, intra_scale=None, dec_a=jnp.exp(bl_a),
                                   dec_b=jnp.exp(bl_b), st_prev=st_ref[bi, s]))
        results = _lin_core(chains, consts)
        for c, (_, st_new) in zip(chains, results):
            st_ref[c["bi"], c["s"]] = st_new
        ms = _head_stats(jnp.concatenate([o * o for o, _ in results], axis=0), mean_op)
        for n, (c, (o, _)) in enumerate(zip(chains, results)):
            y = (o * lax.rsqrt(ms[n * LIN_BLOCK:(n + 1) * LIN_BLOCK] + NORM_EPS) * ng_ref[:, c["cols"]]
                 * _silu(c["gate"]))
            o_ref[c["bi"], rows, c["cols"]] = y.astype(BF16)

    for i in range(n_blocks):
        block(i)


def _ret_kernel(x_ref, dmat_ref, qf_ref, kf_ref, cd_ref, ng_ref, o_ref, st_ref, *, n_blocks, batch):
    @pl.when(pl.program_id(0) == 0)
    def _():
        st_ref[...] = jnp.zeros_like(st_ref)

    consts = _lin_block_consts()
    causal, head_diag, first_rows, low_lanes = consts
    mean_op = jnp.where(head_diag, 1.0 / HEAD_DIM, 0.0).astype(BF16)

    def block(i):
        rows = pl.ds(i * LIN_BLOCK, LIN_BLOCK)
        chains = []
        for bi in range(batch):
            for s in range(RET_W // LANES):
                cols = slice(s * LANES, (s + 1) * LANES)

                def ld(k):
                    return x_ref[bi, rows, k * RET_W + s * LANES:k * RET_W + (s + 1) * LANES]

                q, k, v, gg = ld(0), ld(1), ld(2), ld(3)
                cd = cd_ref[:, cols]
                chains.append(dict(bi=bi, s=s, cols=cols, gate=gg, q_dec=q, q_int=q * qf_ref[:, cols], k_inv=k,
                                   k_dec=k * kf_ref[:, cols], v=v, intra_scale=dmat_ref[s], dec_a=cd, dec_b=cd,
                                   st_prev=st_ref[bi, s]))
        results = _lin_core(chains, consts)
        for c, (_, st_new) in zip(chains, results):
            st_ref[c["bi"], c["s"]] = st_new
        o_all = jnp.concatenate([o for o, _ in results], axis=0)
        oc_all = o_all - _head_stats(o_all, mean_op)
        var_all = _head_stats(oc_all * oc_all, mean_op)
        y_all = oc_all * lax.rsqrt(var_all + NORM_EPS)
        for n, c in enumerate(chains):
            y = y_all[n * LIN_BLOCK:(n + 1) * LIN_BLOCK] * ng_ref[:, c["cols"]] * _silu(c["gate"])
            o_ref[c["bi"], rows, c["cols"]] = y.astype(BF16)

    for i in range(n_blocks):
        block(i)


def _gla_call(gla, lr, a2_pad, a_bias, norm_g, *, layer, batch, seq, tb=LIN_BLOCK):
    n_slab = GLA_W // LANES

    def lspec(shape):
        return pl.BlockSpec((None,) + shape, lambda t: (layer, 0, 0))

    return pl.pallas_call(
        functools.partial(_gla_kernel, n_blocks=tb // LIN_BLOCK, batch=batch),
        grid=(seq // tb,),
        in_specs=[pl.BlockSpec((batch, tb, 4 * GLA_W), lambda t: (0, t, 0)),
                  pl.BlockSpec((batch, tb, LANES), lambda t: (0, t, 0)),
                  lspec((LANES, GLA_W)), lspec((1, GLA_W)), lspec((1, GLA_W))],
        out_specs=pl.BlockSpec((batch, tb, GLA_W), lambda t: (0, t, 0)),
        out_shape=jax.ShapeDtypeStruct((batch, seq, GLA_W), BF16),
        scratch_shapes=[pltpu.VMEM((batch, n_slab, LANES, LANES), F32)],
        compiler_params=_params(1),
        name=f"gla_l{layer}",
    )(gla.reshape(batch, seq, 4 * GLA_W), lr.reshape(batch, seq, LANES), a2_pad, a_bias, norm_g
      ).reshape(batch * seq, GLA_W)


def _ret_call(ret, dmat, qf, kf, cd, norm_g, *, layer, batch, seq, tb=LIN_BLOCK):
    n_slab = RET_W // LANES
    full = lambda a: pl.BlockSpec(a.shape, lambda t: (0,) * a.ndim)
    return pl.pallas_call(
        functools.partial(_ret_kernel, n_blocks=tb // LIN_BLOCK, batch=batch),
        grid=(seq // tb,),
        in_specs=[pl.BlockSpec((batch, tb, 4 * RET_W), lambda t: (0, t, 0)),
                  full(dmat), full(qf), full(kf), full(cd),
                  pl.BlockSpec((None, 1, RET_W), lambda t: (layer, 0, 0))],
        out_specs=pl.BlockSpec((batch, tb, RET_W), lambda t: (0, t, 0)),
        out_shape=jax.ShapeDtypeStruct((batch, seq, RET_W), BF16),
        scratch_shapes=[pltpu.VMEM((batch, n_slab, LANES, LANES), F32)],
        compiler_params=_params(1),
        name=f"ret_l{layer}",
    )(ret.reshape(batch, seq, 4 * RET_W), dmat, qf, kf, cd, norm_g).reshape(batch * seq, RET_W)


def _layout_proj_weights(w_in, nsa_gate_bias):
    n_layers, d, _ = w_in.shape
    w_in = w_in.astype(BF16)
    (g_q, g_k, g_v, g_g, g_lr, n_q, n_kc, n_vc, n_ks, n_vs, n_kw, n_vw, n_gate,
     r_q, r_k, r_v, r_g) = jnp.split(w_in, IN_SPLITS, axis=-1)
    zeros = lambda w: jnp.zeros((n_layers, d, w), w_in.dtype)
    def pad_heads(w, n_heads):
        w = w.reshape(n_layers, d, n_heads, HEAD_DIM)
        return jnp.concatenate([w, jnp.zeros_like(w)], axis=-1).reshape(n_layers, d, n_heads * LANES)

    per_grp = 3 * NSA_REP
    gate_cols = []
    bias_cols = []
    for g in range(NSA_KV_HEADS):
        gate_cols += [n_gate[..., g * per_grp:(g + 1) * per_grp], zeros(LANES - per_grp)]
        bias_cols += [nsa_gate_bias[:, g * per_grp:(g + 1) * per_grp],
                      jnp.zeros((n_layers, LANES - per_grp), nsa_gate_bias.dtype)]
    w = jnp.concatenate([g_q, g_k, g_v, g_g, g_lr, zeros(LANES - GLA_LOWRANK), pad_heads(n_q, NSA_HEADS),
                         n_kc, n_vc, pad_heads(n_ks, NSA_KV_HEADS), pad_heads(n_kw, NSA_KV_HEADS), n_vs, n_vw]
                        + gate_cols + [r_q, r_k, r_v, r_g], axis=-1)
    assert w.shape[-1] == PROJ_W
    bias = jnp.concatenate(bias_cols, axis=-1).reshape(n_layers, 1, NSA_KV_HEADS * LANES)
    return w.astype(BF16), bias


def _layout_cmp_weights(pe_k, pe_v, w1_k, w2_k, w1_v, w2_v):
    n_layers = pe_k.shape[0]
    eye = jnp.eye(2, dtype=BF16)
    w1 = jnp.stack([w1_k, w1_v], axis=1).reshape(n_layers, 2, CMP_LEN, HEAD_DIM, CMP_HIDDEN).astype(BF16)
    blocks = [jnp.pad(w1[:, kv], ((0, 0), (0, 0), (0, 0), ((2 * kv + g) * CMP_HIDDEN, (3 - 2 * kv - g) * CMP_HIDDEN)))
              for kv in range(2) for g in range(NSA_KV_HEADS)]
    w1_full = jnp.stack(blocks, axis=2).reshape(n_layers, CMP_LEN * 2 * NSA_KV_W, 4 * CMP_HIDDEN)
    half = CMP_STRIDE * 2 * NSA_KV_W
    w2 = jnp.stack([w2_k, w2_v], axis=1).astype(BF16)
    w2_full = jnp.einsum('zknd,kq,gh->zqhnkgd', w2, eye, eye)
    w2_full = w2_full.reshape(n_layers, 4 * CMP_HIDDEN, 2 * NSA_KV_W)
    w2_k = w2_full[..., :NSA_KV_W].reshape(n_layers, 4 * CMP_HIDDEN, NSA_KV_HEADS, HEAD_DIM)
    w2_k = jnp.concatenate([w2_k, jnp.zeros_like(w2_k)], axis=-1).reshape(n_layers, 4 * CMP_HIDDEN, -1)
    w2_full = jnp.concatenate([w2_k, w2_full[..., NSA_KV_W:]], axis=-1).astype(BF16)
    pe = jnp.stack([pe_k, pe_v], axis=1)
    pe = jnp.broadcast_to(pe.transpose(0, 2, 1, 3)[:, :, :, None, :],
                          (n_layers, CMP_LEN, 2, NSA_KV_HEADS, HEAD_DIM))
    pe = pe.reshape(n_layers, 1, CMP_LEN * 2 * NSA_KV_W)
    return pe[:, :, :half], pe[:, :, half:], w1_full.reshape(n_layers, 2, half, 4 * CMP_HIDDEN), w2_full


def _rotary_tables(seq):
    half = HEAD_DIM // 2
    inv_freq = ROPE_THETA ** (-jnp.arange(half, dtype=F32) / half)
    ang = jnp.arange(seq).astype(F32)[:, None] * inv_freq[None, :]
    cos, sin = jnp.cos(ang), jnp.sin(ang)
    reps = LANES // HEAD_DIM
    return jnp.tile(jnp.concatenate([cos, cos], axis=-1), (1, reps)), \
        jnp.tile(jnp.concatenate([-sin, sin], axis=-1), (1, reps))


def _retention_tables():
    log_gamma = jnp.log1p(-jnp.exp2(-5.0 - jnp.arange(RET_HEADS, dtype=F32)))
    lg_lane = jnp.repeat(log_gamma, HEAD_DIM)[None, :]
    pos = (jnp.arange(LIN_BLOCK) % CHUNK).astype(F32)[:, None]
    qf = jnp.exp(lg_lane * (pos + 1.0))
    kf = jnp.exp(lg_lane * (CHUNK - 1.0 - pos))
    cd = jnp.exp(lg_lane * CHUNK)
    r = jnp.arange(LIN_BLOCK)
    rel = (r[:, None] - r[None, :]).astype(F32)
    ok = ((r[:, None] // CHUNK) == (r[None, :] // CHUNK)) & (rel >= 0)
    dm = jnp.where(ok[None], jnp.exp(log_gamma[:, None, None] * rel[None]), 0.0)
    dmat = dm.reshape(RET_W // LANES, 2 * LIN_BLOCK, LIN_BLOCK)
    return dmat, qf, kf, cd


def kernel(x, c, w_ada, b_ada, norm_g, ffn1_in, ffn1_out, w_in, gla_a2, gla_a_bias, gla_norm_g,
           nsa_pe_k, nsa_pe_v, nsa_w1_k, nsa_w2_k, nsa_w1_v, nsa_w2_v, nsa_gate_bias, ret_norm_g,
           w_out, ffn2_in, ffn2_out, final_norm_g):
    batch, seq, d = x.shape
    n_layers = w_ada.shape[0]
    assert batch <= 8 and seq % 512 == 0 and seq >= WINDOW + Q_BLOCK

    c_pad = jnp.zeros((8, d), F32).at[:batch].set(c)
    mod = _ada_call(c_pad, w_ada, b_ada).reshape(n_layers * 8 * N_ADA, 1, d)
    ng = norm_g.reshape(n_layers * 3, 1, d)

    w_proj, gate_bias = _layout_proj_weights(w_in, nsa_gate_bias)
    pe_top, pe_bot, w1_cmp, w2_cmp = _layout_cmp_weights(nsa_pe_k, nsa_pe_v, nsa_w1_k, nsa_w2_k, nsa_w1_v, nsa_w2_v)
    cosf, sinf = _rotary_tables(seq)
    dmat, qf, kf, cd = _retention_tables()
    a2_pad = jnp.zeros((n_layers, LANES, GLA_W), F32).at[:, :GLA_LOWRANK].set(gla_a2)
    a_bias = gla_a_bias.reshape(n_layers, 1, GLA_W)
    gla_g = jnp.tile(gla_norm_g, (1, GLA_HEADS)).reshape(n_layers, 1, GLA_W)
    ret_g = jnp.tile(ret_norm_g, (1, RET_HEADS)).reshape(n_layers, 1, RET_W)
    f1_in, f1_out = ffn1_in.astype(BF16), ffn1_out.astype(BF16)
    f2_in, f2_out = ffn2_in.astype(BF16), ffn2_out.astype(BF16)
    w_o = w_out.astype(BF16)

    xs = x.reshape(batch * seq, d)
    for l in range(n_layers):
        xs = _ffn_call(xs, ng, mod, f1_in, f1_out, layer=l, sub=0, seq=seq)
        (gla, lr, qraw, qrot, kcvc, ks, vs, kw, vw, gate, ret) = _proj_call(
            xs, ng, mod, w_proj, gate_bias, cosf, sinf, layer=l, seq=seq)
        kc, vc = _cmp_call(kcvc, pe_top, pe_bot, w1_cmp, w2_cmp, layer=l, batch=batch, seq=seq)
        o_gla = _gla_call(gla, lr, a2_pad, a_bias, gla_g, layer=l, batch=batch, seq=seq)
        o_ret = _ret_call(ret, dmat, qf, kf, cd, ret_g, layer=l, batch=batch, seq=seq)
        o_nsa = _nsa_call(qraw, qrot, kc, vc, ks, vs, kw, vw, gate, layer=l, batch=batch, seq=seq)
        final_g = final_norm_g.reshape(1, d) if l == n_layers - 1 else None
        xs = _ffn_call(xs, ng, mod, f2_in, f2_out, layer=l, sub=2, seq=seq, mixer=(o_gla, o_nsa, o_ret, w_o),
                       final_g=final_g)
    return xs.reshape(batch, seq, d)
```

```python
import functools
import math

import numpy as np
import jax
import jax.numpy as jnp
from jax import lax
from jax.experimental import pallas as pl
from jax.experimental.pallas import tpu as pltpu

F32 = jnp.float32
BF16 = jnp.bfloat16

HEAD_DIM = 64
LANES = 128
GLA_HEADS = 4
NSA_HEADS = 8
NSA_KV_HEADS = 2
NSA_REP = NSA_HEADS // NSA_KV_HEADS
RET_HEADS = 4
GLA_LOWRANK = 16
GLA_TAU = 16.0
CHUNK = 64
CMP_LEN = 32
CMP_STRIDE = 16
CMP_HIDDEN = 128
SLC_LEN = 64
SLC_TOPK = 16
WINDOW = 512
Q_BLOCK = 128
ROPE_THETA = 10000.0
FFN_HALF = 0.5
NORM_EPS = 1e-6
NEG_INF = -1e30
FORCED_SCORE = 1e4
N_ADA = 9
QK_SCALE = HEAD_DIM ** -0.5
SLC_SHIFT = int(math.log2(SLC_LEN))
LOG2_E = math.log2(math.e)

GLA_W = GLA_HEADS * HEAD_DIM
NSA_W = NSA_HEADS * HEAD_DIM
NSA_KV_W = NSA_KV_HEADS * HEAD_DIM
RET_W = RET_HEADS * HEAD_DIM
IN_SIZES = (GLA_W, GLA_W, GLA_W, GLA_W, GLA_LOWRANK,
            NSA_W, NSA_KV_W, NSA_KV_W, NSA_KV_W, NSA_KV_W, NSA_KV_W, NSA_KV_W, 3 * NSA_HEADS,
            RET_W, RET_W, RET_W, RET_W)
IN_SPLITS = tuple(int(s) for s in np.cumsum(IN_SIZES)[:-1])

SEG_GLA = 0
SEG_LR = SEG_GLA + 4 * GLA_W
SEG_NQ = SEG_LR + LANES
SEG_KCVC = SEG_NQ + NSA_HEADS * LANES
SEG_KS = SEG_KCVC + 2 * NSA_KV_W
SEG_KW = SEG_KS + NSA_KV_HEADS * LANES
SEG_VS = SEG_KW + NSA_KV_HEADS * LANES
SEG_VW = SEG_VS + NSA_KV_W
SEG_GATE = SEG_VW + NSA_KV_W
SEG_RET = SEG_GATE + NSA_KV_HEADS * LANES
PROJ_W = SEG_RET + 4 * RET_W

VMEM_LIMIT_BYTES = 56 * 1024 * 1024

NT_DIMS = (((1,), (1,)), ((), ()))


def _dot(a, b):
    return jnp.dot(a, b, preferred_element_type=F32)


def _dot_nt(a, b):
    return lax.dot_general(a, b, NT_DIMS, preferred_element_type=F32)


def _split3(x):
    hi = x.astype(BF16)
    r1 = x - hi.astype(F32)
    mid = r1.astype(BF16)
    lo = (r1 - mid.astype(F32)).astype(BF16)
    return hi, mid, lo


def _dot3_exact_rhs(x, w_bf16):
    hi, mid, lo = _split3(x)
    return _dot(hi, w_bf16) + _dot(mid, w_bf16) + _dot(lo, w_bf16)


def _dot3_exact_lhs(w_bf16, x):
    hi, mid, lo = _split3(x)
    return _dot(w_bf16, hi) + _dot(w_bf16, mid) + _dot(w_bf16, lo)


def _sigmoid(x):
    return 1.0 / (1.0 + jnp.exp(-x))


def _silu(x):
    return x * _sigmoid(x)


def _iota(shape, dim):
    return lax.broadcasted_iota(jnp.int32, shape, dim)


def _params(n_grid):
    return pltpu.CompilerParams(dimension_semantics=("arbitrary",) * n_grid,
                                vmem_limit_bytes=VMEM_LIMIT_BYTES)


def _norm_mod(x, g, sh, sc):
    ms = jnp.mean(x * x, axis=-1, keepdims=True)
    return x * lax.rsqrt(ms + NORM_EPS) * (g * (1.0 + sc)) + sh


def _ada_kernel(c_ref, w_ref, b_ref, o_ref):
    c = c_ref[...]
    o_ref[...] = _dot(_silu(c).astype(BF16), w_ref[...].astype(BF16)) + b_ref[...]


def _ada_call(c_pad, w_ada, b_ada):
    n_layers, d, n_out = w_ada.shape
    tn = 2304 if n_out % 2304 == 0 else n_out
    return pl.pallas_call(
        _ada_kernel,
        grid=(n_layers, n_out // tn),
        in_specs=[pl.BlockSpec((8, d), lambda l, j: (0, 0)),
                  pl.BlockSpec((None, d, tn), lambda l, j: (l, 0, j)),
                  pl.BlockSpec((None, 1, tn), lambda l, j: (l, 0, j))],
        out_specs=pl.BlockSpec((None, 8, tn), lambda l, j: (l, 0, j)),
        out_shape=jax.ShapeDtypeStruct((n_layers, 8, n_out), F32),
        compiler_params=_params(2),
        name="ada_mod",
    )(c_pad, w_ada, b_ada.reshape(n_layers, 1, n_out))


def _ffn_kernel(x_ref, g_ref, sh_ref, sc_ref, gt_ref, win_ref, wout_ref, *rest, d_ff, chunks, mixer, final):
    rest = list(rest)
    x = x_ref[...]
    if mixer:
        og_ref, on_ref, or_ref, gm_ref, wo_ref = rest[:5]
        rest = rest[5:]
        x = x + gm_ref[...] * (_dot(og_ref[...], wo_ref[0:GLA_W, :])
                               + _dot(on_ref[...], wo_ref[GLA_W:GLA_W + NSA_W, :])
                               + _dot(or_ref[...], wo_ref[GLA_W + NSA_W:, :]))
    if final:
        fg_ref, o_ref, act_ref = rest
    else:
        o_ref, act_ref = rest
    h = _norm_mod(x, g_ref[...], sh_ref[...], sc_ref[...]).astype(BF16)
    for (c0, cw) in chunks:
        gate = _dot(h, win_ref[:, c0:c0 + cw])
        up = _dot(h, win_ref[:, d_ff + c0:d_ff + c0 + cw])
        act_ref[:, c0:c0 + cw] = (_silu(gate) * up).astype(BF16)
    y = _dot(act_ref[...], wout_ref[...])
    xn = x + (FFN_HALF * gt_ref[...]) * y
    if final:
        ms = jnp.mean(xn * xn, axis=-1, keepdims=True)
        xn = xn * lax.rsqrt(ms + NORM_EPS) * fg_ref[...]
    o_ref[...] = xn


def _ffn_call(x, ng, mod, w_in, w_out, *, layer, sub, seq, mixer=None, final_g=None, tm=1024):
    m, d = x.shape
    d_ff = w_out.shape[1]
    tiles_per_seq = seq // tm
    chunks, c0 = [], 0
    while c0 < d_ff:
        cw = min(512, d_ff - c0)
        chunks.append((c0, cw))
        c0 += cw

    def mod_spec(k):
        return pl.BlockSpec((None, 1, d), lambda i: ((layer * 8 + i // tiles_per_seq) * N_ADA + k, 0, 0))

    in_specs = [pl.BlockSpec((tm, d), lambda i: (i, 0)),
                pl.BlockSpec((None, 1, d), lambda i: (layer * 3 + sub, 0, 0)),
                mod_spec(3 * sub), mod_spec(3 * sub + 1), mod_spec(3 * sub + 2),
                pl.BlockSpec((None, d, 2 * d_ff), lambda i: (layer, 0, 0), pipeline_mode=pl.Buffered(1)),
                pl.BlockSpec((None, d_ff, d), lambda i: (layer, 0, 0), pipeline_mode=pl.Buffered(1))]
    args = [x, ng, mod, mod, mod, w_in, w_out]
    if mixer is not None:
        o_gla, o_nsa, o_ret, w_mix = mixer
        in_specs += [pl.BlockSpec((tm, o.shape[1]), lambda i: (i, 0)) for o in (o_gla, o_nsa, o_ret)]
        in_specs += [mod_spec(5), pl.BlockSpec((None,) + w_mix.shape[1:], lambda i: (layer, 0, 0),
                                               pipeline_mode=pl.Buffered(1))]
        args += [o_gla, o_nsa, o_ret, mod, w_mix]
    if final_g is not None:
        in_specs.append(pl.BlockSpec((1, d), lambda i: (0, 0)))
        args.append(final_g)
    return pl.pallas_call(
        functools.partial(_ffn_kernel, d_ff=d_ff, chunks=tuple(chunks), mixer=mixer is not None,
                          final=final_g is not None),
        grid=(m // tm,),
        in_specs=in_specs,
        out_specs=pl.BlockSpec((tm, d), lambda i: (i, 0)),
        out_shape=jax.ShapeDtypeStruct((m, d), F32),
        scratch_shapes=[pltpu.VMEM((tm, d_ff), BF16)],
        compiler_params=_params(1),
        name=f"ffn_l{layer}_s{sub}",
    )(*args)


def _rotate(x, cosf, sinf):
    rows, width = x.shape
    first = (_iota((rows, LANES), 1) & (HEAD_DIM - 1)) < HEAD_DIM // 2
    outs = []
    for j in range(width // LANES):
        xs = x[:, j * LANES:(j + 1) * LANES]
        swapped = jnp.where(first, pltpu.roll(xs, LANES - HEAD_DIM // 2, 1), pltpu.roll(xs, HEAD_DIM // 2, 1))
        outs.append(xs * cosf + swapped * sinf)
    return outs[0] if len(outs) == 1 else jnp.concatenate(outs, axis=1)


def _store_group_transposed(dst_ref, x, fill):
    top = _iota((LANES, LANES), 0) < HEAD_DIM
    for c in range(x.shape[0] // LANES):
        piece = jnp.transpose(x[c * LANES:(c + 1) * LANES])
        dst_ref[0, c] = jnp.where(top, piece, fill).astype(dst_ref.dtype)
        dst_ref[1, c] = jnp.where(top, pltpu.roll(piece, HEAD_DIM, 0), fill).astype(dst_ref.dtype)


def _proj_kernel(x_ref, g_ref, sh_ref, sc_ref, w_ref, gb_ref, cos_ref, sin_ref,
                 gla_ref, lr_ref, qraw_ref, qrot_ref, kcvc_ref, ks_ref, vst_ref, kw_ref, vwt_ref,
                 gate_ref, ret_ref, *, tiles_per_seq):
    h = _norm_mod(x_ref[...], g_ref[...], sh_ref[...], sc_ref[...]).astype(BF16)
    cosf = cos_ref[...]
    sinf = sin_ref[...]
    tm = h.shape[0]

    def seg(c0, width):
        return _dot(h, w_ref[:, c0:c0 + width])

    nq = seg(SEG_NQ, NSA_HEADS * LANES)
    qraw_ref[...] = (nq * (QK_SCALE * LOG2_E)).astype(BF16)
    qrot_ref[...] = (_rotate(nq, cosf, sinf) * (QK_SCALE * LOG2_E)).astype(BF16)
    ret_ref[:, 0:RET_W] = _rotate(seg(SEG_RET, RET_W), cosf, sinf)
    ret_ref[:, RET_W:2 * RET_W] = _rotate(seg(SEG_RET + RET_W, RET_W), cosf, sinf) * QK_SCALE
    grp_w = NSA_KV_HEADS * LANES
    pos = (pl.program_id(0) % tiles_per_seq) * tm + _iota((tm, grp_w), 0)
    spare = (_iota((tm, grp_w), 1) & (LANES - 1)) - HEAD_DIM
    blk_onehot = jnp.where(spare == (pos >> SLC_SHIFT), 1.0, 0.0)
    ks_ref[...] = (_rotate(seg(SEG_KS, grp_w), cosf, sinf) + blk_onehot).astype(BF16)
    kw_ref[...] = _rotate(seg(SEG_KW, grp_w), cosf, sinf).astype(BF16)
    _store_group_transposed(vst_ref, seg(SEG_VS, NSA_KV_W), 1.0)
    _store_group_transposed(vwt_ref, seg(SEG_VW, NSA_KV_W), 1.0)
    gate_ref[...] = _sigmoid(seg(SEG_GATE, NSA_KV_HEADS * LANES) + gb_ref[...])
    kcvc_ref[...] = seg(SEG_KCVC, 2 * NSA_KV_W)
    lr_ref[...] = seg(SEG_LR, LANES)
    ret_ref[:, 2 * RET_W:4 * RET_W] = seg(SEG_RET + 2 * RET_W, 2 * RET_W)
    gla_ref[...] = seg(SEG_GLA, 4 * GLA_W)


def _proj_call(x, ng, mod, w_proj, gate_bias, cosf, sinf, *, layer, seq, tm=512):
    m, d = x.shape
    tiles_per_seq = seq // tm

    def mod_spec(k):
        return pl.BlockSpec((None, 1, d), lambda i: ((layer * 8 + i // tiles_per_seq) * N_ADA + k, 0, 0))

    def row_spec(width):
        return pl.BlockSpec((tm, width), lambda i: (i, 0))

    batch = m // seq
    grp_w = NSA_KV_HEADS * LANES

    def vt_spec():
        return pl.BlockSpec((None, NSA_KV_HEADS, tm // LANES, LANES, LANES),
                            lambda i: (i // tiles_per_seq, 0, i % tiles_per_seq, 0, 0))

    vt_shape = jax.ShapeDtypeStruct((batch, NSA_KV_HEADS, seq // LANES, LANES, LANES), BF16)
    out_widths = [(4 * GLA_W, F32), (LANES, F32), (NSA_HEADS * LANES, BF16), (NSA_HEADS * LANES, BF16),
                  (2 * NSA_KV_W, F32), (grp_w, BF16), None, (grp_w, BF16), None,
                  (NSA_KV_HEADS * LANES, F32), (4 * RET_W, F32)]
    return pl.pallas_call(
        functools.partial(_proj_kernel, tiles_per_seq=tiles_per_seq),
        grid=(m // tm,),
        in_specs=[row_spec(d),
                  pl.BlockSpec((None, 1, d), lambda i: (layer * 3 + 1, 0, 0)),
                  mod_spec(3), mod_spec(4),
                  pl.BlockSpec((None, d, PROJ_W), lambda i: (layer, 0, 0), pipeline_mode=pl.Buffered(1)),
                  pl.BlockSpec((None, 1, NSA_KV_HEADS * LANES), lambda i: (layer, 0, 0)),
                  pl.BlockSpec((tm, LANES), lambda i: (i % tiles_per_seq, 0)),
                  pl.BlockSpec((tm, LANES), lambda i: (i % tiles_per_seq, 0))],
        out_specs=[vt_spec() if ow is None else row_spec(ow[0]) for ow in out_widths],
        out_shape=[vt_shape if ow is None else jax.ShapeDtypeStruct((m, ow[0]), ow[1]) for ow in out_widths],
        compiler_params=_params(1),
        name=f"proj_l{layer}",
    )(x, ng, mod, mod, w_proj, gate_bias, cosf, sinf)


def _cmp_kernel(xk_ref, xv_ref, pet_ref, peb_ref, wt_ref, wb_ref, w2_ref, kc_ref, vct_ref):
    n_rows = xk_ref.shape[0] // CMP_STRIDE
    x = jnp.concatenate([ref[pl.ds(l, n_rows, stride=CMP_STRIDE), :]
                         for l in range(CMP_STRIDE) for ref in (xk_ref, xv_ref)], axis=1)
    a = _dot((x + pet_ref[...]).astype(BF16), wt_ref[...])
    b = _dot((x + peb_ref[...]).astype(BF16), wb_ref[...])
    hid = a + pltpu.roll(b, n_rows - 1, 0)
    out = _dot(_silu(hid).astype(BF16), w2_ref[...])
    out = jnp.where(_iota(out.shape, 0) < n_rows - 1, out, 0.0)
    grp_w = NSA_KV_HEADS * LANES
    kc_ref[...] = out[:, 0:grp_w].astype(BF16)
    _store_group_transposed(vct_ref, out[:, grp_w:grp_w + NSA_KV_W], 0.0)


def _cmp_call(kcvc, pe_top, pe_bot, w1_halves, w2, *, layer, batch, seq):
    rows = seq // CMP_STRIDE
    width = CMP_STRIDE * 2 * NSA_KV_W
    x = kcvc.reshape(batch, seq, 2 * NSA_KV_W)
    hid_w = 4 * CMP_HIDDEN
    grp_w = NSA_KV_HEADS * LANES

    def wspec(shape):
        return pl.BlockSpec((None,) + shape, lambda b: (layer, 0, 0))

    return pl.pallas_call(
        _cmp_kernel,
        grid=(batch,),
        in_specs=[pl.BlockSpec((None, seq, NSA_KV_W), lambda b: (b, 0, 0)),
                  pl.BlockSpec((None, seq, NSA_KV_W), lambda b: (b, 0, 1)),
                  wspec((1, width)), wspec((1, width)),
                  pl.BlockSpec((None, None, width, hid_w), lambda b: (layer, 0, 0, 0)),
                  pl.BlockSpec((None, None, width, hid_w), lambda b: (layer, 1, 0, 0)),
                  wspec((hid_w, grp_w + NSA_KV_W))],
        out_specs=[pl.BlockSpec((None, rows, grp_w), lambda b: (b, 0, 0)),
                   pl.BlockSpec((None, NSA_KV_HEADS, rows // LANES, LANES, LANES), lambda b: (b, 0, 0, 0, 0))],
        out_shape=[jax.ShapeDtypeStruct((batch, rows, grp_w), BF16),
                   jax.ShapeDtypeStruct((batch, NSA_KV_HEADS, rows // LANES, LANES, LANES), BF16)],
        compiler_params=_params(1),
        name=f"nsa_compress_l{layer}",
    )(x, x, pe_top, pe_bot, w1_halves, w1_halves, w2)


def _heads_tiled(x):
    return jnp.concatenate([x] * NSA_REP, axis=1)


SLC_UNROLL = 2
NSA_PARTS = 8

def _q_transposed(x):
    x = x.astype(F32)
    return jnp.concatenate([jnp.transpose(x[:, r * LANES:(r + 1) * LANES])[0:HEAD_DIM, :] for r in range(NSA_REP)],
                           axis=1)


def _nsa_compressed(s_c, vc_t, t0, tq, seq):
    n_cp = s_c.shape[0]
    n_cmp = (seq - CMP_LEN) // CMP_STRIDE + 1
    nn = _iota((n_cp, tq), 0)
    tt = t0 + _iota((n_cp, tq), 1)
    bias_c = jnp.where((nn * CMP_STRIDE + (CMP_LEN - 1) <= tt) & (nn < n_cmp), 0.0, NEG_INF)
    has_cmp = t0 + _iota((1, tq), 1) >= CMP_LEN - 1
    p_heads = []
    for r in range(NSA_REP):
        s = s_c[:, r * tq:(r + 1) * tq] + bias_c
        e = jnp.exp2(s - jnp.max(s, axis=0, keepdims=True))
        p_heads.append(e * jnp.where(has_cmp, 1.0 / jnp.sum(e, axis=0, keepdims=True), 0.0))
    o_cmp = _dot(vc_t, jnp.concatenate(p_heads, axis=1).astype(BF16))
    p_sum = p_heads[0]
    for r in range(1, NSA_REP):
        p_sum = p_sum + p_heads[r]
    return o_cmp, p_sum


def _nsa_selection_rows(p_sum, imp_ref, t0, tq, seq, n_rank):
    n_cp = p_sum.shape[0]
    n_cmp = (seq - CMP_LEN) // CMP_STRIDE + 1
    n_slc = seq // SLC_LEN
    n_sel = min(SLC_TOPK, n_slc)
    sb = _iota((LANES, n_cp), 0)
    cb = _iota((LANES, n_cp), 1)
    overlap_t = ((cb * CMP_STRIDE < sb * SLC_LEN + SLC_LEN) & (cb * CMP_STRIDE + CMP_LEN > sb * SLC_LEN)
                 & (sb < n_slc) & (cb < n_cmp))
    imp_t = _dot3_exact_lhs(jnp.where(overlap_t, 1.0, 0.0).astype(BF16), p_sum)
    blk = _iota((LANES, tq), 0)
    cur = (t0 + _iota((LANES, tq), 1)) >> SLC_SHIFT
    forced = (blk == 0) | (blk == cur) | (blk == cur - 1)
    imp_ref[...] = jnp.where(forced, FORCED_SCORE, jnp.where(blk <= cur, imp_t, -1.0))

    def ranked_rows(n_rows):
        if n_rows <= n_sel:
            return jnp.zeros((HEAD_DIM, tq), F32)
        n_grp = n_rows // 8
        vals = [imp_ref[8 * j:8 * (j + 1), :] for j in range(n_grp)]
        ranks = [jnp.zeros((8, tq), F32) for _ in range(n_grp)]
        sub = _iota((8, tq), 0)
        for i in range(n_rows):
            row = jnp.broadcast_to(imp_ref[i:i + 1, :], (8, tq))
            for j in range(n_grp):
                if 8 * j > i:
                    beats = jnp.where(row >= vals[j], 1.0, 0.0)
                elif 8 * j + 7 <= i:
                    beats = jnp.where(row > vals[j], 1.0, 0.0)
                else:
                    beats = jnp.where(sub + 8 * j > i, jnp.where(row >= vals[j], 1.0, 0.0),
                                      jnp.where(row > vals[j], 1.0, 0.0))
                ranks[j] = ranks[j] + beats
        return jnp.concatenate([jnp.where(rk < n_sel, 0.0, NEG_INF) for rk in ranks]
                               + [jnp.zeros((HEAD_DIM - n_rows, tq), F32)] * (n_rows < HEAD_DIM), axis=0)

    return ranked_rows(n_rank)


def _window_scores(q_rot, kw_ref, g, qb, tq, may_precede_start):
    n_wb = WINDOW // tq + 1
    ki = _iota((tq, tq), 0)
    ci = _iota((tq, tq), 1)
    blocks = []
    for j in range(n_wb):
        kb = qb - (n_wb - 1) + j
        kb_read = jnp.maximum(kb, 0) if may_precede_start else kb
        s = _dot(kw_ref[pl.ds(pl.multiple_of(kb_read * tq, tq), tq), g * LANES:(g + 1) * LANES], q_rot)
        if j == 0:
            s = s + _heads_tiled(jnp.where(ki > ci, 0.0, NEG_INF))
        if j == n_wb - 1:
            s = s + _heads_tiled(jnp.where(ki <= ci, 0.0, NEG_INF))
        elif may_precede_start:
            s = s + jnp.where(kb >= 0, 0.0, NEG_INF)
        blocks.append((s, kb_read))
    return blocks


def _window_finish(blocks, vwt_ref, g, tq):
    m_w = blocks[0][0].max(axis=0, keepdims=True)
    for s, _ in blocks[1:]:
        m_w = jnp.maximum(m_w, s.max(axis=0, keepdims=True))
    o_aug = jnp.zeros((LANES, NSA_REP * tq), F32)
    for s, kb_read in blocks:
        o_aug = o_aug + _dot(vwt_ref[g, kb_read], jnp.exp2(s - m_w).astype(BF16))
    return o_aug[0:HEAD_DIM] * (1.0 / o_aug[HEAD_DIM:HEAD_DIM + 1])


def _nsa_kernel(qraw_ref, qrot_ref, kc_ref, vct_ref, ks_ref, vst_ref, kw_ref, vwt_ref, gate_ref, prev_ref,
                o_ref, imp_ref, *, seq, tq, tk, q_lo, n_qb, n_rank, n_cmp_rows):
    del prev_ref
    qb = pl.program_id(1) + q_lo
    t0 = qb * tq
    groups = range(NSA_KV_HEADS)
    n_cols = NSA_REP * tq

    def grp(g):
        return slice(g * LANES, (g + 1) * LANES)

    def q_cols(g):
        return slice(g * NSA_REP * LANES, (g + 1) * NSA_REP * LANES)

    def q_aug(q_t, extra_rows):
        return jnp.concatenate([q_t, extra_rows], axis=0).astype(BF16)

    no_extra = jnp.zeros((HEAD_DIM, n_cols), F32)
    q_rot_t = [_q_transposed(qrot_ref[:, q_cols(g)]) for g in groups]
    s_cmp = [_dot(kc_ref[0:n_cmp_rows, grp(g)], q_aug(_q_transposed(qraw_ref[:, q_cols(g)]), no_extra))
             for g in groups]
    win_blocks = [_window_scores(q_aug(q_rot_t[g], no_extra), kw_ref, g, qb, tq, q_lo * tq < WINDOW)
                  for g in groups]
    o_cmp, q_sel = [], []
    for g in groups:
        vc_t = jnp.concatenate([vct_ref[g, c] for c in range(n_cmp_rows // LANES)], axis=1)
        o_c, p_sum = _nsa_compressed(s_cmp[g], vc_t, t0, tq, seq)
        o_cmp.append(o_c[0:HEAD_DIM])
        sel_neg = _heads_tiled(_nsa_selection_rows(p_sum, imp_ref.at[g], t0, tq, seq, n_rank))
        q_sel.append(q_aug(q_rot_t[g], sel_neg))
    o_win = [_window_finish(win_blocks[g], vwt_ref, g, tq) for g in groups]

    blocks_per_tile = tk // LANES
    n_full = t0 // tk
    min_full = (q_lo * tq) // tk
    max_full = ((q_lo + n_qb - 1) * tq) // tk

    def k_tile(kt, g):
        if isinstance(kt, int):
            return ks_ref[kt * tk:(kt + 1) * tk, grp(g)]
        return ks_ref[pl.ds(pl.multiple_of(kt * tk, tk), tk), grp(g)]

    def v_tile(kt, g):
        return jnp.concatenate([vst_ref[g, kt * blocks_per_tile + c] for c in range(blocks_per_tile)], axis=1)

    def online_update(scores, values, m_i, acc):
        m_new = m_i
        for s in scores:
            m_new = jnp.maximum(m_new, s.max(axis=0, keepdims=True))
        p = jnp.concatenate([jnp.exp2(s - m_new).astype(BF16) for s in scores], axis=0)
        return m_new, jnp.exp2(m_i - m_new) * acc + _dot(jnp.concatenate(values, axis=1), p)

    pair_w = 2 * tq
    chains = [(g, h) for g in groups for h in range(NSA_REP // 2)]

    def q_pair(g, h):
        return q_sel[g][:, h * pair_w:(h + 1) * pair_w]

    tiles = list(range(max_full))
    trips = [tiles[i:i + SLC_UNROLL] for i in range(0, len(tiles), SLC_UNROLL)]

    def trip_scores(trip):
        scores = []
        for g, h in chains:
            per_tile = []
            for kt in trip:
                s = _dot(k_tile(kt, g), q_pair(g, h))
                per_tile.append(s if kt < min_full else s + jnp.where(kt < n_full, 0.0, NEG_INF))
            scores.append(per_tile)
        return scores

    def trip_update(trip, scores, stats):
        return [online_update(scores[c], [v_tile(kt, g) for kt in trip], *stats[c])
                for c, (g, h) in enumerate(chains)]

    causal = n_full * tk + _iota((tk, tq), 0) <= t0 + _iota((tk, tq), 1)
    causal_bias = _heads_tiled(jnp.where(causal, 0.0, NEG_INF))
    s_diag = [_dot(k_tile(n_full, g), q_sel[g]) for g in groups]
    pending = trip_scores(trips[0]) if trips else None
    first = [online_update([s_diag[g] + causal_bias], [v_tile(n_full, g)],
                           jnp.full((1, n_cols), NEG_INF, F32), jnp.zeros((LANES, n_cols), F32)) for g in groups]
    stats = [tuple(x[:, h * pair_w:(h + 1) * pair_w] for x in first[g]) for g, h in chains]
    for i, trip in enumerate(trips):
        upcoming = trip_scores(trips[i + 1]) if i + 1 < len(trips) else None
        stats = trip_update(trip, pending, stats)
        pending = upcoming

    gate_all = gate_ref[...]
    for g in groups:
        acc = jnp.concatenate([stats[c][1] for c, (cg, _) in enumerate(chains) if cg == g], axis=1)
        o_slc = acc[0:HEAD_DIM] * (1.0 / acc[HEAD_DIM:HEAD_DIM + 1])
        gate_t = jnp.transpose(gate_all[:, grp(g)])
        o_heads = []
        for r in range(NSA_REP):
            cols = slice(r * tq, (r + 1) * tq)
            o_heads.append(gate_t[3 * r:3 * r + 1] * o_cmp[g][:, cols] + gate_t[3 * r + 1:3 * r + 2] * o_slc[:, cols]
                           + gate_t[3 * r + 2:3 * r + 3] * o_win[g][:, cols])
        for pair in range(NSA_REP // 2):
            slab = jnp.transpose(jnp.concatenate(o_heads[2 * pair:2 * pair + 2], axis=0))
            c0 = (g * NSA_REP // 2 + pair) * LANES
            o_ref[:, c0:c0 + LANES] = slab.astype(BF16)


def _nsa_call(qraw, qrot, kc, vct, ks, vst, kw, vwt, gate, *, layer, batch, seq, tq=Q_BLOCK, tk=512):
    assert tq == LANES and tk % tq == 0 and seq % tk == 0 and seq // SLC_LEN <= HEAD_DIM
    nq = seq // tq
    n_cp = seq // CMP_STRIDE
    grp_w = NSA_KV_HEADS * LANES

    def k_spec(rows):
        return pl.BlockSpec((None, rows, grp_w), lambda b, i: (b, 0, 0))

    def vt_spec(rows):
        return pl.BlockSpec((None, NSA_KV_HEADS, rows // LANES, LANES, LANES), lambda b, i: (b, 0, 0, 0, 0))

    ks3, kw3 = ks.reshape(batch, seq, grp_w), kw.reshape(batch, seq, grp_w)
    part = nq // NSA_PARTS
    out = jnp.zeros((batch * seq, NSA_W), BF16)
    for p in range(NSA_PARTS):
        q_lo, last_token = p * part, (p + 1) * part * tq - 1
        n_rank = min(seq // SLC_LEN, -(-(last_token // SLC_LEN + 1) // 8) * 8)
        n_cmp_rows = min(n_cp, -(-(last_token // CMP_STRIDE + 1) // LANES) * LANES)

        def row_spec(width, q_lo=q_lo):
            return pl.BlockSpec((tq, width), lambda b, i: (b * nq + q_lo + i, 0))

        n_keys = -(-(last_token + 1) // tk) * tk
        in_specs = [row_spec(NSA_HEADS * LANES), row_spec(NSA_HEADS * LANES), k_spec(n_cmp_rows),
                    vt_spec(n_cmp_rows), k_spec(n_keys), vt_spec(n_keys), k_spec(n_keys), vt_spec(n_keys),
                    row_spec(grp_w)]
        in_specs.append(pl.BlockSpec(memory_space=pl.ANY))
        args = [qraw, qrot, kc, vct, ks3, vst, kw3, vwt, gate, out]
        out = pl.pallas_call(
            functools.partial(_nsa_kernel, seq=seq, tq=tq, tk=tk, q_lo=q_lo, n_qb=part, n_rank=n_rank,
                              n_cmp_rows=n_cmp_rows),
            grid=(batch, part),
            in_specs=in_specs,
            out_specs=pl.BlockSpec((tq, NSA_W), lambda b, i, q_lo=q_lo: (b * nq + q_lo + i, 0)),
            out_shape=jax.ShapeDtypeStruct((batch * seq, NSA_W), BF16),
            input_output_aliases={len(args) - 1: 0},
            scratch_shapes=[pltpu.VMEM((NSA_KV_HEADS, LANES, tq), F32)],
            compiler_params=_params(2),
            name=f"nsa_attn_l{layer}_p{p}",
        )(*args)
    return out


LIN_BLOCK = 2 * CHUNK


def _lin_block_consts():
    r = _iota((LIN_BLOCK, LANES), 0)
    c = _iota((LIN_BLOCK, LANES), 1)
    same_chunk = (r >= CHUNK) == (c >= CHUNK)
    causal = same_chunk & (r >= c)
    head_diag = same_chunk
    first_rows = r < CHUNK
    low_lanes = c < HEAD_DIM
    return causal, head_diag, first_rows, low_lanes


def _head_stats(x, head_diag_mean):
    hi = x.astype(BF16)
    lo = (x - hi.astype(F32)).astype(BF16)
    return _dot(hi, head_diag_mean) + _dot(lo, head_diag_mean)


def _lin_core(chains, consts):
    causal, head_diag, first_rows, low_lanes = consts
    mask2 = jnp.concatenate([causal, causal], axis=0)
    scores, updates = [], []
    for c in chains:
        zero = jnp.zeros_like(c["q_dec"])
        q2 = jnp.concatenate([jnp.where(low_lanes, c["q_dec"], zero), jnp.where(low_lanes, zero, c["q_dec"])], axis=0)
        scores.append(_dot_nt(q2.astype(BF16), c["k_inv"].astype(BF16)))
    for c in chains:
        v_t = jnp.transpose(c["v"]).astype(BF16)
        kd_ab = jnp.concatenate([jnp.where(first_rows, c["k_dec"], 0.0), jnp.where(first_rows, 0.0, c["k_dec"])],
                                axis=1)
        updates.append(_dot(v_t, kd_ab.astype(BF16)))
    intra, states = [], []
    for c, a2 in zip(chains, scores):
        a2 = jnp.where(mask2, a2, 0.0)
        if c["intra_scale"] is not None:
            a2 = a2 * c["intra_scale"]
        o2 = _dot(a2.astype(BF16), c["v"].astype(BF16))
        intra.append(jnp.where(low_lanes, o2[0:LIN_BLOCK], o2[LIN_BLOCK:]))
    out = []
    for c, upd, o_intra in zip(chains, updates, intra):
        st_a = c["st_prev"] * c["dec_a"] + jnp.where(head_diag, upd[:, 0:LANES], 0.0)
        st_b = st_a * c["dec_b"] + jnp.where(head_diag, upd[:, LANES:], 0.0)
        o_ab = _dot_nt(c["q_int"].astype(BF16), jnp.concatenate([c["st_prev"], st_a], axis=0).astype(BF16))
        out.append((o_intra + jnp.where(first_rows, o_ab[:, 0:LANES], o_ab[:, LANES:]), st_b))
    return out


def _gla_kernel(x_ref, lr_ref, a2_ref, ab_ref, ng_ref, o_ref, st_ref, *, n_blocks, batch):
    @pl.when(pl.program_id(0) == 0)
    def _():
        st_ref[...] = jnp.zeros_like(st_ref)

    consts = _lin_block_consts()
    causal, head_diag, first_rows, low_lanes = consts
    ri = _iota((LIN_BLOCK, LANES), 0)
    ci = _iota((LIN_BLOCK, LANES), 1)
    tri_t = jnp.where(((ri >= CHUNK) == (ci >= CHUNK)) & (ri <= ci), 1.0, 0.0).astype(BF16)
    mean_op = jnp.where(head_diag, 1.0 / HEAD_DIM, 0.0).astype(BF16)
    a2h = a2_ref[...].astype(BF16)
    a2l = (a2_ref[...] - a2h.astype(F32)).astype(BF16)

    def block(i):
        rows = pl.ds(i * LIN_BLOCK, LIN_BLOCK)
        lr = jnp.concatenate([lr_ref[bi, rows, :] for bi in range(batch)], axis=0)
        lh = lr.astype(BF16)
        ll = (lr - lh.astype(F32)).astype(BF16)
        z = _dot(lh, a2h) + _dot(lh, a2l) + _dot(ll, a2h) + ab_ref[...]
        log_a = (jnp.minimum(z, 0.0) - jnp.log(1.0 + jnp.exp(-jnp.abs(z)))) * (1.0 / GLA_TAU)
        n_slab = GLA_W // LANES
        la_t = jnp.concatenate([jnp.transpose(log_a[bi * LIN_BLOCK:(bi + 1) * LIN_BLOCK, s * LANES:(s + 1) * LANES])
                                for bi in range(batch) for s in range(n_slab)], axis=0)
        cum_t = _dot3_exact_rhs(la_t, tri_t)
        chains = []
        for bi in range(batch):
            for s in range(n_slab):
                cols = slice(s * LANES, (s + 1) * LANES)
                n = bi * n_slab + s

                def ld(k):
                    return x_ref[bi, rows, k * GLA_W + s * LANES:k * GLA_W + (s + 1) * LANES]

                q, k, v, gg = ld(0), ld(1), ld(2), ld(3)
                b = jnp.transpose(cum_t[n * LANES:(n + 1) * LANES])
                bl_a = b[CHUNK - 1:CHUNK, :]
                bl_b = b[LIN_BLOCK - 1:LIN_BLOCK, :]
                bl = jnp.where(first_rows, bl_a, bl_b)
                q_dec = q * QK_SCALE * jnp.exp(b)
                chains.append(dict(bi=bi, s=s, cols=cols, gate=gg, q_dec=q_dec, q_int=q_dec, k_inv=k * jnp.exp(-b),
                                   k_dec=k * jnp.exp(bl - b), v=v, intra_scale=None, dec_a=jnp.exp(bl_a),
                                   dec_b=jnp.exp(bl_b), st_prev=st_ref[bi, s]))
        results = _lin_core(chains, consts)
        for c, (_, st_new) in zip(chains, results):
            st_ref[c["bi"], c["s"]] = st_new
        ms = _head_stats(jnp.concatenate([o * o for o, _ in results], axis=0), mean_op)
        for n, (c, (o, _)) in enumerate(zip(chains, results)):
            y = (o * lax.rsqrt(ms[n * LIN_BLOCK:(n + 1) * LIN_BLOCK] + NORM_EPS) * ng_ref[:, c["cols"]]
                 * _silu(c["gate"]))
            o_ref[c["bi"], rows, c["cols"]] = y.astype(BF16)

    for i in range(n_blocks):
        block(i)


def _ret_kernel(x_ref, dmat_ref, qf_ref, kf_ref, cd_ref, ng_ref, o_ref, st_ref, *, n_blocks, batch):
    @pl.when(pl.program_id(0) == 0)
    def _():
        st_ref[...] = jnp.zeros_like(st_ref)

    consts = _lin_block_consts()
    causal, head_diag, first_rows, low_lanes = consts
    mean_op = jnp.where(head_diag, 1.0 / HEAD_DIM, 0.0).astype(BF16)

    def block(i):
        rows = pl.ds(i * LIN_BLOCK, LIN_BLOCK)
        chains = []
        for bi in range(batch):
            for s in range(RET_W // LANES):
                cols = slice(s * LANES, (s + 1) * LANES)

                def ld(k):
                    return x_ref[bi, rows, k * RET_W + s * LANES:k * RET_W + (s + 1) * LANES]

                q, k, v, gg = ld(0), ld(1), ld(2), ld(3)
                cd = cd_ref[:, cols]
                chains.append(dict(bi=bi, s=s, cols=cols, gate=gg, q_dec=q, q_int=q * qf_ref[:, cols], k_inv=k,
                                   k_dec=k * kf_ref[:, cols], v=v, intra_scale=dmat_ref[s], dec_a=cd, dec_b=cd,
                                   st_prev=st_ref[bi, s]))
        results = _lin_core(chains, consts)
        for c, (_, st_new) in zip(chains, results):
            st_ref[c["bi"], c["s"]] = st_new
        o_all = jnp.concatenate([o for o, _ in results], axis=0)
        oc_all = o_all - _head_stats(o_all, mean_op)
        var_all = _head_stats(oc_all * oc_all, mean_op)
        y_all = oc_all * lax.rsqrt(var_all + NORM_EPS)
        for n, c in enumerate(chains):
            y = y_all[n * LIN_BLOCK:(n + 1) * LIN_BLOCK] * ng_ref[:, c["cols"]] * _silu(c["gate"])
            o_ref[c["bi"], rows, c["cols"]] = y.astype(BF16)

    for i in range(n_blocks):
        block(i)


def _gla_call(gla, lr, a2_pad, a_bias, norm_g, *, layer, batch, seq, tb=LIN_BLOCK):
    n_slab = GLA_W // LANES

    def lspec(shape):
        return pl.BlockSpec((None,) + shape, lambda t: (layer, 0, 0))

    return pl.pallas_call(
        functools.partial(_gla_kernel, n_blocks=tb // LIN_BLOCK, batch=batch),
        grid=(seq // tb,),
        in_specs=[pl.BlockSpec((batch, tb, 4 * GLA_W), lambda t: (0, t, 0)),
                  pl.BlockSpec((batch, tb, LANES), lambda t: (0, t, 0)),
                  lspec((LANES, GLA_W)), lspec((1, GLA_W)), lspec((1, GLA_W))],
        out_specs=pl.BlockSpec((batch, tb, GLA_W), lambda t: (0, t, 0)),
        out_shape=jax.ShapeDtypeStruct((batch, seq, GLA_W), BF16),
        scratch_shapes=[pltpu.VMEM((batch, n_slab, LANES, LANES), F32)],
        compiler_params=_params(1),
        name=f"gla_l{layer}",
    )(gla.reshape(batch, seq, 4 * GLA_W), lr.reshape(batch, seq, LANES), a2_pad, a_bias, norm_g
      ).reshape(batch * seq, GLA_W)


def _ret_call(ret, dmat, qf, kf, cd, norm_g, *, layer, batch, seq, tb=LIN_BLOCK):
    n_slab = RET_W // LANES
    full = lambda a: pl.BlockSpec(a.shape, lambda t: (0,) * a.ndim)
    return pl.pallas_call(
        functools.partial(_ret_kernel, n_blocks=tb // LIN_BLOCK, batch=batch),
        grid=(seq // tb,),
        in_specs=[pl.BlockSpec((batch, tb, 4 * RET_W), lambda t: (0, t, 0)),
                  full(dmat), full(qf), full(kf), full(cd),
                  pl.BlockSpec((None, 1, RET_W), lambda t: (layer, 0, 0))],
        out_specs=pl.BlockSpec((batch, tb, RET_W), lambda t: (0, t, 0)),
        out_shape=jax.ShapeDtypeStruct((batch, seq, RET_W), BF16),
        scratch_shapes=[pltpu.VMEM((batch, n_slab, LANES, LANES), F32)],
        compiler_params=_params(1),
        name=f"ret_l{layer}",
    )(ret.reshape(batch, seq, 4 * RET_W), dmat, qf, kf, cd, norm_g).reshape(batch * seq, RET_W)


def _layout_proj_weights(w_in, nsa_gate_bias):
    n_layers, d, _ = w_in.shape
    (g_q, g_k, g_v, g_g, g_lr, n_q, n_kc, n_vc, n_ks, n_vs, n_kw, n_vw, n_gate,
     r_q, r_k, r_v, r_g) = jnp.split(w_in, IN_SPLITS, axis=-1)
    zeros = lambda w: jnp.zeros((n_layers, d, w), w_in.dtype)
    def pad_heads(w, n_heads):
        w = w.reshape(n_layers, d, n_heads, HEAD_DIM)
        return jnp.concatenate([w, jnp.zeros_like(w)], axis=-1).reshape(n_layers, d, n_heads * LANES)

    per_grp = 3 * NSA_REP
    gate_cols = []
    bias_cols = []
    for g in range(NSA_KV_HEADS):
        gate_cols += [n_gate[..., g * per_grp:(g + 1) * per_grp], zeros(LANES - per_grp)]
        bias_cols += [nsa_gate_bias[:, g * per_grp:(g + 1) * per_grp],
                      jnp.zeros((n_layers, LANES - per_grp), nsa_gate_bias.dtype)]
    w = jnp.concatenate([g_q, g_k, g_v, g_g, g_lr, zeros(LANES - GLA_LOWRANK), pad_heads(n_q, NSA_HEADS),
                         n_kc, n_vc, pad_heads(n_ks, NSA_KV_HEADS), pad_heads(n_kw, NSA_KV_HEADS), n_vs, n_vw]
                        + gate_cols + [r_q, r_k, r_v, r_g], axis=-1)
    assert w.shape[-1] == PROJ_W
    bias = jnp.concatenate(bias_cols, axis=-1).reshape(n_layers, 1, NSA_KV_HEADS * LANES)
    return w.astype(BF16), bias


def _layout_cmp_weights(pe_k, pe_v, w1_k, w2_k, w1_v, w2_v):
    n_layers = pe_k.shape[0]
    eye = jnp.eye(2, dtype=BF16)
    w1 = jnp.stack([w1_k, w1_v], axis=1).reshape(n_layers, 2, CMP_LEN, HEAD_DIM, CMP_HIDDEN).astype(BF16)
    blocks = [jnp.pad(w1[:, kv], ((0, 0), (0, 0), (0, 0), ((2 * kv + g) * CMP_HIDDEN, (3 - 2 * kv - g) * CMP_HIDDEN)))
              for kv in range(2) for g in range(NSA_KV_HEADS)]
    w1_full = jnp.stack(blocks, axis=2).reshape(n_layers, CMP_LEN * 2 * NSA_KV_W, 4 * CMP_HIDDEN)
    half = CMP_STRIDE * 2 * NSA_KV_W
    w2 = jnp.stack([w2_k, w2_v], axis=1).astype(BF16)
    w2_full = jnp.einsum('zknd,kq,gh->zqhnkgd', w2, eye, eye)
    w2_full = w2_full.reshape(n_layers, 4 * CMP_HIDDEN, 2 * NSA_KV_W)
    w2_k = w2_full[..., :NSA_KV_W].reshape(n_layers, 4 * CMP_HIDDEN, NSA_KV_HEADS, HEAD_DIM)
    w2_k = jnp.concatenate([w2_k, jnp.zeros_like(w2_k)], axis=-1).reshape(n_layers, 4 * CMP_HIDDEN, -1)
    w2_full = jnp.concatenate([w2_k, w2_full[..., NSA_KV_W:]], axis=-1).astype(BF16)
    pe = jnp.stack([pe_k, pe_v], axis=1)
    pe = jnp.broadcast_to(pe.transpose(0, 2, 1, 3)[:, :, :, None, :],
                          (n_layers, CMP_LEN, 2, NSA_KV_HEADS, HEAD_DIM))
    pe = pe.reshape(n_layers, 1, CMP_LEN * 2 * NSA_KV_W)
    return pe[:, :, :half], pe[:, :, half:], w1_full.reshape(n_layers, 2, half, 4 * CMP_HIDDEN), w2_full


def _rotary_tables(seq):
    half = HEAD_DIM // 2
    inv_freq = ROPE_THETA ** (-jnp.arange(half, dtype=F32) / half)
    ang = jnp.arange(seq).astype(F32)[:, None] * inv_freq[None, :]
    cos, sin = jnp.cos(ang), jnp.sin(ang)
    reps = LANES // HEAD_DIM
    return jnp.tile(jnp.concatenate([cos, cos], axis=-1), (1, reps)), \
        jnp.tile(jnp.concatenate([-sin, sin], axis=-1), (1, reps))


def _retention_tables():
    log_gamma = jnp.log1p(-jnp.exp2(-5.0 - jnp.arange(RET_HEADS, dtype=F32)))
    lg_lane = jnp.repeat(log_gamma, HEAD_DIM)[None, :]
    pos = (jnp.arange(LIN_BLOCK) % CHUNK).astype(F32)[:, None]
    qf = jnp.exp(lg_lane * (pos + 1.0))
    kf = jnp.exp(lg_lane * (CHUNK - 1.0 - pos))
    cd = jnp.exp(lg_lane * CHUNK)
    r = jnp.arange(LIN_BLOCK)
    rel = (r[:, None] - r[None, :]).astype(F32)
    ok = ((r[:, None] // CHUNK) == (r[None, :] // CHUNK)) & (rel >= 0)
    dm = jnp.where(ok[None], jnp.exp(log_gamma[:, None, None] * rel[None]), 0.0)
    dmat = dm.reshape(RET_W // LANES, 2 * LIN_BLOCK, LIN_BLOCK)
    return dmat, qf, kf, cd


def kernel(x, c, w_ada, b_ada, norm_g, ffn1_in, ffn1_out, w_in, gla_a2, gla_a_bias, gla_norm_g,
           nsa_pe_k, nsa_pe_v, nsa_w1_k, nsa_w2_k, nsa_w1_v, nsa_w2_v, nsa_gate_bias, ret_norm_g,
           w_out, ffn2_in, ffn2_out, final_norm_g):
    batch, seq, d = x.shape
    n_layers = w_ada.shape[0]
    assert batch <= 8 and seq % 512 == 0 and seq >= WINDOW + Q_BLOCK

    c_pad = jnp.zeros((8, d), F32).at[:batch].set(c)
    mod = _ada_call(c_pad, w_ada, b_ada).reshape(n_layers * 8 * N_ADA, 1, d)
    ng = norm_g.reshape(n_layers * 3, 1, d)

    w_proj, gate_bias = _layout_proj_weights(w_in, nsa_gate_bias)
    pe_top, pe_bot, w1_cmp, w2_cmp = _layout_cmp_weights(nsa_pe_k, nsa_pe_v, nsa_w1_k, nsa_w2_k, nsa_w1_v, nsa_w2_v)
    cosf, sinf = _rotary_tables(seq)
    dmat, qf, kf, cd = _retention_tables()
    a2_pad = jnp.zeros((n_layers, LANES, GLA_W), F32).at[:, :GLA_LOWRANK].set(gla_a2)
    a_bias = gla_a_bias.reshape(n_layers, 1, GLA_W)
    gla_g = jnp.tile(gla_norm_g, (1, GLA_HEADS)).reshape(n_layers, 1, GLA_W)
    ret_g = jnp.tile(ret_norm_g, (1, RET_HEADS)).reshape(n_layers, 1, RET_W)
    f1_in, f1_out = ffn1_in.astype(BF16), ffn1_out.astype(BF16)
    f2_in, f2_out = ffn2_in.astype(BF16), ffn2_out.astype(BF16)
    w_o = w_out.astype(BF16)

    xs = x.reshape(batch * seq, d)
    for l in range(n_layers):
        xs = _ffn_call(xs, ng, mod, f1_in, f1_out, layer=l, sub=0, seq=seq)
        (gla, lr, qraw, qrot, kcvc, ks, vs, kw, vw, gate, ret) = _proj_call(
            xs, ng, mod, w_proj, gate_bias, cosf, sinf, layer=l, seq=seq)
        kc, vc = _cmp_call(kcvc, pe_top, pe_bot, w1_cmp, w2_cmp, layer=l, batch=batch, seq=seq)
        o_gla = _gla_call(gla, lr, a2_pad, a_bias, gla_g, layer=l, batch=batch, seq=seq)
        o_ret = _ret_call(ret, dmat, qf, kf, cd, ret_g, layer=l, batch=batch, seq=seq)
        o_nsa = _nsa_call(qraw, qrot, kc, vc, ks, vs, kw, vw, gate, layer=l, batch=batch, seq=seq)
        final_g = final_norm_g.reshape(1, d) if l == n_layers - 1 else None
        xs = _ffn_call(xs, ng, mod, f2_in, f2_out, layer=l, sub=2, seq=seq, mixer=(o_gla, o_nsa, o_ret, w_o),
                       final_g=final_g)
    return xs.reshape(batch, seq, d)
```

```python
import functools
import math

import numpy as np
import jax
import jax.numpy as jnp
from jax import lax
from jax.experimental import pallas as pl
from jax.experimental.pallas import tpu as pltpu

F32 = jnp.float32
BF16 = jnp.bfloat16

HEAD_DIM = 64
LANES = 128
GLA_HEADS = 4
NSA_HEADS = 8
NSA_KV_HEADS = 2
NSA_REP = NSA_HEADS // NSA_KV_HEADS
RET_HEADS = 4
GLA_LOWRANK = 16
GLA_TAU = 16.0
CHUNK = 64
CMP_LEN = 32
CMP_STRIDE = 16
CMP_HIDDEN = 128
SLC_LEN = 64
SLC_TOPK = 16
WINDOW = 512
Q_BLOCK = 128
ROPE_THETA = 10000.0
FFN_HALF = 0.5
NORM_EPS = 1e-6
NEG_INF = -1e30
FORCED_SCORE = 1e4
N_ADA = 9
QK_SCALE = HEAD_DIM ** -0.5
SLC_SHIFT = int(math.log2(SLC_LEN))
LOG2_E = math.log2(math.e)

GLA_W = GLA_HEADS * HEAD_DIM
NSA_W = NSA_HEADS * HEAD_DIM
NSA_KV_W = NSA_KV_HEADS * HEAD_DIM
RET_W = RET_HEADS * HEAD_DIM
IN_SIZES = (GLA_W, GLA_W, GLA_W, GLA_W, GLA_LOWRANK,
            NSA_W, NSA_KV_W, NSA_KV_W, NSA_KV_W, NSA_KV_W, NSA_KV_W, NSA_KV_W, 3 * NSA_HEADS,
            RET_W, RET_W, RET_W, RET_W)
IN_SPLITS = tuple(int(s) for s in np.cumsum(IN_SIZES)[:-1])

SEG_GLA = 0
SEG_LR = SEG_GLA + 4 * GLA_W
SEG_NQ = SEG_LR + LANES
SEG_KCVC = SEG_NQ + NSA_W
SEG_KS = SEG_KCVC + 2 * NSA_KV_W
SEG_KW = SEG_KS + NSA_KV_HEADS * LANES
SEG_VS = SEG_KW + NSA_KV_HEADS * LANES
SEG_VW = SEG_VS + NSA_KV_W
SEG_GATE = SEG_VW + NSA_KV_W
SEG_RET = SEG_GATE + NSA_KV_HEADS * LANES
PROJ_W = SEG_RET + 4 * RET_W

VMEM_LIMIT_BYTES = 56 * 1024 * 1024

NT_DIMS = (((1,), (1,)), ((), ()))


def _dot(a, b):
    return jnp.dot(a, b, preferred_element_type=F32)


def _dot_nt(a, b):
    return lax.dot_general(a, b, NT_DIMS, preferred_element_type=F32)


def _split3(x):
    hi = x.astype(BF16)
    r1 = x - hi.astype(F32)
    mid = r1.astype(BF16)
    lo = (r1 - mid.astype(F32)).astype(BF16)
    return hi, mid, lo


def _dot3_exact_rhs(x, w_bf16):
    hi, mid, lo = _split3(x)
    return _dot(hi, w_bf16) + _dot(mid, w_bf16) + _dot(lo, w_bf16)


def _dot3_exact_lhs(w_bf16, x):
    hi, mid, lo = _split3(x)
    return _dot(w_bf16, hi) + _dot(w_bf16, mid) + _dot(w_bf16, lo)


def _sigmoid(x):
    return 1.0 / (1.0 + jnp.exp(-x))


def _silu(x):
    return x * _sigmoid(x)


def _iota(shape, dim):
    return lax.broadcasted_iota(jnp.int32, shape, dim)


def _params(n_grid):
    return pltpu.CompilerParams(dimension_semantics=("arbitrary",) * n_grid,
                                vmem_limit_bytes=VMEM_LIMIT_BYTES)


def _norm_mod(x, g, sh, sc):
    ms = jnp.mean(x * x, axis=-1, keepdims=True)
    return x * lax.rsqrt(ms + NORM_EPS) * (g * (1.0 + sc)) + sh


def _ada_kernel(c_ref, w_ref, b_ref, o_ref):
    c = c_ref[...]
    o_ref[...] = _dot(_silu(c).astype(BF16), w_ref[...].astype(BF16)) + b_ref[...]


def _ada_call(c_pad, w_ada, b_ada):
    n_layers, d, n_out = w_ada.shape
    tn = 2304 if n_out % 2304 == 0 else n_out
    return pl.pallas_call(
        _ada_kernel,
        grid=(n_layers, n_out // tn),
        in_specs=[pl.BlockSpec((8, d), lambda l, j: (0, 0)),
                  pl.BlockSpec((None, d, tn), lambda l, j: (l, 0, j)),
                  pl.BlockSpec((None, 1, tn), lambda l, j: (l, 0, j))],
        out_specs=pl.BlockSpec((None, 8, tn), lambda l, j: (l, 0, j)),
        out_shape=jax.ShapeDtypeStruct((n_layers, 8, n_out), F32),
        compiler_params=_params(2),
        name="ada_mod",
    )(c_pad, w_ada, b_ada.reshape(n_layers, 1, n_out))


def _ffn_kernel(x_ref, g_ref, sh_ref, sc_ref, gt_ref, win_ref, wout_ref, *rest, d_ff, chunks, mixer, final):
    rest = list(rest)
    x = x_ref[...]
    if mixer:
        og_ref, on_ref, or_ref, gm_ref, wo_ref = rest[:5]
        rest = rest[5:]
        x = x + gm_ref[...] * (_dot(og_ref[...], wo_ref[0:GLA_W, :])
                               + _dot(on_ref[...], wo_ref[GLA_W:GLA_W + NSA_W, :])
                               + _dot(or_ref[...], wo_ref[GLA_W + NSA_W:, :]))
    if final:
        fg_ref, o_ref, act_ref = rest
    else:
        o_ref, act_ref = rest
    h = _norm_mod(x, g_ref[...], sh_ref[...], sc_ref[...]).astype(BF16)
    for (c0, cw) in chunks:
        gate = _dot(h, win_ref[:, c0:c0 + cw])
        up = _dot(h, win_ref[:, d_ff + c0:d_ff + c0 + cw])
        act_ref[:, c0:c0 + cw] = (_silu(gate) * up).astype(BF16)
    y = _dot(act_ref[...], wout_ref[...])
    xn = x + (FFN_HALF * gt_ref[...]) * y
    if final:
        ms = jnp.mean(xn * xn, axis=-1, keepdims=True)
        xn = xn * lax.rsqrt(ms + NORM_EPS) * fg_ref[...]
    o_ref[...] = xn


def _ffn_call(x, ng, mod, w_in, w_out, *, layer, sub, seq, mixer=None, final_g=None, tm=1024):
    m, d = x.shape
    d_ff = w_out.shape[1]
    tiles_per_seq = seq // tm
    chunks, c0 = [], 0
    while c0 < d_ff:
        cw = min(512, d_ff - c0)
        chunks.append((c0, cw))
        c0 += cw

    def mod_spec(k):
        return pl.BlockSpec((None, 1, d), lambda i: ((layer * 8 + i // tiles_per_seq) * N_ADA + k, 0, 0))

    in_specs = [pl.BlockSpec((tm, d), lambda i: (i, 0)),
                pl.BlockSpec((None, 1, d), lambda i: (layer * 3 + sub, 0, 0)),
                mod_spec(3 * sub), mod_spec(3 * sub + 1), mod_spec(3 * sub + 2),
                pl.BlockSpec((None, d, 2 * d_ff), lambda i: (layer, 0, 0), pipeline_mode=pl.Buffered(1)),
                pl.BlockSpec((None, d_ff, d), lambda i: (layer, 0, 0), pipeline_mode=pl.Buffered(1))]
    args = [x, ng, mod, mod, mod, w_in, w_out]
    if mixer is not None:
        o_gla, o_nsa, o_ret, w_mix = mixer
        in_specs += [pl.BlockSpec((tm, o.shape[1]), lambda i: (i, 0)) for o in (o_gla, o_nsa, o_ret)]
        in_specs += [mod_spec(5), pl.BlockSpec((None,) + w_mix.shape[1:], lambda i: (layer, 0, 0),
                                               pipeline_mode=pl.Buffered(1))]
        args += [o_gla, o_nsa, o_ret, mod, w_mix]
    if final_g is not None:
        in_specs.append(pl.BlockSpec((1, d), lambda i: (0, 0)))
        args.append(final_g)
    return pl.pallas_call(
        functools.partial(_ffn_kernel, d_ff=d_ff, chunks=tuple(chunks), mixer=mixer is not None,
                          final=final_g is not None),
        grid=(m // tm,),
        in_specs=in_specs,
        out_specs=pl.BlockSpec((tm, d), lambda i: (i, 0)),
        out_shape=jax.ShapeDtypeStruct((m, d), F32),
        scratch_shapes=[pltpu.VMEM((tm, d_ff), BF16)],
        compiler_params=_params(1),
        name=f"ffn_l{layer}_s{sub}",
    )(*args)


def _rotate(x, cosf, sinf):
    rows, width = x.shape
    first = (_iota((rows, LANES), 1) & (HEAD_DIM - 1)) < HEAD_DIM // 2
    outs = []
    for j in range(width // LANES):
        xs = x[:, j * LANES:(j + 1) * LANES]
        swapped = jnp.where(first, pltpu.roll(xs, LANES - HEAD_DIM // 2, 1), pltpu.roll(xs, HEAD_DIM // 2, 1))
        outs.append(xs * cosf + swapped * sinf)
    return outs[0] if len(outs) == 1 else jnp.concatenate(outs, axis=1)


def _store_group_transposed(dst_ref, x, fill):
    top = _iota((LANES, LANES), 0) < HEAD_DIM
    for c in range(x.shape[0] // LANES):
        piece = jnp.transpose(x[c * LANES:(c + 1) * LANES])
        dst_ref[0, c] = jnp.where(top, piece, fill).astype(dst_ref.dtype)
        dst_ref[1, c] = jnp.where(top, pltpu.roll(piece, HEAD_DIM, 0), fill).astype(dst_ref.dtype)


def _proj_kernel(x_ref, g_ref, sh_ref, sc_ref, w_ref, gb_ref, cos_ref, sin_ref,
                 gla_ref, lr_ref, qraw_ref, qrot_ref, kcvc_ref, ks_ref, vst_ref, kw_ref, vwt_ref,
                 gate_ref, ret_ref, *, tiles_per_seq):
    h = _norm_mod(x_ref[...], g_ref[...], sh_ref[...], sc_ref[...]).astype(BF16)
    cosf = cos_ref[...]
    sinf = sin_ref[...]
    tm = h.shape[0]

    def seg(c0, width):
        return _dot(h, w_ref[:, c0:c0 + width])

    nq = seg(SEG_NQ, NSA_W)
    for dst_ref, q in ((qraw_ref, nq), (qrot_ref, _rotate(nq, cosf, sinf))):
        q = q * (QK_SCALE * LOG2_E)
        for c in range(tm // LANES):
            for j in range(NSA_W // LANES):
                dst_ref[c, j] = jnp.transpose(q[c * LANES:(c + 1) * LANES, j * LANES:(j + 1) * LANES]).astype(BF16)
    ret_ref[:, 0:RET_W] = _rotate(seg(SEG_RET, RET_W), cosf, sinf)
    ret_ref[:, RET_W:2 * RET_W] = _rotate(seg(SEG_RET + RET_W, RET_W), cosf, sinf) * QK_SCALE
    grp_w = NSA_KV_HEADS * LANES
    pos = (pl.program_id(0) % tiles_per_seq) * tm + _iota((tm, grp_w), 0)
    spare = (_iota((tm, grp_w), 1) & (LANES - 1)) - HEAD_DIM
    blk_onehot = jnp.where(spare == (pos >> SLC_SHIFT), 1.0, 0.0)
    ks_ref[...] = (_rotate(seg(SEG_KS, grp_w), cosf, sinf) + blk_onehot).astype(BF16)
    kw_ref[...] = _rotate(seg(SEG_KW, grp_w), cosf, sinf).astype(BF16)
    _store_group_transposed(vst_ref, seg(SEG_VS, NSA_KV_W), 1.0)
    _store_group_transposed(vwt_ref, seg(SEG_VW, NSA_KV_W), 1.0)
    gate_ref[...] = _sigmoid(seg(SEG_GATE, NSA_KV_HEADS * LANES) + gb_ref[...])
    kcvc_ref[...] = seg(SEG_KCVC, 2 * NSA_KV_W)
    lr_ref[...] = seg(SEG_LR, LANES)
    ret_ref[:, 2 * RET_W:4 * RET_W] = seg(SEG_RET + 2 * RET_W, 2 * RET_W)
    gla_ref[...] = seg(SEG_GLA, 4 * GLA_W)


def _proj_call(x, ng, mod, w_proj, gate_bias, cosf, sinf, *, layer, seq, tm=512):
    m, d = x.shape
    tiles_per_seq = seq // tm

    def mod_spec(k):
        return pl.BlockSpec((None, 1, d), lambda i: ((layer * 8 + i // tiles_per_seq) * N_ADA + k, 0, 0))

    def row_spec(width):
        return pl.BlockSpec((tm, width), lambda i: (i, 0))

    batch = m // seq
    grp_w = NSA_KV_HEADS * LANES

    def vt_spec():
        return pl.BlockSpec((None, NSA_KV_HEADS, tm // LANES, LANES, LANES),
                            lambda i: (i // tiles_per_seq, 0, i % tiles_per_seq, 0, 0))

    vt_shape = jax.ShapeDtypeStruct((batch, NSA_KV_HEADS, seq // LANES, LANES, LANES), BF16)
    n_pairs = NSA_W // LANES
    qt_spec = pl.BlockSpec((tm // LANES, n_pairs, LANES, LANES), lambda i: (i, 0, 0, 0))
    qt_shape = jax.ShapeDtypeStruct((m // LANES, n_pairs, LANES, LANES), BF16)
    out_widths = [(4 * GLA_W, F32), (LANES, F32), "qt", "qt",
                  (2 * NSA_KV_W, F32), (grp_w, BF16), None, (grp_w, BF16), None,
                  (NSA_KV_HEADS * LANES, F32), (4 * RET_W, F32)]

    def out_spec(ow):
        return vt_spec() if ow is None else qt_spec if ow == "qt" else row_spec(ow[0])

    def out_struct(ow):
        return vt_shape if ow is None else qt_shape if ow == "qt" else jax.ShapeDtypeStruct((m, ow[0]), ow[1])

    return pl.pallas_call(
        functools.partial(_proj_kernel, tiles_per_seq=tiles_per_seq),
        grid=(m // tm,),
        in_specs=[row_spec(d),
                  pl.BlockSpec((None, 1, d), lambda i: (layer * 3 + 1, 0, 0)),
                  mod_spec(3), mod_spec(4),
                  pl.BlockSpec((None, d, PROJ_W), lambda i: (layer, 0, 0), pipeline_mode=pl.Buffered(1)),
                  pl.BlockSpec((None, 1, NSA_KV_HEADS * LANES), lambda i: (layer, 0, 0)),
                  pl.BlockSpec((tm, LANES), lambda i: (i % tiles_per_seq, 0)),
                  pl.BlockSpec((tm, LANES), lambda i: (i % tiles_per_seq, 0))],
        out_specs=[out_spec(ow) for ow in out_widths],
        out_shape=[out_struct(ow) for ow in out_widths],
        compiler_params=_params(1),
        name=f"proj_l{layer}",
    )(x, ng, mod, mod, w_proj, gate_bias, cosf, sinf)


def _cmp_kernel(xk_ref, xv_ref, pet_ref, peb_ref, wt_ref, wb_ref, w2_ref, kc_ref, vct_ref):
    n_rows = xk_ref.shape[0] // CMP_STRIDE
    x = jnp.concatenate([ref[pl.ds(l, n_rows, stride=CMP_STRIDE), :]
                         for l in range(CMP_STRIDE) for ref in (xk_ref, xv_ref)], axis=1)
    a = _dot((x + pet_ref[...]).astype(BF16), wt_ref[...])
    b = _dot((x + peb_ref[...]).astype(BF16), wb_ref[...])
    hid = a + pltpu.roll(b, n_rows - 1, 0)
    out = _dot(_silu(hid).astype(BF16), w2_ref[...])
    out = jnp.where(_iota(out.shape, 0) < n_rows - 1, out, 0.0)
    grp_w = NSA_KV_HEADS * LANES
    kc_ref[...] = out[:, 0:grp_w].astype(BF16)
    _store_group_transposed(vct_ref, out[:, grp_w:grp_w + NSA_KV_W], 0.0)


def _cmp_call(kcvc, pe_top, pe_bot, w1_halves, w2, *, layer, batch, seq):
    rows = seq // CMP_STRIDE
    width = CMP_STRIDE * 2 * NSA_KV_W
    x = kcvc.reshape(batch, seq, 2 * NSA_KV_W)
    hid_w = 4 * CMP_HIDDEN
    grp_w = NSA_KV_HEADS * LANES

    def wspec(shape):
        return pl.BlockSpec((None,) + shape, lambda b: (layer, 0, 0))

    return pl.pallas_call(
        _cmp_kernel,
        grid=(batch,),
        in_specs=[pl.BlockSpec((None, seq, NSA_KV_W), lambda b: (b, 0, 0)),
                  pl.BlockSpec((None, seq, NSA_KV_W), lambda b: (b, 0, 1)),
                  wspec((1, width)), wspec((1, width)),
                  pl.BlockSpec((None, None, width, hid_w), lambda b: (layer, 0, 0, 0)),
                  pl.BlockSpec((None, None, width, hid_w), lambda b: (layer, 1, 0, 0)),
                  wspec((hid_w, grp_w + NSA_KV_W))],
        out_specs=[pl.BlockSpec((None, rows, grp_w), lambda b: (b, 0, 0)),
                   pl.BlockSpec((None, NSA_KV_HEADS, rows // LANES, LANES, LANES), lambda b: (b, 0, 0, 0, 0))],
        out_shape=[jax.ShapeDtypeStruct((batch, rows, grp_w), BF16),
                   jax.ShapeDtypeStruct((batch, NSA_KV_HEADS, rows // LANES, LANES, LANES), BF16)],
        compiler_params=_params(1),
        name=f"nsa_compress_l{layer}",
    )(x, x, pe_top, pe_bot, w1_halves, w1_halves, w2)


def _heads_tiled(x):
    return jnp.concatenate([x] * NSA_REP, axis=1)


SLC_UNROLL = 2
NSA_PARTS = 8

def _nsa_compressed(s_c, vc_t, t0, tq, seq):
    n_cp = s_c.shape[0]
    n_cmp = (seq - CMP_LEN) // CMP_STRIDE + 1
    nn = _iota((n_cp, tq), 0)
    tt = t0 + _iota((n_cp, tq), 1)
    bias_c = jnp.where((nn * CMP_STRIDE + (CMP_LEN - 1) <= tt) & (nn < n_cmp), 0.0, NEG_INF)
    has_cmp = t0 + _iota((1, tq), 1) >= CMP_LEN - 1
    p_heads = []
    for r in range(NSA_REP):
        s = s_c[:, r * tq:(r + 1) * tq] + bias_c
        e = jnp.exp2(s - jnp.max(s, axis=0, keepdims=True))
        p_heads.append(e * jnp.where(has_cmp, 1.0 / jnp.sum(e, axis=0, keepdims=True), 0.0))
    o_cmp = _dot(vc_t, jnp.concatenate(p_heads, axis=1).astype(BF16))
    p_sum = p_heads[0]
    for r in range(1, NSA_REP):
        p_sum = p_sum + p_heads[r]
    return o_cmp, p_sum


def _nsa_selection_rows(p_sum, imp_ref, t0, tq, seq, n_rank):
    n_cp = p_sum.shape[0]
    n_cmp = (seq - CMP_LEN) // CMP_STRIDE + 1
    n_slc = seq // SLC_LEN
    n_sel = min(SLC_TOPK, n_slc)
    sb = _iota((LANES, n_cp), 0)
    cb = _iota((LANES, n_cp), 1)
    overlap_t = ((cb * CMP_STRIDE < sb * SLC_LEN + SLC_LEN) & (cb * CMP_STRIDE + CMP_LEN > sb * SLC_LEN)
                 & (sb < n_slc) & (cb < n_cmp))
    imp_t = _dot3_exact_lhs(jnp.where(overlap_t, 1.0, 0.0).astype(BF16), p_sum)
    blk = _iota((LANES, tq), 0)
    cur = (t0 + _iota((LANES, tq), 1)) >> SLC_SHIFT
    forced = (blk == 0) | (blk == cur) | (blk == cur - 1)
    imp_ref[...] = jnp.where(forced, FORCED_SCORE, jnp.where(blk <= cur, imp_t, -1.0))

    def ranked_rows(n_rows):
        if n_rows <= n_sel:
            return jnp.zeros((HEAD_DIM, tq), F32)
        n_grp = n_rows // 8
        vals = [imp_ref[8 * j:8 * (j + 1), :] for j in range(n_grp)]
        ranks = [jnp.zeros((8, tq), F32) for _ in range(n_grp)]
        sub = _iota((8, tq), 0)
        for i in range(n_rows):
            row = jnp.broadcast_to(imp_ref[i:i + 1, :], (8, tq))
            for j in range(n_grp):
                if 8 * j > i:
                    beats = jnp.where(row >= vals[j], 1.0, 0.0)
                elif 8 * j + 7 <= i:
                    beats = jnp.where(row > vals[j], 1.0, 0.0)
                else:
                    beats = jnp.where(sub + 8 * j > i, jnp.where(row >= vals[j], 1.0, 0.0),
                                      jnp.where(row > vals[j], 1.0, 0.0))
                ranks[j] = ranks[j] + beats
        return jnp.concatenate([jnp.where(rk < n_sel, 0.0, NEG_INF) for rk in ranks]
                               + [jnp.zeros((HEAD_DIM - n_rows, tq), F32)] * (n_rows < HEAD_DIM), axis=0)

    return ranked_rows(n_rank)


def _window_scores(q_rot, kw_ref, g, qb, tq, may_precede_start):
    n_wb = WINDOW // tq + 1
    ki = _iota((tq, tq), 0)
    ci = _iota((tq, tq), 1)
    blocks = []
    for j in range(n_wb):
        kb = qb - (n_wb - 1) + j
        kb_read = jnp.maximum(kb, 0) if may_precede_start else kb
        s = _dot(kw_ref[pl.ds(pl.multiple_of(kb_read * tq, tq), tq), g * LANES:(g + 1) * LANES], q_rot)
        if j == 0:
            s = s + _heads_tiled(jnp.where(ki > ci, 0.0, NEG_INF))
        if j == n_wb - 1:
            s = s + _heads_tiled(jnp.where(ki <= ci, 0.0, NEG_INF))
        elif may_precede_start:
            s = s + jnp.where(kb >= 0, 0.0, NEG_INF)
        blocks.append((s, kb_read))
    return blocks


def _window_finish(blocks, vwt_ref, g, tq):
    m_w = blocks[0][0].max(axis=0, keepdims=True)
    for s, _ in blocks[1:]:
        m_w = jnp.maximum(m_w, s.max(axis=0, keepdims=True))
    o_aug = jnp.zeros((LANES, NSA_REP * tq), F32)
    for s, kb_read in blocks:
        o_aug = o_aug + _dot(vwt_ref[g, kb_read], jnp.exp2(s - m_w).astype(BF16))
    return o_aug[0:HEAD_DIM] * (1.0 / o_aug[HEAD_DIM:HEAD_DIM + 1])


def _nsa_kernel(qraw_ref, qrot_ref, kc_ref, vct_ref, ks_ref, vst_ref, kw_ref, vwt_ref, gate_ref, prev_ref,
                o_ref, imp_ref, *, seq, tq, tk, q_lo, n_qb, n_rank, n_cmp_rows):
    del prev_ref
    qb = pl.program_id(1) + q_lo
    t0 = qb * tq
    groups = range(NSA_KV_HEADS)
    n_cols = NSA_REP * tq

    def grp(g):
        return slice(g * LANES, (g + 1) * LANES)

    def q_heads(ref, g):
        return jnp.concatenate([ref[0, (g * NSA_REP + r) // 2, (r % 2) * HEAD_DIM:(r % 2 + 1) * HEAD_DIM, :]
                                for r in range(NSA_REP)], axis=1)

    def q_aug(q_t, extra_rows):
        return jnp.concatenate([q_t, extra_rows.astype(BF16)], axis=0)

    no_extra = jnp.zeros((HEAD_DIM, n_cols), F32)
    q_rot_t = [q_heads(qrot_ref, g) for g in groups]
    s_cmp = [_dot(kc_ref[0:n_cmp_rows, grp(g)], q_aug(q_heads(qraw_ref, g), no_extra)) for g in groups]
    win_blocks = [_window_scores(q_aug(q_rot_t[g], no_extra), kw_ref, g, qb, tq, q_lo * tq < WINDOW)
                  for g in groups]
    o_cmp, q_sel = [], []
    for g in groups:
        vc_t = jnp.concatenate([vct_ref[g, c] for c in range(n_cmp_rows // LANES)], axis=1)
        o_c, p_sum = _nsa_compressed(s_cmp[g], vc_t, t0, tq, seq)
        o_cmp.append(o_c[0:HEAD_DIM])
        sel_neg = _heads_tiled(_nsa_selection_rows(p_sum, imp_ref.at[g], t0, tq, seq, n_rank))
        q_sel.append(q_aug(q_rot_t[g], sel_neg))
    o_win = [_window_finish(win_blocks[g], vwt_ref, g, tq) for g in groups]

    blocks_per_tile = tk // LANES
    n_full = t0 // tk
    min_full = (q_lo * tq) // tk
    max_full = ((q_lo + n_qb - 1) * tq) // tk

    def k_tile(kt, g):
        if isinstance(kt, int):
            return ks_ref[kt * tk:(kt + 1) * tk, grp(g)]
        return ks_ref[pl.ds(pl.multiple_of(kt * tk, tk), tk), grp(g)]

    def v_tile(kt, g):
        return jnp.concatenate([vst_ref[g, kt * blocks_per_tile + c] for c in range(blocks_per_tile)], axis=1)

    def online_update(scores, values, m_i, acc):
        m_new = m_i
        for s in scores:
            m_new = jnp.maximum(m_new, s.max(axis=0, keepdims=True))
        p = jnp.concatenate([jnp.exp2(s - m_new).astype(BF16) for s in scores], axis=0)
        return m_new, jnp.exp2(m_i - m_new) * acc + _dot(jnp.concatenate(values, axis=1), p)

    pair_w = 2 * tq
    chains = [(g, h) for g in groups for h in range(NSA_REP // 2)]

    def q_pair(g, h):
        return q_sel[g][:, h * pair_w:(h + 1) * pair_w]

    tiles = list(range(max_full))
    trips = [tiles[i:i + SLC_UNROLL] for i in range(0, len(tiles), SLC_UNROLL)]

    def trip_scores(trip):
        scores = []
        for g, h in chains:
            per_tile = []
            for kt in trip:
                s = _dot(k_tile(kt, g), q_pair(g, h))
                per_tile.append(s if kt < min_full else s + jnp.where(kt < n_full, 0.0, NEG_INF))
            scores.append(per_tile)
        return scores

    def trip_update(trip, scores, stats):
        return [online_update(scores[c], [v_tile(kt, g) for kt in trip], *stats[c])
                for c, (g, h) in enumerate(chains)]

    causal = n_full * tk + _iota((tk, tq), 0) <= t0 + _iota((tk, tq), 1)
    causal_bias = _heads_tiled(jnp.where(causal, 0.0, NEG_INF))
    s_diag = [_dot(k_tile(n_full, g), q_sel[g]) for g in groups]
    pending = trip_scores(trips[0]) if trips else None
    first = [online_update([s_diag[g] + causal_bias], [v_tile(n_full, g)],
                           jnp.full((1, n_cols), NEG_INF, F32), jnp.zeros((LANES, n_cols), F32)) for g in groups]
    stats = [tuple(x[:, h * pair_w:(h + 1) * pair_w] for x in first[g]) for g, h in chains]
    for i, trip in enumerate(trips):
        upcoming = trip_scores(trips[i + 1]) if i + 1 < len(trips) else None
        stats = trip_update(trip, pending, stats)
        pending = upcoming

    gate_all = gate_ref[...]
    for g in groups:
        acc = jnp.concatenate([stats[c][1] for c, (cg, _) in enumerate(chains) if cg == g], axis=1)
        o_slc = acc[0:HEAD_DIM] * (1.0 / acc[HEAD_DIM:HEAD_DIM + 1])
        gate_t = jnp.transpose(gate_all[:, grp(g)])
        o_heads = []
        for r in range(NSA_REP):
            cols = slice(r * tq, (r + 1) * tq)
            o_heads.append(gate_t[3 * r:3 * r + 1] * o_cmp[g][:, cols] + gate_t[3 * r + 1:3 * r + 2] * o_slc[:, cols]
                           + gate_t[3 * r + 2:3 * r + 3] * o_win[g][:, cols])
        for pair in range(NSA_REP // 2):
            slab = jnp.transpose(jnp.concatenate(o_heads[2 * pair:2 * pair + 2], axis=0))
            c0 = (g * NSA_REP // 2 + pair) * LANES
            o_ref[:, c0:c0 + LANES] = slab.astype(BF16)


def _nsa_call(qraw, qrot, kc, vct, ks, vst, kw, vwt, gate, *, layer, batch, seq, tq=Q_BLOCK, tk=512):
    assert tq == LANES and tk % tq == 0 and seq % tk == 0 and seq // SLC_LEN <= HEAD_DIM
    nq = seq // tq
    n_cp = seq // CMP_STRIDE
    grp_w = NSA_KV_HEADS * LANES

    def k_spec(rows):
        return pl.BlockSpec((None, rows, grp_w), lambda b, i: (b, 0, 0))

    def vt_spec(rows):
        return pl.BlockSpec((None, NSA_KV_HEADS, rows // LANES, LANES, LANES), lambda b, i: (b, 0, 0, 0, 0))

    ks3, kw3 = ks.reshape(batch, seq, grp_w), kw.reshape(batch, seq, grp_w)
    part = nq // NSA_PARTS
    out = jnp.zeros((batch * seq, NSA_W), BF16)
    for p in range(NSA_PARTS):
        q_lo, last_token = p * part, (p + 1) * part * tq - 1
        n_rank = min(seq // SLC_LEN, -(-(last_token // SLC_LEN + 1) // 8) * 8)
        n_cmp_rows = min(n_cp, -(-(last_token // CMP_STRIDE + 1) // LANES) * LANES)

        def row_spec(width, q_lo=q_lo):
            return pl.BlockSpec((tq, width), lambda b, i: (b * nq + q_lo + i, 0))

        n_keys = -(-(last_token + 1) // tk) * tk
        q_spec = pl.BlockSpec((1, NSA_W // LANES, LANES, LANES), lambda b, i, q_lo=q_lo: (b * nq + q_lo + i, 0, 0, 0))
        in_specs = [q_spec, q_spec, k_spec(n_cmp_rows),
                    vt_spec(n_cmp_rows), k_spec(n_keys), vt_spec(n_keys), k_spec(n_keys), vt_spec(n_keys),
                    row_spec(grp_w)]
        in_specs.append(pl.BlockSpec(memory_space=pl.ANY))
        args = [qraw, qrot, kc, vct, ks3, vst, kw3, vwt, gate, out]
        out = pl.pallas_call(
            functools.partial(_nsa_kernel, seq=seq, tq=tq, tk=tk, q_lo=q_lo, n_qb=part, n_rank=n_rank,
                              n_cmp_rows=n_cmp_rows),
            grid=(batch, part),
            in_specs=in_specs,
            out_specs=pl.BlockSpec((tq, NSA_W), lambda b, i, q_lo=q_lo: (b * nq + q_lo + i, 0)),
            out_shape=jax.ShapeDtypeStruct((batch * seq, NSA_W), BF16),
            input_output_aliases={len(args) - 1: 0},
            scratch_shapes=[pltpu.VMEM((NSA_KV_HEADS, LANES, tq), F32)],
            compiler_params=_params(2),
            name=f"nsa_attn_l{layer}_p{p}",
        )(*args)
    return out


LIN_BLOCK = 2 * CHUNK


def _lin_block_consts():
    r = _iota((LIN_BLOCK, LANES), 0)
    c = _iota((LIN_BLOCK, LANES), 1)
    same_chunk = (r >= CHUNK) == (c >= CHUNK)
    causal = same_chunk & (r >= c)
    head_diag = same_chunk
    first_rows = r < CHUNK
    low_lanes = c < HEAD_DIM
    return causal, head_diag, first_rows, low_lanes


def _head_stats(x, head_diag_mean):
    hi = x.astype(BF16)
    lo = (x - hi.astype(F32)).astype(BF16)
    return _dot(hi, head_diag_mean) + _dot(lo, head_diag_mean)


def _lin_core(chains, consts):
    causal, head_diag, first_rows, low_lanes = consts
    mask2 = jnp.concatenate([causal, causal], axis=0)
    scores, updates = [], []
    for c in chains:
        zero = jnp.zeros_like(c["q_dec"])
        q2 = jnp.concatenate([jnp.where(low_lanes, c["q_dec"], zero), jnp.where(low_lanes, zero, c["q_dec"])], axis=0)
        scores.append(_dot_nt(q2.astype(BF16), c["k_inv"].astype(BF16)))
    for c in chains:
        v_t = jnp.transpose(c["v"]).astype(BF16)
        kd_ab = jnp.concatenate([jnp.where(first_rows, c["k_dec"], 0.0), jnp.where(first_rows, 0.0, c["k_dec"])],
                                axis=1)
        updates.append(_dot(v_t, kd_ab.astype(BF16)))
    intra, states = [], []
    for c, a2 in zip(chains, scores):
        a2 = jnp.where(mask2, a2, 0.0)
        if c["intra_scale"] is not None:
            a2 = a2 * c["intra_scale"]
        o2 = _dot(a2.astype(BF16), c["v"].astype(BF16))
        intra.append(jnp.where(low_lanes, o2[0:LIN_BLOCK], o2[LIN_BLOCK:]))
    out = []
    for c, upd, o_intra in zip(chains, updates, intra):
        st_a = c["st_prev"] * c["dec_a"] + jnp.where(head_diag, upd[:, 0:LANES], 0.0)
        st_b = st_a * c["dec_b"] + jnp.where(head_diag, upd[:, LANES:], 0.0)
        o_ab = _dot_nt(c["q_int"].astype(BF16), jnp.concatenate([c["st_prev"], st_a], axis=0).astype(BF16))
        out.append((o_intra + jnp.where(first_rows, o_ab[:, 0:LANES], o_ab[:, LANES:]), st_b))
    return out


def _gla_kernel(x_ref, lr_ref, a2_ref, ab_ref, ng_ref, o_ref, st_ref, *, n_blocks, batch):
    @pl.when(pl.program_id(0) == 0)
    def _():
        st_ref[...] = jnp.zeros_like(st_ref)

    consts = _lin_block_consts()
    causal, head_diag, first_rows, low_lanes = consts
    ri = _iota((LIN_BLOCK, LANES), 0)
    ci = _iota((LIN_BLOCK, LANES), 1)
    tri_t = jnp.where(((ri >= CHUNK) == (ci >= CHUNK)) & (ri <= ci), 1.0, 0.0).astype(BF16)
    mean_op = jnp.where(head_diag, 1.0 / HEAD_DIM, 0.0).astype(BF16)
    a2h = a2_ref[...].astype(BF16)
    a2l = (a2_ref[...] - a2h.astype(F32)).astype(BF16)

    def block(i):
        rows = pl.ds(i * LIN_BLOCK, LIN_BLOCK)
        lr = jnp.concatenate([lr_ref[bi, rows, :] for bi in range(batch)], axis=0)
        lh = lr.astype(BF16)
        ll = (lr - lh.astype(F32)).astype(BF16)
        z = _dot(lh, a2h) + _dot(lh, a2l) + _dot(ll, a2h) + ab_ref[...]
        log_a = (jnp.minimum(z, 0.0) - jnp.log(1.0 + jnp.exp(-jnp.abs(z)))) * (1.0 / GLA_TAU)
        n_slab = GLA_W // LANES
        la_t = jnp.concatenate([jnp.transpose(log_a[bi * LIN_BLOCK:(bi + 1) * LIN_BLOCK, s * LANES:(s + 1) * LANES])
                                for bi in range(batch) for s in range(n_slab)], axis=0)
        cum_t = _dot3_exact_rhs(la_t, tri_t)
        chains = []
        for bi in range(batch):
            for s in range(n_slab):
                cols = slice(s * LANES, (s + 1) * LANES)
                n = bi * n_slab + s

                def ld(k):
                    return x_ref[bi, rows, k * GLA_W + s * LANES:k * GLA_W + (s + 1) * LANES]

                q, k, v, gg = ld(0), ld(1), ld(2), ld(3)
                b = jnp.transpose(cum_t[n * LANES:(n + 1) * LANES])
                bl_a = b[CHUNK - 1:CHUNK, :]
                bl_b = b[LIN_BLOCK - 1:LIN_BLOCK, :]
                bl = jnp.where(first_rows, bl_a, bl_b)
                q_dec = q * QK_SCALE * jnp.exp(b)
                chains.append(dict(bi=bi, s=s, cols=cols, gate=gg, q_dec=q_dec, q_int=q_dec, k_inv=k * jnp.exp(-b),
                                   k_dec=k * jnp.exp(bl - b), v=v, intra_scale=None, dec_a=jnp.exp(bl_a),
                                   dec_b=jnp.exp(bl_b), st_prev=st_ref[bi, s]))
        results = _lin_core(chains, consts)
        for c, (_, st_new) in zip(chains, results):
            st_ref[c["bi"], c["s"]] = st_new
        ms = _head_stats(jnp.concatenate([o * o for o, _ in results], axis=0), mean_op)
        for n, (c, (o, _)) in enumerate(zip(chains, results)):
            y = (o * lax.rsqrt(ms[n * LIN_BLOCK:(n + 1) * LIN_BLOCK] + NORM_EPS) * ng_ref[:, c["cols"]]
                 * _silu(c["gate"]))
            o_ref[c["bi"], rows, c["cols"]] = y.astype(BF16)

    for i in range(n_blocks):
        block(i)


def _ret_kernel(x_ref, dmat_ref, qf_ref, kf_ref, cd_ref, ng_ref, o_ref, st_ref, *, n_blocks, batch):
    @pl.when(pl.program_id(0) == 0)
    def _():
        st_ref[...] = jnp.zeros_like(st_ref)

    consts = _lin_block_consts()
    causal, head_diag, first_rows, low_lanes = consts
    mean_op = jnp.where(head_diag, 1.0 / HEAD_DIM, 0.0).astype(BF16)

    def block(i):
        rows = pl.ds(i * LIN_BLOCK, LIN_BLOCK)
        chains = []
        for bi in range(batch):
            for s in range(RET_W // LANES):
                cols = slice(s * LANES, (s + 1) * LANES)

                def ld(k):
                    return x_ref[bi, rows, k * RET_W + s * LANES:k * RET_W + (s + 1) * LANES]

                q, k, v, gg = ld(0), ld(1), ld(2), ld(3)
                cd = cd_ref[:, cols]
                chains.append(dict(bi=bi, s=s, cols=cols, gate=gg, q_dec=q, q_int=q * qf_ref[:, cols], k_inv=k,
                                   k_dec=k * kf_ref[:, cols], v=v, intra_scale=dmat_ref[s], dec_a=cd, dec_b=cd,
                                   st_prev=st_ref[bi, s]))
        results = _lin_core(chains, consts)
        for c, (_, st_new) in zip(chains, results):
            st_ref[c["bi"], c["s"]] = st_new
        o_all = jnp.concatenate([o for o, _ in results], axis=0)
        oc_all = o_all - _head_stats(o_all, mean_op)
        var_all = _head_stats(oc_all * oc_all, mean_op)
        y_all = oc_all * lax.rsqrt(var_all + NORM_EPS)
        for n, c in enumerate(chains):
            y = y_all[n * LIN_BLOCK:(n + 1) * LIN_BLOCK] * ng_ref[:, c["cols"]] * _silu(c["gate"])
            o_ref[c["bi"], rows, c["cols"]] = y.astype(BF16)

    for i in range(n_blocks):
        block(i)


def _gla_call(gla, lr, a2_pad, a_bias, norm_g, *, layer, batch, seq, tb=LIN_BLOCK):
    n_slab = GLA_W // LANES

    def lspec(shape):
        return pl.BlockSpec((None,) + shape, lambda t: (layer, 0, 0))

    return pl.pallas_call(
        functools.partial(_gla_kernel, n_blocks=tb // LIN_BLOCK, batch=batch),
        grid=(seq // tb,),
        in_specs=[pl.BlockSpec((batch, tb, 4 * GLA_W), lambda t: (0, t, 0)),
                  pl.BlockSpec((batch, tb, LANES), lambda t: (0, t, 0)),
                  lspec((LANES, GLA_W)), lspec((1, GLA_W)), lspec((1, GLA_W))],
        out_specs=pl.BlockSpec((batch, tb, GLA_W), lambda t: (0, t, 0)),
        out_shape=jax.ShapeDtypeStruct((batch, seq, GLA_W), BF16),
        scratch_shapes=[pltpu.VMEM((batch, n_slab, LANES, LANES), F32)],
        compiler_params=_params(1),
        name=f"gla_l{layer}",
    )(gla.reshape(batch, seq, 4 * GLA_W), lr.reshape(batch, seq, LANES), a2_pad, a_bias, norm_g
      ).reshape(batch * seq, GLA_W)


def _ret_call(ret, dmat, qf, kf, cd, norm_g, *, layer, batch, seq, tb=LIN_BLOCK):
    n_slab = RET_W // LANES
    full = lambda a: pl.BlockSpec(a.shape, lambda t: (0,) * a.ndim)
    return pl.pallas_call(
        functools.partial(_ret_kernel, n_blocks=tb // LIN_BLOCK, batch=batch),
        grid=(seq // tb,),
        in_specs=[pl.BlockSpec((batch, tb, 4 * RET_W), lambda t: (0, t, 0)),
                  full(dmat), full(qf), full(kf), full(cd),
                  pl.BlockSpec((None, 1, RET_W), lambda t: (layer, 0, 0))],
        out_specs=pl.BlockSpec((batch, tb, RET_W), lambda t: (0, t, 0)),
        out_shape=jax.ShapeDtypeStruct((batch, seq, RET_W), BF16),
        scratch_shapes=[pltpu.VMEM((batch, n_slab, LANES, LANES), F32)],
        compiler_params=_params(1),
        name=f"ret_l{layer}",
    )(ret.reshape(batch, seq, 4 * RET_W), dmat, qf, kf, cd, norm_g).reshape(batch * seq, RET_W)


def _layout_proj_weights(w_in, nsa_gate_bias):
    n_layers, d, _ = w_in.shape
    (g_q, g_k, g_v, g_g, g_lr, n_q, n_kc, n_vc, n_ks, n_vs, n_kw, n_vw, n_gate,
     r_q, r_k, r_v, r_g) = jnp.split(w_in, IN_SPLITS, axis=-1)
    zeros = lambda w: jnp.zeros((n_layers, d, w), w_in.dtype)
    def pad_heads(w, n_heads):
        w = w.reshape(n_layers, d, n_heads, HEAD_DIM)
        return jnp.concatenate([w, jnp.zeros_like(w)], axis=-1).reshape(n_layers, d, n_heads * LANES)

    per_grp = 3 * NSA_REP
    gate_cols = []
    bias_cols = []
    for g in range(NSA_KV_HEADS):
        gate_cols += [n_gate[..., g * per_grp:(g + 1) * per_grp], zeros(LANES - per_grp)]
        bias_cols += [nsa_gate_bias[:, g * per_grp:(g + 1) * per_grp],
                      jnp.zeros((n_layers, LANES - per_grp), nsa_gate_bias.dtype)]
    w = jnp.concatenate([g_q, g_k, g_v, g_g, g_lr, zeros(LANES - GLA_LOWRANK), n_q,
                         n_kc, n_vc, pad_heads(n_ks, NSA_KV_HEADS), pad_heads(n_kw, NSA_KV_HEADS), n_vs, n_vw]
                        + gate_cols + [r_q, r_k, r_v, r_g], axis=-1)
    assert w.shape[-1] == PROJ_W
    bias = jnp.concatenate(bias_cols, axis=-1).reshape(n_layers, 1, NSA_KV_HEADS * LANES)
    return w.astype(BF16), bias


def _layout_cmp_weights(pe_k, pe_v, w1_k, w2_k, w1_v, w2_v):
    n_layers = pe_k.shape[0]
    eye = jnp.eye(2, dtype=BF16)
    w1 = jnp.stack([w1_k, w1_v], axis=1).reshape(n_layers, 2, CMP_LEN, HEAD_DIM, CMP_HIDDEN).astype(BF16)
    blocks = [jnp.pad(w1[:, kv], ((0, 0), (0, 0), (0, 0), ((2 * kv + g) * CMP_HIDDEN, (3 - 2 * kv - g) * CMP_HIDDEN)))
              for kv in range(2) for g in range(NSA_KV_HEADS)]
    w1_full = jnp.stack(blocks, axis=2).reshape(n_layers, CMP_LEN * 2 * NSA_KV_W, 4 * CMP_HIDDEN)
    half = CMP_STRIDE * 2 * NSA_KV_W
    w2 = jnp.stack([w2_k, w2_v], axis=1).astype(BF16)
    w2_full = jnp.einsum('zknd,kq,gh->zqhnkgd', w2, eye, eye)
    w2_full = w2_full.reshape(n_layers, 4 * CMP_HIDDEN, 2 * NSA_KV_W)
    w2_k = w2_full[..., :NSA_KV_W].reshape(n_layers, 4 * CMP_HIDDEN, NSA_KV_HEADS, HEAD_DIM)
    w2_k = jnp.concatenate([w2_k, jnp.zeros_like(w2_k)], axis=-1).reshape(n_layers, 4 * CMP_HIDDEN, -1)
    w2_full = jnp.concatenate([w2_k, w2_full[..., NSA_KV_W:]], axis=-1).astype(BF16)
    pe = jnp.stack([pe_k, pe_v], axis=1)
    pe = jnp.broadcast_to(pe.transpose(0, 2, 1, 3)[:, :, :, None, :],
                          (n_layers, CMP_LEN, 2, NSA_KV_HEADS, HEAD_DIM))
    pe = pe.reshape(n_layers, 1, CMP_LEN * 2 * NSA_KV_W)
    return pe[:, :, :half], pe[:, :, half:], w1_full.reshape(n_layers, 2, half, 4 * CMP_HIDDEN), w2_full


def _rotary_tables(seq):
    half = HEAD_DIM // 2
    inv_freq = ROPE_THETA ** (-jnp.arange(half, dtype=F32) / half)
    ang = jnp.arange(seq).astype(F32)[:, None] * inv_freq[None, :]
    cos, sin = jnp.cos(ang), jnp.sin(ang)
    reps = LANES // HEAD_DIM
    return jnp.tile(jnp.concatenate([cos, cos], axis=-1), (1, reps)), \
        jnp.tile(jnp.concatenate([-sin, sin], axis=-1), (1, reps))


def _retention_tables():
    log_gamma = jnp.log1p(-jnp.exp2(-5.0 - jnp.arange(RET_HEADS, dtype=F32)))
    lg_lane = jnp.repeat(log_gamma, HEAD_DIM)[None, :]
    pos = (jnp.arange(LIN_BLOCK) % CHUNK).astype(F32)[:, None]
    qf = jnp.exp(lg_lane * (pos + 1.0))
    kf = jnp.exp(lg_lane * (CHUNK - 1.0 - pos))
    cd = jnp.exp(lg_lane * CHUNK)
    r = jnp.arange(LIN_BLOCK)
    rel = (r[:, None] - r[None, :]).astype(F32)
    ok = ((r[:, None] // CHUNK) == (r[None, :] // CHUNK)) & (rel >= 0)
    dm = jnp.where(ok[None], jnp.exp(log_gamma[:, None, None] * rel[None]), 0.0)
    dmat = dm.reshape(RET_W // LANES, 2 * LIN_BLOCK, LIN_BLOCK)
    return dmat, qf, kf, cd


def kernel(x, c, w_ada, b_ada, norm_g, ffn1_in, ffn1_out, w_in, gla_a2, gla_a_bias, gla_norm_g,
           nsa_pe_k, nsa_pe_v, nsa_w1_k, nsa_w2_k, nsa_w1_v, nsa_w2_v, nsa_gate_bias, ret_norm_g,
           w_out, ffn2_in, ffn2_out, final_norm_g):
    batch, seq, d = x.shape
    n_layers = w_ada.shape[0]
    assert batch <= 8 and seq % 512 == 0 and seq >= WINDOW + Q_BLOCK

    c_pad = jnp.zeros((8, d), F32).at[:batch].set(c)
    mod = _ada_call(c_pad, w_ada, b_ada).reshape(n_layers * 8 * N_ADA, 1, d)
    ng = norm_g.reshape(n_layers * 3, 1, d)

    w_proj, gate_bias = _layout_proj_weights(w_in, nsa_gate_bias)
    pe_top, pe_bot, w1_cmp, w2_cmp = _layout_cmp_weights(nsa_pe_k, nsa_pe_v, nsa_w1_k, nsa_w2_k, nsa_w1_v, nsa_w2_v)
    cosf, sinf = _rotary_tables(seq)
    dmat, qf, kf, cd = _retention_tables()
    a2_pad = jnp.zeros((n_layers, LANES, GLA_W), F32).at[:, :GLA_LOWRANK].set(gla_a2)
    a_bias = gla_a_bias.reshape(n_layers, 1, GLA_W)
    gla_g = jnp.tile(gla_norm_g, (1, GLA_HEADS)).reshape(n_layers, 1, GLA_W)
    ret_g = jnp.tile(ret_norm_g, (1, RET_HEADS)).reshape(n_layers, 1, RET_W)
    f1_in, f1_out = ffn1_in.astype(BF16), ffn1_out.astype(BF16)
    f2_in, f2_out = ffn2_in.astype(BF16), ffn2_out.astype(BF16)
    w_o = w_out.astype(BF16)

    xs = x.reshape(batch * seq, d)
    for l in range(n_layers):
        xs = _ffn_call(xs, ng, mod, f1_in, f1_out, layer=l, sub=0, seq=seq)
        (gla, lr, qraw, qrot, kcvc, ks, vs, kw, vw, gate, ret) = _proj_call(
            xs, ng, mod, w_proj, gate_bias, cosf, sinf, layer=l, seq=seq)
        kc, vc = _cmp_call(kcvc, pe_top, pe_bot, w1_cmp, w2_cmp, layer=l, batch=batch, seq=seq)
        o_gla = _gla_call(gla, lr, a2_pad, a_bias, gla_g, layer=l, batch=batch, seq=seq)
        o_ret = _ret_call(ret, dmat, qf, kf, cd, ret_g, layer=l, batch=batch, seq=seq)
        o_nsa = _nsa_call(qraw, qrot, kc, vc, ks, vs, kw, vw, gate, layer=l, batch=batch, seq=seq)
        final_g = final_norm_g.reshape(1, d) if l == n_layers - 1 else None
        xs = _ffn_call(xs, ng, mod, f2_in, f2_out, layer=l, sub=2, seq=seq, mixer=(o_gla, o_nsa, o_ret, w_o),
                       final_g=final_g)
    return xs.reshape(batch, seq, d)
```

```python
import functools
import math

import numpy as np
import jax
import jax.numpy as jnp
from jax import lax
from jax.experimental import pallas as pl
from jax.experimental.pallas import tpu as pltpu

F32 = jnp.float32
BF16 = jnp.bfloat16

HEAD_DIM = 64
LANES = 128
GLA_HEADS = 4
NSA_HEADS = 8
NSA_KV_HEADS = 2
NSA_REP = NSA_HEADS // NSA_KV_HEADS
RET_HEADS = 4
GLA_LOWRANK = 16
GLA_TAU = 16.0
CHUNK = 64
CMP_LEN = 32
CMP_STRIDE = 16
CMP_HIDDEN = 128
SLC_LEN = 64
SLC_TOPK = 16
WINDOW = 512
Q_BLOCK = 128
ROPE_THETA = 10000.0
FFN_HALF = 0.5
NORM_EPS = 1e-6
NEG_INF = -1e30
FORCED_SCORE = 1e4
N_ADA = 9
QK_SCALE = HEAD_DIM ** -0.5
SLC_SHIFT = int(math.log2(SLC_LEN))
LOG2_E = math.log2(math.e)

GLA_W = GLA_HEADS * HEAD_DIM
NSA_W = NSA_HEADS * HEAD_DIM
NSA_KV_W = NSA_KV_HEADS * HEAD_DIM
RET_W = RET_HEADS * HEAD_DIM
IN_SIZES = (GLA_W, GLA_W, GLA_W, GLA_W, GLA_LOWRANK,
            NSA_W, NSA_KV_W, NSA_KV_W, NSA_KV_W, NSA_KV_W, NSA_KV_W, NSA_KV_W, 3 * NSA_HEADS,
            RET_W, RET_W, RET_W, RET_W)
IN_SPLITS = tuple(int(s) for s in np.cumsum(IN_SIZES)[:-1])

SEG_GLA = 0
SEG_LR = SEG_GLA + 4 * GLA_W
SEG_NQ = SEG_LR + LANES
SEG_KCVC = SEG_NQ + NSA_W
SEG_KS = SEG_KCVC + 2 * NSA_KV_W
SEG_KW = SEG_KS + NSA_KV_HEADS * LANES
SEG_VS = SEG_KW + NSA_KV_HEADS * LANES
SEG_VW = SEG_VS + NSA_KV_W
SEG_GATE = SEG_VW + NSA_KV_W
SEG_RET = SEG_GATE + NSA_KV_HEADS * LANES
PROJ_W = SEG_RET + 4 * RET_W

VMEM_LIMIT_BYTES = 56 * 1024 * 1024

NT_DIMS = (((1,), (1,)), ((), ()))


def _dot(a, b):
    return jnp.dot(a, b, preferred_element_type=F32)


def _dot_nt(a, b):
    return lax.dot_general(a, b, NT_DIMS, preferred_element_type=F32)


def _split3(x):
    hi = x.astype(BF16)
    r1 = x - hi.astype(F32)
    mid = r1.astype(BF16)
    lo = (r1 - mid.astype(F32)).astype(BF16)
    return hi, mid, lo


def _dot3_exact_rhs(x, w_bf16):
    hi, mid, lo = _split3(x)
    return _dot(hi, w_bf16) + _dot(mid, w_bf16) + _dot(lo, w_bf16)


def _dot3_exact_lhs(w_bf16, x):
    hi, mid, lo = _split3(x)
    return _dot(w_bf16, hi) + _dot(w_bf16, mid) + _dot(w_bf16, lo)


def _sigmoid(x):
    return 1.0 / (1.0 + jnp.exp(-x))


def _silu(x):
    return x * _sigmoid(x)


def _iota(shape, dim):
    return lax.broadcasted_iota(jnp.int32, shape, dim)


def _params(n_grid):
    return pltpu.CompilerParams(dimension_semantics=("arbitrary",) * n_grid,
                                vmem_limit_bytes=VMEM_LIMIT_BYTES)


def _norm_mod(x, g, sh, sc):
    ms = jnp.mean(x * x, axis=-1, keepdims=True)
    return x * lax.rsqrt(ms + NORM_EPS) * (g * (1.0 + sc)) + sh


def _ada_kernel(c_ref, w_ref, b_ref, o_ref):
    c = c_ref[...]
    o_ref[...] = _dot(_silu(c).astype(BF16), w_ref[...].astype(BF16)) + b_ref[...]


def _ada_call(c_pad, w_ada, b_ada):
    n_layers, d, n_out = w_ada.shape
    tn = 2304 if n_out % 2304 == 0 else n_out
    return pl.pallas_call(
        _ada_kernel,
        grid=(n_layers, n_out // tn),
        in_specs=[pl.BlockSpec((8, d), lambda l, j: (0, 0)),
                  pl.BlockSpec((None, d, tn), lambda l, j: (l, 0, j)),
                  pl.BlockSpec((None, 1, tn), lambda l, j: (l, 0, j))],
        out_specs=pl.BlockSpec((None, 8, tn), lambda l, j: (l, 0, j)),
        out_shape=jax.ShapeDtypeStruct((n_layers, 8, n_out), F32),
        compiler_params=_params(2),
        name="ada_mod",
    )(c_pad, w_ada, b_ada.reshape(n_layers, 1, n_out))


def _ffn_kernel(x_ref, g_ref, sh_ref, sc_ref, gt_ref, win_ref, wout_ref, *rest, d_ff, chunks, mixer, final):
    rest = list(rest)
    x = x_ref[...]
    if mixer:
        og_ref, on_ref, or_ref, gm_ref, wo_ref = rest[:5]
        rest = rest[5:]
        x = x + gm_ref[...] * (_dot(og_ref[...], wo_ref[0:GLA_W, :])
                               + _dot(on_ref[...], wo_ref[GLA_W:GLA_W + NSA_W, :])
                               + _dot(or_ref[...], wo_ref[GLA_W + NSA_W:, :]))
    if final:
        fg_ref, o_ref, act_ref = rest
    else:
        o_ref, act_ref = rest
    h = _norm_mod(x, g_ref[...], sh_ref[...], sc_ref[...]).astype(BF16)
    for (c0, cw) in chunks:
        gate = _dot(h, win_ref[:, c0:c0 + cw])
        up = _dot(h, win_ref[:, d_ff + c0:d_ff + c0 + cw])
        act_ref[:, c0:c0 + cw] = (_silu(gate) * up).astype(BF16)
    y = _dot(act_ref[...], wout_ref[...])
    xn = x + (FFN_HALF * gt_ref[...]) * y
    if final:
        ms = jnp.mean(xn * xn, axis=-1, keepdims=True)
        xn = xn * lax.rsqrt(ms + NORM_EPS) * fg_ref[...]
    o_ref[...] = xn


def _ffn_call(x, ng, mod, w_in, w_out, *, layer, sub, seq, mixer=None, final_g=None, tm=1024):
    m, d = x.shape
    d_ff = w_out.shape[1]
    tiles_per_seq = seq // tm
    chunks, c0 = [], 0
    while c0 < d_ff:
        cw = min(512, d_ff - c0)
        chunks.append((c0, cw))
        c0 += cw

    def mod_spec(k):
        return pl.BlockSpec((None, 1, d), lambda i: ((layer * 8 + i // tiles_per_seq) * N_ADA + k, 0, 0))

    in_specs = [pl.BlockSpec((tm, d), lambda i: (i, 0)),
                pl.BlockSpec((None, 1, d), lambda i: (layer * 3 + sub, 0, 0)),
                mod_spec(3 * sub), mod_spec(3 * sub + 1), mod_spec(3 * sub + 2),
                pl.BlockSpec((None, d, 2 * d_ff), lambda i: (layer, 0, 0), pipeline_mode=pl.Buffered(1)),
                pl.BlockSpec((None, d_ff, d), lambda i: (layer, 0, 0), pipeline_mode=pl.Buffered(1))]
    args = [x, ng, mod, mod, mod, w_in, w_out]
    if mixer is not None:
        o_gla, o_nsa, o_ret, w_mix = mixer
        in_specs += [pl.BlockSpec((tm, o.shape[1]), lambda i: (i, 0)) for o in (o_gla, o_nsa, o_ret)]
        in_specs += [mod_spec(5), pl.BlockSpec((None,) + w_mix.shape[1:], lambda i: (layer, 0, 0),
                                               pipeline_mode=pl.Buffered(1))]
        args += [o_gla, o_nsa, o_ret, mod, w_mix]
    if final_g is not None:
        in_specs.append(pl.BlockSpec((1, d), lambda i: (0, 0)))
        args.append(final_g)
    return pl.pallas_call(
        functools.partial(_ffn_kernel, d_ff=d_ff, chunks=tuple(chunks), mixer=mixer is not None,
                          final=final_g is not None),
        grid=(m // tm,),
        in_specs=in_specs,
        out_specs=pl.BlockSpec((tm, d), lambda i: (i, 0)),
        out_shape=jax.ShapeDtypeStruct((m, d), F32),
        scratch_shapes=[pltpu.VMEM((tm, d_ff), BF16)],
        compiler_params=_params(1),
        name=f"ffn_l{layer}_s{sub}",
    )(*args)


def _rotate(x, cosf, sinf):
    rows, width = x.shape
    first = (_iota((rows, LANES), 1) & (HEAD_DIM - 1)) < HEAD_DIM // 2
    outs = []
    for j in range(width // LANES):
        xs = x[:, j * LANES:(j + 1) * LANES]
        swapped = jnp.where(first, pltpu.roll(xs, LANES - HEAD_DIM // 2, 1), pltpu.roll(xs, HEAD_DIM // 2, 1))
        outs.append(xs * cosf + swapped * sinf)
    return outs[0] if len(outs) == 1 else jnp.concatenate(outs, axis=1)


def _store_group_transposed(dst_ref, x, fill):
    top = _iota((LANES, LANES), 0) < HEAD_DIM
    for c in range(x.shape[0] // LANES):
        piece = jnp.transpose(x[c * LANES:(c + 1) * LANES])
        dst_ref[0, c] = jnp.where(top, piece, fill).astype(dst_ref.dtype)
        dst_ref[1, c] = jnp.where(top, pltpu.roll(piece, HEAD_DIM, 0), fill).astype(dst_ref.dtype)


def _proj_kernel(x_ref, g_ref, sh_ref, sc_ref, w_ref, gb_ref, cos_ref, sin_ref,
                 gla_ref, lr_ref, qraw_ref, qrot_ref, kcvc_ref, ks_ref, vst_ref, kw_ref, vwt_ref,
                 gate_ref, ret_ref, *, tiles_per_seq):
    h = _norm_mod(x_ref[...], g_ref[...], sh_ref[...], sc_ref[...]).astype(BF16)
    cosf = cos_ref[...]
    sinf = sin_ref[...]
    tm = h.shape[0]

    def seg(c0, width):
        return _dot(h, w_ref[:, c0:c0 + width])

    nq = seg(SEG_NQ, NSA_W)
    for dst_ref, q in ((qraw_ref, nq), (qrot_ref, _rotate(nq, cosf, sinf))):
        q = q * (QK_SCALE * LOG2_E)
        for c in range(tm // LANES):
            for j in range(NSA_W // LANES):
                dst_ref[c, j] = jnp.transpose(q[c * LANES:(c + 1) * LANES, j * LANES:(j + 1) * LANES]).astype(BF16)
    ret_ref[:, 0:RET_W] = _rotate(seg(SEG_RET, RET_W), cosf, sinf)
    ret_ref[:, RET_W:2 * RET_W] = _rotate(seg(SEG_RET + RET_W, RET_W), cosf, sinf) * QK_SCALE
    grp_w = NSA_KV_HEADS * LANES
    pos = (pl.program_id(0) % tiles_per_seq) * tm + _iota((tm, grp_w), 0)
    spare = (_iota((tm, grp_w), 1) & (LANES - 1)) - HEAD_DIM
    blk_onehot = jnp.where(spare == (pos >> SLC_SHIFT), 1.0, 0.0)
    ks_ref[...] = (_rotate(seg(SEG_KS, grp_w), cosf, sinf) + blk_onehot).astype(BF16)
    kw_ref[...] = _rotate(seg(SEG_KW, grp_w), cosf, sinf).astype(BF16)
    _store_group_transposed(vst_ref, seg(SEG_VS, NSA_KV_W), 1.0)
    _store_group_transposed(vwt_ref, seg(SEG_VW, NSA_KV_W), 1.0)
    gate_ref[...] = _sigmoid(seg(SEG_GATE, NSA_KV_HEADS * LANES) + gb_ref[...])
    kcvc_ref[...] = seg(SEG_KCVC, 2 * NSA_KV_W)
    lr_ref[...] = seg(SEG_LR, LANES)
    ret_ref[:, 2 * RET_W:4 * RET_W] = seg(SEG_RET + 2 * RET_W, 2 * RET_W)
    gla_ref[...] = seg(SEG_GLA, 4 * GLA_W)


def _proj_call(x, ng, mod, w_proj, gate_bias, cosf, sinf, *, layer, seq, tm=512):
    m, d = x.shape
    tiles_per_seq = seq // tm

    def mod_spec(k):
        return pl.BlockSpec((None, 1, d), lambda i: ((layer * 8 + i // tiles_per_seq) * N_ADA + k, 0, 0))

    def row_spec(width):
        return pl.BlockSpec((tm, width), lambda i: (i, 0))

    batch = m // seq
    grp_w = NSA_KV_HEADS * LANES

    def vt_spec():
        return pl.BlockSpec((None, NSA_KV_HEADS, tm // LANES, LANES, LANES),
                            lambda i: (i // tiles_per_seq, 0, i % tiles_per_seq, 0, 0))

    vt_shape = jax.ShapeDtypeStruct((batch, NSA_KV_HEADS, seq // LANES, LANES, LANES), BF16)
    n_pairs = NSA_W // LANES
    qt_spec = pl.BlockSpec((tm // LANES, n_pairs, LANES, LANES), lambda i: (i, 0, 0, 0))
    qt_shape = jax.ShapeDtypeStruct((m // LANES, n_pairs, LANES, LANES), BF16)
    out_widths = [(4 * GLA_W, F32), (LANES, F32), "qt", "qt",
                  (2 * NSA_KV_W, F32), (grp_w, BF16), None, (grp_w, BF16), None,
                  (NSA_KV_HEADS * LANES, F32), (4 * RET_W, F32)]

    def out_spec(ow):
        return vt_spec() if ow is None else qt_spec if ow == "qt" else row_spec(ow[0])

    def out_struct(ow):
        return vt_shape if ow is None else qt_shape if ow == "qt" else jax.ShapeDtypeStruct((m, ow[0]), ow[1])

    return pl.pallas_call(
        functools.partial(_proj_kernel, tiles_per_seq=tiles_per_seq),
        grid=(m // tm,),
        in_specs=[row_spec(d),
                  pl.BlockSpec((None, 1, d), lambda i: (layer * 3 + 1, 0, 0)),
                  mod_spec(3), mod_spec(4),
                  pl.BlockSpec((None, d, PROJ_W), lambda i: (layer, 0, 0), pipeline_mode=pl.Buffered(1)),
                  pl.BlockSpec((None, 1, NSA_KV_HEADS * LANES), lambda i: (layer, 0, 0)),
                  pl.BlockSpec((tm, LANES), lambda i: (i % tiles_per_seq, 0)),
                  pl.BlockSpec((tm, LANES), lambda i: (i % tiles_per_seq, 0))],
        out_specs=[out_spec(ow) for ow in out_widths],
        out_shape=[out_struct(ow) for ow in out_widths],
        compiler_params=_params(1),
        name=f"proj_l{layer}",
    )(x, ng, mod, mod, w_proj, gate_bias, cosf, sinf)


def _cmp_kernel(xk_ref, xv_ref, pet_ref, peb_ref, wt_ref, wb_ref, w2_ref, kc_ref, vct_ref):
    n_rows = xk_ref.shape[0] // CMP_STRIDE
    x = jnp.concatenate([ref[pl.ds(l, n_rows, stride=CMP_STRIDE), :]
                         for l in range(CMP_STRIDE) for ref in (xk_ref, xv_ref)], axis=1)
    a = _dot((x + pet_ref[...]).astype(BF16), wt_ref[...])
    b = _dot((x + peb_ref[...]).astype(BF16), wb_ref[...])
    hid = a + pltpu.roll(b, n_rows - 1, 0)
    out = _dot(_silu(hid).astype(BF16), w2_ref[...])
    out = jnp.where(_iota(out.shape, 0) < n_rows - 1, out, 0.0)
    grp_w = NSA_KV_HEADS * LANES
    kc_ref[...] = out[:, 0:grp_w].astype(BF16)
    _store_group_transposed(vct_ref, out[:, grp_w:grp_w + NSA_KV_W], 0.0)


def _cmp_call(kcvc, pe_top, pe_bot, w1_halves, w2, *, layer, batch, seq):
    rows = seq // CMP_STRIDE
    width = CMP_STRIDE * 2 * NSA_KV_W
    x = kcvc.reshape(batch, seq, 2 * NSA_KV_W)
    hid_w = 4 * CMP_HIDDEN
    grp_w = NSA_KV_HEADS * LANES

    def wspec(shape):
        return pl.BlockSpec((None,) + shape, lambda b: (layer, 0, 0))

    return pl.pallas_call(
        _cmp_kernel,
        grid=(batch,),
        in_specs=[pl.BlockSpec((None, seq, NSA_KV_W), lambda b: (b, 0, 0)),
                  pl.BlockSpec((None, seq, NSA_KV_W), lambda b: (b, 0, 1)),
                  wspec((1, width)), wspec((1, width)),
                  pl.BlockSpec((None, None, width, hid_w), lambda b: (layer, 0, 0, 0)),
                  pl.BlockSpec((None, None, width, hid_w), lambda b: (layer, 1, 0, 0)),
                  wspec((hid_w, grp_w + NSA_KV_W))],
        out_specs=[pl.BlockSpec((None, rows, grp_w), lambda b: (b, 0, 0)),
                   pl.BlockSpec((None, NSA_KV_HEADS, rows // LANES, LANES, LANES), lambda b: (b, 0, 0, 0, 0))],
        out_shape=[jax.ShapeDtypeStruct((batch, rows, grp_w), BF16),
                   jax.ShapeDtypeStruct((batch, NSA_KV_HEADS, rows // LANES, LANES, LANES), BF16)],
        compiler_params=_params(1),
        name=f"nsa_compress_l{layer}",
    )(x, x, pe_top, pe_bot, w1_halves, w1_halves, w2)


def _heads_tiled(x):
    return jnp.concatenate([x] * NSA_REP, axis=1)


SLC_UNROLL = 4
NSA_PARTS = 8

def _nsa_compressed(s_c, vc_t, t0, tq, seq):
    n_cp = s_c.shape[0]
    n_cmp = (seq - CMP_LEN) // CMP_STRIDE + 1
    nn = _iota((n_cp, tq), 0)
    tt = t0 + _iota((n_cp, tq), 1)
    bias_c = jnp.where((nn * CMP_STRIDE + (CMP_LEN - 1) <= tt) & (nn < n_cmp), 0.0, NEG_INF)
    has_cmp = t0 + _iota((1, tq), 1) >= CMP_LEN - 1
    p_heads = []
    for r in range(NSA_REP):
        s = s_c[:, r * tq:(r + 1) * tq] + bias_c
        e = jnp.exp2(s - jnp.max(s, axis=0, keepdims=True))
        p_heads.append(e * jnp.where(has_cmp, 1.0 / jnp.sum(e, axis=0, keepdims=True), 0.0))
    o_cmp = _dot(vc_t, jnp.concatenate(p_heads, axis=1).astype(BF16))
    p_sum = p_heads[0]
    for r in range(1, NSA_REP):
        p_sum = p_sum + p_heads[r]
    return o_cmp, p_sum


def _nsa_selection_rows(p_sum, imp_ref, t0, tq, seq, n_rank):
    n_cp = p_sum.shape[0]
    n_cmp = (seq - CMP_LEN) // CMP_STRIDE + 1
    n_slc = seq // SLC_LEN
    n_sel = min(SLC_TOPK, n_slc)
    sb = _iota((LANES, n_cp), 0)
    cb = _iota((LANES, n_cp), 1)
    overlap_t = ((cb * CMP_STRIDE < sb * SLC_LEN + SLC_LEN) & (cb * CMP_STRIDE + CMP_LEN > sb * SLC_LEN)
                 & (sb < n_slc) & (cb < n_cmp))
    imp_t = _dot3_exact_lhs(jnp.where(overlap_t, 1.0, 0.0).astype(BF16), p_sum)
    blk = _iota((LANES, tq), 0)
    cur = (t0 + _iota((LANES, tq), 1)) >> SLC_SHIFT
    forced = (blk == 0) | (blk == cur) | (blk == cur - 1)
    imp_ref[...] = jnp.where(forced, FORCED_SCORE, jnp.where(blk <= cur, imp_t, -1.0))

    def ranked_rows(n_rows):
        if n_rows <= n_sel:
            return jnp.zeros((HEAD_DIM, tq), F32)
        n_grp = n_rows // 8
        vals = [imp_ref[8 * j:8 * (j + 1), :] for j in range(n_grp)]
        ranks = [jnp.zeros((8, tq), F32) for _ in range(n_grp)]
        sub = _iota((8, tq), 0)
        for i in range(n_rows):
            row = jnp.broadcast_to(imp_ref[i:i + 1, :], (8, tq))
            for j in range(n_grp):
                if 8 * j > i:
                    beats = jnp.where(row >= vals[j], 1.0, 0.0)
                elif 8 * j + 7 <= i:
                    beats = jnp.where(row > vals[j], 1.0, 0.0)
                else:
                    beats = jnp.where(sub + 8 * j > i, jnp.where(row >= vals[j], 1.0, 0.0),
                                      jnp.where(row > vals[j], 1.0, 0.0))
                ranks[j] = ranks[j] + beats
        return jnp.concatenate([jnp.where(rk < n_sel, 0.0, NEG_INF) for rk in ranks]
                               + [jnp.zeros((HEAD_DIM - n_rows, tq), F32)] * (n_rows < HEAD_DIM), axis=0)

    return ranked_rows(n_rank)


def _window_scores(q_rot, kw_ref, g, qb, tq, may_precede_start):
    n_wb = WINDOW // tq + 1
    ki = _iota((tq, tq), 0)
    ci = _iota((tq, tq), 1)
    blocks = []
    for j in range(n_wb):
        kb = qb - (n_wb - 1) + j
        kb_read = jnp.maximum(kb, 0) if may_precede_start else kb
        s = _dot(kw_ref[pl.ds(pl.multiple_of(kb_read * tq, tq), tq), g * LANES:(g + 1) * LANES], q_rot)
        if j == 0:
            s = s + _heads_tiled(jnp.where(ki > ci, 0.0, NEG_INF))
        if j == n_wb - 1:
            s = s + _heads_tiled(jnp.where(ki <= ci, 0.0, NEG_INF))
        elif may_precede_start:
            s = s + jnp.where(kb >= 0, 0.0, NEG_INF)
        blocks.append((s, kb_read))
    return blocks


def _window_finish(blocks, vwt_ref, g, tq):
    m_w = blocks[0][0].max(axis=0, keepdims=True)
    for s, _ in blocks[1:]:
        m_w = jnp.maximum(m_w, s.max(axis=0, keepdims=True))
    o_aug = jnp.zeros((LANES, NSA_REP * tq), F32)
    for s, kb_read in blocks:
        o_aug = o_aug + _dot(vwt_ref[g, kb_read], jnp.exp2(s - m_w).astype(BF16))
    return o_aug[0:HEAD_DIM] * (1.0 / o_aug[HEAD_DIM:HEAD_DIM + 1])


def _nsa_kernel(qraw_ref, qrot_ref, kc_ref, vct_ref, ks_ref, vst_ref, kw_ref, vwt_ref, gate_ref, prev_ref,
                o_ref, imp_ref, *, seq, tq, tk, q_lo, n_qb, n_rank, n_cmp_rows):
    del prev_ref
    qb = pl.program_id(1) + q_lo
    t0 = qb * tq
    groups = range(NSA_KV_HEADS)
    n_cols = NSA_REP * tq

    def grp(g):
        return slice(g * LANES, (g + 1) * LANES)

    def q_heads(ref, g):
        return jnp.concatenate([ref[0, (g * NSA_REP + r) // 2, (r % 2) * HEAD_DIM:(r % 2 + 1) * HEAD_DIM, :]
                                for r in range(NSA_REP)], axis=1)

    def q_aug(q_t, extra_rows):
        return jnp.concatenate([q_t, extra_rows.astype(BF16)], axis=0)

    no_extra = jnp.zeros((HEAD_DIM, n_cols), F32)
    q_rot_t = [q_heads(qrot_ref, g) for g in groups]
    s_cmp = [_dot(kc_ref[0:n_cmp_rows, grp(g)], q_aug(q_heads(qraw_ref, g), no_extra)) for g in groups]
    win_blocks = [_window_scores(q_aug(q_rot_t[g], no_extra), kw_ref, g, qb, tq, q_lo * tq < WINDOW)
                  for g in groups]
    o_cmp, q_sel = [], []
    for g in groups:
        vc_t = jnp.concatenate([vct_ref[g, c] for c in range(n_cmp_rows // LANES)], axis=1)
        o_c, p_sum = _nsa_compressed(s_cmp[g], vc_t, t0, tq, seq)
        o_cmp.append(o_c[0:HEAD_DIM])
        sel_neg = _heads_tiled(_nsa_selection_rows(p_sum, imp_ref.at[g], t0, tq, seq, n_rank))
        q_sel.append(q_aug(q_rot_t[g], sel_neg))
    o_win = [_window_finish(win_blocks[g], vwt_ref, g, tq) for g in groups]

    blocks_per_tile = tk // LANES
    n_full = t0 // tk
    min_full = (q_lo * tq) // tk
    max_full = ((q_lo + n_qb - 1) * tq) // tk

    def k_tile(kt, g):
        if isinstance(kt, int):
            return ks_ref[kt * tk:(kt + 1) * tk, grp(g)]
        return ks_ref[pl.ds(pl.multiple_of(kt * tk, tk), tk), grp(g)]

    def v_tile(kt, g):
        return jnp.concatenate([vst_ref[g, kt * blocks_per_tile + c] for c in range(blocks_per_tile)], axis=1)

    def online_update(scores, values, m_i, acc):
        m_new = m_i
        for s in scores:
            m_new = jnp.maximum(m_new, s.max(axis=0, keepdims=True))
        p = jnp.concatenate([jnp.exp2(s - m_new).astype(BF16) for s in scores], axis=0)
        return m_new, jnp.exp2(m_i - m_new) * acc + _dot(jnp.concatenate(values, axis=1), p)

    pair_w = 2 * tq
    chains = [(g, h) for g in groups for h in range(NSA_REP // 2)]

    def q_pair(g, h):
        return q_sel[g][:, h * pair_w:(h + 1) * pair_w]

    tiles = list(range(max_full))
    trips = [tiles[i:i + SLC_UNROLL] for i in range(0, len(tiles), SLC_UNROLL)]

    def trip_scores(trip):
        scores = []
        for g, h in chains:
            per_tile = []
            for kt in trip:
                s = _dot(k_tile(kt, g), q_pair(g, h))
                per_tile.append(s if kt < min_full else s + jnp.where(kt < n_full, 0.0, NEG_INF))
            scores.append(per_tile)
        return scores

    def trip_update(trip, scores, stats):
        return [online_update(scores[c], [v_tile(kt, g) for kt in trip], *stats[c])
                for c, (g, h) in enumerate(chains)]

    causal = n_full * tk + _iota((tk, tq), 0) <= t0 + _iota((tk, tq), 1)
    causal_bias = _heads_tiled(jnp.where(causal, 0.0, NEG_INF))
    s_diag = [_dot(k_tile(n_full, g), q_sel[g]) for g in groups]
    pending = trip_scores(trips[0]) if trips else None
    first = [online_update([s_diag[g] + causal_bias], [v_tile(n_full, g)],
                           jnp.full((1, n_cols), NEG_INF, F32), jnp.zeros((LANES, n_cols), F32)) for g in groups]
    stats = [tuple(x[:, h * pair_w:(h + 1) * pair_w] for x in first[g]) for g, h in chains]
    for i, trip in enumerate(trips):
        upcoming = trip_scores(trips[i + 1]) if i + 1 < len(trips) else None
        stats = trip_update(trip, pending, stats)
        pending = upcoming

    gate_all = gate_ref[...]
    for g in groups:
        acc = jnp.concatenate([stats[c][1] for c, (cg, _) in enumerate(chains) if cg == g], axis=1)
        o_slc = acc[0:HEAD_DIM] * (1.0 / acc[HEAD_DIM:HEAD_DIM + 1])
        gate_t = jnp.transpose(gate_all[:, grp(g)])
        o_heads = []
        for r in range(NSA_REP):
            cols = slice(r * tq, (r + 1) * tq)
            o_heads.append(gate_t[3 * r:3 * r + 1] * o_cmp[g][:, cols] + gate_t[3 * r + 1:3 * r + 2] * o_slc[:, cols]
                           + gate_t[3 * r + 2:3 * r + 3] * o_win[g][:, cols])
        for pair in range(NSA_REP // 2):
            slab = jnp.transpose(jnp.concatenate(o_heads[2 * pair:2 * pair + 2], axis=0))
            c0 = (g * NSA_REP // 2 + pair) * LANES
            o_ref[:, c0:c0 + LANES] = slab.astype(BF16)


def _nsa_call(qraw, qrot, kc, vct, ks, vst, kw, vwt, gate, *, layer, batch, seq, tq=Q_BLOCK, tk=256):
    assert tq == LANES and tk % tq == 0 and seq % tk == 0 and seq // SLC_LEN <= HEAD_DIM
    nq = seq // tq
    n_cp = seq // CMP_STRIDE
    grp_w = NSA_KV_HEADS * LANES

    def k_spec(rows):
        return pl.BlockSpec((None, rows, grp_w), lambda b, i: (b, 0, 0))

    def vt_spec(rows):
        return pl.BlockSpec((None, NSA_KV_HEADS, rows // LANES, LANES, LANES), lambda b, i: (b, 0, 0, 0, 0))

    ks3, kw3 = ks.reshape(batch, seq, grp_w), kw.reshape(batch, seq, grp_w)
    part = nq // NSA_PARTS
    out = jnp.zeros((batch * seq, NSA_W), BF16)
    for p in range(NSA_PARTS):
        q_lo, last_token = p * part, (p + 1) * part * tq - 1
        n_rank = min(seq // SLC_LEN, -(-(last_token // SLC_LEN + 1) // 8) * 8)
        n_cmp_rows = min(n_cp, -(-(last_token // CMP_STRIDE + 1) // LANES) * LANES)

        def row_spec(width, q_lo=q_lo):
            return pl.BlockSpec((tq, width), lambda b, i: (b * nq + q_lo + i, 0))

        n_keys = -(-(last_token + 1) // tk) * tk
        q_spec = pl.BlockSpec((1, NSA_W // LANES, LANES, LANES), lambda b, i, q_lo=q_lo: (b * nq + q_lo + i, 0, 0, 0))
        in_specs = [q_spec, q_spec, k_spec(n_cmp_rows),
                    vt_spec(n_cmp_rows), k_spec(n_keys), vt_spec(n_keys), k_spec(n_keys), vt_spec(n_keys),
                    row_spec(grp_w)]
        in_specs.append(pl.BlockSpec(memory_space=pl.ANY))
        args = [qraw, qrot, kc, vct, ks3, vst, kw3, vwt, gate, out]
        out = pl.pallas_call(
            functools.partial(_nsa_kernel, seq=seq, tq=tq, tk=tk, q_lo=q_lo, n_qb=part, n_rank=n_rank,
                              n_cmp_rows=n_cmp_rows),
            grid=(batch, part),
            in_specs=in_specs,
            out_specs=pl.BlockSpec((tq, NSA_W), lambda b, i, q_lo=q_lo: (b * nq + q_lo + i, 0)),
            out_shape=jax.ShapeDtypeStruct((batch * seq, NSA_W), BF16),
            input_output_aliases={len(args) - 1: 0},
            scratch_shapes=[pltpu.VMEM((NSA_KV_HEADS, LANES, tq), F32)],
            compiler_params=_params(2),
            name=f"nsa_attn_l{layer}_p{p}",
        )(*args)
    return out


LIN_BLOCK = 2 * CHUNK


def _lin_block_consts():
    r = _iota((LIN_BLOCK, LANES), 0)
    c = _iota((LIN_BLOCK, LANES), 1)
    same_chunk = (r >= CHUNK) == (c >= CHUNK)
    causal = same_chunk & (r >= c)
    head_diag = same_chunk
    first_rows = r < CHUNK
    low_lanes = c < HEAD_DIM
    return causal, head_diag, first_rows, low_lanes


def _head_stats(x, head_diag_mean):
    hi = x.astype(BF16)
    lo = (x - hi.astype(F32)).astype(BF16)
    return _dot(hi, head_diag_mean) + _dot(lo, head_diag_mean)


def _lin_core(chains, consts):
    causal, head_diag, first_rows, low_lanes = consts
    mask2 = jnp.concatenate([causal, causal], axis=0)
    scores, updates = [], []
    for c in chains:
        zero = jnp.zeros_like(c["q_dec"])
        q2 = jnp.concatenate([jnp.where(low_lanes, c["q_dec"], zero), jnp.where(low_lanes, zero, c["q_dec"])], axis=0)
        scores.append(_dot_nt(q2.astype(BF16), c["k_inv"].astype(BF16)))
    for c in chains:
        v_t = jnp.transpose(c["v"]).astype(BF16)
        kd_ab = jnp.concatenate([jnp.where(first_rows, c["k_dec"], 0.0), jnp.where(first_rows, 0.0, c["k_dec"])],
                                axis=1)
        updates.append(_dot(v_t, kd_ab.astype(BF16)))
    intra, states = [], []
    for c, a2 in zip(chains, scores):
        a2 = jnp.where(mask2, a2, 0.0)
        if c["intra_scale"] is not None:
            a2 = a2 * c["intra_scale"]
        o2 = _dot(a2.astype(BF16), c["v"].astype(BF16))
        intra.append(jnp.where(low_lanes, o2[0:LIN_BLOCK], o2[LIN_BLOCK:]))
    out = []
    for c, upd, o_intra in zip(chains, updates, intra):
        st_a = c["st_prev"] * c["dec_a"] + jnp.where(head_diag, upd[:, 0:LANES], 0.0)
        st_b = st_a * c["dec_b"] + jnp.where(head_diag, upd[:, LANES:], 0.0)
        o_ab = _dot_nt(c["q_int"].astype(BF16), jnp.concatenate([c["st_prev"], st_a], axis=0).astype(BF16))
        out.append((o_intra + jnp.where(first_rows, o_ab[:, 0:LANES], o_ab[:, LANES:]), st_b))
    return out


def _gla_kernel(x_ref, lr_ref, a2_ref, ab_ref, ng_ref, o_ref, st_ref, *, n_blocks, batch):
    @pl.when(pl.program_id(0) == 0)
    def _():
        st_ref[...] = jnp.zeros_like(st_ref)

    consts = _lin_block_consts()
    causal, head_diag, first_rows, low_lanes = consts
    ri = _iota((LIN_BLOCK, LANES), 0)
    ci = _iota((LIN_BLOCK, LANES), 1)
    tri_t = jnp.where(((ri >= CHUNK) == (ci >= CHUNK)) & (ri <= ci), 1.0, 0.0).astype(BF16)
    mean_op = jnp.where(head_diag, 1.0 / HEAD_DIM, 0.0).astype(BF16)
    a2h = a2_ref[...].astype(BF16)
    a2l = (a2_ref[...] - a2h.astype(F32)).astype(BF16)

    def block(i):
        rows = pl.ds(i * LIN_BLOCK, LIN_BLOCK)
        lr = jnp.concatenate([lr_ref[bi, rows, :] for bi in range(batch)], axis=0)
        lh = lr.astype(BF16)
        ll = (lr - lh.astype(F32)).astype(BF16)
        z = _dot(lh, a2h) + _dot(lh, a2l) + _dot(ll, a2h) + ab_ref[...]
        log_a = (jnp.minimum(z, 0.0) - jnp.log(1.0 + jnp.exp(-jnp.abs(z)))) * (1.0 / GLA_TAU)
        n_slab = GLA_W // LANES
        la_t = jnp.concatenate([jnp.transpose(log_a[bi * LIN_BLOCK:(bi + 1) * LIN_BLOCK, s * LANES:(s + 1) * LANES])
                                for bi in range(batch) for s in range(n_slab)], axis=0)
        cum_t = _dot3_exact_rhs(la_t, tri_t)
        chains = []
        for bi in range(batch):
            for s in range(n_slab):
                cols = slice(s * LANES, (s + 1) * LANES)
                n = bi * n_slab + s

                def ld(k):
                    return x_ref[bi, rows, k * GLA_W + s * LANES:k * GLA_W + (s + 1) * LANES]

                q, k, v, gg = ld(0), ld(1), ld(2), ld(3)
                b = jnp.transpose(cum_t[n * LANES:(n + 1) * LANES])
                bl_a = b[CHUNK - 1:CHUNK, :]
                bl_b = b[LIN_BLOCK - 1:LIN_BLOCK, :]
                bl = jnp.where(first_rows, bl_a, bl_b)
                q_dec = q * QK_SCALE * jnp.exp(b)
                chains.append(dict(bi=bi, s=s, cols=cols, gate=gg, q_dec=q_dec, q_int=q_dec, k_inv=k * jnp.exp(-b),
                                   k_dec=k * jnp.exp(bl - b), v=v, intra_scale=None, dec_a=jnp.exp(bl_a),
                                   dec_b=jnp.exp(bl_b), st_prev=st_ref[bi, s]))
        results = _lin_core(chains, consts)
        for c, (_, st_new) in zip(chains, results):
            st_ref[c["bi"], c["s"]] = st_new
        ms = _head_stats(jnp.concatenate([o * o for o, _ in results], axis=0), mean_op)
        for n, (c, (o, _)) in enumerate(zip(chains, results)):
            y = (o * lax.rsqrt(ms[n * LIN_BLOCK:(n + 1) * LIN_BLOCK] + NORM_EPS) * ng_ref[:, c["cols"]]
                 * _silu(c["gate"]))
            o_ref[c["bi"], rows, c["cols"]] = y.astype(BF16)

    for i in range(n_blocks):
        block(i)


def _ret_kernel(x_ref, dmat_ref, qf_ref, kf_ref, cd_ref, ng_ref, o_ref, st_ref, *, n_blocks, batch):
    @pl.when(pl.program_id(0) == 0)
    def _():
        st_ref[...] = jnp.zeros_like(st_ref)

    consts = _lin_block_consts()
    causal, head_diag, first_rows, low_lanes = consts
    mean_op = jnp.where(head_diag, 1.0 / HEAD_DIM, 0.0).astype(BF16)

    def block(i):
        rows = pl.ds(i * LIN_BLOCK, LIN_BLOCK)
        chains = []
        for bi in range(batch):
            for s in range(RET_W // LANES):
                cols = slice(s * LANES, (s + 1) * LANES)

                def ld(k):
                    return x_ref[bi, rows, k * RET_W + s * LANES:k * RET_W + (s + 1) * LANES]

                q, k, v, gg = ld(0), ld(1), ld(2), ld(3)
                cd = cd_ref[:, cols]
                chains.append(dict(bi=bi, s=s, cols=cols, gate=gg, q_dec=q, q_int=q * qf_ref[:, cols], k_inv=k,
                                   k_dec=k * kf_ref[:, cols], v=v, intra_scale=dmat_ref[s], dec_a=cd, dec_b=cd,
                                   st_prev=st_ref[bi, s]))
        results = _lin_core(chains, consts)
        for c, (_, st_new) in zip(chains, results):
            st_ref[c["bi"], c["s"]] = st_new
        o_all = jnp.concatenate([o for o, _ in results], axis=0)
        oc_all = o_all - _head_stats(o_all, mean_op)
        var_all = _head_stats(oc_all * oc_all, mean_op)
        y_all = oc_all * lax.rsqrt(var_all + NORM_EPS)
        for n, c in enumerate(chains):
            y = y_all[n * LIN_BLOCK:(n + 1) * LIN_BLOCK] * ng_ref[:, c["cols"]] * _silu(c["gate"])
            o_ref[c["bi"], rows, c["cols"]] = y.astype(BF16)

    for i in range(n_blocks):
        block(i)


def _gla_call(gla, lr, a2_pad, a_bias, norm_g, *, layer, batch, seq, tb=LIN_BLOCK):
    n_slab = GLA_W // LANES

    def lspec(shape):
        return pl.BlockSpec((None,) + shape, lambda t: (layer, 0, 0))

    return pl.pallas_call(
        functools.partial(_gla_kernel, n_blocks=tb // LIN_BLOCK, batch=batch),
        grid=(seq // tb,),
        in_specs=[pl.BlockSpec((batch, tb, 4 * GLA_W), lambda t: (0, t, 0)),
                  pl.BlockSpec((batch, tb, LANES), lambda t: (0, t, 0)),
                  lspec((LANES, GLA_W)), lspec((1, GLA_W)), lspec((1, GLA_W))],
        out_specs=pl.BlockSpec((batch, tb, GLA_W), lambda t: (0, t, 0)),
        out_shape=jax.ShapeDtypeStruct((batch, seq, GLA_W), BF16),
        scratch_shapes=[pltpu.VMEM((batch, n_slab, LANES, LANES), F32)],
        compiler_params=_params(1),
        name=f"gla_l{layer}",
    )(gla.reshape(batch, seq, 4 * GLA_W), lr.reshape(batch, seq, LANES), a2_pad, a_bias, norm_g
      ).reshape(batch * seq, GLA_W)


def _ret_call(ret, dmat, qf, kf, cd, norm_g, *, layer, batch, seq, tb=LIN_BLOCK):
    n_slab = RET_W // LANES
    full = lambda a: pl.BlockSpec(a.shape, lambda t: (0,) * a.ndim)
    return pl.pallas_call(
        functools.partial(_ret_kernel, n_blocks=tb // LIN_BLOCK, batch=batch),
        grid=(seq // tb,),
        in_specs=[pl.BlockSpec((batch, tb, 4 * RET_W), lambda t: (0, t, 0)),
                  full(dmat), full(qf), full(kf), full(cd),
                  pl.BlockSpec((None, 1, RET_W), lambda t: (layer, 0, 0))],
        out_specs=pl.BlockSpec((batch, tb, RET_W), lambda t: (0, t, 0)),
        out_shape=jax.ShapeDtypeStruct((batch, seq, RET_W), BF16),
        scratch_shapes=[pltpu.VMEM((batch, n_slab, LANES, LANES), F32)],
        compiler_params=_params(1),
        name=f"ret_l{layer}",
    )(ret.reshape(batch, seq, 4 * RET_W), dmat, qf, kf, cd, norm_g).reshape(batch * seq, RET_W)


def _layout_proj_weights(w_in, nsa_gate_bias):
    n_layers, d, _ = w_in.shape
    (g_q, g_k, g_v, g_g, g_lr, n_q, n_kc, n_vc, n_ks, n_vs, n_kw, n_vw, n_gate,
     r_q, r_k, r_v, r_g) = jnp.split(w_in, IN_SPLITS, axis=-1)
    zeros = lambda w: jnp.zeros((n_layers, d, w), w_in.dtype)
    def pad_heads(w, n_heads):
        w = w.reshape(n_layers, d, n_heads, HEAD_DIM)
        return jnp.concatenate([w, jnp.zeros_like(w)], axis=-1).reshape(n_layers, d, n_heads * LANES)

    per_grp = 3 * NSA_REP
    gate_cols = []
    bias_cols = []
    for g in range(NSA_KV_HEADS):
        gate_cols += [n_gate[..., g * per_grp:(g + 1) * per_grp], zeros(LANES - per_grp)]
        bias_cols += [nsa_gate_bias[:, g * per_grp:(g + 1) * per_grp],
                      jnp.zeros((n_layers, LANES - per_grp), nsa_gate_bias.dtype)]
    w = jnp.concatenate([g_q, g_k, g_v, g_g, g_lr, zeros(LANES - GLA_LOWRANK), n_q,
                         n_kc, n_vc, pad_heads(n_ks, NSA_KV_HEADS), pad_heads(n_kw, NSA_KV_HEADS), n_vs, n_vw]
                        + gate_cols + [r_q, r_k, r_v, r_g], axis=-1)
    assert w.shape[-1] == PROJ_W
    bias = jnp.concatenate(bias_cols, axis=-1).reshape(n_layers, 1, NSA_KV_HEADS * LANES)
    return w.astype(BF16), bias


def _layout_cmp_weights(pe_k, pe_v, w1_k, w2_k, w1_v, w2_v):
    n_layers = pe_k.shape[0]
    eye = jnp.eye(2, dtype=BF16)
    w1 = jnp.stack([w1_k, w1_v], axis=1).reshape(n_layers, 2, CMP_LEN, HEAD_DIM, CMP_HIDDEN).astype(BF16)
    blocks = [jnp.pad(w1[:, kv], ((0, 0), (0, 0), (0, 0), ((2 * kv + g) * CMP_HIDDEN, (3 - 2 * kv - g) * CMP_HIDDEN)))
              for kv in range(2) for g in range(NSA_KV_HEADS)]
    w1_full = jnp.stack(blocks, axis=2).reshape(n_layers, CMP_LEN * 2 * NSA_KV_W, 4 * CMP_HIDDEN)
    half = CMP_STRIDE * 2 * NSA_KV_W
    w2 = jnp.stack([w2_k, w2_v], axis=1).astype(BF16)
    w2_full = jnp.einsum('zknd,kq,gh->zqhnkgd', w2, eye, eye)
    w2_full = w2_full.reshape(n_layers, 4 * CMP_HIDDEN, 2 * NSA_KV_W)
    w2_k = w2_full[..., :NSA_KV_W].reshape(n_layers, 4 * CMP_HIDDEN, NSA_KV_HEADS, HEAD_DIM)
    w2_k = jnp.concatenate([w2_k, jnp.zeros_like(w2_k)], axis=-1).reshape(n_layers, 4 * CMP_HIDDEN, -1)
    w2_full = jnp.concatenate([w2_k, w2_full[..., NSA_KV_W:]], axis=-1).astype(BF16)
    pe = jnp.stack([pe_k, pe_v], axis=1)
    pe = jnp.broadcast_to(pe.transpose(0, 2, 1, 3)[:, :, :, None, :],
                          (n_layers, CMP_LEN, 2, NSA_KV_HEADS, HEAD_DIM))
    pe = pe.reshape(n_layers, 1, CMP_LEN * 2 * NSA_KV_W)
    return pe[:, :, :half], pe[:, :, half:], w1_full.reshape(n_layers, 2, half, 4 * CMP_HIDDEN), w2_full


def _rotary_tables(seq):
    half = HEAD_DIM // 2
    inv_freq = ROPE_THETA ** (-jnp.arange(half, dtype=F32) / half)
    ang = jnp.arange(seq).astype(F32)[:, None] * inv_freq[None, :]
    cos, sin = jnp.cos(ang), jnp.sin(ang)
    reps = LANES // HEAD_DIM
    return jnp.tile(jnp.concatenate([cos, cos], axis=-1), (1, reps)), \
        jnp.tile(jnp.concatenate([-sin, sin], axis=-1), (1, reps))


def _retention_tables():
    log_gamma = jnp.log1p(-jnp.exp2(-5.0 - jnp.arange(RET_HEADS, dtype=F32)))
    lg_lane = jnp.repeat(log_gamma, HEAD_DIM)[None, :]
    pos = (jnp.arange(LIN_BLOCK) % CHUNK).astype(F32)[:, None]
    qf = jnp.exp(lg_lane * (pos + 1.0))
    kf = jnp.exp(lg_lane * (CHUNK - 1.0 - pos))
    cd = jnp.exp(lg_lane * CHUNK)
    r = jnp.arange(LIN_BLOCK)
    rel = (r[:, None] - r[None, :]).astype(F32)
    ok = ((r[:, None] // CHUNK) == (r[None, :] // CHUNK)) & (rel >= 0)
    dm = jnp.where(ok[None], jnp.exp(log_gamma[:, None, None] * rel[None]), 0.0)
    dmat = dm.reshape(RET_W // LANES, 2 * LIN_BLOCK, LIN_BLOCK)
    return dmat, qf, kf, cd


def kernel(x, c, w_ada, b_ada, norm_g, ffn1_in, ffn1_out, w_in, gla_a2, gla_a_bias, gla_norm_g,
           nsa_pe_k, nsa_pe_v, nsa_w1_k, nsa_w2_k, nsa_w1_v, nsa_w2_v, nsa_gate_bias, ret_norm_g,
           w_out, ffn2_in, ffn2_out, final_norm_g):
    batch, seq, d = x.shape
    n_layers = w_ada.shape[0]
    assert batch <= 8 and seq % 512 == 0 and seq >= WINDOW + Q_BLOCK

    c_pad = jnp.zeros((8, d), F32).at[:batch].set(c)
    mod = _ada_call(c_pad, w_ada, b_ada).reshape(n_layers * 8 * N_ADA, 1, d)
    ng = norm_g.reshape(n_layers * 3, 1, d)

    w_proj, gate_bias = _layout_proj_weights(w_in, nsa_gate_bias)
    pe_top, pe_bot, w1_cmp, w2_cmp = _layout_cmp_weights(nsa_pe_k, nsa_pe_v, nsa_w1_k, nsa_w2_k, nsa_w1_v, nsa_w2_v)
    cosf, sinf = _rotary_tables(seq)
    dmat, qf, kf, cd = _retention_tables()
    a2_pad = jnp.zeros((n_layers, LANES, GLA_W), F32).at[:, :GLA_LOWRANK].set(gla_a2)
    a_bias = gla_a_bias.reshape(n_layers, 1, GLA_W)
    gla_g = jnp.tile(gla_norm_g, (1, GLA_HEADS)).reshape(n_layers, 1, GLA_W)
    ret_g = jnp.tile(ret_norm_g, (1, RET_HEADS)).reshape(n_layers, 1, RET_W)
    f1_in, f1_out = ffn1_in.astype(BF16), ffn1_out.astype(BF16)
    f2_in, f2_out = ffn2_in.astype(BF16), ffn2_out.astype(BF16)
    w_o = w_out.astype(BF16)

    xs = x.reshape(batch * seq, d)
    for l in range(n_layers):
        xs = _ffn_call(xs, ng, mod, f1_in, f1_out, layer=l, sub=0, seq=seq)
        (gla, lr, qraw, qrot, kcvc, ks, vs, kw, vw, gate, ret) = _proj_call(
            xs, ng, mod, w_proj, gate_bias, cosf, sinf, layer=l, seq=seq)
        kc, vc = _cmp_call(kcvc, pe_top, pe_bot, w1_cmp, w2_cmp, layer=l, batch=batch, seq=seq)
        o_gla = _gla_call(gla, lr, a2_pad, a_bias, gla_g, layer=l, batch=batch, seq=seq)
        o_ret = _ret_call(ret, dmat, qf, kf, cd, ret_g, layer=l, batch=batch, seq=seq)
        o_nsa = _nsa_call(qraw, qrot, kc, vc, ks, vs, kw, vw, gate, layer=l, batch=batch, seq=seq)
        final_g = final_norm_g.reshape(1, d) if l == n_layers - 1 else None
        xs = _ffn_call(xs, ng, mod, f2_in, f2_out, layer=l, sub=2, seq=seq, mixer=(o_gla, o_nsa, o_ret, w_o),
                       final_g=final_g)
    return xs.reshape(batch, seq, d)
```

```python
import functools
import math

import numpy as np
import jax
import jax.numpy as jnp
from jax import lax
from jax.experimental import pallas as pl
from jax.experimental.pallas import tpu as pltpu

F32 = jnp.float32
BF16 = jnp.bfloat16

HEAD_DIM = 64
LANES = 128
GLA_HEADS = 4
NSA_HEADS = 8
NSA_KV_HEADS = 2
NSA_REP = NSA_HEADS // NSA_KV_HEADS
RET_HEADS = 4
GLA_LOWRANK = 16
GLA_TAU = 16.0
CHUNK = 64
CMP_LEN = 32
CMP_STRIDE = 16
CMP_HIDDEN = 128
SLC_LEN = 64
SLC_TOPK = 16
WINDOW = 512
Q_BLOCK = 128
ROPE_THETA = 10000.0
FFN_HALF = 0.5
NORM_EPS = 1e-6
NEG_INF = -1e30
FORCED_SCORE = 1e4
N_ADA = 9
QK_SCALE = HEAD_DIM ** -0.5
SLC_SHIFT = int(math.log2(SLC_LEN))
LOG2_E = math.log2(math.e)

GLA_W = GLA_HEADS * HEAD_DIM
NSA_W = NSA_HEADS * HEAD_DIM
NSA_KV_W = NSA_KV_HEADS * HEAD_DIM
RET_W = RET_HEADS * HEAD_DIM
IN_SIZES = (GLA_W, GLA_W, GLA_W, GLA_W, GLA_LOWRANK,
            NSA_W, NSA_KV_W, NSA_KV_W, NSA_KV_W, NSA_KV_W, NSA_KV_W, NSA_KV_W, 3 * NSA_HEADS,
            RET_W, RET_W, RET_W, RET_W)
IN_SPLITS = tuple(int(s) for s in np.cumsum(IN_SIZES)[:-1])

SEG_GLA = 0
SEG_LR = SEG_GLA + 4 * GLA_W
SEG_NQ = SEG_LR + LANES
SEG_KCVC = SEG_NQ + NSA_W
SEG_KS = SEG_KCVC + 2 * NSA_KV_W
SEG_KW = SEG_KS + NSA_KV_HEADS * LANES
SEG_VS = SEG_KW + NSA_KV_HEADS * LANES
SEG_VW = SEG_VS + NSA_KV_W
SEG_GATE = SEG_VW + NSA_KV_W
SEG_RET = SEG_GATE + NSA_KV_HEADS * LANES
PROJ_W = SEG_RET + 4 * RET_W

VMEM_LIMIT_BYTES = 56 * 1024 * 1024

NT_DIMS = (((1,), (1,)), ((), ()))


def _dot(a, b):
    return jnp.dot(a, b, preferred_element_type=F32)


def _dot_nt(a, b):
    return lax.dot_general(a, b, NT_DIMS, preferred_element_type=F32)


def _split3(x):
    hi = x.astype(BF16)
    r1 = x - hi.astype(F32)
    mid = r1.astype(BF16)
    lo = (r1 - mid.astype(F32)).astype(BF16)
    return hi, mid, lo


def _dot3_exact_rhs(x, w_bf16):
    hi, mid, lo = _split3(x)
    return _dot(hi, w_bf16) + _dot(mid, w_bf16) + _dot(lo, w_bf16)


def _dot3_exact_lhs(w_bf16, x):
    hi, mid, lo = _split3(x)
    return _dot(w_bf16, hi) + _dot(w_bf16, mid) + _dot(w_bf16, lo)


def _sigmoid(x):
    return 1.0 / (1.0 + jnp.exp(-x))


def _silu(x):
    return x * _sigmoid(x)


def _iota(shape, dim):
    return lax.broadcasted_iota(jnp.int32, shape, dim)


def _params(n_grid):
    return pltpu.CompilerParams(dimension_semantics=("arbitrary",) * n_grid,
                                vmem_limit_bytes=VMEM_LIMIT_BYTES)


def _norm_mod(x, g, sh, sc):
    ms = jnp.mean(x * x, axis=-1, keepdims=True)
    return x * lax.rsqrt(ms + NORM_EPS) * (g * (1.0 + sc)) + sh


def _ada_kernel(c_ref, w_ref, b_ref, o_ref):
    c = c_ref[...]
    o_ref[...] = _dot(_silu(c).astype(BF16), w_ref[...].astype(BF16)) + b_ref[...]


def _ada_call(c_pad, w_ada, b_ada):
    n_layers, d, n_out = w_ada.shape
    tn = 2304 if n_out % 2304 == 0 else n_out
    return pl.pallas_call(
        _ada_kernel,
        grid=(n_layers, n_out // tn),
        in_specs=[pl.BlockSpec((8, d), lambda l, j: (0, 0)),
                  pl.BlockSpec((None, d, tn), lambda l, j: (l, 0, j)),
                  pl.BlockSpec((None, 1, tn), lambda l, j: (l, 0, j))],
        out_specs=pl.BlockSpec((None, 8, tn), lambda l, j: (l, 0, j)),
        out_shape=jax.ShapeDtypeStruct((n_layers, 8, n_out), F32),
        compiler_params=_params(2),
        name="ada_mod",
    )(c_pad, w_ada, b_ada.reshape(n_layers, 1, n_out))


def _ffn_kernel(x_ref, g_ref, sh_ref, sc_ref, gt_ref, win_ref, wout_ref, *rest, d_ff, chunks, mixer, final):
    rest = list(rest)
    x = x_ref[...]
    if mixer:
        og_ref, on_ref, or_ref, gm_ref, wo_ref = rest[:5]
        rest = rest[5:]
        x = x + gm_ref[...] * (_dot(og_ref[...], wo_ref[0:GLA_W, :])
                               + _dot(on_ref[...], wo_ref[GLA_W:GLA_W + NSA_W, :])
                               + _dot(or_ref[...], wo_ref[GLA_W + NSA_W:, :]))
    if final:
        fg_ref, o_ref, act_ref = rest
    else:
        o_ref, act_ref = rest
    h = _norm_mod(x, g_ref[...], sh_ref[...], sc_ref[...]).astype(BF16)
    for (c0, cw) in chunks:
        gate = _dot(h, win_ref[:, c0:c0 + cw])
        up = _dot(h, win_ref[:, d_ff + c0:d_ff + c0 + cw])
        act_ref[:, c0:c0 + cw] = (_silu(gate) * up).astype(BF16)
    y = _dot(act_ref[...], wout_ref[...])
    xn = x + (FFN_HALF * gt_ref[...]) * y
    if final:
        ms = jnp.mean(xn * xn, axis=-1, keepdims=True)
        xn = xn * lax.rsqrt(ms + NORM_EPS) * fg_ref[...]
    o_ref[...] = xn


def _ffn_call(x, ng, mod, w_in, w_out, *, layer, sub, seq, mixer=None, final_g=None, tm=1024):
    m, d = x.shape
    d_ff = w_out.shape[1]
    tiles_per_seq = seq // tm
    chunks, c0 = [], 0
    while c0 < d_ff:
        cw = min(512, d_ff - c0)
        chunks.append((c0, cw))
        c0 += cw

    def mod_spec(k):
        return pl.BlockSpec((None, 1, d), lambda i: ((layer * 8 + i // tiles_per_seq) * N_ADA + k, 0, 0))

    in_specs = [pl.BlockSpec((tm, d), lambda i: (i, 0)),
                pl.BlockSpec((None, 1, d), lambda i: (layer * 3 + sub, 0, 0)),
                mod_spec(3 * sub), mod_spec(3 * sub + 1), mod_spec(3 * sub + 2),
                pl.BlockSpec((None, d, 2 * d_ff), lambda i: (layer, 0, 0), pipeline_mode=pl.Buffered(1)),
                pl.BlockSpec((None, d_ff, d), lambda i: (layer, 0, 0), pipeline_mode=pl.Buffered(1))]
    args = [x, ng, mod, mod, mod, w_in, w_out]
    if mixer is not None:
        o_gla, o_nsa, o_ret, w_mix = mixer
        in_specs += [pl.BlockSpec((tm, o.shape[1]), lambda i: (i, 0)) for o in (o_gla, o_nsa, o_ret)]
        in_specs += [mod_spec(5), pl.BlockSpec((None,) + w_mix.shape[1:], lambda i: (layer, 0, 0),
                                               pipeline_mode=pl.Buffered(1))]
        args += [o_gla, o_nsa, o_ret, mod, w_mix]
    if final_g is not None:
        in_specs.append(pl.BlockSpec((1, d), lambda i: (0, 0)))
        args.append(final_g)
    return pl.pallas_call(
        functools.partial(_ffn_kernel, d_ff=d_ff, chunks=tuple(chunks), mixer=mixer is not None,
                          final=final_g is not None),
        grid=(m // tm,),
        in_specs=in_specs,
        out_specs=pl.BlockSpec((tm, d), lambda i: (i, 0)),
        out_shape=jax.ShapeDtypeStruct((m, d), F32),
        scratch_shapes=[pltpu.VMEM((tm, d_ff), BF16)],
        compiler_params=_params(1),
        name=f"ffn_l{layer}_s{sub}",
    )(*args)


def _rotate(x, cosf, sinf):
    rows, width = x.shape
    first = (_iota((rows, LANES), 1) & (HEAD_DIM - 1)) < HEAD_DIM // 2
    outs = []
    for j in range(width // LANES):
        xs = x[:, j * LANES:(j + 1) * LANES]
        swapped = jnp.where(first, pltpu.roll(xs, LANES - HEAD_DIM // 2, 1), pltpu.roll(xs, HEAD_DIM // 2, 1))
        outs.append(xs * cosf + swapped * sinf)
    return outs[0] if len(outs) == 1 else jnp.concatenate(outs, axis=1)


def _store_group_transposed(dst_ref, x, fill):
    top = _iota((LANES, LANES), 0) < HEAD_DIM
    for c in range(x.shape[0] // LANES):
        piece = jnp.transpose(x[c * LANES:(c + 1) * LANES])
        dst_ref[0, c] = jnp.where(top, piece, fill).astype(dst_ref.dtype)
        dst_ref[1, c] = jnp.where(top, pltpu.roll(piece, HEAD_DIM, 0), fill).astype(dst_ref.dtype)


def _proj_kernel(x_ref, g_ref, sh_ref, sc_ref, w_ref, gb_ref, cos_ref, sin_ref,
                 gla_ref, lr_ref, qraw_ref, qrot_ref, kcvc_ref, ks_ref, vst_ref, kw_ref, vwt_ref,
                 gate_ref, ret_ref, *, tiles_per_seq):
    h = _norm_mod(x_ref[...], g_ref[...], sh_ref[...], sc_ref[...]).astype(BF16)
    cosf = cos_ref[...]
    sinf = sin_ref[...]
    tm = h.shape[0]

    def seg(c0, width):
        return _dot(h, w_ref[:, c0:c0 + width])

    nq = seg(SEG_NQ, NSA_W)
    for dst_ref, q in ((qraw_ref, nq), (qrot_ref, _rotate(nq, cosf, sinf))):
        q = q * (QK_SCALE * LOG2_E)
        for c in range(tm // LANES):
            for j in range(NSA_W // LANES):
                dst_ref[c, j] = jnp.transpose(q[c * LANES:(c + 1) * LANES, j * LANES:(j + 1) * LANES]).astype(BF16)
    ret_ref[:, 0:RET_W] = _rotate(seg(SEG_RET, RET_W), cosf, sinf)
    ret_ref[:, RET_W:2 * RET_W] = _rotate(seg(SEG_RET + RET_W, RET_W), cosf, sinf) * QK_SCALE
    grp_w = NSA_KV_HEADS * LANES
    pos = (pl.program_id(0) % tiles_per_seq) * tm + _iota((tm, grp_w), 0)
    spare = (_iota((tm, grp_w), 1) & (LANES - 1)) - HEAD_DIM
    blk_onehot = jnp.where(spare == (pos >> SLC_SHIFT), 1.0, 0.0)
    ks_ref[...] = (_rotate(seg(SEG_KS, grp_w), cosf, sinf) + blk_onehot).astype(BF16)
    kw_ref[...] = _rotate(seg(SEG_KW, grp_w), cosf, sinf).astype(BF16)
    _store_group_transposed(vst_ref, seg(SEG_VS, NSA_KV_W), 1.0)
    _store_group_transposed(vwt_ref, seg(SEG_VW, NSA_KV_W), 1.0)
    gate_ref[...] = _sigmoid(seg(SEG_GATE, NSA_KV_HEADS * LANES) + gb_ref[...])
    kcvc_ref[...] = seg(SEG_KCVC, 2 * NSA_KV_W)
    lr_ref[...] = seg(SEG_LR, LANES)
    ret_ref[:, 2 * RET_W:4 * RET_W] = seg(SEG_RET + 2 * RET_W, 2 * RET_W)
    gla_ref[...] = seg(SEG_GLA, 4 * GLA_W)


def _proj_call(x, ng, mod, w_proj, gate_bias, cosf, sinf, *, layer, seq, tm=512):
    m, d = x.shape
    tiles_per_seq = seq // tm

    def mod_spec(k):
        return pl.BlockSpec((None, 1, d), lambda i: ((layer * 8 + i // tiles_per_seq) * N_ADA + k, 0, 0))

    def row_spec(width):
        return pl.BlockSpec((tm, width), lambda i: (i, 0))

    batch = m // seq
    grp_w = NSA_KV_HEADS * LANES

    def vt_spec():
        return pl.BlockSpec((None, NSA_KV_HEADS, tm // LANES, LANES, LANES),
                            lambda i: (i // tiles_per_seq, 0, i % tiles_per_seq, 0, 0))

    vt_shape = jax.ShapeDtypeStruct((batch, NSA_KV_HEADS, seq // LANES, LANES, LANES), BF16)
    n_pairs = NSA_W // LANES
    qt_spec = pl.BlockSpec((tm // LANES, n_pairs, LANES, LANES), lambda i: (i, 0, 0, 0))
    qt_shape = jax.ShapeDtypeStruct((m // LANES, n_pairs, LANES, LANES), BF16)
    out_widths = [(4 * GLA_W, F32), (LANES, F32), "qt", "qt",
                  (2 * NSA_KV_W, F32), (grp_w, BF16), None, (grp_w, BF16), None,
                  (NSA_KV_HEADS * LANES, F32), (4 * RET_W, F32)]

    def out_spec(ow):
        return vt_spec() if ow is None else qt_spec if ow == "qt" else row_spec(ow[0])

    def out_struct(ow):
        return vt_shape if ow is None else qt_shape if ow == "qt" else jax.ShapeDtypeStruct((m, ow[0]), ow[1])

    return pl.pallas_call(
        functools.partial(_proj_kernel, tiles_per_seq=tiles_per_seq),
        grid=(m // tm,),
        in_specs=[row_spec(d),
                  pl.BlockSpec((None, 1, d), lambda i: (layer * 3 + 1, 0, 0)),
                  mod_spec(3), mod_spec(4),
                  pl.BlockSpec((None, d, PROJ_W), lambda i: (layer, 0, 0), pipeline_mode=pl.Buffered(1)),
                  pl.BlockSpec((None, 1, NSA_KV_HEADS * LANES), lambda i: (layer, 0, 0)),
                  pl.BlockSpec((tm, LANES), lambda i: (i % tiles_per_seq, 0)),
                  pl.BlockSpec((tm, LANES), lambda i: (i % tiles_per_seq, 0))],
        out_specs=[out_spec(ow) for ow in out_widths],
        out_shape=[out_struct(ow) for ow in out_widths],
        compiler_params=_params(1),
        name=f"proj_l{layer}",
    )(x, ng, mod, mod, w_proj, gate_bias, cosf, sinf)


def _cmp_kernel(xk_ref, xv_ref, pet_ref, peb_ref, wt_ref, wb_ref, w2_ref, kc_ref, vct_ref):
    n_rows = xk_ref.shape[0] // CMP_STRIDE
    x = jnp.concatenate([ref[pl.ds(l, n_rows, stride=CMP_STRIDE), :]
                         for l in range(CMP_STRIDE) for ref in (xk_ref, xv_ref)], axis=1)
    a = _dot((x + pet_ref[...]).astype(BF16), wt_ref[...])
    b = _dot((x + peb_ref[...]).astype(BF16), wb_ref[...])
    hid = a + pltpu.roll(b, n_rows - 1, 0)
    out = _dot(_silu(hid).astype(BF16), w2_ref[...])
    out = jnp.where(_iota(out.shape, 0) < n_rows - 1, out, 0.0)
    grp_w = NSA_KV_HEADS * LANES
    kc_ref[...] = out[:, 0:grp_w].astype(BF16)
    _store_group_transposed(vct_ref, out[:, grp_w:grp_w + NSA_KV_W], 0.0)


def _cmp_call(kcvc, pe_top, pe_bot, w1_halves, w2, *, layer, batch, seq):
    rows = seq // CMP_STRIDE
    width = CMP_STRIDE * 2 * NSA_KV_W
    x = kcvc.reshape(batch, seq, 2 * NSA_KV_W)
    hid_w = 4 * CMP_HIDDEN
    grp_w = NSA_KV_HEADS * LANES

    def wspec(shape):
        return pl.BlockSpec((None,) + shape, lambda b: (layer, 0, 0))

    return pl.pallas_call(
        _cmp_kernel,
        grid=(batch,),
        in_specs=[pl.BlockSpec((None, seq, NSA_KV_W), lambda b: (b, 0, 0)),
                  pl.BlockSpec((None, seq, NSA_KV_W), lambda b: (b, 0, 1)),
                  wspec((1, width)), wspec((1, width)),
                  pl.BlockSpec((None, None, width, hid_w), lambda b: (layer, 0, 0, 0)),
                  pl.BlockSpec((None, None, width, hid_w), lambda b: (layer, 1, 0, 0)),
                  wspec((hid_w, grp_w + NSA_KV_W))],
        out_specs=[pl.BlockSpec((None, rows, grp_w), lambda b: (b, 0, 0)),
                   pl.BlockSpec((None, NSA_KV_HEADS, rows // LANES, LANES, LANES), lambda b: (b, 0, 0, 0, 0))],
        out_shape=[jax.ShapeDtypeStruct((batch, rows, grp_w), BF16),
                   jax.ShapeDtypeStruct((batch, NSA_KV_HEADS, rows // LANES, LANES, LANES), BF16)],
        compiler_params=_params(1),
        name=f"nsa_compress_l{layer}",
    )(x, x, pe_top, pe_bot, w1_halves, w1_halves, w2)


def _heads_tiled(x):
    return jnp.concatenate([x] * NSA_REP, axis=1)


SLC_UNROLL = 4
NSA_PARTS = 8

def _nsa_compressed(s_c, vc_t, t0, tq, seq):
    n_cp = s_c.shape[0]
    n_cmp = (seq - CMP_LEN) // CMP_STRIDE + 1
    nn = _iota((n_cp, tq), 0)
    tt = t0 + _iota((n_cp, tq), 1)
    bias_c = jnp.where((nn * CMP_STRIDE + (CMP_LEN - 1) <= tt) & (nn < n_cmp), 0.0, NEG_INF)
    has_cmp = t0 + _iota((1, tq), 1) >= CMP_LEN - 1
    p_heads = []
    for r in range(NSA_REP):
        s = s_c[:, r * tq:(r + 1) * tq] + bias_c
        e = jnp.exp2(s - jnp.max(s, axis=0, keepdims=True))
        p_heads.append(e * jnp.where(has_cmp, 1.0 / jnp.sum(e, axis=0, keepdims=True), 0.0))
    o_cmp = _dot(vc_t, jnp.concatenate(p_heads, axis=1).astype(BF16))
    p_sum = p_heads[0]
    for r in range(1, NSA_REP):
        p_sum = p_sum + p_heads[r]
    return o_cmp, p_sum


def _nsa_selection_rows(p_sum, imp_ref, t0, tq, seq, n_rank):
    n_cp = p_sum.shape[0]
    n_cmp = (seq - CMP_LEN) // CMP_STRIDE + 1
    n_slc = seq // SLC_LEN
    n_sel = min(SLC_TOPK, n_slc)
    sb = _iota((LANES, n_cp), 0)
    cb = _iota((LANES, n_cp), 1)
    overlap_t = ((cb * CMP_STRIDE < sb * SLC_LEN + SLC_LEN) & (cb * CMP_STRIDE + CMP_LEN > sb * SLC_LEN)
                 & (sb < n_slc) & (cb < n_cmp))
    imp_t = _dot3_exact_lhs(jnp.where(overlap_t, 1.0, 0.0).astype(BF16), p_sum)
    blk = _iota((LANES, tq), 0)
    cur = (t0 + _iota((LANES, tq), 1)) >> SLC_SHIFT
    forced = (blk == 0) | (blk == cur) | (blk == cur - 1)
    imp_ref[...] = jnp.where(forced, FORCED_SCORE, jnp.where(blk <= cur, imp_t, -1.0))

    def ranked_rows(n_rows):
        if n_rows <= n_sel:
            return jnp.zeros((HEAD_DIM, tq), F32)
        n_grp = n_rows // 8
        vals = [imp_ref[8 * j:8 * (j + 1), :] for j in range(n_grp)]
        ranks = [jnp.zeros((8, tq), F32) for _ in range(n_grp)]
        sub = _iota((8, tq), 0)
        for i in range(n_rows):
            row = jnp.broadcast_to(imp_ref[i:i + 1, :], (8, tq))
            for j in range(n_grp):
                if 8 * j > i:
                    beats = jnp.where(row >= vals[j], 1.0, 0.0)
                elif 8 * j + 7 <= i:
                    beats = jnp.where(row > vals[j], 1.0, 0.0)
                else:
                    beats = jnp.where(sub + 8 * j > i, jnp.where(row >= vals[j], 1.0, 0.0),
                                      jnp.where(row > vals[j], 1.0, 0.0))
                ranks[j] = ranks[j] + beats
        return jnp.concatenate([jnp.where(rk < n_sel, 0.0, NEG_INF) for rk in ranks]
                               + [jnp.zeros((HEAD_DIM - n_rows, tq), F32)] * (n_rows < HEAD_DIM), axis=0)

    return ranked_rows(n_rank)


def _window_scores(q_rot, kw_ref, g, qb, tq, may_precede_start):
    n_wb = WINDOW // tq + 1
    ki = _iota((tq, tq), 0)
    ci = _iota((tq, tq), 1)
    blocks = []
    for j in range(n_wb):
        kb = qb - (n_wb - 1) + j
        kb_read = jnp.maximum(kb, 0) if may_precede_start else kb
        s = _dot(kw_ref[pl.ds(pl.multiple_of(kb_read * tq, tq), tq), g * LANES:(g + 1) * LANES], q_rot)
        if j == 0:
            s = s + _heads_tiled(jnp.where(ki > ci, 0.0, NEG_INF))
        if j == n_wb - 1:
            s = s + _heads_tiled(jnp.where(ki <= ci, 0.0, NEG_INF))
        elif may_precede_start:
            s = s + jnp.where(kb >= 0, 0.0, NEG_INF)
        blocks.append((s, kb_read))
    return blocks


def _window_finish(blocks, vwt_ref, g, tq):
    m_w = blocks[0][0].max(axis=0, keepdims=True)
    for s, _ in blocks[1:]:
        m_w = jnp.maximum(m_w, s.max(axis=0, keepdims=True))
    o_aug = jnp.zeros((LANES, NSA_REP * tq), F32)
    for s, kb_read in blocks:
        o_aug = o_aug + _dot(vwt_ref[g, kb_read], jnp.exp2(s - m_w).astype(BF16))
    return o_aug[0:HEAD_DIM] * (1.0 / o_aug[HEAD_DIM:HEAD_DIM + 1])


def _nsa_kernel(qraw_ref, qrot_ref, kc_ref, vct_ref, ks_ref, vst_ref, kw_ref, vwt_ref, gate_ref, prev_ref,
                o_ref, imp_ref, *, seq, tq, tk, q_lo, n_qb, n_rank, n_cmp_rows):
    del prev_ref
    qb = pl.program_id(1) + q_lo
    t0 = qb * tq
    groups = range(NSA_KV_HEADS)
    n_cols = NSA_REP * tq

    def grp(g):
        return slice(g * LANES, (g + 1) * LANES)

    def q_heads(ref, g):
        return jnp.concatenate([ref[0, (g * NSA_REP + r) // 2, (r % 2) * HEAD_DIM:(r % 2 + 1) * HEAD_DIM, :]
                                for r in range(NSA_REP)], axis=1)

    def q_aug(q_t, extra_rows):
        return jnp.concatenate([q_t, extra_rows.astype(BF16)], axis=0)

    no_extra = jnp.zeros((HEAD_DIM, n_cols), F32)
    q_rot_t = [q_heads(qrot_ref, g) for g in groups]
    s_cmp = [_dot(kc_ref[0:n_cmp_rows, grp(g)], q_aug(q_heads(qraw_ref, g), no_extra)) for g in groups]
    win_blocks = [_window_scores(q_aug(q_rot_t[g], no_extra), kw_ref, g, qb, tq, q_lo * tq < WINDOW)
                  for g in groups]
    o_cmp, q_sel = [], []
    for g in groups:
        vc_t = jnp.concatenate([vct_ref[g, c] for c in range(n_cmp_rows // LANES)], axis=1)
        o_c, p_sum = _nsa_compressed(s_cmp[g], vc_t, t0, tq, seq)
        o_cmp.append(o_c[0:HEAD_DIM])
        sel_neg = _heads_tiled(_nsa_selection_rows(p_sum, imp_ref.at[g], t0, tq, seq, n_rank))
        q_sel.append(q_aug(q_rot_t[g], sel_neg))
    o_win = [_window_finish(win_blocks[g], vwt_ref, g, tq) for g in groups]

    blocks_per_tile = tk // LANES
    n_full = t0 // tk
    min_full = (q_lo * tq) // tk
    max_full = ((q_lo + n_qb - 1) * tq) // tk

    def k_tile(kt, g):
        if isinstance(kt, int):
            return ks_ref[kt * tk:(kt + 1) * tk, grp(g)]
        return ks_ref[pl.ds(pl.multiple_of(kt * tk, tk), tk), grp(g)]

    def v_tile(kt, g):
        return jnp.concatenate([vst_ref[g, kt * blocks_per_tile + c] for c in range(blocks_per_tile)], axis=1)

    def online_update(scores, values, m_i, acc):
        m_new = m_i
        for s in scores:
            m_new = jnp.maximum(m_new, s.max(axis=0, keepdims=True))
        p = jnp.concatenate([jnp.exp2(s - m_new).astype(BF16) for s in scores], axis=0)
        return m_new, jnp.exp2(m_i - m_new) * acc + _dot(jnp.concatenate(values, axis=1), p)

    pair_w = 2 * tq
    chains = [(g, h) for g in groups for h in range(NSA_REP // 2)]

    def q_pair(g, h):
        return q_sel[g][:, h * pair_w:(h + 1) * pair_w]

    tiles = list(range(max_full))
    trips = [tiles[i:i + SLC_UNROLL] for i in range(0, len(tiles), SLC_UNROLL)]

    def trip_scores(trip):
        scores = []
        for g, h in chains:
            per_tile = []
            for kt in trip:
                s = _dot(k_tile(kt, g), q_pair(g, h))
                per_tile.append(s if kt < min_full else s + jnp.where(kt < n_full, 0.0, NEG_INF))
            scores.append(per_tile)
        return scores

    def trip_update(trip, scores, stats):
        return [online_update(scores[c], [v_tile(kt, g) for kt in trip], *stats[c])
                for c, (g, h) in enumerate(chains)]

    causal = n_full * tk + _iota((tk, tq), 0) <= t0 + _iota((tk, tq), 1)
    causal_bias = _heads_tiled(jnp.where(causal, 0.0, NEG_INF))
    s_diag = [_dot(k_tile(n_full, g), q_sel[g]) for g in groups]
    pending = trip_scores(trips[0]) if trips else None
    first = [online_update([s_diag[g] + causal_bias], [v_tile(n_full, g)],
                           jnp.full((1, n_cols), NEG_INF, F32), jnp.zeros((LANES, n_cols), F32)) for g in groups]
    stats = [tuple(x[:, h * pair_w:(h + 1) * pair_w] for x in first[g]) for g, h in chains]
    for i, trip in enumerate(trips):
        upcoming = trip_scores(trips[i + 1]) if i + 1 < len(trips) else None
        stats = trip_update(trip, pending, stats)
        pending = upcoming

    gate_all = gate_ref[...]
    for g in groups:
        acc = jnp.concatenate([stats[c][1] for c, (cg, _) in enumerate(chains) if cg == g], axis=1)
        o_slc = acc[0:HEAD_DIM] * (1.0 / acc[HEAD_DIM:HEAD_DIM + 1])
        gate_t = jnp.transpose(gate_all[:, grp(g)])
        o_heads = []
        for r in range(NSA_REP):
            cols = slice(r * tq, (r + 1) * tq)
            o_heads.append(gate_t[3 * r:3 * r + 1] * o_cmp[g][:, cols] + gate_t[3 * r + 1:3 * r + 2] * o_slc[:, cols]
                           + gate_t[3 * r + 2:3 * r + 3] * o_win[g][:, cols])
        for pair in range(NSA_REP // 2):
            slab = jnp.transpose(jnp.concatenate(o_heads[2 * pair:2 * pair + 2], axis=0))
            c0 = (g * NSA_REP // 2 + pair) * LANES
            o_ref[:, c0:c0 + LANES] = slab.astype(BF16)


def _nsa_call(qraw, qrot, kc, vct, ks, vst, kw, vwt, gate, *, layer, batch, seq, tq=Q_BLOCK, tk=256):
    assert tq == LANES and tk % tq == 0 and seq % tk == 0 and seq // SLC_LEN <= HEAD_DIM
    nq = seq // tq
    n_cp = seq // CMP_STRIDE
    grp_w = NSA_KV_HEADS * LANES

    def k_spec(rows):
        return pl.BlockSpec((None, rows, grp_w), lambda b, i: (b, 0, 0))

    def vt_spec(rows):
        return pl.BlockSpec((None, NSA_KV_HEADS, rows // LANES, LANES, LANES), lambda b, i: (b, 0, 0, 0, 0))

    ks3, kw3 = ks.reshape(batch, seq, grp_w), kw.reshape(batch, seq, grp_w)
    part = nq // NSA_PARTS
    out = jnp.zeros((batch * seq, NSA_W), BF16)
    for p in range(NSA_PARTS):
        q_lo, last_token = p * part, (p + 1) * part * tq - 1
        n_rank = min(seq // SLC_LEN, -(-(last_token // SLC_LEN + 1) // 8) * 8)
        n_cmp_rows = min(n_cp, -(-(last_token // CMP_STRIDE + 1) // LANES) * LANES)

        def row_spec(width, q_lo=q_lo):
            return pl.BlockSpec((tq, width), lambda b, i: (b * nq + q_lo + i, 0))

        n_keys = -(-(last_token + 1) // tk) * tk
        q_spec = pl.BlockSpec((1, NSA_W // LANES, LANES, LANES), lambda b, i, q_lo=q_lo: (b * nq + q_lo + i, 0, 0, 0))
        in_specs = [q_spec, q_spec, k_spec(n_cmp_rows),
                    vt_spec(n_cmp_rows), k_spec(n_keys), vt_spec(n_keys), k_spec(n_keys), vt_spec(n_keys),
                    row_spec(grp_w)]
        in_specs.append(pl.BlockSpec(memory_space=pl.ANY))
        args = [qraw, qrot, kc, vct, ks3, vst, kw3, vwt, gate, out]
        out = pl.pallas_call(
            functools.partial(_nsa_kernel, seq=seq, tq=tq, tk=tk, q_lo=q_lo, n_qb=part, n_rank=n_rank,
                              n_cmp_rows=n_cmp_rows),
            grid=(batch, part),
            in_specs=in_specs,
            out_specs=pl.BlockSpec((tq, NSA_W), lambda b, i, q_lo=q_lo: (b * nq + q_lo + i, 0)),
            out_shape=jax.ShapeDtypeStruct((batch * seq, NSA_W), BF16),
            input_output_aliases={len(args) - 1: 0},
            scratch_shapes=[pltpu.VMEM((NSA_KV_HEADS, LANES, tq), F32)],
            compiler_params=_params(2),
            name=f"nsa_attn_l{layer}_p{p}",
        )(*args)
    return out


LIN_BLOCK = 2 * CHUNK


def _lin_block_consts():
    r = _iota((LIN_BLOCK, LANES), 0)
    c = _iota((LIN_BLOCK, LANES), 1)
    same_chunk = (r >= CHUNK) == (c >= CHUNK)
    causal = same_chunk & (r >= c)
    head_diag = same_chunk
    first_rows = r < CHUNK
    low_lanes = c < HEAD_DIM
    return causal, head_diag, first_rows, low_lanes


def _head_stats(x, head_diag_mean):
    hi = x.astype(BF16)
    lo = (x - hi.astype(F32)).astype(BF16)
    return _dot(hi, head_diag_mean) + _dot(lo, head_diag_mean)


def _lin_core(chains, consts):
    causal, head_diag, first_rows, low_lanes = consts
    mask2 = jnp.concatenate([causal, causal], axis=0)
    scores, updates = [], []
    for c in chains:
        zero = jnp.zeros_like(c["q_dec"])
        q2 = jnp.concatenate([jnp.where(low_lanes, c["q_dec"], zero), jnp.where(low_lanes, zero, c["q_dec"])], axis=0)
        scores.append(_dot_nt(q2.astype(BF16), c["k_inv"].astype(BF16)))
    for c in chains:
        v_t = jnp.transpose(c["v"]).astype(BF16)
        kd_ab = jnp.concatenate([jnp.where(first_rows, c["k_dec"], 0.0), jnp.where(first_rows, 0.0, c["k_dec"])],
                                axis=1)
        updates.append(_dot(v_t, kd_ab.astype(BF16)))
    intra, states = [], []
    for c, a2 in zip(chains, scores):
        a2 = jnp.where(mask2, a2, 0.0)
        if c["intra_scale"] is not None:
            a2 = a2 * c["intra_scale"]
        o2 = _dot(a2.astype(BF16), c["v"].astype(BF16))
        intra.append(jnp.where(low_lanes, o2[0:LIN_BLOCK], o2[LIN_BLOCK:]))
    out = []
    for c, upd, o_intra in zip(chains, updates, intra):
        st_a = c["st_prev"] * c["dec_a"] + jnp.where(head_diag, upd[:, 0:LANES], 0.0)
        st_b = st_a * c["dec_b"] + jnp.where(head_diag, upd[:, LANES:], 0.0)
        o_ab = _dot_nt(c["q_int"].astype(BF16), jnp.concatenate([c["st_prev"], st_a], axis=0).astype(BF16))
        out.append((o_intra + jnp.where(first_rows, o_ab[:, 0:LANES], o_ab[:, LANES:]), st_b))
    return out


def _gla_kernel(x_ref, lr_ref, a2_ref, ab_ref, ng_ref, o_ref, st_ref, *, n_blocks, batch):
    @pl.when(pl.program_id(0) == 0)
    def _():
        st_ref[...] = jnp.zeros_like(st_ref)

    consts = _lin_block_consts()
    causal, head_diag, first_rows, low_lanes = consts
    ri = _iota((LIN_BLOCK, LANES), 0)
    ci = _iota((LIN_BLOCK, LANES), 1)
    tri_t = jnp.where(((ri >= CHUNK) == (ci >= CHUNK)) & (ri <= ci), 1.0, 0.0).astype(BF16)
    mean_op = jnp.where(head_diag, 1.0 / HEAD_DIM, 0.0).astype(BF16)
    a2h = a2_ref[...].astype(BF16)
    a2l = (a2_ref[...] - a2h.astype(F32)).astype(BF16)

    def block(i):
        rows = pl.ds(i * LIN_BLOCK, LIN_BLOCK)
        lr = jnp.concatenate([lr_ref[bi, rows, :] for bi in range(batch)], axis=0)
        lh = lr.astype(BF16)
        ll = (lr - lh.astype(F32)).astype(BF16)
        z = _dot(lh, a2h) + _dot(lh, a2l) + _dot(ll, a2h) + ab_ref[...]
        log_a = (jnp.minimum(z, 0.0) - jnp.log(1.0 + jnp.exp(-jnp.abs(z)))) * (1.0 / GLA_TAU)
        n_slab = GLA_W // LANES
        la_t = jnp.concatenate([jnp.transpose(log_a[bi * LIN_BLOCK:(bi + 1) * LIN_BLOCK, s * LANES:(s + 1) * LANES])
                                for bi in range(batch) for s in range(n_slab)], axis=0)
        cum_t = _dot3_exact_rhs(la_t, tri_t)
        chains = []
        for bi in range(batch):
            for s in range(n_slab):
                cols = slice(s * LANES, (s + 1) * LANES)
                n = bi * n_slab + s

                def ld(k):
                    return x_ref[bi, rows, k * GLA_W + s * LANES:k * GLA_W + (s + 1) * LANES]

                q, k, v, gg = ld(0), ld(1), ld(2), ld(3)
                b = jnp.transpose(cum_t[n * LANES:(n + 1) * LANES])
                bl_a = b[CHUNK - 1:CHUNK, :]
                bl_b = b[LIN_BLOCK - 1:LIN_BLOCK, :]
                bl = jnp.where(first_rows, bl_a, bl_b)
                q_dec = q * QK_SCALE * jnp.exp(b)
                chains.append(dict(bi=bi, s=s, cols=cols, gate=gg, q_dec=q_dec, q_int=q_dec, k_inv=k * jnp.exp(-b),
                                   k_dec=k * jnp.exp(bl - b), v=v, intra_scale=None, dec_a=jnp.exp(bl_a),
                                   dec_b=jnp.exp(bl_b), st_prev=st_ref[bi, s]))
        results = _lin_core(chains, consts)
        for c, (_, st_new) in zip(chains, results):
            st_ref[c["bi"], c["s"]] = st_new
        ms = _head_stats(jnp.concatenate([o * o for o, _ in results], axis=0), mean_op)
        for n, (c, (o, _)) in enumerate(zip(chains, results)):
            y = (o * lax.rsqrt(ms[n * LIN_BLOCK:(n + 1) * LIN_BLOCK] + NORM_EPS) * ng_ref[:, c["cols"]]
                 * _silu(c["gate"]))
            o_ref[c["bi"], rows, c["cols"]] = y.astype(BF16)

    for i in range(n_blocks):
        block(i)


def _ret_kernel(x_ref, dmat_ref, qf_ref, kf_ref, cd_ref, ng_ref, o_ref, st_ref, *, n_blocks, batch):
    @pl.when(pl.program_id(0) == 0)
    def _():
        st_ref[...] = jnp.zeros_like(st_ref)

    consts = _lin_block_consts()
    causal, head_diag, first_rows, low_lanes = consts
    mean_op = jnp.where(head_diag, 1.0 / HEAD_DIM, 0.0).astype(BF16)

    def block(i):
        rows = pl.ds(i * LIN_BLOCK, LIN_BLOCK)
        chains = []
        for bi in range(batch):
            for s in range(RET_W // LANES):
                cols = slice(s * LANES, (s + 1) * LANES)

                def ld(k):
                    return x_ref[bi, rows, k * RET_W + s * LANES:k * RET_W + (s + 1) * LANES]

                q, k, v, gg = ld(0), ld(1), ld(2), ld(3)
                cd = cd_ref[:, cols]
                chains.append(dict(bi=bi, s=s, cols=cols, gate=gg, q_dec=q, q_int=q * qf_ref[:, cols], k_inv=k,
                                   k_dec=k * kf_ref[:, cols], v=v, intra_scale=dmat_ref[s], dec_a=cd, dec_b=cd,
                                   st_prev=st_ref[bi, s]))
        results = _lin_core(chains, consts)
        for c, (_, st_new) in zip(chains, results):
            st_ref[c["bi"], c["s"]] = st_new
        o_all = jnp.concatenate([o for o, _ in results], axis=0)
        oc_all = o_all - _head_stats(o_all, mean_op)
        var_all = _head_stats(oc_all * oc_all, mean_op)
        y_all = oc_all * lax.rsqrt(var_all + NORM_EPS)
        for n, c in enumerate(chains):
            y = y_all[n * LIN_BLOCK:(n + 1) * LIN_BLOCK] * ng_ref[:, c["cols"]] * _silu(c["gate"])
            o_ref[c["bi"], rows, c["cols"]] = y.astype(BF16)

    for i in range(n_blocks):
        block(i)


def _gla_call(gla, lr, a2_pad, a_bias, norm_g, *, layer, batch, seq, tb=LIN_BLOCK):
    n_slab = GLA_W // LANES

    def lspec(shape):
        return pl.BlockSpec((None,) + shape, lambda t: (layer, 0, 0))

    return pl.pallas_call(
        functools.partial(_gla_kernel, n_blocks=tb // LIN_BLOCK, batch=batch),
        grid=(seq // tb,),
        in_specs=[pl.BlockSpec((batch, tb, 4 * GLA_W), lambda t: (0, t, 0)),
                  pl.BlockSpec((batch, tb, LANES), lambda t: (0, t, 0)),
                  lspec((LANES, GLA_W)), lspec((1, GLA_W)), lspec((1, GLA_W))],
        out_specs=pl.BlockSpec((batch, tb, GLA_W), lambda t: (0, t, 0)),
        out_shape=jax.ShapeDtypeStruct((batch, seq, GLA_W), BF16),
        scratch_shapes=[pltpu.VMEM((batch, n_slab, LANES, LANES), F32)],
        compiler_params=_params(1),
        name=f"gla_l{layer}",
    )(gla.reshape(batch, seq, 4 * GLA_W), lr.reshape(batch, seq, LANES), a2_pad, a_bias, norm_g
      ).reshape(batch * seq, GLA_W)


def _ret_call(ret, dmat, qf, kf, cd, norm_g, *, layer, batch, seq, tb=LIN_BLOCK):
    n_slab = RET_W // LANES
    full = lambda a: pl.BlockSpec(a.shape, lambda t: (0,) * a.ndim)
    return pl.pallas_call(
        functools.partial(_ret_kernel, n_blocks=tb // LIN_BLOCK, batch=batch),
        grid=(seq // tb,),
        in_specs=[pl.BlockSpec((batch, tb, 4 * RET_W), lambda t: (0, t, 0)),
                  full(dmat), full(qf), full(kf), full(cd),
                  pl.BlockSpec((None, 1, RET_W), lambda t: (layer, 0, 0))],
        out_specs=pl.BlockSpec((batch, tb, RET_W), lambda t: (0, t, 0)),
        out_shape=jax.ShapeDtypeStruct((batch, seq, RET_W), BF16),
        scratch_shapes=[pltpu.VMEM((batch, n_slab, LANES, LANES), F32)],
        compiler_params=_params(1),
        name=f"ret_l{layer}",
    )(ret.reshape(batch, seq, 4 * RET_W), dmat, qf, kf, cd, norm_g).reshape(batch * seq, RET_W)


def _lin_pair_kernel(xg_ref, lr_ref, a2_ref, ab_ref, ngg_ref, xr_ref, dmat_ref, qf_ref, kf_ref, cd_ref, ngr_ref,
                     og_ref, or_ref, stg_ref, str_ref, *, batch):
    @pl.when(pl.program_id(0) == 0)
    def _():
        stg_ref[...] = jnp.zeros_like(stg_ref)
        str_ref[...] = jnp.zeros_like(str_ref)

    consts = _lin_block_consts()
    causal, head_diag, first_rows, low_lanes = consts
    ri = _iota((LIN_BLOCK, LANES), 0)
    ci = _iota((LIN_BLOCK, LANES), 1)
    tri_t = jnp.where(((ri >= CHUNK) == (ci >= CHUNK)) & (ri <= ci), 1.0, 0.0).astype(BF16)
    mean_op = jnp.where(head_diag, 1.0 / HEAD_DIM, 0.0).astype(BF16)
    a2h = a2_ref[...].astype(BF16)
    a2l = (a2_ref[...] - a2h.astype(F32)).astype(BF16)
    n_slab = GLA_W // LANES

    lr = jnp.concatenate([lr_ref[bi] for bi in range(batch)], axis=0)
    lh = lr.astype(BF16)
    ll = (lr - lh.astype(F32)).astype(BF16)
    z = _dot(lh, a2h) + _dot(lh, a2l) + _dot(ll, a2h) + ab_ref[...]
    log_a = (jnp.minimum(z, 0.0) - jnp.log(1.0 + jnp.exp(-jnp.abs(z)))) * (1.0 / GLA_TAU)
    la_t = jnp.concatenate([jnp.transpose(log_a[bi * LIN_BLOCK:(bi + 1) * LIN_BLOCK, s * LANES:(s + 1) * LANES])
                            for bi in range(batch) for s in range(n_slab)], axis=0)
    cum_t = _dot3_exact_rhs(la_t, tri_t)
    chains = []
    for bi in range(batch):
        for s in range(n_slab):
            cols = slice(s * LANES, (s + 1) * LANES)
            n = bi * n_slab + s
            q, k, v, gg = (xg_ref[bi, :, j * GLA_W + s * LANES:j * GLA_W + (s + 1) * LANES] for j in range(4))
            b = jnp.transpose(cum_t[n * LANES:(n + 1) * LANES])
            bl_a = b[CHUNK - 1:CHUNK, :]
            bl_b = b[LIN_BLOCK - 1:LIN_BLOCK, :]
            bl = jnp.where(first_rows, bl_a, bl_b)
            q_dec = q * QK_SCALE * jnp.exp(b)
            chains.append(dict(kind="gla", bi=bi, s=s, cols=cols, gate=gg, q_dec=q_dec, q_int=q_dec,
                               k_inv=k * jnp.exp(-b), k_dec=k * jnp.exp(bl - b), v=v, intra_scale=None,
                               dec_a=jnp.exp(bl_a), dec_b=jnp.exp(bl_b), st_prev=stg_ref[bi, s]))
    for bi in range(batch):
        for s in range(RET_W // LANES):
            cols = slice(s * LANES, (s + 1) * LANES)
            q, k, v, gg = (xr_ref[bi, :, j * RET_W + s * LANES:j * RET_W + (s + 1) * LANES] for j in range(4))
            cd = cd_ref[:, cols]
            chains.append(dict(kind="ret", bi=bi, s=s, cols=cols, gate=gg, q_dec=q, q_int=q * qf_ref[:, cols], k_inv=k,
                               k_dec=k * kf_ref[:, cols], v=v, intra_scale=dmat_ref[s], dec_a=cd, dec_b=cd,
                               st_prev=str_ref[bi, s]))
    results = _lin_core(chains, consts)
    for c, (_, st_new) in zip(chains, results):
        (stg_ref if c["kind"] == "gla" else str_ref)[c["bi"], c["s"]] = st_new
    n_gla = batch * n_slab
    o_gla = jnp.concatenate([o for o, _ in results[:n_gla]], axis=0)
    o_ret = jnp.concatenate([o for o, _ in results[n_gla:]], axis=0)
    ms = _head_stats(o_gla * o_gla, mean_op)
    oc = o_ret - _head_stats(o_ret, mean_op)
    var = _head_stats(oc * oc, mean_op)
    y_gla = o_gla * lax.rsqrt(ms + NORM_EPS)
    y_ret = oc * lax.rsqrt(var + NORM_EPS)
    for n, c in enumerate(chains):
        if c["kind"] == "gla":
            y = y_gla[n * LIN_BLOCK:(n + 1) * LIN_BLOCK] * ngg_ref[:, c["cols"]] * _silu(c["gate"])
            og_ref[c["bi"], :, c["cols"]] = y.astype(BF16)
        else:
            m = n - n_gla
            y = y_ret[m * LIN_BLOCK:(m + 1) * LIN_BLOCK] * ngr_ref[:, c["cols"]] * _silu(c["gate"])
            or_ref[c["bi"], :, c["cols"]] = y.astype(BF16)


def _lin_pair_call(gla, lr, a2_pad, a_bias, gla_g, ret, dmat, qf, kf, cd, ret_g, *, layer, batch, seq):
    tb = LIN_BLOCK

    def lspec(shape):
        return pl.BlockSpec((None,) + shape, lambda t: (layer, 0, 0))

    def tok(width):
        return pl.BlockSpec((batch, tb, width), lambda t: (0, t, 0))

    full = lambda a: pl.BlockSpec(a.shape, lambda t: (0,) * a.ndim)
    og, orr = pl.pallas_call(
        functools.partial(_lin_pair_kernel, batch=batch),
        grid=(seq // tb,),
        in_specs=[tok(4 * GLA_W), tok(LANES), lspec((LANES, GLA_W)), lspec((1, GLA_W)), lspec((1, GLA_W)),
                  tok(4 * RET_W), full(dmat), full(qf), full(kf), full(cd), lspec((1, RET_W))],
        out_specs=[tok(GLA_W), tok(RET_W)],
        out_shape=[jax.ShapeDtypeStruct((batch, seq, GLA_W), BF16), jax.ShapeDtypeStruct((batch, seq, RET_W), BF16)],
        scratch_shapes=[pltpu.VMEM((batch, GLA_W // LANES, LANES, LANES), F32),
                        pltpu.VMEM((batch, RET_W // LANES, LANES, LANES), F32)],
        compiler_params=_params(1),
        name=f"lin_pair_l{layer}",
    )(gla.reshape(batch, seq, 4 * GLA_W), lr.reshape(batch, seq, LANES), a2_pad, a_bias, gla_g,
      ret.reshape(batch, seq, 4 * RET_W), dmat, qf, kf, cd, ret_g)
    return og.reshape(batch * seq, GLA_W), orr.reshape(batch * seq, RET_W)


def _layout_proj_weights(w_in, nsa_gate_bias):
    n_layers, d, _ = w_in.shape
    (g_q, g_k, g_v, g_g, g_lr, n_q, n_kc, n_vc, n_ks, n_vs, n_kw, n_vw, n_gate,
     r_q, r_k, r_v, r_g) = jnp.split(w_in, IN_SPLITS, axis=-1)
    zeros = lambda w: jnp.zeros((n_layers, d, w), w_in.dtype)
    def pad_heads(w, n_heads):
        w = w.reshape(n_layers, d, n_heads, HEAD_DIM)
        return jnp.concatenate([w, jnp.zeros_like(w)], axis=-1).reshape(n_layers, d, n_heads * LANES)

    per_grp = 3 * NSA_REP
    gate_cols = []
    bias_cols = []
    for g in range(NSA_KV_HEADS):
        gate_cols += [n_gate[..., g * per_grp:(g + 1) * per_grp], zeros(LANES - per_grp)]
        bias_cols += [nsa_gate_bias[:, g * per_grp:(g + 1) * per_grp],
                      jnp.zeros((n_layers, LANES - per_grp), nsa_gate_bias.dtype)]
    w = jnp.concatenate([g_q, g_k, g_v, g_g, g_lr, zeros(LANES - GLA_LOWRANK), n_q,
                         n_kc, n_vc, pad_heads(n_ks, NSA_KV_HEADS), pad_heads(n_kw, NSA_KV_HEADS), n_vs, n_vw]
                        + gate_cols + [r_q, r_k, r_v, r_g], axis=-1)
    assert w.shape[-1] == PROJ_W
    bias = jnp.concatenate(bias_cols, axis=-1).reshape(n_layers, 1, NSA_KV_HEADS * LANES)
    return w.astype(BF16), bias


def _layout_cmp_weights(pe_k, pe_v, w1_k, w2_k, w1_v, w2_v):
    n_layers = pe_k.shape[0]
    eye = jnp.eye(2, dtype=BF16)
    w1 = jnp.stack([w1_k, w1_v], axis=1).reshape(n_layers, 2, CMP_LEN, HEAD_DIM, CMP_HIDDEN).astype(BF16)
    blocks = [jnp.pad(w1[:, kv], ((0, 0), (0, 0), (0, 0), ((2 * kv + g) * CMP_HIDDEN, (3 - 2 * kv - g) * CMP_HIDDEN)))
              for kv in range(2) for g in range(NSA_KV_HEADS)]
    w1_full = jnp.stack(blocks, axis=2).reshape(n_layers, CMP_LEN * 2 * NSA_KV_W, 4 * CMP_HIDDEN)
    half = CMP_STRIDE * 2 * NSA_KV_W
    w2 = jnp.stack([w2_k, w2_v], axis=1).astype(BF16)
    w2_full = jnp.einsum('zknd,kq,gh->zqhnkgd', w2, eye, eye)
    w2_full = w2_full.reshape(n_layers, 4 * CMP_HIDDEN, 2 * NSA_KV_W)
    w2_k = w2_full[..., :NSA_KV_W].reshape(n_layers, 4 * CMP_HIDDEN, NSA_KV_HEADS, HEAD_DIM)
    w2_k = jnp.concatenate([w2_k, jnp.zeros_like(w2_k)], axis=-1).reshape(n_layers, 4 * CMP_HIDDEN, -1)
    w2_full = jnp.concatenate([w2_k, w2_full[..., NSA_KV_W:]], axis=-1).astype(BF16)
    pe = jnp.stack([pe_k, pe_v], axis=1)
    pe = jnp.broadcast_to(pe.transpose(0, 2, 1, 3)[:, :, :, None, :],
                          (n_layers, CMP_LEN, 2, NSA_KV_HEADS, HEAD_DIM))
    pe = pe.reshape(n_layers, 1, CMP_LEN * 2 * NSA_KV_W)
    return pe[:, :, :half], pe[:, :, half:], w1_full.reshape(n_layers, 2, half, 4 * CMP_HIDDEN), w2_full


def _rotary_tables(seq):
    half = HEAD_DIM // 2
    inv_freq = ROPE_THETA ** (-jnp.arange(half, dtype=F32) / half)
    ang = jnp.arange(seq).astype(F32)[:, None] * inv_freq[None, :]
    cos, sin = jnp.cos(ang), jnp.sin(ang)
    reps = LANES // HEAD_DIM
    return jnp.tile(jnp.concatenate([cos, cos], axis=-1), (1, reps)), \
        jnp.tile(jnp.concatenate([-sin, sin], axis=-1), (1, reps))


def _retention_tables():
    log_gamma = jnp.log1p(-jnp.exp2(-5.0 - jnp.arange(RET_HEADS, dtype=F32)))
    lg_lane = jnp.repeat(log_gamma, HEAD_DIM)[None, :]
    pos = (jnp.arange(LIN_BLOCK) % CHUNK).astype(F32)[:, None]
    qf = jnp.exp(lg_lane * (pos + 1.0))
    kf = jnp.exp(lg_lane * (CHUNK - 1.0 - pos))
    cd = jnp.exp(lg_lane * CHUNK)
    r = jnp.arange(LIN_BLOCK)
    rel = (r[:, None] - r[None, :]).astype(F32)
    ok = ((r[:, None] // CHUNK) == (r[None, :] // CHUNK)) & (rel >= 0)
    dm = jnp.where(ok[None], jnp.exp(log_gamma[:, None, None] * rel[None]), 0.0)
    dmat = dm.reshape(RET_W // LANES, 2 * LIN_BLOCK, LIN_BLOCK)
    return dmat, qf, kf, cd


def kernel(x, c, w_ada, b_ada, norm_g, ffn1_in, ffn1_out, w_in, gla_a2, gla_a_bias, gla_norm_g,
           nsa_pe_k, nsa_pe_v, nsa_w1_k, nsa_w2_k, nsa_w1_v, nsa_w2_v, nsa_gate_bias, ret_norm_g,
           w_out, ffn2_in, ffn2_out, final_norm_g):
    batch, seq, d = x.shape
    n_layers = w_ada.shape[0]
    assert batch <= 8 and seq % 512 == 0 and seq >= WINDOW + Q_BLOCK

    c_pad = jnp.zeros((8, d), F32).at[:batch].set(c)
    mod = _ada_call(c_pad, w_ada, b_ada).reshape(n_layers * 8 * N_ADA, 1, d)
    ng = norm_g.reshape(n_layers * 3, 1, d)

    w_proj, gate_bias = _layout_proj_weights(w_in, nsa_gate_bias)
    pe_top, pe_bot, w1_cmp, w2_cmp = _layout_cmp_weights(nsa_pe_k, nsa_pe_v, nsa_w1_k, nsa_w2_k, nsa_w1_v, nsa_w2_v)
    cosf, sinf = _rotary_tables(seq)
    dmat, qf, kf, cd = _retention_tables()
    a2_pad = jnp.zeros((n_layers, LANES, GLA_W), F32).at[:, :GLA_LOWRANK].set(gla_a2)
    a_bias = gla_a_bias.reshape(n_layers, 1, GLA_W)
    gla_g = jnp.tile(gla_norm_g, (1, GLA_HEADS)).reshape(n_layers, 1, GLA_W)
    ret_g = jnp.tile(ret_norm_g, (1, RET_HEADS)).reshape(n_layers, 1, RET_W)
    f1_in, f1_out = ffn1_in.astype(BF16), ffn1_out.astype(BF16)
    f2_in, f2_out = ffn2_in.astype(BF16), ffn2_out.astype(BF16)
    w_o = w_out.astype(BF16)

    xs = x.reshape(batch * seq, d)
    for l in range(n_layers):
        xs = _ffn_call(xs, ng, mod, f1_in, f1_out, layer=l, sub=0, seq=seq)
        (gla, lr, qraw, qrot, kcvc, ks, vs, kw, vw, gate, ret) = _proj_call(
            xs, ng, mod, w_proj, gate_bias, cosf, sinf, layer=l, seq=seq)
        kc, vc = _cmp_call(kcvc, pe_top, pe_bot, w1_cmp, w2_cmp, layer=l, batch=batch, seq=seq)
        o_gla, o_ret = _lin_pair_call(gla, lr, a2_pad, a_bias, gla_g, ret, dmat, qf, kf, cd, ret_g,
                                      layer=l, batch=batch, seq=seq)
        o_nsa = _nsa_call(qraw, qrot, kc, vc, ks, vs, kw, vw, gate, layer=l, batch=batch, seq=seq)
        final_g = final_norm_g.reshape(1, d) if l == n_layers - 1 else None
        xs = _ffn_call(xs, ng, mod, f2_in, f2_out, layer=l, sub=2, seq=seq, mixer=(o_gla, o_nsa, o_ret, w_o),
                       final_g=final_g)
    return xs.reshape(batch, seq, d)
```
